```python
import math
import jax
import jax.numpy as jnp
from jax import lax
import numpy as np

D_MODEL = 1024
BATCH = 4
SEQ = 8192
DEPTH = 2
DEC_BATCH = 32
DEC_SEQ = 4
PAST_LEN = 16384
PAGE_SIZE = 128

N_EVEN = (DEPTH + 1) // 2
N_ODD = DEPTH // 2
NORM_EPS = 1e-6
F32 = jnp.float32

RW_HEAD = 64
RW_HEADS = (D_MODEL // 2) // RW_HEAD
RW_DIM = RW_HEADS * RW_HEAD
RW_DECAY_LORA = 64
RW_ICL_LORA = 64
RW_GATE_LORA = 128
RW_COLS = 3 * RW_DIM + RW_DECAY_LORA + RW_ICL_LORA + RW_GATE_LORA
RW_SPLITS = [RW_DIM, 2 * RW_DIM, 3 * RW_DIM, 3 * RW_DIM + RW_DECAY_LORA, 3 * RW_DIM + RW_DECAY_LORA + RW_ICL_LORA]
RW_LN_EPS = 64e-5

SSM_INNER = D_MODEL
SSM_HEAD = 64
SSM_HEADS = SSM_INNER // SSM_HEAD
SSM_GROUPS = 2
SSM_STATE = 128
SSM_CONV = 4
SSM_CONV_DIM = SSM_INNER + 2 * SSM_GROUPS * SSM_STATE
SSM_COLS = SSM_INNER + SSM_CONV_DIM + SSM_HEADS
SSD_CHUNK = 128

IN_AB = RW_COLS + SSM_COLS
MIX_AB = RW_DIM + SSM_INNER

ATT_HEADS = 16
ATT_KV_HEADS = 4
ATT_HEAD = D_MODEL // ATT_HEADS
ATT_GROUP = ATT_HEADS // ATT_KV_HEADS
QKV_COLS = (ATT_HEADS + 2 * ATT_KV_HEADS) * ATT_HEAD
MOBA_BLOCK = 256
MOBA_TOPK = 3
MOBA_QCHUNK = 32

FFN_DIM = 2816
N_EXPERTS = 8
MOE_TOPK = 2
EXPERT_DIM = 1408

kernel_name = 'rwkv7_mamba2_moba_hybrid_step'


def rmsnorm(x, g):
    xf = x.astype(F32)
    y = xf * lax.rsqrt(jnp.mean(xf * xf, axis=-1, keepdims=True) + NORM_EPS)
    return (y * g.astype(F32)).astype(x.dtype)


def swiglu(h, wg, wu, wd):
    return (jax.nn.silu(h @ wg) * (h @ wu)) @ wd


def rwkv7_mix(p, shift_buf, wkv0, mu, w0, w2, a0, a2, g2, k_k, k_a, r_k, ln_w, ln_b):
    B, T, _ = p.shape
    prev = jnp.concatenate([shift_buf[:, None, :].astype(p.dtype), p[:, :-1]], axis=1)
    u = (p + (prev - p) * mu).astype(F32)
    r, k, v, wd, ad, gd = jnp.split(u, RW_SPLITS, axis=-1)
    w = -jax.nn.softplus(-(w0.astype(F32) + jnp.tanh(wd) @ w2.astype(F32))) - 0.5
    decay = jnp.exp(-jnp.exp(w))
    a = jax.nn.sigmoid(a0.astype(F32) + ad @ a2.astype(F32))
    g = jax.nn.sigmoid(gd) @ g2.astype(F32)
    heads = lambda t: t.reshape(B, T, RW_HEADS, RW_HEAD)
    kk = heads(k * k_k.astype(F32))
    kk = kk / jnp.maximum(jnp.sqrt(jnp.sum(kk * kk, axis=-1, keepdims=True)), 1e-12)
    k = heads(k * (1.0 + (a - 1.0) * k_a.astype(F32)))
    r, v, a, decay = heads(r), heads(v), heads(a), heads(decay)

    def step(S, inp):
        r_t, w_t, k_t, v_t, kk_t, a_t = inp
        s_kk = jnp.einsum('bhvk,bhk->bhv', S, kk_t)
        S = S * w_t[:, :, None, :] - s_kk[..., None] * (kk_t * a_t)[:, :, None, :] + v_t[..., None] * k_t[:, :, None, :]
        return S, jnp.einsum('bhvk,bhk->bhv', S, r_t)

    seq_first = lambda t: jnp.swapaxes(t, 0, 1)
    s_final, y = lax.scan(step, wkv0.astype(F32), tuple(seq_first(t) for t in (r, decay, k, v, kk, a)))
    y = seq_first(y)
    mean = jnp.mean(y, axis=-1, keepdims=True)
    var = jnp.mean(jnp.square(y - mean), axis=-1, keepdims=True)
    y = ((y - mean) * lax.rsqrt(var + RW_LN_EPS)).reshape(B, T, RW_DIM) * ln_w.astype(F32) + ln_b.astype(F32)
    y = y + (jnp.sum(r * k * r_k.astype(F32), axis=-1, keepdims=True) * v).reshape(B, T, RW_DIM)
    y = y * g
    return y.astype(p.dtype), p[:, -1], s_final.astype(wkv0.dtype)


def ssd_chunked(x, dt, A, bm, cm, h0):
    B, T = x.shape[:2]
    G, E = SSM_GROUPS, SSM_HEADS // SSM_GROUPS
    lc = min(SSD_CHUNK, T)
    nc = T // lc
    x = x.reshape(B, nc, lc, G, E, SSM_HEAD)
    dt = dt.reshape(B, nc, lc, G, E)
    bm = bm.reshape(B, nc, lc, G, SSM_STATE)
    cm = cm.reshape(B, nc, lc, G, SSM_STATE)
    acum = jnp.cumsum(dt * A.reshape(G, E), axis=2)
    causal = jnp.tril(jnp.ones((lc, lc), dtype=bool))[:, :, None, None]
    seg = acum[:, :, :, None] - acum[:, :, None, :]
    decay_ij = jnp.exp(jnp.where(causal, seg, -jnp.inf))
    xdt = dt[..., None] * x
    cb = jnp.einsum('bcign,bcjgn->bcijg', cm, bm)
    y_diag = jnp.einsum('bcijg,bcijge,bcjgep->bcigep', cb, decay_ij, xdt)
    decay_end = jnp.exp(acum[:, :, -1:] - acum)
    states = jnp.einsum('bclgn,bclgep->bcgepn', bm, decay_end[..., None] * xdt)
    chunk_decay = jnp.exp(acum[:, :, -1])

    def step(h, inp):
        s, d = inp
        return d[..., None, None] * h + s, h

    h_last, h_in = lax.scan(step, h0.reshape(B, G, E, SSM_HEAD, SSM_STATE),
                            (jnp.moveaxis(states, 1, 0), jnp.moveaxis(chunk_decay, 1, 0)))
    h_in = jnp.moveaxis(h_in, 0, 1)
    y_off = jnp.einsum('bcign,bcige,bcgepn->bcigep', cm, jnp.exp(acum), h_in)
    y = (y_diag + y_off).reshape(B, T, SSM_HEADS, SSM_HEAD)
    return y, h_last.reshape(B, SSM_HEADS, SSM_HEAD, SSM_STATE)


def mamba2_mix(p, conv_buf, ssm0, conv_w, conv_b, dt_bias, a_log, d_skip, norm_w):
    B, T, _ = p.shape
    z, xbc, dt = jnp.split(p, [SSM_INNER, SSM_INNER + SSM_CONV_DIM], axis=-1)
    full = jnp.concatenate([conv_buf.astype(p.dtype), xbc], axis=1)
    conv = conv_b + full[:, 0:T] * conv_w[0]
    for i in range(1, SSM_CONV):
        conv = conv + full[:, i:i + T] * conv_w[i]
    xbc = jax.nn.silu(conv.astype(F32))
    xs, bm, cm = jnp.split(xbc, [SSM_INNER, SSM_INNER + SSM_GROUPS * SSM_STATE], axis=-1)
    xs = xs.reshape(B, T, SSM_HEADS, SSM_HEAD)
    bm = bm.reshape(B, T, SSM_GROUPS, SSM_STATE)
    cm = cm.reshape(B, T, SSM_GROUPS, SSM_STATE)
    dt = jax.nn.softplus(dt.astype(F32) + dt_bias.astype(F32))
    A = -jnp.exp(a_log.astype(F32))
    y, h_final = ssd_chunked(xs, dt, A, bm, cm, ssm0.astype(F32))
    y = y + d_skip.astype(F32)[:, None] * xs
    y = y.reshape(B, T, SSM_INNER) * jax.nn.silu(z.astype(F32))
    yg = y.reshape(B, T, SSM_GROUPS, SSM_INNER // SSM_GROUPS)
    yg = yg * lax.rsqrt(jnp.mean(yg * yg, axis=-1, keepdims=True) + NORM_EPS)
    y = yg.reshape(B, T, SSM_INNER) * norm_w.astype(F32)
    new_buf = full[:, full.shape[1] - (SSM_CONV - 1):]
    return y.astype(p.dtype), new_buf, h_final.astype(ssm0.dtype)


def mixer_ab(h, shift_buf, wkv0, conv_buf, ssm0, w_in, mu, w0, w2, a0, a2, g2, k_k, k_a, r_k,
             ln_w, ln_b, conv_w, conv_b, dt_bias, a_log, d_skip, norm_w, w_out):
    proj = h @ w_in
    y_rw, shift_new, wkv_new = rwkv7_mix(proj[..., :RW_COLS], shift_buf, wkv0, mu, w0, w2, a0, a2, g2,
                                         k_k, k_a, r_k, ln_w, ln_b)
    y_ssm, conv_new, ssm_new = mamba2_mix(proj[..., RW_COLS:], conv_buf, ssm0, conv_w, conv_b,
                                          dt_bias, a_log, d_skip, norm_w)
    out = jnp.concatenate([y_rw, y_ssm], axis=-1) @ w_out
    return out, (shift_new, wkv_new, conv_new, ssm_new)


def alibi_slopes():
    return jnp.asarray((2.0 ** (-8.0 * np.arange(1, ATT_HEADS + 1) / ATT_HEADS)).astype(np.float32))


def attn_proj(h, w_qkv):
    B, T, _ = h.shape
    qkv = h @ w_qkv
    q = qkv[..., :ATT_HEADS * ATT_HEAD].reshape(B, T, ATT_HEADS, ATT_HEAD)
    k = qkv[..., ATT_HEADS * ATT_HEAD:(ATT_HEADS + ATT_KV_HEADS) * ATT_HEAD].reshape(B, T, ATT_KV_HEADS, ATT_HEAD)
    v = qkv[..., (ATT_HEADS + ATT_KV_HEADS) * ATT_HEAD:].reshape(B, T, ATT_KV_HEADS, ATT_HEAD)
    return q, k, v


def pad_blocks(means):
    extra = max(MOBA_TOPK - means.shape[2], 0)
    return jnp.pad(means, ((0, 0), (0, 0), (0, extra), (0, 0)))


def moba_select(q, means, n_past):
    gate = jnp.einsum('bhqd,bhnd->bhqn', q.astype(F32), means)
    nb = means.shape[2]
    gate = jnp.where(jnp.arange(nb)[None, None, None, :] < n_past[None, None, :, None], gate, -jnp.inf)
    _, idx = lax.top_k(gate, MOBA_TOPK)
    ok = jnp.arange(MOBA_TOPK)[None, :] < n_past[:, None]
    return idx, ok


def moba_core(q, q_pos, k_sel, v_sel, sel_pos, sel_ok, k_own, v_own, own_pos, slopes):
    scale = ATT_HEAD ** -0.5
    slope = slopes[None, :, None, None]
    s_sel = jnp.einsum('bhqd,bhqsd->bhqs', q, k_sel, preferred_element_type=F32) * scale
    s_sel = s_sel - slope * (q_pos[None, None, :, None] - sel_pos).astype(F32)
    s_sel = jnp.where(sel_ok[None, None], s_sel, -jnp.inf)
    dist = q_pos[:, None] - own_pos[None, :]
    own_ok = (dist >= 0) & ((own_pos[None, :] // MOBA_BLOCK) == (q_pos[:, None] // MOBA_BLOCK))
    s_own = jnp.einsum('bhqd,bhld->bhql', q, k_own, preferred_element_type=F32) * scale
    s_own = jnp.where(own_ok[None, None], s_own - slope * dist.astype(F32)[None, None], -jnp.inf)
    probs = jax.nn.softmax(jnp.concatenate([s_sel, s_own], axis=-1), axis=-1)
    n_sel = s_sel.shape[-1]
    out = (jnp.einsum('bhqs,bhqsd->bhqd', probs[..., :n_sel].astype(v_sel.dtype), v_sel, preferred_element_type=F32)
           + jnp.einsum('bhql,bhld->bhqd', probs[..., n_sel:].astype(v_own.dtype), v_own, preferred_element_type=F32))
    return out.astype(q.dtype)


def moba_prompt(q, k, v, slopes):
    B, T = q.shape[:2]
    nb = -(-T // MOBA_BLOCK)
    pad = nb * MOBA_BLOCK - T
    to_blocks = lambda t: jnp.pad(t, ((0, 0), (0, pad), (0, 0), (0, 0))).reshape(
        B, nb, MOBA_BLOCK, ATT_KV_HEADS, ATT_HEAD).transpose(0, 3, 1, 2, 4)
    kb, vb = to_blocks(k), to_blocks(v)
    means = pad_blocks(jnp.repeat(jnp.mean(kb.astype(F32), axis=3), ATT_GROUP, axis=1))
    qh = q.transpose(0, 2, 1, 3)
    bi = jnp.arange(B)[:, None, None, None]
    kvh = (jnp.arange(ATT_HEADS) // ATT_GROUP)[None, :, None, None]

    def chunk(c):
        t0 = c * MOBA_QCHUNK
        qc = lax.dynamic_slice_in_dim(qh, t0, MOBA_QCHUNK, axis=2)
        q_pos = t0 + jnp.arange(MOBA_QCHUNK)
        idx, ok = moba_select(qc, means, q_pos // MOBA_BLOCK)
        idx = jnp.minimum(idx, nb - 1)
        k_sel = kb[bi, kvh, idx].reshape(B, ATT_HEADS, MOBA_QCHUNK, MOBA_TOPK * MOBA_BLOCK, ATT_HEAD)
        v_sel = vb[bi, kvh, idx].reshape(B, ATT_HEADS, MOBA_QCHUNK, MOBA_TOPK * MOBA_BLOCK, ATT_HEAD)
        sel_pos = (idx[..., None] * MOBA_BLOCK + jnp.arange(MOBA_BLOCK)).reshape(B, ATT_HEADS, MOBA_QCHUNK, -1)
        ob = t0 // MOBA_BLOCK
        k_own = jnp.repeat(lax.dynamic_index_in_dim(kb, ob, axis=2, keepdims=False), ATT_GROUP, axis=1)
        v_own = jnp.repeat(lax.dynamic_index_in_dim(vb, ob, axis=2, keepdims=False), ATT_GROUP, axis=1)
        own_pos = ob * MOBA_BLOCK + jnp.arange(MOBA_BLOCK)
        return moba_core(qc, q_pos, k_sel, v_sel, sel_pos, jnp.repeat(ok, MOBA_BLOCK, axis=1),
                         k_own, v_own, own_pos, slopes)

    out = lax.map(chunk, jnp.arange(T // MOBA_QCHUNK))
    return out.transpose(1, 0, 3, 2, 4).reshape(B, T, ATT_HEADS, ATT_HEAD)


def moba_sample(q, k_new, v_new, ck, cv, page_table, slopes):
    DB, Q = q.shape[:2]
    n_pages = page_table.shape[1]
    past = n_pages * PAGE_SIZE
    ppb = MOBA_BLOCK // PAGE_SIZE
    nbp = past // MOBA_BLOCK
    page_sum = jnp.sum(ck[page_table].astype(F32), axis=2)
    means = jnp.sum(page_sum[:, :nbp * ppb].reshape(DB, nbp, ppb, ATT_KV_HEADS, ATT_HEAD), axis=2) / MOBA_BLOCK
    means = pad_blocks(jnp.repeat(means.transpose(0, 2, 1, 3), ATT_GROUP, axis=1))
    qh = q.transpose(0, 2, 1, 3)
    q_pos = past + jnp.arange(Q)
    idx, ok = moba_select(qh, means, q_pos // MOBA_BLOCK)
    idx = jnp.clip(idx, 0, max(nbp - 1, 0))
    lp = jnp.minimum(idx[..., None] * ppb + jnp.arange(ppb), n_pages - 1)
    phys = page_table[jnp.arange(DB)[:, None, None, None, None], lp]
    kvh = (jnp.arange(ATT_HEADS) // ATT_GROUP)[None, :, None, None, None]
    k_sel = ck[phys, :, kvh].reshape(DB, ATT_HEADS, Q, MOBA_TOPK * MOBA_BLOCK, ATT_HEAD)
    v_sel = cv[phys, :, kvh].reshape(DB, ATT_HEADS, Q, MOBA_TOPK * MOBA_BLOCK, ATT_HEAD)
    sel_pos = (idx[..., None] * MOBA_BLOCK + jnp.arange(MOBA_BLOCK)).reshape(DB, ATT_HEADS, Q, -1)
    own_start = nbp * MOBA_BLOCK
    n_own_pages = (past - own_start) // PAGE_SIZE
    own_pages = page_table[:, own_start // PAGE_SIZE:]
    k_past_own = ck[own_pages].reshape(DB, n_own_pages * PAGE_SIZE, ATT_KV_HEADS, ATT_HEAD)
    v_past_own = cv[own_pages].reshape(DB, n_own_pages * PAGE_SIZE, ATT_KV_HEADS, ATT_HEAD)
    k_own = jnp.concatenate([k_past_own.astype(k_new.dtype), k_new], axis=1).transpose(0, 2, 1, 3)
    v_own = jnp.concatenate([v_past_own.astype(v_new.dtype), v_new], axis=1).transpose(0, 2, 1, 3)
    own_pos = own_start + jnp.arange(k_own.shape[2])
    out = moba_core(qh, q_pos, k_sel, v_sel, sel_pos, jnp.repeat(ok, MOBA_BLOCK, axis=1),
                    jnp.repeat(k_own, ATT_GROUP, axis=1), jnp.repeat(v_own, ATT_GROUP, axis=1), own_pos, slopes)
    return out.transpose(0, 2, 1, 3)


def moe_swiglu(h, router, wg, wu, wd):
    logits = (h @ router).astype(F32)
    top_val, top_idx = lax.top_k(logits, MOE_TOPK)
    top_w = jax.nn.softmax(top_val, axis=-1)
    combine = jnp.sum(jax.nn.one_hot(top_idx, N_EXPERTS, dtype=F32) * top_w[..., None], axis=-2)
    out = jnp.zeros_like(h)
    for e in range(N_EXPERTS):
        out = out + combine[..., e:e + 1].astype(h.dtype) * swiglu(h, wg[e], wu[e], wd[e])
    return out


def setup_inputs(seed: int = 0) -> dict:
    key = jax.random.key(seed)
    ks = iter(jax.random.split(key, 64))

    def nrm(shape, scale):
        return jax.random.normal(next(ks), shape, F32) * scale

    def uni(shape, lo, hi):
        return jax.random.uniform(next(ks), shape, F32, lo, hi)

    ne, no = N_EVEN, N_ODD
    n_pages = PAST_LEN // PAGE_SIZE
    n_pool = (5 * DEC_BATCH * n_pages) // 4
    page_table = jax.random.permutation(next(ks), n_pool)[:DEC_BATCH * n_pages].reshape(DEC_BATCH, n_pages).astype(jnp.int32)
    dt0 = jnp.exp(uni((ne, SSM_HEADS), math.log(1e-3), math.log(1e-1)))
    return {
        'x_prompt': nrm((BATCH, SEQ, D_MODEL), 1.0),
        'x_sample': nrm((DEC_BATCH, DEC_SEQ, D_MODEL), 1.0),
        'state_rwkv_shift': nrm((ne, DEC_BATCH, RW_COLS), 1.0),
        'state_rwkv_wkv': nrm((ne, DEC_BATCH, RW_HEADS, RW_HEAD, RW_HEAD), 0.5),
        'state_ssm_conv': nrm((ne, DEC_BATCH, SSM_CONV - 1, SSM_CONV_DIM), 1.0),
        'state_ssm': nrm((ne, DEC_BATCH, SSM_HEADS, SSM_HEAD, SSM_STATE), 0.3),
        'cache_k': nrm((no, n_pool, PAGE_SIZE, ATT_KV_HEADS, ATT_HEAD), 1.0),
        'cache_v': nrm((no, n_pool, PAGE_SIZE, ATT_KV_HEADS, ATT_HEAD), 1.0),
        'page_table': page_table,
        'norm_mix': 1.0 + nrm((DEPTH, D_MODEL), 0.02),
        'norm_ffn': 1.0 + nrm((DEPTH, D_MODEL), 0.02),
        'norm_final': 1.0 + nrm((D_MODEL,), 0.02),
        'w_in_ab': nrm((ne, D_MODEL, IN_AB), D_MODEL ** -0.5),
        'rwkv_mu': uni((ne, RW_COLS), 0.0, 1.0),
        'rwkv_w0': uni((ne, RW_DIM), -6.5, -1.5),
        'rwkv_w2': nrm((ne, RW_DECAY_LORA, RW_DIM), 0.5 * RW_DECAY_LORA ** -0.5),
        'rwkv_a0': nrm((ne, RW_DIM), 0.1),
        'rwkv_a2': nrm((ne, RW_ICL_LORA, RW_DIM), 0.5 * RW_ICL_LORA ** -0.5),
        'rwkv_g2': nrm((ne, RW_GATE_LORA, RW_DIM), RW_GATE_LORA ** -0.5),
        'rwkv_k_k': 0.85 + nrm((ne, RW_DIM), 0.05),
        'rwkv_k_a': 1.0 + nrm((ne, RW_DIM), 0.05),
        'rwkv_r_k': nrm((ne, RW_HEADS, RW_HEAD), 0.1),
        'rwkv_ln_w': 1.0 + nrm((ne, RW_DIM), 0.02),
        'rwkv_ln_b': nrm((ne, RW_DIM), 0.02),
        'ssm_conv_w': nrm((ne, SSM_CONV, SSM_CONV_DIM), SSM_CONV ** -0.5),
        'ssm_conv_b': nrm((ne, SSM_CONV_DIM), 0.02),
        'ssm_dt_bias': dt0 + jnp.log(-jnp.expm1(-dt0)),
        'ssm_a_log': jnp.log(uni((ne, SSM_HEADS), 1.0, 16.0)),
        'ssm_d': 1.0 + nrm((ne, SSM_HEADS), 0.1),
        'ssm_norm_w': 1.0 + nrm((ne, SSM_INNER), 0.02),
        'w_out_ab': nrm((ne, MIX_AB, D_MODEL), MIX_AB ** -0.5),
        'ffn_w_gate': nrm((ne, D_MODEL, FFN_DIM), D_MODEL ** -0.5),
        'ffn_w_up': nrm((ne, D_MODEL, FFN_DIM), D_MODEL ** -0.5),
        'ffn_w_down': nrm((ne, FFN_DIM, D_MODEL), FFN_DIM ** -0.5),
        'attn_w_qkv': nrm((no, D_MODEL, QKV_COLS), D_MODEL ** -0.5),
        'attn_w_o': nrm((no, ATT_HEADS * ATT_HEAD, D_MODEL), (ATT_HEADS * ATT_HEAD) ** -0.5),
        'moe_router': nrm((no, D_MODEL, N_EXPERTS), D_MODEL ** -0.5),
        'moe_w_gate': nrm((no, N_EXPERTS, D_MODEL, EXPERT_DIM), D_MODEL ** -0.5),
        'moe_w_up': nrm((no, N_EXPERTS, D_MODEL, EXPERT_DIM), D_MODEL ** -0.5),
        'moe_w_down': nrm((no, N_EXPERTS, EXPERT_DIM, D_MODEL), EXPERT_DIM ** -0.5),
    }


def reference(x_prompt, x_sample, state_rwkv_shift, state_rwkv_wkv, state_ssm_conv, state_ssm,
              cache_k, cache_v, page_table, norm_mix, norm_ffn, norm_final,
              w_in_ab, rwkv_mu, rwkv_w0, rwkv_w2, rwkv_a0, rwkv_a2, rwkv_g2, rwkv_k_k, rwkv_k_a, rwkv_r_k,
              rwkv_ln_w, rwkv_ln_b, ssm_conv_w, ssm_conv_b, ssm_dt_bias, ssm_a_log, ssm_d, ssm_norm_w,
              w_out_ab, ffn_w_gate, ffn_w_up, ffn_w_down, attn_w_qkv, attn_w_o,
              moe_router, moe_w_gate, moe_w_up, moe_w_down):
    slopes = alibi_slopes()
    bp, tp = x_prompt.shape[:2]
    db, ts = x_sample.shape[:2]
    xp, xs = x_prompt, x_sample
    p_shift, p_wkv, p_conv, p_ssm, p_k, p_v = [], [], [], [], [], []
    s_shift, s_wkv, s_conv, s_ssm, s_k, s_v = [], [], [], [], [], []
    for l in range(DEPTH):
        i = l // 2
        hp = rmsnorm(xp, norm_mix[l])
        hs = rmsnorm(xs, norm_mix[l])
        if l % 2 == 0:
            w_ab = (w_in_ab[i], rwkv_mu[i], rwkv_w0[i], rwkv_w2[i], rwkv_a0[i], rwkv_a2[i], rwkv_g2[i],
                    rwkv_k_k[i], rwkv_k_a[i], rwkv_r_k[i], rwkv_ln_w[i], rwkv_ln_b[i],
                    ssm_conv_w[i], ssm_conv_b[i], ssm_dt_bias[i], ssm_a_log[i], ssm_d[i], ssm_norm_w[i], w_out_ab[i])
            mp, st_p = mixer_ab(hp,
                                jnp.zeros((bp, RW_COLS), xp.dtype),
                                jnp.zeros((bp, RW_HEADS, RW_HEAD, RW_HEAD), xp.dtype),
                                jnp.zeros((bp, SSM_CONV - 1, SSM_CONV_DIM), xp.dtype),
                                jnp.zeros((bp, SSM_HEADS, SSM_HEAD, SSM_STATE), xp.dtype), *w_ab)
            ms, st_s = mixer_ab(hs, state_rwkv_shift[i], state_rwkv_wkv[i], state_ssm_conv[i], state_ssm[i], *w_ab)
            p_shift.append(st_p[0]); p_wkv.append(st_p[1]); p_conv.append(st_p[2]); p_ssm.append(st_p[3])
            s_shift.append(st_s[0]); s_wkv.append(st_s[1]); s_conv.append(st_s[2]); s_ssm.append(st_s[3])
        else:
            qp, kp, vp = attn_proj(hp, attn_w_qkv[i])
            mp = moba_prompt(qp, kp, vp, slopes).reshape(bp, tp, ATT_HEADS * ATT_HEAD) @ attn_w_o[i]
            qs, kn, vn = attn_proj(hs, attn_w_qkv[i])
            ms = moba_sample(qs, kn, vn, cache_k[i], cache_v[i], page_table, slopes).reshape(
                db, ts, ATT_HEADS * ATT_HEAD) @ attn_w_o[i]
            p_k.append(kp); p_v.append(vp); s_k.append(kn); s_v.append(vn)
        xp = xp + mp
        xs = xs + ms
        hp = rmsnorm(xp, norm_ffn[l])
        hs = rmsnorm(xs, norm_ffn[l])
        if l % 2 == 0:
            xp = xp + swiglu(hp, ffn_w_gate[i], ffn_w_up[i], ffn_w_down[i])
            xs = xs + swiglu(hs, ffn_w_gate[i], ffn_w_up[i], ffn_w_down[i])
        else:
            xp = xp + moe_swiglu(hp, moe_router[i], moe_w_gate[i], moe_w_up[i], moe_w_down[i])
            xs = xs + moe_swiglu(hs, moe_router[i], moe_w_gate[i], moe_w_up[i], moe_w_down[i])
    y_prompt = rmsnorm(xp, norm_final)
    y_sample = rmsnorm(xs, norm_final)
    return (y_prompt, y_sample,
            jnp.stack(p_shift), jnp.stack(p_wkv), jnp.stack(p_conv), jnp.stack(p_ssm), jnp.stack(p_k), jnp.stack(p_v),
            jnp.stack(s_shift), jnp.stack(s_wkv), jnp.stack(s_conv), jnp.stack(s_ssm), jnp.stack(s_k), jnp.stack(s_v))
```

```python
import functools

import numpy as np
import jax
import jax.numpy as jnp
from jax import lax
from jax.experimental import pallas as pl
from jax.experimental.pallas import tpu as pltpu

F32 = jnp.float32
BF16 = jnp.bfloat16
HI = lax.Precision.HIGHEST
NEG_INF = float("-inf")

D_MODEL = 1024
NORM_EPS = 1e-6

RW_HEAD = 64
RW_HEADS = 8
RW_DIM = RW_HEADS * RW_HEAD
RW_LORA = 128
RW_GATE = 128
RW_COLS = 3 * RW_DIM + RW_LORA + RW_GATE
RW_LN_EPS = 64e-5

SSM_INNER = 1024
SSM_HEAD = 64
SSM_HEADS = 16
SSM_GROUPS = 2
SSM_STATE = 128
SSM_CONV = 4
SSM_CONV_DIM = SSM_INNER + 2 * SSM_GROUPS * SSM_STATE
SSD_CHUNK = 128

PROJ_RW = 0
PROJ_DT = RW_COLS
PROJ_Z = 2048
PROJ_XBC = 3072
PROJ_COLS = PROJ_XBC + SSM_CONV_DIM

ATT_HEADS = 16
ATT_KV_HEADS = 4
ATT_HEAD = 64
ATT_GROUP = ATT_HEADS // ATT_KV_HEADS
ATT_Q_COLS = ATT_HEADS * ATT_HEAD
ATT_KV_COLS = ATT_KV_HEADS * ATT_HEAD
MOBA_BLOCK = 256
MOBA_TOPK = 3
MASK_NEG = -1e30
PAGE_SIZE = 128

N_EXPERTS = 8

VMEM_LIMIT = 56 * 1024 * 1024


def _cparams(sem):
    return pltpu.CompilerParams(dimension_semantics=sem, vmem_limit_bytes=VMEM_LIMIT)


def _nt(a, b, precision=None):
    return lax.dot_general(a, b, (((1,), (1,)), ((), ())), precision=precision,
                           preferred_element_type=F32)


def _dot(a, b, precision=None):
    return jnp.dot(a, b, precision=precision, preferred_element_type=F32)


def _rms(x, g):
    return x * lax.rsqrt(jnp.mean(x * x, axis=-1, keepdims=True) + NORM_EPS) * g


def _softplus(x):
    return jnp.maximum(x, 0.0) + jnp.log(1.0 + jnp.exp(-jnp.abs(x)))


def _sigmoid(x):
    return 1.0 / (1.0 + jnp.exp(-x))


def _silu(x):
    return x * _sigmoid(x)


def _nm_body(x_ref, g_ref, w_ref, o_ref, h_sc):
    @pl.when(pl.program_id(1) == 0)
    def _():
        h_sc[...] = _rms(x_ref[...], g_ref[...]).astype(BF16)

    o_ref[...] = _dot(h_sc[...], w_ref[...])


def norm_matmul(x, g, w, tm, tn):
    m, k = x.shape
    n = w.shape[1]
    return pl.pallas_call(
        _nm_body,
        grid=(m // tm, n // tn),
        in_specs=[pl.BlockSpec((tm, k), lambda i, j: (i, 0)),
                  pl.BlockSpec((1, k), lambda i, j: (0, 0)),
                  pl.BlockSpec((k, tn), lambda i, j: (0, j))],
        out_specs=pl.BlockSpec((tm, tn), lambda i, j: (i, j)),
        out_shape=jax.ShapeDtypeStruct((m, n), F32),
        scratch_shapes=[pltpu.VMEM((tm, k), BF16)],
        compiler_params=_cparams(("parallel", "arbitrary")),
    )(x, g, w)


def _mmres_body(*refs, n):
    res_ref, a_refs, w_refs, o_ref = refs[0], refs[1:1 + n], refs[1 + n:1 + 2 * n], refs[1 + 2 * n]
    acc = res_ref[...]
    for a_ref, w_ref in zip(a_refs, w_refs):
        acc = acc + _dot(a_ref[...].astype(BF16), w_ref[...])
    o_ref[...] = acc


def matmul_residual(res, acts, ws, tm):
    m, d = res.shape
    n = len(acts)
    in_specs = [pl.BlockSpec((tm, d), lambda i: (i, 0))]
    in_specs += [pl.BlockSpec((tm, a.shape[1]), lambda i: (i, 0)) for a in acts]
    in_specs += [pl.BlockSpec(w.shape, lambda i: (0, 0)) for w in ws]
    return pl.pallas_call(
        functools.partial(_mmres_body, n=n),
        grid=(m // tm,),
        in_specs=in_specs,
        out_specs=pl.BlockSpec((tm, d), lambda i: (i, 0)),
        out_shape=jax.ShapeDtypeStruct((m, d), F32),
        compiler_params=_cparams(("parallel",)),
    )(res, *acts, *ws)


def _ffn_body(x_ref, g_ref, wg_ref, wu_ref, wd_ref, o_ref, h_sc, acc_sc):
    j = pl.program_id(1)

    @pl.when(j == 0)
    def _():
        h_sc[...] = _rms(x_ref[...], g_ref[...]).astype(BF16)
        acc_sc[...] = jnp.zeros_like(acc_sc)

    h = h_sc[...]
    act = _silu(_dot(h, wg_ref[...])) * _dot(h, wu_ref[...])
    acc_sc[...] += _dot(act.astype(BF16), wd_ref[...])

    @pl.when(j == pl.num_programs(1) - 1)
    def _():
        o_ref[...] = x_ref[...] + acc_sc[...]


def ffn_swiglu(x, g, wg, wu, wd, tm, tf):
    m, d = x.shape
    f = wg.shape[1]
    return pl.pallas_call(
        _ffn_body,
        grid=(m // tm, f // tf),
        in_specs=[pl.BlockSpec((tm, d), lambda i, j: (i, 0)),
                  pl.BlockSpec((1, d), lambda i, j: (0, 0)),
                  pl.BlockSpec((d, tf), lambda i, j: (0, j)),
                  pl.BlockSpec((d, tf), lambda i, j: (0, j)),
                  pl.BlockSpec((tf, d), lambda i, j: (j, 0))],
        out_specs=pl.BlockSpec((tm, d), lambda i, j: (i, 0)),
        out_shape=jax.ShapeDtypeStruct((m, d), F32),
        scratch_shapes=[pltpu.VMEM((tm, d), BF16), pltpu.VMEM((tm, d), F32)],
        compiler_params=_cparams(("parallel", "arbitrary")),
    )(x, g, wg, wu, wd)


def _moe_body(x_ref, g_ref, r_ref, wg_ref, wu_ref, wd_ref, gf_ref, o_ref, h_sc, comb_sc, acc_sc, *, final_norm):
    e = pl.program_id(1)
    tm = x_ref.shape[0]
    lane = lax.broadcasted_iota(jnp.int32, (tm, 128), 1)

    @pl.when(e == 0)
    def _():
        hf = _rms(x_ref[...], g_ref[...])
        h_sc[...] = hf.astype(BF16)
        lanef = lane.astype(F32)
        logits = jnp.where(lane < N_EXPERTS, _dot(hf, r_ref[...], HI), NEG_INF)
        m1 = jnp.max(logits, axis=1, keepdims=True)
        i1 = jnp.min(jnp.where(logits == m1, lanef, 128.0), axis=1, keepdims=True)
        mask1 = lanef == i1
        rest = jnp.where(mask1, NEG_INF, logits)
        m2 = jnp.max(rest, axis=1, keepdims=True)
        i2 = jnp.min(jnp.where(rest == m2, lanef, 128.0), axis=1, keepdims=True)
        mask2 = lanef == i2
        e2 = jnp.exp(m2 - m1)
        den = 1.0 + e2
        comb_sc[...] = jnp.where(mask1, 1.0 / den, 0.0) + jnp.where(mask2, e2 / den, 0.0)
        acc_sc[...] = jnp.zeros_like(acc_sc)

    h = h_sc[...]
    act = _silu(_dot(h, wg_ref[0])) * _dot(h, wu_ref[0])
    c = jnp.sum(jnp.where(lane == e, comb_sc[...], 0.0), axis=1, keepdims=True)
    acc_sc[...] += c * _dot(act.astype(BF16), wd_ref[0])

    @pl.when(e == pl.num_programs(1) - 1)
    def _():
        y = x_ref[...] + acc_sc[...]
        if final_norm:
            y = _rms(y, gf_ref[...])
        o_ref[...] = y


def moe_swiglu(x, g, router, wg, wu, wd, gfinal, tm, final_norm):
    m, d = x.shape
    ne, _, fe = wg.shape
    return pl.pallas_call(
        functools.partial(_moe_body, final_norm=final_norm),
        grid=(m // tm, ne),
        in_specs=[pl.BlockSpec((tm, d), lambda i, e: (i, 0)),
                  pl.BlockSpec((1, d), lambda i, e: (0, 0)),
                  pl.BlockSpec((d, 128), lambda i, e: (0, 0)),
                  pl.BlockSpec((1, d, fe), lambda i, e: (e, 0, 0)),
                  pl.BlockSpec((1, d, fe), lambda i, e: (e, 0, 0)),
                  pl.BlockSpec((1, fe, d), lambda i, e: (e, 0, 0)),
                  pl.BlockSpec((1, d), lambda i, e: (0, 0))],
        out_specs=pl.BlockSpec((tm, d), lambda i, e: (i, 0)),
        out_shape=jax.ShapeDtypeStruct((m, d), F32),
        scratch_shapes=[pltpu.VMEM((tm, d), BF16), pltpu.VMEM((tm, 128), F32), pltpu.VMEM((tm, d), F32)],
        compiler_params=_cparams(("parallel", "arbitrary")),
    )(x, g, router, wg, wu, wd, gfinal)


def _rwkv_body(p_ref, sh0_ref, s0_ref, mu_ref, w0_ref, a0_ref, kk_ref, ka_ref, rk_ref, lnw_ref, lnb_ref,
               wl_ref, g2_ref, bones_ref, y_ref, sfin_ref,
               s_sc, prev_sc, kk_sc, w_sc, b_sc, k_sc, r_sc, v_sc, g_sc, y_sc, *, tc, t_valid):
    c = pl.program_id(1)

    @pl.when(c == 0)
    def _():
        s_sc[...] = s0_ref[0]
        prev_sc[...] = sh0_ref[0]

    p = p_ref[0]
    row = lax.broadcasted_iota(jnp.int32, p.shape, 0)
    prev = jnp.where(row == 0, prev_sc[...], pltpu.roll(p, 1, axis=0))
    prev_sc[...] = p[tc - 1:tc, :]
    u = p + (prev - p) * mu_ref[...]
    r = u[:, 0:RW_DIM]
    k = u[:, RW_DIM:2 * RW_DIM]
    v = u[:, 2 * RW_DIM:3 * RW_DIM]
    lo = u[:, 3 * RW_DIM:3 * RW_DIM + RW_LORA]
    gd = u[:, 3 * RW_DIM + RW_LORA:]
    lane = lax.broadcasted_iota(jnp.int32, lo.shape, 1)
    lora = _dot(jnp.where(lane < RW_LORA // 2, jnp.tanh(lo), lo), wl_ref[...], HI)
    wlog = -_softplus(-(w0_ref[...] + lora[:, :RW_DIM])) - 0.5
    dec = jnp.exp(-jnp.exp(wlog))
    a = _sigmoid(a0_ref[...] + lora[:, RW_DIM:])
    g_sc[...] = _dot(_sigmoid(gd), g2_ref[...], HI)
    bones = bones_ref[...]
    kk = k * kk_ref[...]
    kk = kk / jnp.maximum(jnp.sqrt(_dot(kk * kk, bones, HI)), 1e-12)
    k2 = k * (1.0 + (a - 1.0) * ka_ref[...])
    for sc, arr in ((kk_sc, kk), (w_sc, dec), (b_sc, kk * a), (k_sc, k2), (r_sc, r), (v_sc, v)):
        sc[0] = arr
        sc[1] = pltpu.roll(arr, RW_DIM - RW_HEAD, axis=1)
    if t_valid < tc:
        y_sc[...] = jnp.zeros_like(y_sc)

    eye = (lax.broadcasted_iota(jnp.int32, (RW_HEAD, RW_HEAD), 0)
           == lax.broadcasted_iota(jnp.int32, (RW_HEAD, RW_HEAD), 1))

    row8 = lax.broadcasted_iota(jnp.int32, (8, RW_HEAD), 0)
    steps = 8 if t_valid % 8 == 0 else t_valid

    def group(t8, carry):
        t0 = pl.multiple_of(t8 * 8, 8)
        for h in range(RW_HEADS):
            plane, base = h % 2, (h // 2) * 128
            kk8, w8, b8, k8, r8, v8 = (sc[plane, pl.ds(t0, 8), base:base + RW_HEAD]
                                       for sc in (kk_sc, w_sc, b_sc, k_sc, r_sc, v_sc))
            s = s_sc[h]
            y8 = jnp.zeros((8, RW_HEAD), F32)
            for j in range(steps):
                sk = jnp.sum(s * kk8[j:j + 1], axis=1, keepdims=True)
                vcol = jnp.sum(jnp.where(eye, v8[j:j + 1], 0.0), axis=1, keepdims=True)
                s = s * w8[j:j + 1] - sk * b8[j:j + 1] + vcol * k8[j:j + 1]
                ycol = jnp.sum(s * r8[j:j + 1], axis=1, keepdims=True)
                yrow = jnp.sum(jnp.where(eye, ycol, 0.0), axis=0, keepdims=True)
                y8 = jnp.where(row8 == j, yrow, y8)
            s_sc[h] = s
            y_sc[h, pl.ds(t0, 8), :] = y8
        return carry

    lax.fori_loop(0, -(-t_valid // 8), group, 0)

    y = jnp.concatenate([y_sc[h] for h in range(RW_HEADS)], axis=1)
    inv = 1.0 / RW_HEAD
    yc = y - _dot(y, bones, HI) * inv
    var = _dot(yc * yc, bones, HI) * inv
    yn = yc * lax.rsqrt(var + RW_LN_EPS) * lnw_ref[...] + lnb_ref[...]
    bonus = _dot(r_sc[0] * k_sc[0] * rk_ref[...], bones, HI) * v_sc[0]
    y_ref[0] = (yn + bonus) * g_sc[...]

    @pl.when(c == pl.num_programs(1) - 1)
    def _():
        sfin_ref[0] = s_sc[...]


def rwkv7(proj, shift0, wkv0, prm, tc, t_valid):
    b, t, _ = proj.shape
    vec = lambda n: pl.BlockSpec((1, n), lambda i, c: (0, 0))
    full = lambda a: pl.BlockSpec(a.shape, lambda i, c: (0,) * a.ndim)
    plane = pltpu.VMEM((2, tc, RW_DIM), F32)
    return pl.pallas_call(
        functools.partial(_rwkv_body, tc=tc, t_valid=t_valid),
        grid=(b, t // tc),
        in_specs=[pl.BlockSpec((1, tc, RW_COLS), lambda i, c: (i, c, 0)),
                  pl.BlockSpec((1, 1, RW_COLS), lambda i, c: (i, 0, 0)),
                  pl.BlockSpec((1, RW_HEADS, RW_HEAD, RW_HEAD), lambda i, c: (i, 0, 0, 0)),
                  vec(RW_COLS), vec(RW_DIM), vec(RW_DIM), vec(RW_DIM), vec(RW_DIM), vec(RW_DIM), vec(RW_DIM),
                  vec(RW_DIM), full(prm["wl"]), full(prm["g2"]), full(prm["bones"])],
        out_specs=[pl.BlockSpec((1, tc, RW_DIM), lambda i, c: (i, c, 0)),
                   pl.BlockSpec((1, RW_HEADS, RW_HEAD, RW_HEAD), lambda i, c: (i, 0, 0, 0))],
        out_shape=[jax.ShapeDtypeStruct((b, t, RW_DIM), F32),
                   jax.ShapeDtypeStruct((b, RW_HEADS, RW_HEAD, RW_HEAD), F32)],
        scratch_shapes=[pltpu.VMEM((RW_HEADS, RW_HEAD, RW_HEAD), F32), pltpu.VMEM((1, RW_COLS), F32),
                        plane, plane, plane, plane, plane, plane,
                        pltpu.VMEM((tc, RW_DIM), F32), pltpu.VMEM((RW_HEADS, tc, RW_HEAD), F32)],
        compiler_params=_cparams(("parallel", "arbitrary")),
    )(proj, shift0, wkv0, prm["mu"], prm["w0"], prm["a0"], prm["k_k"], prm["k_a"], prm["r_k"],
      prm["ln_w"], prm["ln_b"], prm["wl"], prm["g2"], prm["bones"])


def _ssd_body(z_ref, xbc_ref, dt_ref, conv0_ref, h0_ref, cw_ref, cb_ref, dtb_ref, alog_ref, dsk_ref, nw_ref,
              e_ref, et_ref, y_ref, hfin_ref, h_sc, ext_sc, *, t_valid):
    c = pl.program_id(1)
    ln = SSD_CHUNK

    @pl.when(c == 0)
    def _():
        h_sc[...] = h0_ref[0]
        ext_sc[0:8, :] = conv0_ref[0]

    xbc = xbc_ref[0]
    ext_sc[8:8 + ln, :] = xbc
    cw = cw_ref[...]
    conv = cb_ref[...] + ext_sc[5:5 + ln, :] * cw[0:1]
    conv = conv + ext_sc[6:6 + ln, :] * cw[1:2]
    conv = conv + ext_sc[7:7 + ln, :] * cw[2:3]
    conv = conv + xbc * cw[3:4]
    ext_sc[0:8, :] = xbc[ln - 8:ln, :]
    act = _silu(conv)
    xs = act[:, :SSM_INNER]
    bm = act[:, SSM_INNER:SSM_INNER + SSM_GROUPS * SSM_STATE].astype(BF16)
    cm = act[:, SSM_INNER + SSM_GROUPS * SSM_STATE:].astype(BF16)

    lane = lax.broadcasted_iota(jnp.int32, (ln, 128), 1)
    rowi = lax.broadcasted_iota(jnp.int32, (ln, 128), 0)
    dt = _softplus(dt_ref[0] + dtb_ref[...])
    if t_valid < ln:
        dt = jnp.where(rowi < t_valid, dt, 0.0)
    a = dt * jnp.where(lane[0:1] < SSM_HEADS, -jnp.exp(alog_ref[...]), 0.0)
    causal = lane <= rowi
    tri = causal.astype(F32)
    eye = (lane == rowi).astype(F32)
    acum = _dot(tri, a, HI)
    acum_t = _nt(_nt(eye, a, HI), tri, HI)
    e = e_ref[...]
    dt_full = _dot(dt, e, HI)
    ac_full = _dot(acum, e, HI)
    xdt = xs * dt_full
    xw = xdt * jnp.exp(ac_full[ln - 1:ln, :] - ac_full)
    eac = jnp.exp(ac_full)
    cd = jnp.broadcast_to(jnp.exp(acum_t[:, ln - 1:ln]), (128, 128))
    cd_rows = _dot(et_ref[...], cd, HI)

    ys = []
    for g in range(SSM_GROUPS):
        bm_g = bm[:, g * SSM_STATE:(g + 1) * SSM_STATE]
        cm_g = cm[:, g * SSM_STATE:(g + 1) * SSM_STATE]
        cbm = _nt(cm_g, bm_g)
        pairs = SSM_HEADS // SSM_GROUPS // 2
        for q in range(g * pairs, (g + 1) * pairs):
            sl = slice(q * 128, (q + 1) * 128)
            xdt_p = xdt[:, sl].astype(BF16)
            yd = []
            for h in (2 * q, 2 * q + 1):
                seg = jnp.where(causal, acum[:, h:h + 1] - acum_t[h:h + 1, :], NEG_INF)
                yd.append(_dot((cbm * jnp.exp(seg)).astype(BF16), xdt_p))
            hp = h_sc[sl, :]
            y_off = _nt(cm_g, hp.astype(BF16)) * eac[:, sl]
            st = _dot(xw[:, sl].T.astype(BF16), bm_g)
            h_sc[sl, :] = cd_rows[sl, :] * hp + st
            ys.append(jnp.where(lane < SSM_HEAD, yd[0], yd[1]) + y_off)
    y = jnp.concatenate(ys, axis=1) + dsk_ref[...] * xs
    y = y * _silu(z_ref[0])
    gw = SSM_INNER // SSM_GROUPS
    outs = []
    for g in range(SSM_GROUPS):
        yg = y[:, g * gw:(g + 1) * gw]
        outs.append(yg * lax.rsqrt(jnp.mean(yg * yg, axis=1, keepdims=True) + NORM_EPS))
    y_ref[0] = jnp.concatenate(outs, axis=1) * nw_ref[...]

    @pl.when(c == pl.num_programs(1) - 1)
    def _():
        hfin_ref[0] = h_sc[...]


def mamba2(proj, conv0, ssm0, prm, t_valid):
    b, t, _ = proj.shape
    ln = SSD_CHUNK
    vec = lambda n: pl.BlockSpec((1, n), lambda i, c: (0, 0))
    full = lambda a: pl.BlockSpec(a.shape, lambda i, c: (0,) * a.ndim)
    return pl.pallas_call(
        functools.partial(_ssd_body, t_valid=t_valid),
        grid=(b, t // ln),
        in_specs=[pl.BlockSpec((1, ln, SSM_INNER), lambda i, c: (i, c, PROJ_Z // SSM_INNER)),
                  pl.BlockSpec((1, ln, SSM_CONV_DIM), lambda i, c: (i, c, PROJ_XBC // SSM_CONV_DIM)),
                  pl.BlockSpec((1, ln, 128), lambda i, c: (i, c, PROJ_DT // 128)),
                  pl.BlockSpec((1, 8, SSM_CONV_DIM), lambda i, c: (i, 0, 0)),
                  pl.BlockSpec((1, SSM_INNER, SSM_STATE), lambda i, c: (i, 0, 0)),
                  full(prm["conv_w"]), vec(SSM_CONV_DIM), vec(128), vec(128), vec(SSM_INNER), vec(SSM_INNER),
                  full(prm["expand"]), full(prm["expand_t"])],
        out_specs=[pl.BlockSpec((1, ln, SSM_INNER), lambda i, c: (i, c, 0)),
                   pl.BlockSpec((1, SSM_INNER, SSM_STATE), lambda i, c: (i, 0, 0))],
        out_shape=[jax.ShapeDtypeStruct((b, t, SSM_INNER), F32),
                   jax.ShapeDtypeStruct((b, SSM_INNER, SSM_STATE), F32)],
        scratch_shapes=[pltpu.VMEM((SSM_INNER, SSM_STATE), F32), pltpu.VMEM((8 + ln, SSM_CONV_DIM), F32)],
        compiler_params=_cparams(("parallel", "arbitrary")),
    )(proj, proj, proj, conv0, ssm0, prm["conv_w"], prm["conv_b"], prm["dt_bias"], prm["a_log"],
      prm["d_full"], prm["norm_w"], prm["expand"], prm["expand_t"])


def _alibi_slopes():
    return (2.0 ** (-8.0 * np.arange(1, ATT_HEADS + 1) / ATT_HEADS)).astype(np.float32)


def _slope_features():
    s = _alibi_slopes()
    bf = lambda a: a.astype(jnp.bfloat16).astype(np.float32)
    hi = bf(s)
    mid = bf(s - hi)
    lo = bf(s - hi - mid)
    out = np.zeros((ATT_HEADS, 32), np.float32)
    for i, piece in enumerate((hi, mid, lo)):
        out[:, i] = piece
        out[:, 3 + i] = piece
    return out


def _top3_threshold(gate):
    v = gate
    for _ in range(MOBA_TOPK - 1):
        v = jnp.where(v == jnp.max(v, axis=1, keepdims=True), NEG_INF, v)
    return jnp.max(v, axis=1, keepdims=True)


def _blockmean_body(k_ref, o_ref):
    o_ref[0] = jnp.mean(k_ref[...], axis=0, keepdims=True)


def block_means(qkv, nblocks):
    return pl.pallas_call(
        _blockmean_body,
        grid=(nblocks,),
        in_specs=[pl.BlockSpec((MOBA_BLOCK, ATT_KV_COLS), lambda i: (i, ATT_Q_COLS // ATT_KV_COLS))],
        out_specs=pl.BlockSpec((1, 1, ATT_KV_COLS), lambda i: (i, 0, 0)),
        out_shape=jax.ShapeDtypeStruct((nblocks, 1, ATT_KV_COLS), F32),
        compiler_params=_cparams(("parallel",)),
    )(qkv)


def _moba_prompt_body(q_ref, k_ref, v_ref, mean_ref, slf_ref, o_ref, m_sc, acc_sc, *, nb):
    qb = pl.program_id(1)
    blk = MOBA_BLOCK
    rows = ATT_GROUP * blk
    scale = ATT_HEAD ** -0.5
    nbl = mean_ref.shape[1]
    blane = lax.broadcasted_iota(jnp.int32, (rows, nbl), 1)

    krow = lax.broadcasted_iota(jnp.int32, (blk, ATT_HEAD), 0)
    klane = lax.broadcasted_iota(jnp.int32, (blk, ATT_HEAD), 1)
    off_feat = jnp.where((klane >= 35) & (klane < 38), krow.astype(F32), 0.0)
    ones_feat = jnp.where(klane == 0, 1.0, 0.0).astype(BF16)
    qi = lax.broadcasted_iota(jnp.int32, (rows, blk), 0) % blk
    ki = lax.broadcasted_iota(jnp.int32, (rows, blk), 1)
    causal = ki <= qi

    def kv_tiles(j, g, feat):
        start = pl.multiple_of(j * blk, blk)
        kj = k_ref[0, pl.ds(start, blk), g * ATT_HEAD:(g + 1) * ATT_HEAD]
        vj = v_ref[0, pl.ds(start, blk), g * ATT_HEAD:(g + 1) * ATT_HEAD]
        return jnp.concatenate([kj, feat], axis=1), jnp.concatenate([vj, ones_feat], axis=1)

    for g in range(ATT_KV_HEADS):
        qg = jnp.concatenate([q_ref[0, :, (ATT_GROUP * g + hh) * ATT_HEAD:(ATT_GROUP * g + hh + 1) * ATT_HEAD]
                              for hh in range(ATT_GROUP)], axis=0)
        gate = _nt(qg, mean_ref[0, :, g * ATT_HEAD:(g + 1) * ATT_HEAD], HI)
        gate = jnp.where(blane < qb, gate, NEG_INF)
        sel = gate >= _top3_threshold(gate)
        selneg = jnp.where(sel & (blane < qb), 0.0, MASK_NEG)
        slf = jnp.concatenate([jnp.broadcast_to(slf_ref[ATT_GROUP * g + hh:ATT_GROUP * g + hh + 1, :], (blk, 32))
                               for hh in range(ATT_GROUP)], axis=0)
        pieces = [(qg * scale).astype(BF16), selneg.astype(BF16)]
        if nbl < 32:
            pieces.append(jnp.zeros((rows, 32 - nbl), BF16))
        pieces.append(slf.astype(BF16))
        qf = jnp.concatenate(pieces, axis=1)

        kf, vf = kv_tiles(qb, g, off_feat.astype(BF16))
        s = jnp.where(causal, _nt(qf, kf), NEG_INF)
        m = jnp.max(s, axis=1, keepdims=True)
        m_sc[...] = jnp.broadcast_to(m, (rows, 128))
        acc_sc[...] = _dot(jnp.exp(s - m).astype(BF16), vf)

        def past(j, carry):
            feat = jnp.where(klane == j, 1.0, 0.0)
            feat = feat + jnp.where((klane >= 32) & (klane < 35), ((j - qb) * blk).astype(F32), 0.0) + off_feat
            kf, vf = kv_tiles(j, g, feat.astype(BF16))
            s = _nt(qf, kf)
            m_old = m_sc[:, 0:1]
            m_new = jnp.maximum(m_old, jnp.max(s, axis=1, keepdims=True))
            acc_sc[...] = acc_sc[...] * jnp.exp(m_old - m_new) + _dot(jnp.exp(s - m_new).astype(BF16), vf)
            m_sc[...] = jnp.broadcast_to(m_new, (rows, 128))
            return carry

        lax.fori_loop(0, qb, past, 0)
        acc = acc_sc[...]
        out = acc[:, :ATT_HEAD] / acc[:, ATT_HEAD:ATT_HEAD + 1]
        o_ref[0, :, g * ATT_GROUP * ATT_HEAD:(g + 1) * ATT_GROUP * ATT_HEAD] = jnp.concatenate(
            [out[hh * blk:(hh + 1) * blk] for hh in range(ATT_GROUP)], axis=1)


def moba_prompt(qkv, kv_bf, means, slf):
    b, t, _ = qkv.shape
    nb = t // MOBA_BLOCK
    nbl = means.shape[1]
    rows = ATT_GROUP * MOBA_BLOCK
    return pl.pallas_call(
        functools.partial(_moba_prompt_body, nb=nb),
        grid=(b, nb),
        in_specs=[pl.BlockSpec((1, MOBA_BLOCK, ATT_Q_COLS), lambda i, j: (i, j, 0)),
                  pl.BlockSpec((1, t, ATT_KV_COLS), lambda i, j: (i, 0, 0)),
                  pl.BlockSpec((1, t, ATT_KV_COLS), lambda i, j: (i, 0, 1)),
                  pl.BlockSpec((1, nbl, ATT_KV_COLS), lambda i, j: (i, 0, 0)),
                  pl.BlockSpec((ATT_HEADS, 32), lambda i, j: (0, 0))],
        out_specs=pl.BlockSpec((1, MOBA_BLOCK, ATT_Q_COLS), lambda i, j: (i, j, 0)),
        out_shape=jax.ShapeDtypeStruct((b, t, ATT_Q_COLS), F32),
        scratch_shapes=[pltpu.VMEM((rows, 128), F32), pltpu.VMEM((rows, 128), F32)],
        compiler_params=_cparams(("parallel", "arbitrary")),
    )(qkv, kv_bf, kv_bf, means, slf)


SAMPLE_BLOCKS_PER_STEP = 2


def _moba_sample_body(pt_ref, *refs, nbp, bps, nq):
    npg = 2 * bps
    k_refs, v_refs = refs[:npg], refs[npg:2 * npg]
    q_ref, kn_ref, vn_ref, slope_ref, o_ref, mean_sc, m_sc, l_sc, acc_sc = refs[2 * npg:]
    j = pl.program_id(1)
    blk = MOBA_BLOCK
    rows = ATT_GROUP * nq
    past = nbp * blk
    scale = ATT_HEAD ** -0.5
    lane = lax.broadcasted_iota(jnp.int32, (rows, 128), 1)
    qi = (lax.broadcasted_iota(jnp.int32, (rows, 1), 0) % nq).astype(F32)

    @pl.when(j == 0)
    def _():
        mean_sc[...] = jnp.zeros_like(mean_sc)
        m_sc[...] = jnp.full(m_sc.shape, NEG_INF, F32)
        l_sc[...] = jnp.zeros_like(l_sc)

    for s in range(bps):
        bi = j * bps + s
        kblk = jnp.concatenate([k_refs[2 * s][0], k_refs[2 * s + 1][0]], axis=0)
        vblk = jnp.concatenate([v_refs[2 * s][0], v_refs[2 * s + 1][0]], axis=0)
        mean_sc[pl.ds(bi, 1), :] = jnp.sum(kblk, axis=0, keepdims=True) * (1.0 / blk)
        off = lax.broadcasted_iota(jnp.int32, (1, blk), 1).astype(F32)
        base = (past - bi * blk).astype(F32)
        for g in range(ATT_KV_HEADS):
            qg = (q_ref[0, g] * scale).astype(BF16)
            kg = kblk[:, g * ATT_HEAD:(g + 1) * ATT_HEAD].astype(BF16)
            vg = vblk[:, g * ATT_HEAD:(g + 1) * ATT_HEAD].astype(BF16)
            sc = _nt(qg, kg) - slope_ref[g] * (base + qi - off)
            m = jnp.max(sc, axis=1, keepdims=True)
            p = jnp.exp(sc - m)
            m_sc[g] = jnp.where(lane == bi, m, m_sc[g])
            l_sc[g] = jnp.where(lane == bi, jnp.sum(p, axis=1, keepdims=True), l_sc[g])
            acc_sc[bi, g] = _dot(p.astype(BF16), vg)

    @pl.when(j == pl.num_programs(1) - 1)
    def _():
        for g in range(ATT_KV_HEADS):
            qf = q_ref[0, g]
            gate = _nt(qf, mean_sc[:, g * ATT_HEAD:(g + 1) * ATT_HEAD], HI)
            gate = jnp.where(lane < nbp, gate, NEG_INF)
            sel = gate >= _top3_threshold(gate)
            ki = lax.broadcasted_iota(jnp.int32, (rows, 8), 1).astype(F32)
            s_own = _nt((qf * scale).astype(BF16), kn_ref[0, g].astype(BF16)) - slope_ref[g] * (qi - ki)
            s_own = jnp.where(ki <= qi, s_own, NEG_INF)
            m_all = jnp.where(sel, m_sc[g], NEG_INF)
            mx = jnp.maximum(jnp.max(m_all, axis=1, keepdims=True), jnp.max(s_own, axis=1, keepdims=True))
            w = jnp.where(sel, jnp.exp(m_all - mx), 0.0)
            p_own = jnp.exp(s_own - mx)
            den = jnp.sum(w * l_sc[g], axis=1, keepdims=True) + jnp.sum(p_own, axis=1, keepdims=True)
            acc = _dot(p_own.astype(BF16), vn_ref[0, g].astype(BF16))
            for n in range(nbp):
                acc = acc + w[:, n:n + 1] * acc_sc[n, g]
            o_ref[0, g] = acc / den


def moba_sample(page_table, cache_k, cache_v, q_rows, k_new, v_new, slope_rows, nq):
    db, n_pages = page_table.shape
    nbp = n_pages * PAGE_SIZE // MOBA_BLOCK
    bps = SAMPLE_BLOCKS_PER_STEP
    rows = ATT_GROUP * nq

    def page_spec(slot):
        return pl.BlockSpec((1, PAGE_SIZE, ATT_KV_COLS),
                            lambda b, j, pt: (pt[b * n_pages + j * 2 * bps + slot], 0, 0))

    small = lambda shp: pl.BlockSpec((1,) + shp, lambda b, j, pt: (b,) + (0,) * len(shp))
    grid_spec = pltpu.PrefetchScalarGridSpec(
        num_scalar_prefetch=1,
        grid=(db, nbp // bps),
        in_specs=[page_spec(s) for s in range(2 * bps)] * 2
        + [small((ATT_KV_HEADS, rows, ATT_HEAD)), small((ATT_KV_HEADS, 8, ATT_HEAD)),
           small((ATT_KV_HEADS, 8, ATT_HEAD)),
           pl.BlockSpec((ATT_KV_HEADS, rows, 1), lambda b, j, pt: (0, 0, 0))],
        out_specs=small((ATT_KV_HEADS, rows, ATT_HEAD)),
        scratch_shapes=[pltpu.VMEM((128, ATT_KV_COLS), F32), pltpu.VMEM((ATT_KV_HEADS, rows, 128), F32),
                        pltpu.VMEM((ATT_KV_HEADS, rows, 128), F32),
                        pltpu.VMEM((nbp, ATT_KV_HEADS, rows, ATT_HEAD), F32)],
    )
    return pl.pallas_call(
        functools.partial(_moba_sample_body, nbp=nbp, bps=bps, nq=nq),
        grid_spec=grid_spec,
        out_shape=jax.ShapeDtypeStruct((db, ATT_KV_HEADS, rows, ATT_HEAD), F32),
        compiler_params=_cparams(("parallel", "arbitrary")),
    )(page_table.reshape(-1), *([cache_k] * (2 * bps)), *([cache_v] * (2 * bps)), q_rows, k_new, v_new, slope_rows)


def _row(a):
    return a.reshape(1, -1).astype(F32)


def _even_params(i, w_in_ab, rwkv_mu, rwkv_w0, rwkv_w2, rwkv_a0, rwkv_a2, rwkv_g2, rwkv_k_k, rwkv_k_a, rwkv_r_k,
                 rwkv_ln_w, rwkv_ln_b, ssm_conv_w, ssm_conv_b, ssm_dt_bias, ssm_a_log, ssm_d, ssm_norm_w):
    w = w_in_ab[i]
    ssm0 = RW_COLS
    zeros = lambda n: jnp.zeros((D_MODEL, n), w.dtype)
    w_pack = jnp.concatenate([
        w[:, :RW_COLS],
        w[:, ssm0 + SSM_INNER + SSM_CONV_DIM:], zeros(PROJ_Z - PROJ_DT - SSM_HEADS),
        w[:, ssm0:ssm0 + SSM_INNER],
        w[:, ssm0 + SSM_INNER:ssm0 + SSM_INNER + SSM_CONV_DIM]], axis=1).astype(BF16)
    half = RW_LORA // 2
    zl = jnp.zeros((half, RW_DIM), F32)
    wl = jnp.concatenate([jnp.concatenate([rwkv_w2[i], zl], axis=1),
                          jnp.concatenate([zl, rwkv_a2[i]], axis=1)], axis=0)
    head_of = np.arange(RW_DIM) // RW_HEAD
    rw = dict(mu=_row(rwkv_mu[i]), w0=_row(rwkv_w0[i]), a0=_row(rwkv_a0[i]), k_k=_row(rwkv_k_k[i]),
              k_a=_row(rwkv_k_a[i]), r_k=_row(rwkv_r_k[i]), ln_w=_row(rwkv_ln_w[i]), ln_b=_row(rwkv_ln_b[i]),
              wl=wl, g2=rwkv_g2[i], bones=jnp.asarray((head_of[:, None] == head_of[None, :]).astype(np.float32)))
    pad128 = lambda a: jnp.pad(_row(a), ((0, 0), (0, 128 - a.shape[-1])))
    expand = (np.arange(128)[:, None] == (np.arange(SSM_INNER) // SSM_HEAD)[None, :]).astype(np.float32)
    ssm = dict(conv_w=ssm_conv_w[i], conv_b=_row(ssm_conv_b[i]), dt_bias=pad128(ssm_dt_bias[i]),
               a_log=pad128(ssm_a_log[i]), d_full=_row(jnp.repeat(ssm_d[i], SSM_HEAD)), norm_w=_row(ssm_norm_w[i]),
               expand=jnp.asarray(expand), expand_t=jnp.asarray(expand.T))
    return w_pack, rw, ssm


def _mixer_ab(x, b, t, norm_g, w_pack, rw, ssm, w_out, shift0, wkv0, conv0, ssm0, tm):
    proj = norm_matmul(x, norm_g, w_pack, tm, SSM_CONV_DIM).reshape(b, t, PROJ_COLS)
    shift_new = proj[:, t - 1, :RW_COLS]
    conv_new = proj[:, t - (SSM_CONV - 1):, PROJ_XBC:]
    if t % SSD_CHUNK == 0:
        proj_rw, proj_ssm, tc = proj, proj, SSD_CHUNK
    else:
        proj_rw = jnp.pad(proj, ((0, 0), (0, 8 - t), (0, 0)))
        proj_ssm = jnp.pad(proj, ((0, 0), (0, SSD_CHUNK - t), (0, 0)))
        tc = 8
    y_rw, wkv_new = rwkv7(proj_rw, shift0.reshape(b, 1, RW_COLS), wkv0, rw, tc, min(t, tc))
    conv0p = jnp.pad(conv0, ((0, 0), (8 - (SSM_CONV - 1), 0), (0, 0)))
    y_ssm, ssm_new = mamba2(proj_ssm, conv0p, ssm0.reshape(b, SSM_INNER, SSM_STATE), ssm, min(t, SSD_CHUNK))
    x = matmul_residual(x, [y_rw[:, :t].reshape(b * t, RW_DIM), y_ssm[:, :t].reshape(b * t, SSM_INNER)],
                        [w_out[:RW_DIM], w_out[RW_DIM:]], tm)
    return x, (shift_new, wkv_new, conv_new, ssm_new.reshape(b, SSM_HEADS, SSM_HEAD, SSM_STATE))


def kernel(x_prompt, x_sample, state_rwkv_shift, state_rwkv_wkv, state_ssm_conv, state_ssm, cache_k, cache_v, page_table, norm_mix, norm_ffn, norm_final, w_in_ab, rwkv_mu, rwkv_w0, rwkv_w2, rwkv_a0, rwkv_a2, rwkv_g2, rwkv_k_k, rwkv_k_a, rwkv_r_k, rwkv_ln_w, rwkv_ln_b, ssm_conv_w, ssm_conv_b, ssm_dt_bias, ssm_a_log, ssm_d, ssm_norm_w, w_out_ab, ffn_w_gate, ffn_w_up, ffn_w_down, attn_w_qkv, attn_w_o, moe_router, moe_w_gate, moe_w_up, moe_w_down):
    bp, tp, _ = x_prompt.shape
    db, ts, _ = x_sample.shape
    depth = norm_mix.shape[0]
    tm_p, tm_s = 512, db * ts
    xp = x_prompt.reshape(bp * tp, D_MODEL)
    xs = x_sample.reshape(db * ts, D_MODEL)
    slopes = _alibi_slopes()
    slf = jnp.asarray(_slope_features())
    slope_rows = jnp.asarray(np.repeat(slopes, ts).reshape(ATT_KV_HEADS, ATT_GROUP * ts, 1))
    st = {n: [] for n in ("p_shift", "p_wkv", "p_conv", "p_ssm", "p_k", "p_v",
                          "s_shift", "s_wkv", "s_conv", "s_ssm", "s_k", "s_v")}
    for l in range(depth):
        i = l // 2
        g_mix, g_ffn = _row(norm_mix[l]), _row(norm_ffn[l])
        if l % 2 == 0:
            w_pack, rw, ssm = _even_params(i, w_in_ab, rwkv_mu, rwkv_w0, rwkv_w2, rwkv_a0, rwkv_a2, rwkv_g2,
                                           rwkv_k_k, rwkv_k_a, rwkv_r_k, rwkv_ln_w, rwkv_ln_b, ssm_conv_w,
                                           ssm_conv_b, ssm_dt_bias, ssm_a_log, ssm_d, ssm_norm_w)
            w_out = w_out_ab[i].astype(BF16)
            xp, sp = _mixer_ab(xp, bp, tp, g_mix, w_pack, rw, ssm, w_out,
                               jnp.zeros((bp, RW_COLS), F32), jnp.zeros((bp, RW_HEADS, RW_HEAD, RW_HEAD), F32),
                               jnp.zeros((bp, SSM_CONV - 1, SSM_CONV_DIM), F32),
                               jnp.zeros((bp, SSM_HEADS, SSM_HEAD, SSM_STATE), F32), tm_p)
            xs, ss = _mixer_ab(xs, db, ts, g_mix, w_pack, rw, ssm, w_out, state_rwkv_shift[i], state_rwkv_wkv[i],
                               state_ssm_conv[i], state_ssm[i], tm_s)
            for pre, new in (("p", sp), ("s", ss)):
                for name, val in zip(("shift", "wkv", "conv", "ssm"), new):
                    st[f"{pre}_{name}"].append(val)
            wg, wu, wd = ffn_w_gate[i].astype(BF16), ffn_w_up[i].astype(BF16), ffn_w_down[i].astype(BF16)
            xp = ffn_swiglu(xp, g_ffn, wg, wu, wd, tm_p, 256)
            xs = ffn_swiglu(xs, g_ffn, wg, wu, wd, tm_s, 256)
        else:
            w_qkv, w_o = attn_w_qkv[i].astype(BF16), attn_w_o[i].astype(BF16)
            qkv_p = norm_matmul(xp, g_mix, w_qkv, tm_p, 512)
            nb = tp // MOBA_BLOCK
            means = block_means(qkv_p, bp * nb).reshape(bp, nb, ATT_KV_COLS)
            kv_bf = qkv_p[:, ATT_Q_COLS:].astype(BF16).reshape(bp, tp, 2 * ATT_KV_COLS)
            o_p = moba_prompt(qkv_p.reshape(bp, tp, -1), kv_bf, means, slf)
            xp = matmul_residual(xp, [o_p.reshape(bp * tp, ATT_Q_COLS)], [w_o], tm_p)
            st["p_k"].append(qkv_p[:, ATT_Q_COLS:ATT_Q_COLS + ATT_KV_COLS].reshape(bp, tp, ATT_KV_HEADS, ATT_HEAD))
            st["p_v"].append(qkv_p[:, ATT_Q_COLS + ATT_KV_COLS:].reshape(bp, tp, ATT_KV_HEADS, ATT_HEAD))

            qkv_s = norm_matmul(xs, g_mix, w_qkv, tm_s, 512).reshape(db, ts, -1)
            kn = qkv_s[..., ATT_Q_COLS:ATT_Q_COLS + ATT_KV_COLS].reshape(db, ts, ATT_KV_HEADS, ATT_HEAD)
            vn = qkv_s[..., ATT_Q_COLS + ATT_KV_COLS:].reshape(db, ts, ATT_KV_HEADS, ATT_HEAD)
            q_rows = qkv_s[..., :ATT_Q_COLS].reshape(db, ts, ATT_HEADS, ATT_HEAD).transpose(0, 2, 1, 3).reshape(
                db, ATT_KV_HEADS, ATT_GROUP * ts, ATT_HEAD)
            new_rows = lambda a: jnp.pad(a.transpose(0, 2, 1, 3), ((0, 0), (0, 0), (0, 8 - ts), (0, 0)))
            n_pool = cache_k.shape[1]
            o_s = moba_sample(page_table, cache_k[i].reshape(n_pool, PAGE_SIZE, ATT_KV_COLS),
                              cache_v[i].reshape(n_pool, PAGE_SIZE, ATT_KV_COLS), q_rows, new_rows(kn), new_rows(vn),
                              slope_rows, ts)
            o_s = o_s.reshape(db, ATT_HEADS, ts, ATT_HEAD).transpose(0, 2, 1, 3).reshape(db * ts, ATT_Q_COLS)
            xs = matmul_residual(xs, [o_s], [w_o], tm_s)
            st["s_k"].append(kn)
            st["s_v"].append(vn)

            router = jnp.pad(moe_router[i], ((0, 0), (0, 128 - N_EXPERTS)))
            wg, wu, wd = moe_w_gate[i].astype(BF16), moe_w_up[i].astype(BF16), moe_w_down[i].astype(BF16)
            final = l == depth - 1
            xp = moe_swiglu(xp, g_ffn, router, wg, wu, wd, _row(norm_final), tm_p, final)
            xs = moe_swiglu(xs, g_ffn, router, wg, wu, wd, _row(norm_final), tm_s, final)
    y_prompt = xp.reshape(bp, tp, D_MODEL)
    y_sample = xs.reshape(db, ts, D_MODEL)
    stack = lambda n: jnp.stack(st[n])
    return (y_prompt, y_sample,
            stack("p_shift"), stack("p_wkv"), stack("p_conv"), stack("p_ssm"), stack("p_k"), stack("p_v"),
            stack("s_shift"), stack("s_wkv"), stack("s_conv"), stack("s_ssm"), stack("s_k"), stack("s_v"))
```

```python
import functools

import numpy as np
import jax
import jax.numpy as jnp
from jax import lax
from jax.experimental import pallas as pl
from jax.experimental.pallas import tpu as pltpu

F32 = jnp.float32
BF16 = jnp.bfloat16
HI = lax.Precision.HIGHEST
NEG_INF = float("-inf")

D_MODEL = 1024
NORM_EPS = 1e-6

RW_HEAD = 64
RW_HEADS = 8
RW_DIM = RW_HEADS * RW_HEAD
RW_LORA = 128
RW_GATE = 128
RW_COLS = 3 * RW_DIM + RW_LORA + RW_GATE
RW_LN_EPS = 64e-5

SSM_INNER = 1024
SSM_HEAD = 64
SSM_HEADS = 16
SSM_GROUPS = 2
SSM_STATE = 128
SSM_CONV = 4
SSM_CONV_DIM = SSM_INNER + 2 * SSM_GROUPS * SSM_STATE
SSD_CHUNK = 128

PROJ_RW = 0
PROJ_DT = RW_COLS
PROJ_Z = 2048
PROJ_XBC = 3072
PROJ_COLS = PROJ_XBC + SSM_CONV_DIM

ATT_HEADS = 16
ATT_KV_HEADS = 4
ATT_HEAD = 64
ATT_GROUP = ATT_HEADS // ATT_KV_HEADS
ATT_Q_COLS = ATT_HEADS * ATT_HEAD
ATT_KV_COLS = ATT_KV_HEADS * ATT_HEAD
MOBA_BLOCK = 256
MOBA_TOPK = 3
MASK_NEG = -1e30
PAGE_SIZE = 128

N_EXPERTS = 8

VMEM_LIMIT = 56 * 1024 * 1024


def _cparams(sem):
    return pltpu.CompilerParams(dimension_semantics=sem, vmem_limit_bytes=VMEM_LIMIT)


def _nt(a, b, precision=None):
    return lax.dot_general(a, b, (((1,), (1,)), ((), ())), precision=precision,
                           preferred_element_type=F32)


def _dot(a, b, precision=None):
    return jnp.dot(a, b, precision=precision, preferred_element_type=F32)


def _split2(x):
    hi = x.astype(BF16)
    return hi, (x - hi.astype(F32)).astype(BF16)


def _cat3(x, axis):
    hi, lo = _split2(x)
    return jnp.concatenate([hi, lo, hi], axis=axis)


def _cat3w(w, axis):
    hi, lo = _split2(w)
    return jnp.concatenate([hi, hi, lo], axis=axis)


def _rms(x, g):
    return x * lax.rsqrt(jnp.mean(x * x, axis=-1, keepdims=True) + NORM_EPS) * g


def _softplus(x):
    return jnp.maximum(x, 0.0) + jnp.log(1.0 + jnp.exp(-jnp.abs(x)))


def _sigmoid(x):
    return 1.0 / (1.0 + jnp.exp(-x))


def _silu(x):
    return x * _sigmoid(x)


def _nm_body(x_ref, g_ref, w_ref, o_ref, h_sc):
    @pl.when(pl.program_id(1) == 0)
    def _():
        h_sc[...] = _rms(x_ref[...], g_ref[...]).astype(BF16)

    o_ref[...] = _dot(h_sc[...], w_ref[...])


def norm_matmul(x, g, w, tm, tn):
    m, k = x.shape
    n = w.shape[1]
    return pl.pallas_call(
        _nm_body,
        grid=(m // tm, n // tn),
        in_specs=[pl.BlockSpec((tm, k), lambda i, j: (i, 0)),
                  pl.BlockSpec((1, k), lambda i, j: (0, 0)),
                  pl.BlockSpec((k, tn), lambda i, j: (0, j))],
        out_specs=pl.BlockSpec((tm, tn), lambda i, j: (i, j)),
        out_shape=jax.ShapeDtypeStruct((m, n), F32),
        scratch_shapes=[pltpu.VMEM((tm, k), BF16)],
        compiler_params=_cparams(("parallel", "arbitrary")),
    )(x, g, w)


def _mmres_body(*refs, n):
    res_ref, a_refs, w_refs, o_ref = refs[0], refs[1:1 + n], refs[1 + n:1 + 2 * n], refs[1 + 2 * n]
    acc = res_ref[...]
    for a_ref, w_ref in zip(a_refs, w_refs):
        acc = acc + _dot(a_ref[...].astype(BF16), w_ref[...])
    o_ref[...] = acc


def matmul_residual(res, acts, ws, tm):
    m, d = res.shape
    n = len(acts)
    in_specs = [pl.BlockSpec((tm, d), lambda i: (i, 0))]
    in_specs += [pl.BlockSpec((tm, a.shape[1]), lambda i: (i, 0)) for a in acts]
    in_specs += [pl.BlockSpec(w.shape, lambda i: (0, 0)) for w in ws]
    return pl.pallas_call(
        functools.partial(_mmres_body, n=n),
        grid=(m // tm,),
        in_specs=in_specs,
        out_specs=pl.BlockSpec((tm, d), lambda i: (i, 0)),
        out_shape=jax.ShapeDtypeStruct((m, d), F32),
        compiler_params=_cparams(("parallel",)),
    )(res, *acts, *ws)


def _ffn_body(x_ref, g_ref, wg_ref, wu_ref, wd_ref, o_ref, h_sc, acc_sc):
    j = pl.program_id(1)

    @pl.when(j == 0)
    def _():
        h_sc[...] = _rms(x_ref[...], g_ref[...]).astype(BF16)
        acc_sc[...] = jnp.zeros_like(acc_sc)

    h = h_sc[...]
    act = _silu(_dot(h, wg_ref[...])) * _dot(h, wu_ref[...])
    acc_sc[...] += _dot(act.astype(BF16), wd_ref[...])

    @pl.when(j == pl.num_programs(1) - 1)
    def _():
        o_ref[...] = x_ref[...] + acc_sc[...]


def ffn_swiglu(x, g, wg, wu, wd, tm, tf):
    m, d = x.shape
    f = wg.shape[1]
    return pl.pallas_call(
        _ffn_body,
        grid=(m // tm, f // tf),
        in_specs=[pl.BlockSpec((tm, d), lambda i, j: (i, 0)),
                  pl.BlockSpec((1, d), lambda i, j: (0, 0)),
                  pl.BlockSpec((d, tf), lambda i, j: (0, j)),
                  pl.BlockSpec((d, tf), lambda i, j: (0, j)),
                  pl.BlockSpec((tf, d), lambda i, j: (j, 0))],
        out_specs=pl.BlockSpec((tm, d), lambda i, j: (i, 0)),
        out_shape=jax.ShapeDtypeStruct((m, d), F32),
        scratch_shapes=[pltpu.VMEM((tm, d), BF16), pltpu.VMEM((tm, d), F32)],
        compiler_params=_cparams(("parallel", "arbitrary")),
    )(x, g, wg, wu, wd)


def _moe_body(x_ref, g_ref, r_ref, wg_ref, wu_ref, wd_ref, gf_ref, o_ref, h_sc, comb_sc, acc_sc, *, final_norm):
    e = pl.program_id(1)
    tm = x_ref.shape[0]
    lane = lax.broadcasted_iota(jnp.int32, (tm, 128), 1)

    @pl.when(e == 0)
    def _():
        hf = _rms(x_ref[...], g_ref[...])
        h_sc[...] = hf.astype(BF16)
        lanef = lane.astype(F32)
        logits = jnp.where(lane < N_EXPERTS, _dot(hf, r_ref[...], HI), NEG_INF)
        m1 = jnp.max(logits, axis=1, keepdims=True)
        i1 = jnp.min(jnp.where(logits == m1, lanef, 128.0), axis=1, keepdims=True)
        mask1 = lanef == i1
        rest = jnp.where(mask1, NEG_INF, logits)
        m2 = jnp.max(rest, axis=1, keepdims=True)
        i2 = jnp.min(jnp.where(rest == m2, lanef, 128.0), axis=1, keepdims=True)
        mask2 = lanef == i2
        e2 = jnp.exp(m2 - m1)
        den = 1.0 + e2
        comb_sc[...] = jnp.where(mask1, 1.0 / den, 0.0) + jnp.where(mask2, e2 / den, 0.0)
        acc_sc[...] = jnp.zeros_like(acc_sc)

    h = h_sc[...]
    act = _silu(_dot(h, wg_ref[0])) * _dot(h, wu_ref[0])
    c = jnp.sum(jnp.where(lane == e, comb_sc[...], 0.0), axis=1, keepdims=True)
    acc_sc[...] += c * _dot(act.astype(BF16), wd_ref[0])

    @pl.when(e == pl.num_programs(1) - 1)
    def _():
        y = x_ref[...] + acc_sc[...]
        if final_norm:
            y = _rms(y, gf_ref[...])
        o_ref[...] = y


def moe_swiglu(x, g, router, wg, wu, wd, gfinal, tm, final_norm):
    m, d = x.shape
    ne, _, fe = wg.shape
    return pl.pallas_call(
        functools.partial(_moe_body, final_norm=final_norm),
        grid=(m // tm, ne),
        in_specs=[pl.BlockSpec((tm, d), lambda i, e: (i, 0)),
                  pl.BlockSpec((1, d), lambda i, e: (0, 0)),
                  pl.BlockSpec((d, 128), lambda i, e: (0, 0)),
                  pl.BlockSpec((1, d, fe), lambda i, e: (e, 0, 0)),
                  pl.BlockSpec((1, d, fe), lambda i, e: (e, 0, 0)),
                  pl.BlockSpec((1, fe, d), lambda i, e: (e, 0, 0)),
                  pl.BlockSpec((1, d), lambda i, e: (0, 0))],
        out_specs=pl.BlockSpec((tm, d), lambda i, e: (i, 0)),
        out_shape=jax.ShapeDtypeStruct((m, d), F32),
        scratch_shapes=[pltpu.VMEM((tm, d), BF16), pltpu.VMEM((tm, 128), F32), pltpu.VMEM((tm, d), F32)],
        compiler_params=_cparams(("parallel", "arbitrary")),
    )(x, g, router, wg, wu, wd, gfinal)


RW_PAIRS = RW_HEADS // 2
RW_ROWS = RW_PAIRS * RW_HEAD


def _rwkv_body(p_ref, sh0_ref, s0_ref, mu_ref, w0_ref, a0_ref, kk_ref, ka_ref, rk_ref, lnw_ref, lnb_ref,
               wl_ref, g2_ref, b2_ref, y_ref, sfin_ref,
               s_sc, prev_sc, kk_sc, w_sc, b_sc, k_sc, r_sc, v_sc, g_sc, y_sc, *, nbb, tc, t_valid):
    c = pl.program_id(1)

    @pl.when(c == 0)
    def _():
        s_sc[...] = s0_ref[...]
        prev_sc[...] = sh0_ref[...]

    b2 = b2_ref[...]

    def head_sums(x, low=True):
        if low:
            hi, lo = _split2(x)
        else:
            hi = x.astype(BF16)
            lo = jnp.zeros_like(hi)
        return _dot(jnp.concatenate([hi, lo], axis=1), b2)

    def head_sums_wide(x):
        return jnp.concatenate([head_sums(x[:, q * 128:(q + 1) * 128]) for q in range(RW_PAIRS)], axis=1)

    for bi in range(nbb):
        p = p_ref[bi]
        row = lax.broadcasted_iota(jnp.int32, p.shape, 0)
        prev = jnp.where(row == 0, prev_sc[bi], pltpu.roll(p, 1, axis=0))
        prev_sc[bi] = p[tc - 1:tc, :]
        u = p + (prev - p) * mu_ref[...]
        r = u[:, 0:RW_DIM]
        k = u[:, RW_DIM:2 * RW_DIM]
        v = u[:, 2 * RW_DIM:3 * RW_DIM]
        lo = u[:, 3 * RW_DIM:3 * RW_DIM + RW_LORA]
        gd = u[:, 3 * RW_DIM + RW_LORA:]
        lane = lax.broadcasted_iota(jnp.int32, lo.shape, 1)
        lora = _dot(_cat3(jnp.where(lane < RW_LORA // 2, jnp.tanh(lo), lo), 1), wl_ref[...])
        wlog = -_softplus(-(w0_ref[...] + lora[:, :RW_DIM])) - 0.5
        a = _sigmoid(a0_ref[...] + lora[:, RW_DIM:])
        g_sc[bi] = _dot(_cat3(_sigmoid(gd), 1), g2_ref[...])
        kk = k * kk_ref[...]
        kk = kk / jnp.maximum(jnp.sqrt(head_sums_wide(kk * kk)), 1e-12)
        kk_sc[bi] = kk
        w_sc[bi] = jnp.exp(-jnp.exp(wlog))
        b_sc[bi] = kk * a
        k_sc[bi] = k * (1.0 + (a - 1.0) * ka_ref[...])
        r_sc[bi] = r
        v_sc[bi] = v

    rows = nbb * RW_ROWS
    vi = lax.broadcasted_iota(jnp.int32, (rows, 128), 0) % RW_HEAD
    li = lax.broadcasted_iota(jnp.int32, (rows, 128), 1)
    first = li < RW_HEAD
    diag_a, diag_b = li == vi, li == vi + RW_HEAD
    diag = diag_a | diag_b
    row8 = lax.broadcasted_iota(jnp.int32, (8, RW_DIM), 0)
    steps = 8 if t_valid % 8 == 0 else t_valid

    def rows_of(x8, j):
        return jnp.concatenate([jnp.broadcast_to(x8[bi][j:j + 1, q * 128:(q + 1) * 128], (RW_HEAD, 128))
                                for bi in range(nbb) for q in range(RW_PAIRS)], axis=0)

    def group(t8, carry):
        t0 = pl.multiple_of(t8 * 8, 8)
        kk8, w8, b8, k8, r8, v8 = ([sc[bi, pl.ds(t0, 8), :] for bi in range(nbb)]
                                   for sc in (kk_sc, w_sc, b_sc, k_sc, r_sc, v_sc))
        s = s_sc[...].reshape(rows, 128)
        y8 = [jnp.zeros((8, RW_DIM), F32) for _ in range(nbb)]
        for j in range(steps):
            sk = head_sums(s * rows_of(kk8, j))
            vr = rows_of(v8, j)
            vcol = jnp.where(first, jnp.sum(jnp.where(diag_a, vr, 0.0), axis=1, keepdims=True),
                             jnp.sum(jnp.where(diag_b, vr, 0.0), axis=1, keepdims=True))
            s = s * rows_of(w8, j) - sk * rows_of(b8, j) + vcol * rows_of(k8, j)
            yd = jnp.where(diag, head_sums(s * rows_of(r8, j), low=False), 0.0)
            for bi in range(nbb):
                yrow = jnp.concatenate(
                    [jnp.sum(yd[(bi * RW_PAIRS + q) * RW_HEAD:(bi * RW_PAIRS + q + 1) * RW_HEAD], axis=0, keepdims=True)
                     for q in range(RW_PAIRS)], axis=1)
                y8[bi] = jnp.where(row8 == j, yrow, y8[bi])
        s_sc[...] = s.reshape(nbb, RW_ROWS, 128)
        for bi in range(nbb):
            y_sc[bi, pl.ds(t0, 8), :] = y8[bi]
        return carry

    lax.fori_loop(0, -(-t_valid // 8), group, 0)

    inv = 1.0 / RW_HEAD
    for bi in range(nbb):
        y = y_sc[bi]
        yc = y - head_sums_wide(y) * inv
        var = head_sums_wide(yc * yc) * inv
        yn = yc * lax.rsqrt(var + RW_LN_EPS) * lnw_ref[...] + lnb_ref[...]
        bonus = head_sums_wide(r_sc[bi] * k_sc[bi] * rk_ref[...]) * v_sc[bi]
        y_ref[bi] = (yn + bonus) * g_sc[bi]

    @pl.when(c == pl.num_programs(1) - 1)
    def _():
        sfin_ref[...] = s_sc[...]


def _wkv_to_rows(s):
    b = s.shape[0]
    return s.reshape(b, RW_PAIRS, 2, RW_HEAD, RW_HEAD).transpose(0, 1, 3, 2, 4).reshape(b, RW_ROWS, 128)


def _rows_to_wkv(s):
    b = s.shape[0]
    return s.reshape(b, RW_PAIRS, RW_HEAD, 2, RW_HEAD).transpose(0, 1, 3, 2, 4).reshape(b, RW_HEADS, RW_HEAD, RW_HEAD)


def rwkv7(proj, shift0, wkv0, prm, nbb, tc, t_valid):
    b, t, _ = proj.shape
    vec = lambda n: pl.BlockSpec((1, n), lambda i, c: (0, 0))
    full = lambda a: pl.BlockSpec(a.shape, lambda i, c: (0,) * a.ndim)
    seq = pltpu.VMEM((nbb, tc, RW_DIM), F32)
    y, s_fin = pl.pallas_call(
        functools.partial(_rwkv_body, nbb=nbb, tc=tc, t_valid=t_valid),
        grid=(b // nbb, t // tc),
        in_specs=[pl.BlockSpec((nbb, tc, RW_COLS), lambda i, c: (i, c, 0)),
                  pl.BlockSpec((nbb, 1, RW_COLS), lambda i, c: (i, 0, 0)),
                  pl.BlockSpec((nbb, RW_ROWS, 128), lambda i, c: (i, 0, 0)),
                  vec(RW_COLS), vec(RW_DIM), vec(RW_DIM), vec(RW_DIM), vec(RW_DIM), vec(RW_DIM), vec(RW_DIM),
                  vec(RW_DIM), full(prm["wl"]), full(prm["g2"]), full(prm["bones2"])],
        out_specs=[pl.BlockSpec((nbb, tc, RW_DIM), lambda i, c: (i, c, 0)),
                   pl.BlockSpec((nbb, RW_ROWS, 128), lambda i, c: (i, 0, 0))],
        out_shape=[jax.ShapeDtypeStruct((b, t, RW_DIM), F32),
                   jax.ShapeDtypeStruct((b, RW_ROWS, 128), F32)],
        scratch_shapes=[pltpu.VMEM((nbb, RW_ROWS, 128), F32), pltpu.VMEM((nbb, 1, RW_COLS), F32),
                        seq, seq, seq, seq, seq, seq, seq, seq],
        compiler_params=_cparams(("parallel", "arbitrary")),
    )(proj, shift0, _wkv_to_rows(wkv0), prm["mu"], prm["w0"], prm["a0"], prm["k_k"], prm["k_a"], prm["r_k"],
      prm["ln_w"], prm["ln_b"], prm["wl"], prm["g2"], prm["bones2"])
    return y, _rows_to_wkv(s_fin)


def _ssd_body(z_ref, xbc_ref, dt_ref, conv0_ref, h0_ref, cw_ref, cb_ref, dtb_ref, alog_ref, dsk_ref, nw_ref,
              e_ref, et_ref, y_ref, hfin_ref, h_sc, ext_sc, *, t_valid):
    c = pl.program_id(1)
    ln = SSD_CHUNK

    @pl.when(c == 0)
    def _():
        h_sc[...] = h0_ref[0]
        ext_sc[0:8, :] = conv0_ref[0]

    xbc = xbc_ref[0]
    ext_sc[8:8 + ln, :] = xbc
    cw = cw_ref[...]
    conv = cb_ref[...] + ext_sc[5:5 + ln, :] * cw[0:1]
    conv = conv + ext_sc[6:6 + ln, :] * cw[1:2]
    conv = conv + ext_sc[7:7 + ln, :] * cw[2:3]
    conv = conv + xbc * cw[3:4]
    ext_sc[0:8, :] = xbc[ln - 8:ln, :]
    act = _silu(conv)
    xs = act[:, :SSM_INNER]
    bm = act[:, SSM_INNER:SSM_INNER + SSM_GROUPS * SSM_STATE].astype(BF16)
    cm = act[:, SSM_INNER + SSM_GROUPS * SSM_STATE:].astype(BF16)

    lane = lax.broadcasted_iota(jnp.int32, (ln, 128), 1)
    rowi = lax.broadcasted_iota(jnp.int32, (ln, 128), 0)
    dt = _softplus(dt_ref[0] + dtb_ref[...])
    if t_valid < ln:
        dt = jnp.where(rowi < t_valid, dt, 0.0)
    a = dt * jnp.where(lane[0:1] < SSM_HEADS, -jnp.exp(alog_ref[...]), 0.0)
    causal = lane <= rowi
    tri = causal.astype(F32)
    eye = (lane == rowi).astype(F32)
    acum = _dot(tri, a, HI)
    acum_t = _nt(_nt(eye, a, HI), tri, HI)
    e = e_ref[...]
    dt_full = _dot(dt, e, HI)
    ac_full = _dot(acum, e, HI)
    xdt = xs * dt_full
    xw = xdt * jnp.exp(ac_full[ln - 1:ln, :] - ac_full)
    eac = jnp.exp(ac_full)
    cd = jnp.broadcast_to(jnp.exp(acum_t[:, ln - 1:ln]), (128, 128))
    cd_rows = _dot(et_ref[...], cd, HI)

    ys = []
    for g in range(SSM_GROUPS):
        bm_g = bm[:, g * SSM_STATE:(g + 1) * SSM_STATE]
        cm_g = cm[:, g * SSM_STATE:(g + 1) * SSM_STATE]
        cbm = _nt(cm_g, bm_g)
        pairs = SSM_HEADS // SSM_GROUPS // 2
        for q in range(g * pairs, (g + 1) * pairs):
            sl = slice(q * 128, (q + 1) * 128)
            xdt_p = xdt[:, sl].astype(BF16)
            yd = []
            for h in (2 * q, 2 * q + 1):
                seg = jnp.where(causal, acum[:, h:h + 1] - acum_t[h:h + 1, :], NEG_INF)
                yd.append(_dot((cbm * jnp.exp(seg)).astype(BF16), xdt_p))
            hp = h_sc[sl, :]
            y_off = _nt(cm_g, hp.astype(BF16)) * eac[:, sl]
            st = _dot(xw[:, sl].T.astype(BF16), bm_g)
            h_sc[sl, :] = cd_rows[sl, :] * hp + st
            ys.append(jnp.where(lane < SSM_HEAD, yd[0], yd[1]) + y_off)
    y = jnp.concatenate(ys, axis=1) + dsk_ref[...] * xs
    y = y * _silu(z_ref[0])
    gw = SSM_INNER // SSM_GROUPS
    outs = []
    for g in range(SSM_GROUPS):
        yg = y[:, g * gw:(g + 1) * gw]
        outs.append(yg * lax.rsqrt(jnp.mean(yg * yg, axis=1, keepdims=True) + NORM_EPS))
    y_ref[0] = jnp.concatenate(outs, axis=1) * nw_ref[...]

    @pl.when(c == pl.num_programs(1) - 1)
    def _():
        hfin_ref[0] = h_sc[...]


def mamba2(proj, conv0, ssm0, prm, t_valid):
    b, t, _ = proj.shape
    ln = SSD_CHUNK
    vec = lambda n: pl.BlockSpec((1, n), lambda i, c: (0, 0))
    full = lambda a: pl.BlockSpec(a.shape, lambda i, c: (0,) * a.ndim)
    return pl.pallas_call(
        functools.partial(_ssd_body, t_valid=t_valid),
        grid=(b, t // ln),
        in_specs=[pl.BlockSpec((1, ln, SSM_INNER), lambda i, c: (i, c, PROJ_Z // SSM_INNER)),
                  pl.BlockSpec((1, ln, SSM_CONV_DIM), lambda i, c: (i, c, PROJ_XBC // SSM_CONV_DIM)),
                  pl.BlockSpec((1, ln, 128), lambda i, c: (i, c, PROJ_DT // 128)),
                  pl.BlockSpec((1, 8, SSM_CONV_DIM), lambda i, c: (i, 0, 0)),
                  pl.BlockSpec((1, SSM_INNER, SSM_STATE), lambda i, c: (i, 0, 0)),
                  full(prm["conv_w"]), vec(SSM_CONV_DIM), vec(128), vec(128), vec(SSM_INNER), vec(SSM_INNER),
                  full(prm["expand"]), full(prm["expand_t"])],
        out_specs=[pl.BlockSpec((1, ln, SSM_INNER), lambda i, c: (i, c, 0)),
                   pl.BlockSpec((1, SSM_INNER, SSM_STATE), lambda i, c: (i, 0, 0))],
        out_shape=[jax.ShapeDtypeStruct((b, t, SSM_INNER), F32),
                   jax.ShapeDtypeStruct((b, SSM_INNER, SSM_STATE), F32)],
        scratch_shapes=[pltpu.VMEM((SSM_INNER, SSM_STATE), F32), pltpu.VMEM((8 + ln, SSM_CONV_DIM), F32)],
        compiler_params=_cparams(("parallel", "arbitrary")),
    )(proj, proj, proj, conv0, ssm0, prm["conv_w"], prm["conv_b"], prm["dt_bias"], prm["a_log"],
      prm["d_full"], prm["norm_w"], prm["expand"], prm["expand_t"])


def _alibi_slopes():
    return (2.0 ** (-8.0 * np.arange(1, ATT_HEADS + 1) / ATT_HEADS)).astype(np.float32)


def _slope_features():
    s = _alibi_slopes()
    bf = lambda a: a.astype(jnp.bfloat16).astype(np.float32)
    hi = bf(s)
    mid = bf(s - hi)
    lo = bf(s - hi - mid)
    out = np.zeros((ATT_HEADS, 32), np.float32)
    for i, piece in enumerate((hi, mid, lo)):
        out[:, i] = piece
        out[:, 3 + i] = piece
    return out


def _top3_threshold(gate):
    v = gate
    for _ in range(MOBA_TOPK - 1):
        v = jnp.where(v == jnp.max(v, axis=1, keepdims=True), NEG_INF, v)
    return jnp.max(v, axis=1, keepdims=True)


def _blockmean_body(k_ref, o_ref):
    o_ref[0] = jnp.mean(k_ref[...], axis=0, keepdims=True)


def block_means(qkv, nblocks):
    return pl.pallas_call(
        _blockmean_body,
        grid=(nblocks,),
        in_specs=[pl.BlockSpec((MOBA_BLOCK, ATT_KV_COLS), lambda i: (i, ATT_Q_COLS // ATT_KV_COLS))],
        out_specs=pl.BlockSpec((1, 1, ATT_KV_COLS), lambda i: (i, 0, 0)),
        out_shape=jax.ShapeDtypeStruct((nblocks, 1, ATT_KV_COLS), F32),
        compiler_params=_cparams(("parallel",)),
    )(qkv)


def _kv_features(k, v):
    b, t = k.shape[:2]
    pos = np.arange(t)
    kfeat = np.zeros((t, ATT_HEAD), np.float32)
    kfeat[pos, pos // MOBA_BLOCK] = 1.0
    kfeat[:, 32:35] = (pos // MOBA_BLOCK * MOBA_BLOCK)[:, None]
    kfeat[:, 35:38] = (pos % MOBA_BLOCK)[:, None]
    vfeat = np.zeros((t, ATT_HEAD), np.float32)
    vfeat[:, 0] = 1.0
    ext = lambda a, f: jnp.concatenate(
        [a.transpose(0, 2, 1, 3).astype(BF16),
         jnp.broadcast_to(jnp.asarray(f, BF16), (b, ATT_KV_HEADS, t, ATT_HEAD))], axis=-1)
    return ext(k, kfeat), ext(v, vfeat)


def _moba_prompt_body(q_ref, kf_ref, vf_ref, mean_ref, slf_ref, o_ref, qf_sc, m_sc, acc_sc):
    qb = pl.program_id(2)
    blk = MOBA_BLOCK
    rows = ATT_GROUP * blk
    scale = ATT_HEAD ** -0.5
    nbl = mean_ref.shape[2]
    blane = lax.broadcasted_iota(jnp.int32, (rows, nbl), 1)

    q4 = q_ref[0]
    qg = jnp.concatenate([q4[:, hh * ATT_HEAD:(hh + 1) * ATT_HEAD] for hh in range(ATT_GROUP)], axis=0)
    gate = jnp.where(blane < qb, _nt(_cat3(qg, 1), _cat3w(mean_ref[0, 0], 1)), NEG_INF)
    sel = gate >= _top3_threshold(gate)
    selneg = jnp.where((blane < qb) & jnp.logical_not(sel), MASK_NEG, 0.0)
    slf = jnp.concatenate([jnp.broadcast_to(slf_ref[0, hh:hh + 1, :], (blk, 32)) for hh in range(ATT_GROUP)], axis=0)
    pieces = [(qg * scale).astype(BF16), selneg.astype(BF16)]
    if nbl < 32:
        pieces.append(jnp.zeros((rows, 32 - nbl), BF16))
    pieces.append(slf.astype(BF16))
    qf_sc[...] = jnp.concatenate(pieces, axis=1)

    def kv_tiles(j):
        start = pl.multiple_of(j * blk, blk)
        return kf_ref[0, 0, pl.ds(start, blk), :], vf_ref[0, 0, pl.ds(start, blk), :]

    qi = lax.broadcasted_iota(jnp.int32, (rows, blk), 0) % blk
    ki = lax.broadcasted_iota(jnp.int32, (rows, blk), 1)
    kf, vf = kv_tiles(qb)
    s = jnp.where(ki <= qi, _nt(qf_sc[...], kf), NEG_INF)
    m = jnp.max(s, axis=1, keepdims=True)
    m_sc[...] = jnp.broadcast_to(m, (rows, 128))
    acc_sc[...] = _dot(jnp.exp(s - m).astype(BF16), vf)

    def past(j, carry):
        kf, vf = kv_tiles(j)
        s = _nt(qf_sc[...], kf)
        m_old = m_sc[...]
        m_new = jnp.maximum(m_old, jnp.max(s, axis=1, keepdims=True))
        p = jnp.exp(s - jnp.concatenate([m_new, m_new], axis=1))
        acc_sc[...] = acc_sc[...] * jnp.exp(m_old - m_new) + _dot(p.astype(BF16), vf)
        m_sc[...] = m_new
        return carry

    lax.fori_loop(0, qb, past, 0)
    acc = acc_sc[...]
    out = acc[:, :ATT_HEAD] / acc[:, ATT_HEAD:ATT_HEAD + 1]
    o_ref[0] = jnp.concatenate([out[hh * blk:(hh + 1) * blk] for hh in range(ATT_GROUP)], axis=1)


def moba_prompt(qkv, kf, vf, means, slf):
    b, t, _ = qkv.shape
    nb = t // MOBA_BLOCK
    nbl = means.shape[2]
    assert nbl <= 32
    rows = ATT_GROUP * MOBA_BLOCK
    gw = ATT_GROUP * ATT_HEAD
    return pl.pallas_call(
        _moba_prompt_body,
        grid=(b, ATT_KV_HEADS, nb),
        in_specs=[pl.BlockSpec((1, MOBA_BLOCK, gw), lambda i, g, j: (i, j, g)),
                  pl.BlockSpec((1, 1, t, 128), lambda i, g, j: (i, g, 0, 0)),
                  pl.BlockSpec((1, 1, t, 128), lambda i, g, j: (i, g, 0, 0)),
                  pl.BlockSpec((1, 1, nbl, ATT_HEAD), lambda i, g, j: (i, g, 0, 0)),
                  pl.BlockSpec((1, ATT_GROUP, 32), lambda i, g, j: (g, 0, 0))],
        out_specs=pl.BlockSpec((1, MOBA_BLOCK, gw), lambda i, g, j: (i, j, g)),
        out_shape=jax.ShapeDtypeStruct((b, t, ATT_Q_COLS), F32),
        scratch_shapes=[pltpu.VMEM((rows, 128), BF16), pltpu.VMEM((rows, 128), F32), pltpu.VMEM((rows, 128), F32)],
        compiler_params=_cparams(("parallel", "parallel", "arbitrary")),
    )(qkv, kf, vf, means, slf)


SAMPLE_BLOCKS_PER_STEP = 2


def _moba_sample_body(pt_ref, *refs, nbp, bps, nq):
    npg = 2 * bps
    k_refs, v_refs = refs[:npg], refs[npg:2 * npg]
    q_ref, kn_ref, vn_ref, slope_ref, o_ref, mean_sc, m_sc, l_sc, acc_sc = refs[2 * npg:]
    j = pl.program_id(1)
    blk = MOBA_BLOCK
    rows = ATT_GROUP * nq
    past = nbp * blk
    scale = ATT_HEAD ** -0.5
    lane = lax.broadcasted_iota(jnp.int32, (rows, 128), 1)
    qi = (lax.broadcasted_iota(jnp.int32, (rows, 1), 0) % nq).astype(F32)

    @pl.when(j == 0)
    def _():
        mean_sc[...] = jnp.zeros_like(mean_sc)
        m_sc[...] = jnp.full(m_sc.shape, NEG_INF, F32)
        l_sc[...] = jnp.zeros_like(l_sc)

    for s in range(bps):
        bi = j * bps + s
        kblk = jnp.concatenate([k_refs[2 * s][0], k_refs[2 * s + 1][0]], axis=0)
        vblk = jnp.concatenate([v_refs[2 * s][0], v_refs[2 * s + 1][0]], axis=0)
        mean_sc[pl.ds(bi, 1), :] = jnp.sum(kblk, axis=0, keepdims=True) * (1.0 / blk)
        off = lax.broadcasted_iota(jnp.int32, (1, blk), 1).astype(F32)
        base = (past - bi * blk).astype(F32)
        for g in range(ATT_KV_HEADS):
            qg = (q_ref[0, g] * scale).astype(BF16)
            kg = kblk[:, g * ATT_HEAD:(g + 1) * ATT_HEAD].astype(BF16)
            vg = vblk[:, g * ATT_HEAD:(g + 1) * ATT_HEAD].astype(BF16)
            sc = _nt(qg, kg) - slope_ref[g] * (base + qi - off)
            m = jnp.max(sc, axis=1, keepdims=True)
            p = jnp.exp(sc - m)
            m_sc[g] = jnp.where(lane == bi, m, m_sc[g])
            l_sc[g] = jnp.where(lane == bi, jnp.sum(p, axis=1, keepdims=True), l_sc[g])
            acc_sc[bi, g] = _dot(p.astype(BF16), vg)

    @pl.when(j == pl.num_programs(1) - 1)
    def _():
        for g in range(ATT_KV_HEADS):
            qf = q_ref[0, g]
            gate = _nt(qf, mean_sc[:, g * ATT_HEAD:(g + 1) * ATT_HEAD], HI)
            gate = jnp.where(lane < nbp, gate, NEG_INF)
            sel = gate >= _top3_threshold(gate)
            ki = lax.broadcasted_iota(jnp.int32, (rows, 8), 1).astype(F32)
            s_own = _nt((qf * scale).astype(BF16), kn_ref[0, g].astype(BF16)) - slope_ref[g] * (qi - ki)
            s_own = jnp.where(ki <= qi, s_own, NEG_INF)
            m_all = jnp.where(sel, m_sc[g], NEG_INF)
            mx = jnp.maximum(jnp.max(m_all, axis=1, keepdims=True), jnp.max(s_own, axis=1, keepdims=True))
            w = jnp.where(sel, jnp.exp(m_all - mx), 0.0)
            p_own = jnp.exp(s_own - mx)
            den = jnp.sum(w * l_sc[g], axis=1, keepdims=True) + jnp.sum(p_own, axis=1, keepdims=True)
            acc = _dot(p_own.astype(BF16), vn_ref[0, g].astype(BF16))
            for n in range(nbp):
                acc = acc + w[:, n:n + 1] * acc_sc[n, g]
            o_ref[0, g] = acc / den


def moba_sample(page_table, cache_k, cache_v, q_rows, k_new, v_new, slope_rows, nq):
    db, n_pages = page_table.shape
    nbp = n_pages * PAGE_SIZE // MOBA_BLOCK
    bps = SAMPLE_BLOCKS_PER_STEP
    rows = ATT_GROUP * nq

    def page_spec(slot):
        return pl.BlockSpec((1, PAGE_SIZE, ATT_KV_COLS),
                            lambda b, j, pt: (pt[b * n_pages + j * 2 * bps + slot], 0, 0))

    small = lambda shp: pl.BlockSpec((1,) + shp, lambda b, j, pt: (b,) + (0,) * len(shp))
    grid_spec = pltpu.PrefetchScalarGridSpec(
        num_scalar_prefetch=1,
        grid=(db, nbp // bps),
        in_specs=[page_spec(s) for s in range(2 * bps)] * 2
        + [small((ATT_KV_HEADS, rows, ATT_HEAD)), small((ATT_KV_HEADS, 8, ATT_HEAD)),
           small((ATT_KV_HEADS, 8, ATT_HEAD)),
           pl.BlockSpec((ATT_KV_HEADS, rows, 1), lambda b, j, pt: (0, 0, 0))],
        out_specs=small((ATT_KV_HEADS, rows, ATT_HEAD)),
        scratch_shapes=[pltpu.VMEM((128, ATT_KV_COLS), F32), pltpu.VMEM((ATT_KV_HEADS, rows, 128), F32),
                        pltpu.VMEM((ATT_KV_HEADS, rows, 128), F32),
                        pltpu.VMEM((nbp, ATT_KV_HEADS, rows, ATT_HEAD), F32)],
    )
    return pl.pallas_call(
        functools.partial(_moba_sample_body, nbp=nbp, bps=bps, nq=nq),
        grid_spec=grid_spec,
        out_shape=jax.ShapeDtypeStruct((db, ATT_KV_HEADS, rows, ATT_HEAD), F32),
        compiler_params=_cparams(("parallel", "arbitrary")),
    )(page_table.reshape(-1), *([cache_k] * (2 * bps)), *([cache_v] * (2 * bps)), q_rows, k_new, v_new, slope_rows)


def _row(a):
    return a.reshape(1, -1).astype(F32)


def _even_params(i, w_in_ab, rwkv_mu, rwkv_w0, rwkv_w2, rwkv_a0, rwkv_a2, rwkv_g2, rwkv_k_k, rwkv_k_a, rwkv_r_k,
                 rwkv_ln_w, rwkv_ln_b, ssm_conv_w, ssm_conv_b, ssm_dt_bias, ssm_a_log, ssm_d, ssm_norm_w):
    w = w_in_ab[i]
    ssm0 = RW_COLS
    zeros = lambda n: jnp.zeros((D_MODEL, n), w.dtype)
    w_pack = jnp.concatenate([
        w[:, :RW_COLS],
        w[:, ssm0 + SSM_INNER + SSM_CONV_DIM:], zeros(PROJ_Z - PROJ_DT - SSM_HEADS),
        w[:, ssm0:ssm0 + SSM_INNER],
        w[:, ssm0 + SSM_INNER:ssm0 + SSM_INNER + SSM_CONV_DIM]], axis=1).astype(BF16)
    half = RW_LORA // 2
    zl = jnp.zeros((half, RW_DIM), F32)
    wl = jnp.concatenate([jnp.concatenate([rwkv_w2[i], zl], axis=1),
                          jnp.concatenate([zl, rwkv_a2[i]], axis=1)], axis=0)
    half_of = np.arange(128) // RW_HEAD
    bones2 = np.tile((half_of[:, None] == half_of[None, :]).astype(np.float32), (2, 1))
    rw = dict(mu=_row(rwkv_mu[i]), w0=_row(rwkv_w0[i]), a0=_row(rwkv_a0[i]), k_k=_row(rwkv_k_k[i]),
              k_a=_row(rwkv_k_a[i]), r_k=_row(rwkv_r_k[i]), ln_w=_row(rwkv_ln_w[i]), ln_b=_row(rwkv_ln_b[i]),
              wl=_cat3w(wl, 0), g2=_cat3w(rwkv_g2[i], 0), bones2=jnp.asarray(bones2, BF16))
    pad128 = lambda a: jnp.pad(_row(a), ((0, 0), (0, 128 - a.shape[-1])))
    expand = (np.arange(128)[:, None] == (np.arange(SSM_INNER) // SSM_HEAD)[None, :]).astype(np.float32)
    ssm = dict(conv_w=ssm_conv_w[i], conv_b=_row(ssm_conv_b[i]), dt_bias=pad128(ssm_dt_bias[i]),
               a_log=pad128(ssm_a_log[i]), d_full=_row(jnp.repeat(ssm_d[i], SSM_HEAD)), norm_w=_row(ssm_norm_w[i]),
               expand=jnp.asarray(expand), expand_t=jnp.asarray(expand.T))
    return w_pack, rw, ssm


def _mixer_ab(x, b, t, norm_g, w_pack, rw, ssm, w_out, shift0, wkv0, conv0, ssm0, tm):
    proj = norm_matmul(x, norm_g, w_pack, tm, SSM_CONV_DIM).reshape(b, t, PROJ_COLS)
    shift_new = proj[:, t - 1, :RW_COLS]
    conv_new = proj[:, t - (SSM_CONV - 1):, PROJ_XBC:]
    if t % SSD_CHUNK == 0:
        proj_rw, proj_ssm, tc = proj, proj, SSD_CHUNK
    else:
        proj_rw = jnp.pad(proj, ((0, 0), (0, 8 - t), (0, 0)))
        proj_ssm = jnp.pad(proj, ((0, 0), (0, SSD_CHUNK - t), (0, 0)))
        tc = 8
    nbb = 2 if b <= 4 else 4
    y_rw, wkv_new = rwkv7(proj_rw, shift0.reshape(b, 1, RW_COLS), wkv0, rw, nbb, tc, min(t, tc))
    conv0p = jnp.pad(conv0, ((0, 0), (8 - (SSM_CONV - 1), 0), (0, 0)))
    y_ssm, ssm_new = mamba2(proj_ssm, conv0p, ssm0.reshape(b, SSM_INNER, SSM_STATE), ssm, min(t, SSD_CHUNK))
    x = matmul_residual(x, [y_rw[:, :t].reshape(b * t, RW_DIM), y_ssm[:, :t].reshape(b * t, SSM_INNER)],
                        [w_out[:RW_DIM], w_out[RW_DIM:]], tm)
    return x, (shift_new, wkv_new, conv_new, ssm_new.reshape(b, SSM_HEADS, SSM_HEAD, SSM_STATE))


def kernel(x_prompt, x_sample, state_rwkv_shift, state_rwkv_wkv, state_ssm_conv, state_ssm, cache_k, cache_v, page_table, norm_mix, norm_ffn, norm_final, w_in_ab, rwkv_mu, rwkv_w0, rwkv_w2, rwkv_a0, rwkv_a2, rwkv_g2, rwkv_k_k, rwkv_k_a, rwkv_r_k, rwkv_ln_w, rwkv_ln_b, ssm_conv_w, ssm_conv_b, ssm_dt_bias, ssm_a_log, ssm_d, ssm_norm_w, w_out_ab, ffn_w_gate, ffn_w_up, ffn_w_down, attn_w_qkv, attn_w_o, moe_router, moe_w_gate, moe_w_up, moe_w_down):
    bp, tp, _ = x_prompt.shape
    db, ts, _ = x_sample.shape
    depth = norm_mix.shape[0]
    tm_p, tm_s = 512, db * ts
    xp = x_prompt.reshape(bp * tp, D_MODEL)
    xs = x_sample.reshape(db * ts, D_MODEL)
    slopes = _alibi_slopes()
    slf = jnp.asarray(_slope_features().reshape(ATT_KV_HEADS, ATT_GROUP, 32))
    slope_rows = jnp.asarray(np.repeat(slopes, ts).reshape(ATT_KV_HEADS, ATT_GROUP * ts, 1))
    st = {n: [] for n in ("p_shift", "p_wkv", "p_conv", "p_ssm", "p_k", "p_v",
                          "s_shift", "s_wkv", "s_conv", "s_ssm", "s_k", "s_v")}
    for l in range(depth):
        i = l // 2
        g_mix, g_ffn = _row(norm_mix[l]), _row(norm_ffn[l])
        if l % 2 == 0:
            w_pack, rw, ssm = _even_params(i, w_in_ab, rwkv_mu, rwkv_w0, rwkv_w2, rwkv_a0, rwkv_a2, rwkv_g2,
                                           rwkv_k_k, rwkv_k_a, rwkv_r_k, rwkv_ln_w, rwkv_ln_b, ssm_conv_w,
                                           ssm_conv_b, ssm_dt_bias, ssm_a_log, ssm_d, ssm_norm_w)
            w_out = w_out_ab[i].astype(BF16)
            xp, sp = _mixer_ab(xp, bp, tp, g_mix, w_pack, rw, ssm, w_out,
                               jnp.zeros((bp, RW_COLS), F32), jnp.zeros((bp, RW_HEADS, RW_HEAD, RW_HEAD), F32),
                               jnp.zeros((bp, SSM_CONV - 1, SSM_CONV_DIM), F32),
                               jnp.zeros((bp, SSM_HEADS, SSM_HEAD, SSM_STATE), F32), tm_p)
            xs, ss = _mixer_ab(xs, db, ts, g_mix, w_pack, rw, ssm, w_out, state_rwkv_shift[i], state_rwkv_wkv[i],
                               state_ssm_conv[i], state_ssm[i], tm_s)
            for pre, new in (("p", sp), ("s", ss)):
                for name, val in zip(("shift", "wkv", "conv", "ssm"), new):
                    st[f"{pre}_{name}"].append(val)
            wg, wu, wd = ffn_w_gate[i].astype(BF16), ffn_w_up[i].astype(BF16), ffn_w_down[i].astype(BF16)
            xp = ffn_swiglu(xp, g_ffn, wg, wu, wd, tm_p, 256)
            xs = ffn_swiglu(xs, g_ffn, wg, wu, wd, tm_s, 256)
        else:
            w_qkv, w_o = attn_w_qkv[i].astype(BF16), attn_w_o[i].astype(BF16)
            qkv_p = norm_matmul(xp, g_mix, w_qkv, tm_p, 512)
            nb = tp // MOBA_BLOCK
            means = block_means(qkv_p, bp * nb).reshape(bp, nb, ATT_KV_HEADS, ATT_HEAD).transpose(0, 2, 1, 3)
            k_p = qkv_p[:, ATT_Q_COLS:ATT_Q_COLS + ATT_KV_COLS].reshape(bp, tp, ATT_KV_HEADS, ATT_HEAD)
            v_p = qkv_p[:, ATT_Q_COLS + ATT_KV_COLS:].reshape(bp, tp, ATT_KV_HEADS, ATT_HEAD)
            kf, vf = _kv_features(k_p, v_p)
            o_p = moba_prompt(qkv_p.reshape(bp, tp, -1), kf, vf, means, slf)
            xp = matmul_residual(xp, [o_p.reshape(bp * tp, ATT_Q_COLS)], [w_o], tm_p)
            st["p_k"].append(k_p)
            st["p_v"].append(v_p)

            qkv_s = norm_matmul(xs, g_mix, w_qkv, tm_s, 512).reshape(db, ts, -1)
            kn = qkv_s[..., ATT_Q_COLS:ATT_Q_COLS + ATT_KV_COLS].reshape(db, ts, ATT_KV_HEADS, ATT_HEAD)
            vn = qkv_s[..., ATT_Q_COLS + ATT_KV_COLS:].reshape(db, ts, ATT_KV_HEADS, ATT_HEAD)
            q_rows = qkv_s[..., :ATT_Q_COLS].reshape(db, ts, ATT_HEADS, ATT_HEAD).transpose(0, 2, 1, 3).reshape(
                db, ATT_KV_HEADS, ATT_GROUP * ts, ATT_HEAD)
            new_rows = lambda a: jnp.pad(a.transpose(0, 2, 1, 3), ((0, 0), (0, 0), (0, 8 - ts), (0, 0)))
            n_pool = cache_k.shape[1]
            o_s = moba_sample(page_table, cache_k[i].reshape(n_pool, PAGE_SIZE, ATT_KV_COLS),
                              cache_v[i].reshape(n_pool, PAGE_SIZE, ATT_KV_COLS), q_rows, new_rows(kn), new_rows(vn),
                              slope_rows, ts)
            o_s = o_s.reshape(db, ATT_HEADS, ts, ATT_HEAD).transpose(0, 2, 1, 3).reshape(db * ts, ATT_Q_COLS)
            xs = matmul_residual(xs, [o_s], [w_o], tm_s)
            st["s_k"].append(kn)
            st["s_v"].append(vn)

            router = jnp.pad(moe_router[i], ((0, 0), (0, 128 - N_EXPERTS)))
            wg, wu, wd = moe_w_gate[i].astype(BF16), moe_w_up[i].astype(BF16), moe_w_down[i].astype(BF16)
            final = l == depth - 1
            xp = moe_swiglu(xp, g_ffn, router, wg, wu, wd, _row(norm_final), tm_p, final)
            xs = moe_swiglu(xs, g_ffn, router, wg, wu, wd, _row(norm_final), tm_s, final)
    y_prompt = xp.reshape(bp, tp, D_MODEL)
    y_sample = xs.reshape(db, ts, D_MODEL)
    stack = lambda n: jnp.stack(st[n])
    return (y_prompt, y_sample,
            stack("p_shift"), stack("p_wkv"), stack("p_conv"), stack("p_ssm"), stack("p_k"), stack("p_v"),
            stack("s_shift"), stack("s_wkv"), stack("s_conv"), stack("s_ssm"), stack("s_k"), stack("s_v"))
```

```python
import functools

import numpy as np
import jax
import jax.numpy as jnp
from jax import lax
from jax.experimental import pallas as pl
from jax.experimental.pallas import tpu as pltpu

F32 = jnp.float32
BF16 = jnp.bfloat16
HI = lax.Precision.HIGHEST
NEG_INF = float("-inf")

D_MODEL = 1024
NORM_EPS = 1e-6

RW_HEAD = 64
RW_HEADS = 8
RW_DIM = RW_HEADS * RW_HEAD
RW_LORA = 128
RW_GATE = 128
RW_COLS = 3 * RW_DIM + RW_LORA + RW_GATE
RW_LN_EPS = 64e-5

SSM_INNER = 1024
SSM_HEAD = 64
SSM_HEADS = 16
SSM_GROUPS = 2
SSM_STATE = 128
SSM_CONV = 4
SSM_CONV_DIM = SSM_INNER + 2 * SSM_GROUPS * SSM_STATE
SSD_CHUNK = 128

PROJ_RW = 0
PROJ_DT = RW_COLS
PROJ_Z = 2048
PROJ_XBC = 3072
PROJ_COLS = PROJ_XBC + SSM_CONV_DIM

ATT_HEADS = 16
ATT_KV_HEADS = 4
ATT_HEAD = 64
ATT_GROUP = ATT_HEADS // ATT_KV_HEADS
ATT_Q_COLS = ATT_HEADS * ATT_HEAD
ATT_KV_COLS = ATT_KV_HEADS * ATT_HEAD
MOBA_BLOCK = 256
MOBA_TOPK = 3
MASK_NEG = -1e30
PAGE_SIZE = 128

N_EXPERTS = 8

VMEM_LIMIT = 56 * 1024 * 1024


def _cparams(sem):
    return pltpu.CompilerParams(dimension_semantics=sem, vmem_limit_bytes=VMEM_LIMIT)


def _nt(a, b, precision=None):
    return lax.dot_general(a, b, (((1,), (1,)), ((), ())), precision=precision,
                           preferred_element_type=F32)


def _dot(a, b, precision=None):
    return jnp.dot(a, b, precision=precision, preferred_element_type=F32)


def _split2(x):
    hi = x.astype(BF16)
    return hi, (x - hi.astype(F32)).astype(BF16)


def _cat3(x, axis):
    hi, lo = _split2(x)
    return jnp.concatenate([hi, lo, hi], axis=axis)


def _cat3w(w, axis):
    hi, lo = _split2(w)
    return jnp.concatenate([hi, hi, lo], axis=axis)


def _rms(x, g):
    return x * lax.rsqrt(jnp.mean(x * x, axis=-1, keepdims=True) + NORM_EPS) * g


def _softplus(x):
    return jnp.maximum(x, 0.0) + jnp.log(1.0 + jnp.exp(-jnp.abs(x)))


def _sigmoid(x):
    return 1.0 / (1.0 + jnp.exp(-x))


def _silu(x):
    return x * _sigmoid(x)


def _nm_body(x_ref, g_ref, w_ref, o_ref, h_sc):
    @pl.when(pl.program_id(1) == 0)
    def _():
        h_sc[...] = _rms(x_ref[...], g_ref[...]).astype(BF16)

    o_ref[...] = _dot(h_sc[...], w_ref[...])


def norm_matmul(x, g, w, tm, tn):
    m, k = x.shape
    n = w.shape[1]
    return pl.pallas_call(
        _nm_body,
        grid=(m // tm, n // tn),
        in_specs=[pl.BlockSpec((tm, k), lambda i, j: (i, 0)),
                  pl.BlockSpec((1, k), lambda i, j: (0, 0)),
                  pl.BlockSpec((k, tn), lambda i, j: (0, j))],
        out_specs=pl.BlockSpec((tm, tn), lambda i, j: (i, j)),
        out_shape=jax.ShapeDtypeStruct((m, n), F32),
        scratch_shapes=[pltpu.VMEM((tm, k), BF16)],
        compiler_params=_cparams(("parallel", "arbitrary")),
    )(x, g, w)


def _mmres_body(*refs, n):
    res_ref, a_refs, w_refs, o_ref = refs[0], refs[1:1 + n], refs[1 + n:1 + 2 * n], refs[1 + 2 * n]
    acc = res_ref[...]
    for a_ref, w_ref in zip(a_refs, w_refs):
        acc = acc + _dot(a_ref[...].astype(BF16), w_ref[...])
    o_ref[...] = acc


def matmul_residual(res, acts, ws, tm):
    m, d = res.shape
    n = len(acts)
    in_specs = [pl.BlockSpec((tm, d), lambda i: (i, 0))]
    in_specs += [pl.BlockSpec((tm, a.shape[1]), lambda i: (i, 0)) for a in acts]
    in_specs += [pl.BlockSpec(w.shape, lambda i: (0, 0)) for w in ws]
    return pl.pallas_call(
        functools.partial(_mmres_body, n=n),
        grid=(m // tm,),
        in_specs=in_specs,
        out_specs=pl.BlockSpec((tm, d), lambda i: (i, 0)),
        out_shape=jax.ShapeDtypeStruct((m, d), F32),
        compiler_params=_cparams(("parallel",)),
    )(res, *acts, *ws)


def _ffn_body(x_ref, g_ref, wg_ref, wu_ref, wd_ref, o_ref, h_sc, acc_sc):
    j = pl.program_id(1)

    @pl.when(j == 0)
    def _():
        h_sc[...] = _rms(x_ref[...], g_ref[...]).astype(BF16)
        acc_sc[...] = jnp.zeros_like(acc_sc)

    h = h_sc[...]
    act = _silu(_dot(h, wg_ref[...])) * _dot(h, wu_ref[...])
    acc_sc[...] += _dot(act.astype(BF16), wd_ref[...])

    @pl.when(j == pl.num_programs(1) - 1)
    def _():
        o_ref[...] = x_ref[...] + acc_sc[...]


def ffn_swiglu(x, g, wg, wu, wd, tm, tf):
    m, d = x.shape
    f = wg.shape[1]
    return pl.pallas_call(
        _ffn_body,
        grid=(m // tm, f // tf),
        in_specs=[pl.BlockSpec((tm, d), lambda i, j: (i, 0)),
                  pl.BlockSpec((1, d), lambda i, j: (0, 0)),
                  pl.BlockSpec((d, tf), lambda i, j: (0, j)),
                  pl.BlockSpec((d, tf), lambda i, j: (0, j)),
                  pl.BlockSpec((tf, d), lambda i, j: (j, 0))],
        out_specs=pl.BlockSpec((tm, d), lambda i, j: (i, 0)),
        out_shape=jax.ShapeDtypeStruct((m, d), F32),
        scratch_shapes=[pltpu.VMEM((tm, d), BF16), pltpu.VMEM((tm, d), F32)],
        compiler_params=_cparams(("parallel", "arbitrary")),
    )(x, g, wg, wu, wd)


def _moe_body(x_ref, g_ref, r_ref, wg_ref, wu_ref, wd_ref, gf_ref, o_ref, h_sc, comb_sc, acc_sc, *, final_norm):
    e = pl.program_id(1)
    tm = x_ref.shape[0]
    lane = lax.broadcasted_iota(jnp.int32, (tm, 128), 1)

    @pl.when(e == 0)
    def _():
        hf = _rms(x_ref[...], g_ref[...])
        h_sc[...] = hf.astype(BF16)
        lanef = lane.astype(F32)
        logits = jnp.where(lane < N_EXPERTS, _dot(hf, r_ref[...], HI), NEG_INF)
        m1 = jnp.max(logits, axis=1, keepdims=True)
        i1 = jnp.min(jnp.where(logits == m1, lanef, 128.0), axis=1, keepdims=True)
        mask1 = lanef == i1
        rest = jnp.where(mask1, NEG_INF, logits)
        m2 = jnp.max(rest, axis=1, keepdims=True)
        i2 = jnp.min(jnp.where(rest == m2, lanef, 128.0), axis=1, keepdims=True)
        mask2 = lanef == i2
        e2 = jnp.exp(m2 - m1)
        den = 1.0 + e2
        comb_sc[...] = jnp.where(mask1, 1.0 / den, 0.0) + jnp.where(mask2, e2 / den, 0.0)
        acc_sc[...] = jnp.zeros_like(acc_sc)

    h = h_sc[...]
    act = _silu(_dot(h, wg_ref[0])) * _dot(h, wu_ref[0])
    c = jnp.sum(jnp.where(lane == e, comb_sc[...], 0.0), axis=1, keepdims=True)
    acc_sc[...] += c * _dot(act.astype(BF16), wd_ref[0])

    @pl.when(e == pl.num_programs(1) - 1)
    def _():
        y = x_ref[...] + acc_sc[...]
        if final_norm:
            y = _rms(y, gf_ref[...])
        o_ref[...] = y


def moe_swiglu(x, g, router, wg, wu, wd, gfinal, tm, final_norm):
    m, d = x.shape
    ne, _, fe = wg.shape
    return pl.pallas_call(
        functools.partial(_moe_body, final_norm=final_norm),
        grid=(m // tm, ne),
        in_specs=[pl.BlockSpec((tm, d), lambda i, e: (i, 0)),
                  pl.BlockSpec((1, d), lambda i, e: (0, 0)),
                  pl.BlockSpec((d, 128), lambda i, e: (0, 0)),
                  pl.BlockSpec((1, d, fe), lambda i, e: (e, 0, 0)),
                  pl.BlockSpec((1, d, fe), lambda i, e: (e, 0, 0)),
                  pl.BlockSpec((1, fe, d), lambda i, e: (e, 0, 0)),
                  pl.BlockSpec((1, d), lambda i, e: (0, 0))],
        out_specs=pl.BlockSpec((tm, d), lambda i, e: (i, 0)),
        out_shape=jax.ShapeDtypeStruct((m, d), F32),
        scratch_shapes=[pltpu.VMEM((tm, d), BF16), pltpu.VMEM((tm, 128), F32), pltpu.VMEM((tm, d), F32)],
        compiler_params=_cparams(("parallel", "arbitrary")),
    )(x, g, router, wg, wu, wd, gfinal)


RW_PAIRS = RW_HEADS // 2
RW_ROWS = RW_PAIRS * RW_HEAD


def _rwkv_body(p_ref, sh0_ref, s0_ref, mu_ref, w0_ref, a0_ref, kk_ref, ka_ref, rk_ref, lnw_ref, lnb_ref,
               wl_ref, g2_ref, b2_ref, y_ref, sfin_ref,
               s_sc, prev_sc, kk_sc, w_sc, b_sc, k_sc, r_sc, v_sc, g_sc, y_sc, *, nbb, tc, t_valid):
    c = pl.program_id(1)

    @pl.when(c == 0)
    def _():
        s_sc[...] = s0_ref[...]
        prev_sc[...] = sh0_ref[...]

    b2 = b2_ref[...]

    def head_sums(x, low=True):
        if low:
            hi, lo = _split2(x)
        else:
            hi = x.astype(BF16)
            lo = jnp.zeros_like(hi)
        return _dot(jnp.concatenate([hi, lo], axis=1), b2)

    def head_sums_wide(x):
        return jnp.concatenate([head_sums(x[:, q * 128:(q + 1) * 128]) for q in range(RW_PAIRS)], axis=1)

    for bi in range(nbb):
        p = p_ref[bi]
        row = lax.broadcasted_iota(jnp.int32, p.shape, 0)
        prev = jnp.where(row == 0, prev_sc[bi], pltpu.roll(p, 1, axis=0))
        prev_sc[bi] = p[tc - 1:tc, :]
        u = p + (prev - p) * mu_ref[...]
        r = u[:, 0:RW_DIM]
        k = u[:, RW_DIM:2 * RW_DIM]
        v = u[:, 2 * RW_DIM:3 * RW_DIM]
        lo = u[:, 3 * RW_DIM:3 * RW_DIM + RW_LORA]
        gd = u[:, 3 * RW_DIM + RW_LORA:]
        lane = lax.broadcasted_iota(jnp.int32, lo.shape, 1)
        lora = _dot(_cat3(jnp.where(lane < RW_LORA // 2, jnp.tanh(lo), lo), 1), wl_ref[...])
        wlog = -_softplus(-(w0_ref[...] + lora[:, :RW_DIM])) - 0.5
        a = _sigmoid(a0_ref[...] + lora[:, RW_DIM:])
        g_sc[bi] = _dot(_cat3(_sigmoid(gd), 1), g2_ref[...])
        kk = k * kk_ref[...]
        kk = kk / jnp.maximum(jnp.sqrt(head_sums_wide(kk * kk)), 1e-12)
        kk_sc[bi] = kk
        w_sc[bi] = jnp.exp(-jnp.exp(wlog))
        b_sc[bi] = kk * a
        k_sc[bi] = k * (1.0 + (a - 1.0) * ka_ref[...])
        r_sc[bi] = r
        v_sc[bi] = v

    rows = nbb * RW_ROWS
    vi = lax.broadcasted_iota(jnp.int32, (rows, 128), 0) % RW_HEAD
    li = lax.broadcasted_iota(jnp.int32, (rows, 128), 1)
    first = li < RW_HEAD
    diag_a, diag_b = li == vi, li == vi + RW_HEAD
    diag = diag_a | diag_b
    row8 = lax.broadcasted_iota(jnp.int32, (8, RW_DIM), 0)
    steps = 8 if t_valid % 8 == 0 else t_valid

    def rows_of(x8, j):
        return jnp.concatenate([jnp.broadcast_to(x8[bi][j:j + 1, q * 128:(q + 1) * 128], (RW_HEAD, 128))
                                for bi in range(nbb) for q in range(RW_PAIRS)], axis=0)

    def group(t8, carry):
        t0 = pl.multiple_of(t8 * 8, 8)
        kk8, w8, b8, k8, r8, v8 = ([sc[bi, pl.ds(t0, 8), :] for bi in range(nbb)]
                                   for sc in (kk_sc, w_sc, b_sc, k_sc, r_sc, v_sc))
        s = s_sc[...].reshape(rows, 128)
        y8 = [jnp.zeros((8, RW_DIM), F32) for _ in range(nbb)]
        for j in range(steps):
            sk = head_sums(s * rows_of(kk8, j))
            vr = rows_of(v8, j)
            vcol = jnp.where(first, jnp.sum(jnp.where(diag_a, vr, 0.0), axis=1, keepdims=True),
                             jnp.sum(jnp.where(diag_b, vr, 0.0), axis=1, keepdims=True))
            s = s * rows_of(w8, j) - sk * rows_of(b8, j) + vcol * rows_of(k8, j)
            yd = jnp.where(diag, head_sums(s * rows_of(r8, j), low=False), 0.0)
            for bi in range(nbb):
                yrow = jnp.concatenate(
                    [jnp.sum(yd[(bi * RW_PAIRS + q) * RW_HEAD:(bi * RW_PAIRS + q + 1) * RW_HEAD], axis=0, keepdims=True)
                     for q in range(RW_PAIRS)], axis=1)
                y8[bi] = jnp.where(row8 == j, yrow, y8[bi])
        s_sc[...] = s.reshape(nbb, RW_ROWS, 128)
        for bi in range(nbb):
            y_sc[bi, pl.ds(t0, 8), :] = y8[bi]
        return carry

    lax.fori_loop(0, -(-t_valid // 8), group, 0)

    inv = 1.0 / RW_HEAD
    for bi in range(nbb):
        y = y_sc[bi]
        yc = y - head_sums_wide(y) * inv
        var = head_sums_wide(yc * yc) * inv
        yn = yc * lax.rsqrt(var + RW_LN_EPS) * lnw_ref[...] + lnb_ref[...]
        bonus = head_sums_wide(r_sc[bi] * k_sc[bi] * rk_ref[...]) * v_sc[bi]
        y_ref[bi] = (yn + bonus) * g_sc[bi]

    @pl.when(c == pl.num_programs(1) - 1)
    def _():
        sfin_ref[...] = s_sc[...]


def _wkv_to_rows(s):
    b = s.shape[0]
    return s.reshape(b, RW_PAIRS, 2, RW_HEAD, RW_HEAD).transpose(0, 1, 3, 2, 4).reshape(b, RW_ROWS, 128)


def _rows_to_wkv(s):
    b = s.shape[0]
    return s.reshape(b, RW_PAIRS, RW_HEAD, 2, RW_HEAD).transpose(0, 1, 3, 2, 4).reshape(b, RW_HEADS, RW_HEAD, RW_HEAD)


def rwkv7(proj, shift0, wkv0, prm, nbb, tc, t_valid):
    b, t, _ = proj.shape
    vec = lambda n: pl.BlockSpec((1, n), lambda i, c: (0, 0))
    full = lambda a: pl.BlockSpec(a.shape, lambda i, c: (0,) * a.ndim)
    seq = pltpu.VMEM((nbb, tc, RW_DIM), F32)
    y, s_fin = pl.pallas_call(
        functools.partial(_rwkv_body, nbb=nbb, tc=tc, t_valid=t_valid),
        grid=(b // nbb, t // tc),
        in_specs=[pl.BlockSpec((nbb, tc, RW_COLS), lambda i, c: (i, c, 0)),
                  pl.BlockSpec((nbb, 1, RW_COLS), lambda i, c: (i, 0, 0)),
                  pl.BlockSpec((nbb, RW_ROWS, 128), lambda i, c: (i, 0, 0)),
                  vec(RW_COLS), vec(RW_DIM), vec(RW_DIM), vec(RW_DIM), vec(RW_DIM), vec(RW_DIM), vec(RW_DIM),
                  vec(RW_DIM), full(prm["wl"]), full(prm["g2"]), full(prm["bones2"])],
        out_specs=[pl.BlockSpec((nbb, tc, RW_DIM), lambda i, c: (i, c, 0)),
                   pl.BlockSpec((nbb, RW_ROWS, 128), lambda i, c: (i, 0, 0))],
        out_shape=[jax.ShapeDtypeStruct((b, t, RW_DIM), F32),
                   jax.ShapeDtypeStruct((b, RW_ROWS, 128), F32)],
        scratch_shapes=[pltpu.VMEM((nbb, RW_ROWS, 128), F32), pltpu.VMEM((nbb, 1, RW_COLS), F32),
                        seq, seq, seq, seq, seq, seq, seq, seq],
        compiler_params=_cparams(("parallel", "arbitrary")),
    )(proj, shift0, _wkv_to_rows(wkv0), prm["mu"], prm["w0"], prm["a0"], prm["k_k"], prm["k_a"], prm["r_k"],
      prm["ln_w"], prm["ln_b"], prm["wl"], prm["g2"], prm["bones2"])
    return y, _rows_to_wkv(s_fin)


def _ssd_body(z_ref, xbc_ref, dt_ref, conv0_ref, h0_ref, cw_ref, cb_ref, dtb_ref, alog_ref, dsk_ref, nw_ref,
              e_ref, et_ref, y_ref, hfin_ref, h_sc, ext_sc, *, t_valid):
    c = pl.program_id(1)
    ln = SSD_CHUNK

    @pl.when(c == 0)
    def _():
        h_sc[...] = h0_ref[0]
        ext_sc[0:8, :] = conv0_ref[0]

    xbc = xbc_ref[0]
    ext_sc[8:8 + ln, :] = xbc
    cw = cw_ref[...]
    conv = cb_ref[...] + ext_sc[5:5 + ln, :] * cw[0:1]
    conv = conv + ext_sc[6:6 + ln, :] * cw[1:2]
    conv = conv + ext_sc[7:7 + ln, :] * cw[2:3]
    conv = conv + xbc * cw[3:4]
    ext_sc[0:8, :] = xbc[ln - 8:ln, :]
    act = _silu(conv)
    xs = act[:, :SSM_INNER]
    bm = act[:, SSM_INNER:SSM_INNER + SSM_GROUPS * SSM_STATE].astype(BF16)
    cm = act[:, SSM_INNER + SSM_GROUPS * SSM_STATE:].astype(BF16)

    lane = lax.broadcasted_iota(jnp.int32, (ln, 128), 1)
    rowi = lax.broadcasted_iota(jnp.int32, (ln, 128), 0)
    dt = _softplus(dt_ref[0] + dtb_ref[...])
    if t_valid < ln:
        dt = jnp.where(rowi < t_valid, dt, 0.0)
    a = dt * jnp.where(lane[0:1] < SSM_HEADS, -jnp.exp(alog_ref[...]), 0.0)
    causal = lane <= rowi
    tri = causal.astype(F32)
    eye = (lane == rowi).astype(F32)
    acum = _dot(tri, a, HI)
    acum_t = _nt(_nt(eye, a, HI), tri, HI)
    e = e_ref[...]
    dt_full = _dot(dt, e, HI)
    ac_full = _dot(acum, e, HI)
    xdt = xs * dt_full
    xw = xdt * jnp.exp(ac_full[ln - 1:ln, :] - ac_full)
    eac = jnp.exp(ac_full)
    cd = jnp.broadcast_to(jnp.exp(acum_t[:, ln - 1:ln]), (128, 128))
    cd_rows = _dot(et_ref[...], cd, HI)

    ys = []
    for g in range(SSM_GROUPS):
        bm_g = bm[:, g * SSM_STATE:(g + 1) * SSM_STATE]
        cm_g = cm[:, g * SSM_STATE:(g + 1) * SSM_STATE]
        cbm = _nt(cm_g, bm_g)
        pairs = SSM_HEADS // SSM_GROUPS // 2
        for q in range(g * pairs, (g + 1) * pairs):
            sl = slice(q * 128, (q + 1) * 128)
            xdt_p = xdt[:, sl].astype(BF16)
            yd = []
            for h in (2 * q, 2 * q + 1):
                seg = jnp.where(causal, acum[:, h:h + 1] - acum_t[h:h + 1, :], NEG_INF)
                yd.append(_dot((cbm * jnp.exp(seg)).astype(BF16), xdt_p))
            hp = h_sc[sl, :]
            y_off = _nt(cm_g, hp.astype(BF16)) * eac[:, sl]
            st = _dot(xw[:, sl].T.astype(BF16), bm_g)
            h_sc[sl, :] = cd_rows[sl, :] * hp + st
            ys.append(jnp.where(lane < SSM_HEAD, yd[0], yd[1]) + y_off)
    y = jnp.concatenate(ys, axis=1) + dsk_ref[...] * xs
    y = y * _silu(z_ref[0])
    gw = SSM_INNER // SSM_GROUPS
    outs = []
    for g in range(SSM_GROUPS):
        yg = y[:, g * gw:(g + 1) * gw]
        outs.append(yg * lax.rsqrt(jnp.mean(yg * yg, axis=1, keepdims=True) + NORM_EPS))
    y_ref[0] = jnp.concatenate(outs, axis=1) * nw_ref[...]

    @pl.when(c == pl.num_programs(1) - 1)
    def _():
        hfin_ref[0] = h_sc[...]


def mamba2(proj, conv0, ssm0, prm, t_valid):
    b, t, _ = proj.shape
    ln = SSD_CHUNK
    vec = lambda n: pl.BlockSpec((1, n), lambda i, c: (0, 0))
    full = lambda a: pl.BlockSpec(a.shape, lambda i, c: (0,) * a.ndim)
    return pl.pallas_call(
        functools.partial(_ssd_body, t_valid=t_valid),
        grid=(b, t // ln),
        in_specs=[pl.BlockSpec((1, ln, SSM_INNER), lambda i, c: (i, c, PROJ_Z // SSM_INNER)),
                  pl.BlockSpec((1, ln, SSM_CONV_DIM), lambda i, c: (i, c, PROJ_XBC // SSM_CONV_DIM)),
                  pl.BlockSpec((1, ln, 128), lambda i, c: (i, c, PROJ_DT // 128)),
                  pl.BlockSpec((1, 8, SSM_CONV_DIM), lambda i, c: (i, 0, 0)),
                  pl.BlockSpec((1, SSM_INNER, SSM_STATE), lambda i, c: (i, 0, 0)),
                  full(prm["conv_w"]), vec(SSM_CONV_DIM), vec(128), vec(128), vec(SSM_INNER), vec(SSM_INNER),
                  full(prm["expand"]), full(prm["expand_t"])],
        out_specs=[pl.BlockSpec((1, ln, SSM_INNER), lambda i, c: (i, c, 0)),
                   pl.BlockSpec((1, SSM_INNER, SSM_STATE), lambda i, c: (i, 0, 0))],
        out_shape=[jax.ShapeDtypeStruct((b, t, SSM_INNER), F32),
                   jax.ShapeDtypeStruct((b, SSM_INNER, SSM_STATE), F32)],
        scratch_shapes=[pltpu.VMEM((SSM_INNER, SSM_STATE), F32), pltpu.VMEM((8 + ln, SSM_CONV_DIM), F32)],
        compiler_params=_cparams(("parallel", "arbitrary")),
    )(proj, proj, proj, conv0, ssm0, prm["conv_w"], prm["conv_b"], prm["dt_bias"], prm["a_log"],
      prm["d_full"], prm["norm_w"], prm["expand"], prm["expand_t"])


def _alibi_slopes():
    return (2.0 ** (-8.0 * np.arange(1, ATT_HEADS + 1) / ATT_HEADS)).astype(np.float32)


def _slope_features():
    s = _alibi_slopes()
    bf = lambda a: a.astype(jnp.bfloat16).astype(np.float32)
    hi = bf(s)
    mid = bf(s - hi)
    lo = bf(s - hi - mid)
    out = np.zeros((ATT_HEADS, 32), np.float32)
    for i, piece in enumerate((hi, mid, lo)):
        out[:, i] = piece
        out[:, 3 + i] = piece
    return out


def _top3_threshold(gate):
    v = gate
    for _ in range(MOBA_TOPK - 1):
        v = jnp.where(v == jnp.max(v, axis=1, keepdims=True), NEG_INF, v)
    return jnp.max(v, axis=1, keepdims=True)


def _blockmean_body(k_ref, o_ref):
    o_ref[0] = jnp.mean(k_ref[...], axis=0, keepdims=True)


def block_means(qkv, nblocks):
    return pl.pallas_call(
        _blockmean_body,
        grid=(nblocks,),
        in_specs=[pl.BlockSpec((MOBA_BLOCK, ATT_KV_COLS), lambda i: (i, ATT_Q_COLS // ATT_KV_COLS))],
        out_specs=pl.BlockSpec((1, 1, ATT_KV_COLS), lambda i: (i, 0, 0)),
        out_shape=jax.ShapeDtypeStruct((nblocks, 1, ATT_KV_COLS), F32),
        compiler_params=_cparams(("parallel",)),
    )(qkv)


def _qkv_body(x_ref, g_ref, w_ref, qkv_ref, kf_ref, vf_ref, *, tiles_per_seq):
    tm = x_ref.shape[0]
    qkv = _dot(_rms(x_ref[...], g_ref[...]).astype(BF16), w_ref[...])
    qkv_ref[...] = qkv
    pos = (pl.program_id(0) % tiles_per_seq) * tm + lax.broadcasted_iota(jnp.int32, (tm, ATT_HEAD), 0)
    lane = lax.broadcasted_iota(jnp.int32, (tm, ATT_HEAD), 1)
    blk_id, off = pos // MOBA_BLOCK, pos % MOBA_BLOCK
    kfeat = jnp.where(lane == blk_id, 1, 0)
    kfeat = jnp.where((lane >= 32) & (lane < 35), blk_id * MOBA_BLOCK, kfeat)
    kfeat = jnp.where((lane >= 35) & (lane < 38), off, kfeat).astype(F32).astype(BF16)
    vfeat = jnp.where(lane == 0, 1.0, 0.0).astype(BF16)
    for g in range(ATT_KV_HEADS):
        k0 = ATT_Q_COLS + g * ATT_HEAD
        v0 = ATT_Q_COLS + ATT_KV_COLS + g * ATT_HEAD
        kf_ref[0, g] = jnp.concatenate([qkv[:, k0:k0 + ATT_HEAD].astype(BF16), kfeat], axis=1)
        vf_ref[0, g] = jnp.concatenate([qkv[:, v0:v0 + ATT_HEAD].astype(BF16), vfeat], axis=1)


def qkv_proj(x, g, w, b, t, tm):
    m, d = x.shape
    n = w.shape[1]
    tps = t // tm
    ext = jax.ShapeDtypeStruct((b, ATT_KV_HEADS, t, 128), BF16)
    ext_spec = pl.BlockSpec((1, ATT_KV_HEADS, tm, 128), lambda i: (i // tps, 0, i % tps, 0))
    return pl.pallas_call(
        functools.partial(_qkv_body, tiles_per_seq=tps),
        grid=(m // tm,),
        in_specs=[pl.BlockSpec((tm, d), lambda i: (i, 0)), pl.BlockSpec((1, d), lambda i: (0, 0)),
                  pl.BlockSpec((d, n), lambda i: (0, 0))],
        out_specs=[pl.BlockSpec((tm, n), lambda i: (i, 0)), ext_spec, ext_spec],
        out_shape=[jax.ShapeDtypeStruct((m, n), F32), ext, ext],
        compiler_params=_cparams(("parallel",)),
    )(x, g, w)


def _moba_prompt_body(q_ref, kf_ref, vf_ref, mean_ref, slf_ref, o_ref, qf_sc, m_sc, acc_sc):
    qb = pl.program_id(2)
    blk = MOBA_BLOCK
    rows = ATT_GROUP * blk
    scale = ATT_HEAD ** -0.5

    q4 = q_ref[0]
    qg = jnp.concatenate([q4[:, hh * ATT_HEAD:(hh + 1) * ATT_HEAD] for hh in range(ATT_GROUP)], axis=0)
    past_blk = lax.broadcasted_iota(jnp.int32, (128, rows), 0) < qb
    gate = jnp.where(past_blk, _nt(_cat3w(mean_ref[0, 0], 1), _cat3(qg, 1)), NEG_INF)
    v = gate
    for _ in range(MOBA_TOPK - 1):
        v = jnp.where(v == jnp.max(v, axis=0, keepdims=True), NEG_INF, v)
    thr = jnp.max(v, axis=0, keepdims=True)
    selneg = jnp.where(past_blk & (gate < thr), MASK_NEG, 0.0).T[:, :32]
    slf = jnp.concatenate([jnp.broadcast_to(slf_ref[0, hh:hh + 1, :], (blk, 32)) for hh in range(ATT_GROUP)], axis=0)
    qf_sc[...] = jnp.concatenate([(qg * scale).astype(BF16), selneg.astype(BF16), slf.astype(BF16)], axis=1)

    def kv_tiles(j):
        start = pl.multiple_of(j * blk, blk)
        return kf_ref[0, 0, pl.ds(start, blk), :], vf_ref[0, 0, pl.ds(start, blk), :]

    qi = lax.broadcasted_iota(jnp.int32, (rows, blk), 0) % blk
    ki = lax.broadcasted_iota(jnp.int32, (rows, blk), 1)
    kf, vf = kv_tiles(qb)
    s = jnp.where(ki <= qi, _nt(qf_sc[...], kf), NEG_INF)
    m = jnp.max(s, axis=1, keepdims=True)
    m_sc[...] = jnp.broadcast_to(m, (rows, 128))
    acc_sc[...] = _dot(jnp.exp(s - m).astype(BF16), vf)

    def past(j, carry):
        kf, vf = kv_tiles(j)
        s = _nt(qf_sc[...], kf)
        m_old = m_sc[...]
        m_new = jnp.maximum(m_old, jnp.max(s, axis=1, keepdims=True))
        p = jnp.exp(s - jnp.concatenate([m_new, m_new], axis=1))
        acc_sc[...] = acc_sc[...] * jnp.exp(m_old - m_new) + _dot(p.astype(BF16), vf)
        m_sc[...] = m_new
        return carry

    lax.fori_loop(0, qb, past, 0)
    acc = acc_sc[...]
    out = acc[:, :ATT_HEAD] / acc[:, ATT_HEAD:ATT_HEAD + 1]
    o_ref[0] = jnp.concatenate([out[hh * blk:(hh + 1) * blk] for hh in range(ATT_GROUP)], axis=1)


def moba_prompt(qkv, kf, vf, means, slf):
    b, t, _ = qkv.shape
    nb = t // MOBA_BLOCK
    assert nb <= 32
    nbl = 128
    means = jnp.pad(means, ((0, 0), (0, 0), (0, nbl - nb), (0, 0)))
    rows = ATT_GROUP * MOBA_BLOCK
    gw = ATT_GROUP * ATT_HEAD
    return pl.pallas_call(
        _moba_prompt_body,
        grid=(b, ATT_KV_HEADS, nb),
        in_specs=[pl.BlockSpec((1, MOBA_BLOCK, gw), lambda i, g, j: (i, j, g)),
                  pl.BlockSpec((1, 1, t, 128), lambda i, g, j: (i, g, 0, 0)),
                  pl.BlockSpec((1, 1, t, 128), lambda i, g, j: (i, g, 0, 0)),
                  pl.BlockSpec((1, 1, nbl, ATT_HEAD), lambda i, g, j: (i, g, 0, 0)),
                  pl.BlockSpec((1, ATT_GROUP, 32), lambda i, g, j: (g, 0, 0))],
        out_specs=pl.BlockSpec((1, MOBA_BLOCK, gw), lambda i, g, j: (i, j, g)),
        out_shape=jax.ShapeDtypeStruct((b, t, ATT_Q_COLS), F32),
        scratch_shapes=[pltpu.VMEM((rows, 128), BF16), pltpu.VMEM((rows, 128), F32), pltpu.VMEM((rows, 128), F32)],
        compiler_params=_cparams(("parallel", "parallel", "arbitrary")),
    )(qkv, kf, vf, means, slf)


def _moba_sample_body(pt_ref, *refs, nbp, bps, nq):
    npg = 2 * bps
    k_refs, v_refs = refs[:npg], refs[npg:2 * npg]
    q_ref, kn_ref, vn_ref, slope_ref, o_ref, mean_sc, m_sc, l_sc, acc_sc = refs[2 * npg:]
    j = pl.program_id(1)
    blk = MOBA_BLOCK
    rows = ATT_HEADS * nq
    grows = ATT_GROUP * nq
    past = nbp * blk
    scale = ATT_HEAD ** -0.5
    lane = lax.broadcasted_iota(jnp.int32, (rows, 128), 1)
    qi = (lax.broadcasted_iota(jnp.int32, (rows, 1), 0) % nq).astype(F32)
    qf = q_ref[0]
    qs = (qf * scale).astype(BF16)
    slope = slope_ref[...]

    @pl.when(j == 0)
    def _():
        mean_sc[...] = jnp.zeros_like(mean_sc)
        m_sc[...] = jnp.full(m_sc.shape, NEG_INF, F32)
        l_sc[...] = jnp.zeros_like(l_sc)

    off = lax.broadcasted_iota(jnp.int32, (1, blk), 1).astype(F32)
    for s in range(bps):
        bi = j * bps + s
        kblk = jnp.concatenate([k_refs[2 * s][0], k_refs[2 * s + 1][0]], axis=0)
        vblk = jnp.concatenate([v_refs[2 * s][0], v_refs[2 * s + 1][0]], axis=0)
        mean_sc[pl.ds(bi, 1), :] = jnp.sum(kblk, axis=0, keepdims=True) * (1.0 / blk)
        sc = _nt(qs, kblk.astype(BF16)) - slope * ((past - bi * blk).astype(F32) + qi - off)
        m = jnp.max(sc, axis=1, keepdims=True)
        p = jnp.exp(sc - m)
        m_sc[...] = jnp.where(lane == bi, m, m_sc[...])
        l_sc[...] = jnp.where(lane == bi, jnp.sum(p, axis=1, keepdims=True), l_sc[...])
        acc_sc[bi] = _dot(p.astype(BF16), vblk.astype(BF16))

    @pl.when(j == pl.num_programs(1) - 1)
    def _():
        gate = jnp.where(lane < nbp, _nt(_cat3(qf, 1), _cat3w(mean_sc[...], 1)), NEG_INF)
        sel = gate >= _top3_threshold(gate)
        ki = lax.broadcasted_iota(jnp.int32, (rows, 8), 1).astype(F32)
        s_own = jnp.where(ki <= qi, _nt(qs, kn_ref[0].astype(BF16)) - slope * (qi - ki), NEG_INF)
        m_all = jnp.where(sel, m_sc[...], NEG_INF)
        mx = jnp.maximum(jnp.max(m_all, axis=1, keepdims=True), jnp.max(s_own, axis=1, keepdims=True))
        w = jnp.where(sel, jnp.exp(m_all - mx), 0.0)
        p_own = jnp.exp(s_own - mx)
        den = jnp.sum(w * l_sc[...], axis=1, keepdims=True) + jnp.sum(p_own, axis=1, keepdims=True)
        acc = _dot(p_own.astype(BF16), vn_ref[0].astype(BF16))
        for n in range(nbp):
            acc = acc + w[:, n:n + 1] * acc_sc[n]
        out = acc / den
        for g in range(ATT_KV_HEADS):
            o_ref[0, g] = out[g * grows:(g + 1) * grows, g * ATT_HEAD:(g + 1) * ATT_HEAD]


SAMPLE_BLOCKS_PER_STEP = 4


def _pages2d(cache):
    return cache.reshape(cache.shape[0], PAGE_SIZE, ATT_KV_COLS)


def _block_diag_queries(q):
    db, nq, _ = q.shape
    qh = q.reshape(db, nq, ATT_KV_HEADS, ATT_GROUP, ATT_HEAD).transpose(0, 2, 3, 1, 4)
    eye = jnp.eye(ATT_KV_HEADS, dtype=q.dtype)
    return (qh[:, :, :, :, None, :] * eye[None, :, None, None, :, None]).reshape(db, ATT_HEADS * nq, ATT_KV_COLS)


def _new_rows(a):
    db, nq = a.shape[:2]
    return jnp.pad(a.reshape(db, nq, ATT_KV_COLS), ((0, 0), (0, 8 - nq), (0, 0)))


def moba_sample(page_table, cache_k, cache_v, q_bd, k_new, v_new, slope_rows, nq):
    db, n_pages = page_table.shape
    nbp = n_pages * PAGE_SIZE // MOBA_BLOCK
    bps = min(SAMPLE_BLOCKS_PER_STEP, nbp)
    rows = ATT_HEADS * nq
    grows = ATT_GROUP * nq

    def page_spec(slot):
        return pl.BlockSpec((1, PAGE_SIZE, ATT_KV_COLS),
                            lambda b, j, pt: (pt[b * n_pages + j * 2 * bps + slot], 0, 0))

    small = lambda shp: pl.BlockSpec((1,) + shp, lambda b, j, pt: (b,) + (0,) * len(shp))
    grid_spec = pltpu.PrefetchScalarGridSpec(
        num_scalar_prefetch=1,
        grid=(db, nbp // bps),
        in_specs=[page_spec(s) for s in range(2 * bps)] * 2
        + [small((rows, ATT_KV_COLS)), small((8, ATT_KV_COLS)), small((8, ATT_KV_COLS)),
           pl.BlockSpec((rows, 1), lambda b, j, pt: (0, 0))],
        out_specs=small((ATT_KV_HEADS, grows, ATT_HEAD)),
        scratch_shapes=[pltpu.VMEM((128, ATT_KV_COLS), F32), pltpu.VMEM((rows, 128), F32),
                        pltpu.VMEM((rows, 128), F32), pltpu.VMEM((nbp, rows, ATT_KV_COLS), F32)],
    )
    return pl.pallas_call(
        functools.partial(_moba_sample_body, nbp=nbp, bps=bps, nq=nq),
        grid_spec=grid_spec,
        out_shape=jax.ShapeDtypeStruct((db, ATT_KV_HEADS, grows, ATT_HEAD), F32),
        compiler_params=_cparams(("parallel", "arbitrary")),
    )(page_table.reshape(-1), *([cache_k] * (2 * bps)), *([cache_v] * (2 * bps)), q_bd, k_new, v_new, slope_rows)


def _row(a):
    return a.reshape(1, -1).astype(F32)


def _even_params(i, w_in_ab, rwkv_mu, rwkv_w0, rwkv_w2, rwkv_a0, rwkv_a2, rwkv_g2, rwkv_k_k, rwkv_k_a, rwkv_r_k,
                 rwkv_ln_w, rwkv_ln_b, ssm_conv_w, ssm_conv_b, ssm_dt_bias, ssm_a_log, ssm_d, ssm_norm_w):
    w = w_in_ab[i]
    ssm0 = RW_COLS
    zeros = lambda n: jnp.zeros((D_MODEL, n), w.dtype)
    w_pack = jnp.concatenate([
        w[:, :RW_COLS],
        w[:, ssm0 + SSM_INNER + SSM_CONV_DIM:], zeros(PROJ_Z - PROJ_DT - SSM_HEADS),
        w[:, ssm0:ssm0 + SSM_INNER],
        w[:, ssm0 + SSM_INNER:ssm0 + SSM_INNER + SSM_CONV_DIM]], axis=1).astype(BF16)
    half = RW_LORA // 2
    zl = jnp.zeros((half, RW_DIM), F32)
    wl = jnp.concatenate([jnp.concatenate([rwkv_w2[i], zl], axis=1),
                          jnp.concatenate([zl, rwkv_a2[i]], axis=1)], axis=0)
    half_of = np.arange(128) // RW_HEAD
    bones2 = np.tile((half_of[:, None] == half_of[None, :]).astype(np.float32), (2, 1))
    rw = dict(mu=_row(rwkv_mu[i]), w0=_row(rwkv_w0[i]), a0=_row(rwkv_a0[i]), k_k=_row(rwkv_k_k[i]),
              k_a=_row(rwkv_k_a[i]), r_k=_row(rwkv_r_k[i]), ln_w=_row(rwkv_ln_w[i]), ln_b=_row(rwkv_ln_b[i]),
              wl=_cat3w(wl, 0), g2=_cat3w(rwkv_g2[i], 0), bones2=jnp.asarray(bones2, BF16))
    pad128 = lambda a: jnp.pad(_row(a), ((0, 0), (0, 128 - a.shape[-1])))
    expand = (np.arange(128)[:, None] == (np.arange(SSM_INNER) // SSM_HEAD)[None, :]).astype(np.float32)
    ssm = dict(conv_w=ssm_conv_w[i], conv_b=_row(ssm_conv_b[i]), dt_bias=pad128(ssm_dt_bias[i]),
               a_log=pad128(ssm_a_log[i]), d_full=_row(jnp.repeat(ssm_d[i], SSM_HEAD)), norm_w=_row(ssm_norm_w[i]),
               expand=jnp.asarray(expand), expand_t=jnp.asarray(expand.T))
    return w_pack, rw, ssm


def _mixer_ab(x, b, t, norm_g, w_pack, rw, ssm, w_out, shift0, wkv0, conv0, ssm0, tm):
    proj = norm_matmul(x, norm_g, w_pack, tm, SSM_CONV_DIM).reshape(b, t, PROJ_COLS)
    shift_new = proj[:, t - 1, :RW_COLS]
    conv_new = proj[:, t - (SSM_CONV - 1):, PROJ_XBC:]
    if t % SSD_CHUNK == 0:
        proj_rw, proj_ssm, tc = proj, proj, SSD_CHUNK
    else:
        proj_rw = jnp.pad(proj, ((0, 0), (0, 8 - t), (0, 0)))
        proj_ssm = jnp.pad(proj, ((0, 0), (0, SSD_CHUNK - t), (0, 0)))
        tc = 8
    nbb = 2 if b <= 4 else 4
    y_rw, wkv_new = rwkv7(proj_rw, shift0.reshape(b, 1, RW_COLS), wkv0, rw, nbb, tc, min(t, tc))
    conv0p = jnp.pad(conv0, ((0, 0), (8 - (SSM_CONV - 1), 0), (0, 0)))
    y_ssm, ssm_new = mamba2(proj_ssm, conv0p, ssm0.reshape(b, SSM_INNER, SSM_STATE), ssm, min(t, SSD_CHUNK))
    x = matmul_residual(x, [y_rw[:, :t].reshape(b * t, RW_DIM), y_ssm[:, :t].reshape(b * t, SSM_INNER)],
                        [w_out[:RW_DIM], w_out[RW_DIM:]], tm)
    return x, (shift_new, wkv_new, conv_new, ssm_new.reshape(b, SSM_HEADS, SSM_HEAD, SSM_STATE))


def kernel(x_prompt, x_sample, state_rwkv_shift, state_rwkv_wkv, state_ssm_conv, state_ssm, cache_k, cache_v, page_table, norm_mix, norm_ffn, norm_final, w_in_ab, rwkv_mu, rwkv_w0, rwkv_w2, rwkv_a0, rwkv_a2, rwkv_g2, rwkv_k_k, rwkv_k_a, rwkv_r_k, rwkv_ln_w, rwkv_ln_b, ssm_conv_w, ssm_conv_b, ssm_dt_bias, ssm_a_log, ssm_d, ssm_norm_w, w_out_ab, ffn_w_gate, ffn_w_up, ffn_w_down, attn_w_qkv, attn_w_o, moe_router, moe_w_gate, moe_w_up, moe_w_down):
    bp, tp, _ = x_prompt.shape
    db, ts, _ = x_sample.shape
    depth = norm_mix.shape[0]
    tm_p, tm_s = 512, db * ts
    xp = x_prompt.reshape(bp * tp, D_MODEL)
    xs = x_sample.reshape(db * ts, D_MODEL)
    slopes = _alibi_slopes()
    slf = jnp.asarray(_slope_features().reshape(ATT_KV_HEADS, ATT_GROUP, 32))
    slope_rows = jnp.asarray(np.repeat(slopes, ts).reshape(ATT_HEADS * ts, 1))
    st = {n: [] for n in ("p_shift", "p_wkv", "p_conv", "p_ssm", "p_k", "p_v",
                          "s_shift", "s_wkv", "s_conv", "s_ssm", "s_k", "s_v")}
    for l in range(depth):
        i = l // 2
        g_mix, g_ffn = _row(norm_mix[l]), _row(norm_ffn[l])
        if l % 2 == 0:
            w_pack, rw, ssm = _even_params(i, w_in_ab, rwkv_mu, rwkv_w0, rwkv_w2, rwkv_a0, rwkv_a2, rwkv_g2,
                                           rwkv_k_k, rwkv_k_a, rwkv_r_k, rwkv_ln_w, rwkv_ln_b, ssm_conv_w,
                                           ssm_conv_b, ssm_dt_bias, ssm_a_log, ssm_d, ssm_norm_w)
            w_out = w_out_ab[i].astype(BF16)
            xp, sp = _mixer_ab(xp, bp, tp, g_mix, w_pack, rw, ssm, w_out,
                               jnp.zeros((bp, RW_COLS), F32), jnp.zeros((bp, RW_HEADS, RW_HEAD, RW_HEAD), F32),
                               jnp.zeros((bp, SSM_CONV - 1, SSM_CONV_DIM), F32),
                               jnp.zeros((bp, SSM_HEADS, SSM_HEAD, SSM_STATE), F32), tm_p)
            xs, ss = _mixer_ab(xs, db, ts, g_mix, w_pack, rw, ssm, w_out, state_rwkv_shift[i], state_rwkv_wkv[i],
                               state_ssm_conv[i], state_ssm[i], tm_s)
            for pre, new in (("p", sp), ("s", ss)):
                for name, val in zip(("shift", "wkv", "conv", "ssm"), new):
                    st[f"{pre}_{name}"].append(val)
            wg, wu, wd = ffn_w_gate[i].astype(BF16), ffn_w_up[i].astype(BF16), ffn_w_down[i].astype(BF16)
            xp = ffn_swiglu(xp, g_ffn, wg, wu, wd, tm_p, 256)
            xs = ffn_swiglu(xs, g_ffn, wg, wu, wd, tm_s, 256)
        else:
            w_qkv, w_o = attn_w_qkv[i].astype(BF16), attn_w_o[i].astype(BF16)
            qkv_p, kf, vf = qkv_proj(xp, g_mix, w_qkv, bp, tp, tm_p)
            nb = tp // MOBA_BLOCK
            means = block_means(qkv_p, bp * nb).reshape(bp, nb, ATT_KV_HEADS, ATT_HEAD).transpose(0, 2, 1, 3)
            k_p = qkv_p[:, ATT_Q_COLS:ATT_Q_COLS + ATT_KV_COLS].reshape(bp, tp, ATT_KV_HEADS, ATT_HEAD)
            v_p = qkv_p[:, ATT_Q_COLS + ATT_KV_COLS:].reshape(bp, tp, ATT_KV_HEADS, ATT_HEAD)
            o_p = moba_prompt(qkv_p.reshape(bp, tp, -1), kf, vf, means, slf)
            xp = matmul_residual(xp, [o_p.reshape(bp * tp, ATT_Q_COLS)], [w_o], tm_p)
            st["p_k"].append(k_p)
            st["p_v"].append(v_p)

            qkv_s = norm_matmul(xs, g_mix, w_qkv, tm_s, 512).reshape(db, ts, -1)
            kn = qkv_s[..., ATT_Q_COLS:ATT_Q_COLS + ATT_KV_COLS].reshape(db, ts, ATT_KV_HEADS, ATT_HEAD)
            vn = qkv_s[..., ATT_Q_COLS + ATT_KV_COLS:].reshape(db, ts, ATT_KV_HEADS, ATT_HEAD)
            o_s = moba_sample(page_table, _pages2d(cache_k[i]), _pages2d(cache_v[i]),
                              _block_diag_queries(qkv_s[..., :ATT_Q_COLS]), _new_rows(kn), _new_rows(vn),
                              slope_rows, ts)
            o_s = o_s.reshape(db, ATT_HEADS, ts, ATT_HEAD).transpose(0, 2, 1, 3).reshape(db * ts, ATT_Q_COLS)
            xs = matmul_residual(xs, [o_s], [w_o], tm_s)
            st["s_k"].append(kn)
            st["s_v"].append(vn)

            router = jnp.pad(moe_router[i], ((0, 0), (0, 128 - N_EXPERTS)))
            wg, wu, wd = moe_w_gate[i].astype(BF16), moe_w_up[i].astype(BF16), moe_w_down[i].astype(BF16)
            final = l == depth - 1
            xp = moe_swiglu(xp, g_ffn, router, wg, wu, wd, _row(norm_final), tm_p, final)
            xs = moe_swiglu(xs, g_ffn, router, wg, wu, wd, _row(norm_final), tm_s, final)
    y_prompt = xp.reshape(bp, tp, D_MODEL)
    y_sample = xs.reshape(db, ts, D_MODEL)
    stack = lambda n: jnp.stack(st[n])
    return (y_prompt, y_sample,
            stack("p_shift"), stack("p_wkv"), stack("p_conv"), stack("p_ssm"), stack("p_k"), stack("p_v"),
            stack("s_shift"), stack("s_wkv"), stack("s_conv"), stack("s_ssm"), stack("s_k"), stack("s_v"))
```

```python
import functools

import numpy as np
import jax
import jax.numpy as jnp
from jax import lax
from jax.experimental import pallas as pl
from jax.experimental.pallas import tpu as pltpu

F32 = jnp.float32
BF16 = jnp.bfloat16
HI = lax.Precision.HIGHEST
NEG_INF = float("-inf")

D_MODEL = 1024
NORM_EPS = 1e-6

RW_HEAD = 64
RW_HEADS = 8
RW_DIM = RW_HEADS * RW_HEAD
RW_LORA = 128
RW_GATE = 128
RW_COLS = 3 * RW_DIM + RW_LORA + RW_GATE
RW_LN_EPS = 64e-5

SSM_INNER = 1024
SSM_HEAD = 64
SSM_HEADS = 16
SSM_GROUPS = 2
SSM_STATE = 128
SSM_CONV = 4
SSM_CONV_DIM = SSM_INNER + 2 * SSM_GROUPS * SSM_STATE
SSD_CHUNK = 128

PROJ_RW = 0
PROJ_DT = RW_COLS
PROJ_Z = 2048
PROJ_XBC = 3072
PROJ_COLS = PROJ_XBC + SSM_CONV_DIM

ATT_HEADS = 16
ATT_KV_HEADS = 4
ATT_HEAD = 64
ATT_GROUP = ATT_HEADS // ATT_KV_HEADS
ATT_Q_COLS = ATT_HEADS * ATT_HEAD
ATT_KV_COLS = ATT_KV_HEADS * ATT_HEAD
MOBA_BLOCK = 256
MOBA_TOPK = 3
MASK_NEG = -1e30
PAGE_SIZE = 128

N_EXPERTS = 8

VMEM_LIMIT = 56 * 1024 * 1024


def _cparams(sem):
    return pltpu.CompilerParams(dimension_semantics=sem, vmem_limit_bytes=VMEM_LIMIT)


def _nt(a, b, precision=None):
    return lax.dot_general(a, b, (((1,), (1,)), ((), ())), precision=precision,
                           preferred_element_type=F32)


def _dot(a, b, precision=None):
    return jnp.dot(a, b, precision=precision, preferred_element_type=F32)


def _split2(x):
    hi = x.astype(BF16)
    return hi, (x - hi.astype(F32)).astype(BF16)


def _cat3(x, axis):
    hi, lo = _split2(x)
    return jnp.concatenate([hi, lo, hi], axis=axis)


def _cat3w(w, axis):
    hi, lo = _split2(w)
    return jnp.concatenate([hi, hi, lo], axis=axis)


def _rms(x, g):
    return x * lax.rsqrt(jnp.mean(x * x, axis=-1, keepdims=True) + NORM_EPS) * g


def _softplus(x):
    return jnp.maximum(x, 0.0) + jnp.log(1.0 + jnp.exp(-jnp.abs(x)))


def _sigmoid(x):
    return 1.0 / (1.0 + jnp.exp(-x))


def _silu(x):
    return x * _sigmoid(x)


def _nm_body(x_ref, g_ref, w_ref, o_ref, h_sc):
    @pl.when(pl.program_id(1) == 0)
    def _():
        h_sc[...] = _rms(x_ref[...], g_ref[...]).astype(BF16)

    o_ref[...] = _dot(h_sc[...], w_ref[...])


def norm_matmul(x, g, w, tm, tn):
    m, k = x.shape
    n = w.shape[1]
    return pl.pallas_call(
        _nm_body,
        grid=(m // tm, n // tn),
        in_specs=[pl.BlockSpec((tm, k), lambda i, j: (i, 0)),
                  pl.BlockSpec((1, k), lambda i, j: (0, 0)),
                  pl.BlockSpec((k, tn), lambda i, j: (0, j))],
        out_specs=pl.BlockSpec((tm, tn), lambda i, j: (i, j)),
        out_shape=jax.ShapeDtypeStruct((m, n), F32),
        scratch_shapes=[pltpu.VMEM((tm, k), BF16)],
        compiler_params=_cparams(("parallel", "arbitrary")),
    )(x, g, w)


def _mmres_body(*refs, n):
    res_ref, a_refs, w_refs, o_ref = refs[0], refs[1:1 + n], refs[1 + n:1 + 2 * n], refs[1 + 2 * n]
    acc = res_ref[...]
    for a_ref, w_ref in zip(a_refs, w_refs):
        acc = acc + _dot(a_ref[...].astype(BF16), w_ref[...])
    o_ref[...] = acc


def matmul_residual(res, acts, ws, tm):
    m, d = res.shape
    n = len(acts)
    in_specs = [pl.BlockSpec((tm, d), lambda i: (i, 0))]
    in_specs += [pl.BlockSpec((tm, a.shape[1]), lambda i: (i, 0)) for a in acts]
    in_specs += [pl.BlockSpec(w.shape, lambda i: (0, 0)) for w in ws]
    return pl.pallas_call(
        functools.partial(_mmres_body, n=n),
        grid=(m // tm,),
        in_specs=in_specs,
        out_specs=pl.BlockSpec((tm, d), lambda i: (i, 0)),
        out_shape=jax.ShapeDtypeStruct((m, d), F32),
        compiler_params=_cparams(("parallel",)),
    )(res, *acts, *ws)


def _ffn_body(x_ref, g_ref, wg_ref, wu_ref, wd_ref, o_ref, h_sc, acc_sc):
    j = pl.program_id(1)

    @pl.when(j == 0)
    def _():
        h_sc[...] = _rms(x_ref[...], g_ref[...]).astype(BF16)
        acc_sc[...] = jnp.zeros_like(acc_sc)

    h = h_sc[...]
    act = _silu(_dot(h, wg_ref[...])) * _dot(h, wu_ref[...])
    acc_sc[...] += _dot(act.astype(BF16), wd_ref[...])

    @pl.when(j == pl.num_programs(1) - 1)
    def _():
        o_ref[...] = x_ref[...] + acc_sc[...]


def ffn_swiglu(x, g, wg, wu, wd, tm, tf):
    m, d = x.shape
    f = wg.shape[1]
    return pl.pallas_call(
        _ffn_body,
        grid=(m // tm, f // tf),
        in_specs=[pl.BlockSpec((tm, d), lambda i, j: (i, 0)),
                  pl.BlockSpec((1, d), lambda i, j: (0, 0)),
                  pl.BlockSpec((d, tf), lambda i, j: (0, j)),
                  pl.BlockSpec((d, tf), lambda i, j: (0, j)),
                  pl.BlockSpec((tf, d), lambda i, j: (j, 0))],
        out_specs=pl.BlockSpec((tm, d), lambda i, j: (i, 0)),
        out_shape=jax.ShapeDtypeStruct((m, d), F32),
        scratch_shapes=[pltpu.VMEM((tm, d), BF16), pltpu.VMEM((tm, d), F32)],
        compiler_params=_cparams(("parallel", "arbitrary")),
    )(x, g, wg, wu, wd)


def _moe_body(x_ref, g_ref, r_ref, wg_ref, wu_ref, wd_ref, gf_ref, o_ref, h_sc, comb_sc, acc_sc, *, final_norm):
    e = pl.program_id(1)
    tm = x_ref.shape[0]
    lane = lax.broadcasted_iota(jnp.int32, (tm, 128), 1)

    @pl.when(e == 0)
    def _():
        hf = _rms(x_ref[...], g_ref[...])
        h_sc[...] = hf.astype(BF16)
        lanef = lane.astype(F32)
        logits = jnp.where(lane < N_EXPERTS, _dot(hf, r_ref[...], HI), NEG_INF)
        m1 = jnp.max(logits, axis=1, keepdims=True)
        i1 = jnp.min(jnp.where(logits == m1, lanef, 128.0), axis=1, keepdims=True)
        mask1 = lanef == i1
        rest = jnp.where(mask1, NEG_INF, logits)
        m2 = jnp.max(rest, axis=1, keepdims=True)
        i2 = jnp.min(jnp.where(rest == m2, lanef, 128.0), axis=1, keepdims=True)
        mask2 = lanef == i2
        e2 = jnp.exp(m2 - m1)
        den = 1.0 + e2
        comb_sc[...] = jnp.where(mask1, 1.0 / den, 0.0) + jnp.where(mask2, e2 / den, 0.0)
        acc_sc[...] = jnp.zeros_like(acc_sc)

    h = h_sc[...]
    act = _silu(_dot(h, wg_ref[0])) * _dot(h, wu_ref[0])
    c = jnp.sum(jnp.where(lane == e, comb_sc[...], 0.0), axis=1, keepdims=True)
    acc_sc[...] += c * _dot(act.astype(BF16), wd_ref[0])

    @pl.when(e == pl.num_programs(1) - 1)
    def _():
        y = x_ref[...] + acc_sc[...]
        if final_norm:
            y = _rms(y, gf_ref[...])
        o_ref[...] = y


def moe_swiglu(x, g, router, wg, wu, wd, gfinal, tm, final_norm):
    m, d = x.shape
    ne, _, fe = wg.shape
    return pl.pallas_call(
        functools.partial(_moe_body, final_norm=final_norm),
        grid=(m // tm, ne),
        in_specs=[pl.BlockSpec((tm, d), lambda i, e: (i, 0)),
                  pl.BlockSpec((1, d), lambda i, e: (0, 0)),
                  pl.BlockSpec((d, 128), lambda i, e: (0, 0)),
                  pl.BlockSpec((1, d, fe), lambda i, e: (e, 0, 0)),
                  pl.BlockSpec((1, d, fe), lambda i, e: (e, 0, 0)),
                  pl.BlockSpec((1, fe, d), lambda i, e: (e, 0, 0)),
                  pl.BlockSpec((1, d), lambda i, e: (0, 0))],
        out_specs=pl.BlockSpec((tm, d), lambda i, e: (i, 0)),
        out_shape=jax.ShapeDtypeStruct((m, d), F32),
        scratch_shapes=[pltpu.VMEM((tm, d), BF16), pltpu.VMEM((tm, 128), F32), pltpu.VMEM((tm, d), F32)],
        compiler_params=_cparams(("parallel", "arbitrary")),
    )(x, g, router, wg, wu, wd, gfinal)


RW_PAIRS = RW_HEADS // 2
RW_ROWS = RW_PAIRS * RW_HEAD


def _rwkv_body(p_ref, sh0_ref, s0_ref, mu_ref, w0_ref, a0_ref, kk_ref, ka_ref, rk_ref, lnw_ref, lnb_ref,
               wl_ref, g2_ref, b2_ref, y_ref, sfin_ref,
               s_sc, prev_sc, kk_sc, w_sc, b_sc, k_sc, r_sc, v_sc, g_sc, y_sc, *, nbb, tc, t_valid):
    c = pl.program_id(1)

    @pl.when(c == 0)
    def _():
        s_sc[...] = s0_ref[...]
        prev_sc[...] = sh0_ref[...]

    b2 = b2_ref[...]

    def head_sums(x, low=True):
        if low:
            hi, lo = _split2(x)
        else:
            hi = x.astype(BF16)
            lo = jnp.zeros_like(hi)
        return _dot(jnp.concatenate([hi, lo], axis=1), b2)

    def head_sums_wide(x):
        return jnp.concatenate([head_sums(x[:, q * 128:(q + 1) * 128]) for q in range(RW_PAIRS)], axis=1)

    for bi in range(nbb):
        p = p_ref[bi]
        row = lax.broadcasted_iota(jnp.int32, p.shape, 0)
        prev = jnp.where(row == 0, prev_sc[bi], pltpu.roll(p, 1, axis=0))
        prev_sc[bi] = p[tc - 1:tc, :]
        u = p + (prev - p) * mu_ref[...]
        r = u[:, 0:RW_DIM]
        k = u[:, RW_DIM:2 * RW_DIM]
        v = u[:, 2 * RW_DIM:3 * RW_DIM]
        lo = u[:, 3 * RW_DIM:3 * RW_DIM + RW_LORA]
        gd = u[:, 3 * RW_DIM + RW_LORA:]
        lane = lax.broadcasted_iota(jnp.int32, lo.shape, 1)
        lora = _dot(_cat3(jnp.where(lane < RW_LORA // 2, jnp.tanh(lo), lo), 1), wl_ref[...])
        wlog = -_softplus(-(w0_ref[...] + lora[:, :RW_DIM])) - 0.5
        a = _sigmoid(a0_ref[...] + lora[:, RW_DIM:])
        g_sc[bi] = _dot(_cat3(_sigmoid(gd), 1), g2_ref[...])
        kk = k * kk_ref[...]
        kk = kk / jnp.maximum(jnp.sqrt(head_sums_wide(kk * kk)), 1e-12)
        kk_sc[bi] = kk
        w_sc[bi] = jnp.exp(-jnp.exp(wlog))
        b_sc[bi] = kk * a
        k_sc[bi] = k * (1.0 + (a - 1.0) * ka_ref[...])
        r_sc[bi] = r
        v_sc[bi] = v

    rows = nbb * RW_ROWS
    vi = lax.broadcasted_iota(jnp.int32, (rows, 128), 0) % RW_HEAD
    li = lax.broadcasted_iota(jnp.int32, (rows, 128), 1)
    first = li < RW_HEAD
    diag_a, diag_b = li == vi, li == vi + RW_HEAD
    diag = diag_a | diag_b
    row8 = lax.broadcasted_iota(jnp.int32, (8, RW_DIM), 0)
    steps = 8 if t_valid % 8 == 0 else t_valid

    def rows_of(x8, j):
        return jnp.concatenate([jnp.broadcast_to(x8[bi][j:j + 1, q * 128:(q + 1) * 128], (RW_HEAD, 128))
                                for bi in range(nbb) for q in range(RW_PAIRS)], axis=0)

    def group(t8, carry):
        t0 = pl.multiple_of(t8 * 8, 8)
        kk8, w8, b8, k8, r8, v8 = ([sc[bi, pl.ds(t0, 8), :] for bi in range(nbb)]
                                   for sc in (kk_sc, w_sc, b_sc, k_sc, r_sc, v_sc))
        s = s_sc[...].reshape(rows, 128)
        y8 = [jnp.zeros((8, RW_DIM), F32) for _ in range(nbb)]
        for j in range(steps):
            sk = head_sums(s * rows_of(kk8, j))
            vr = rows_of(v8, j)
            vcol = jnp.where(first, jnp.sum(jnp.where(diag_a, vr, 0.0), axis=1, keepdims=True),
                             jnp.sum(jnp.where(diag_b, vr, 0.0), axis=1, keepdims=True))
            s = s * rows_of(w8, j) - sk * rows_of(b8, j) + vcol * rows_of(k8, j)
            yd = jnp.where(diag, head_sums(s * rows_of(r8, j), low=False), 0.0)
            for bi in range(nbb):
                yrow = jnp.concatenate(
                    [jnp.sum(yd[(bi * RW_PAIRS + q) * RW_HEAD:(bi * RW_PAIRS + q + 1) * RW_HEAD], axis=0, keepdims=True)
                     for q in range(RW_PAIRS)], axis=1)
                y8[bi] = jnp.where(row8 == j, yrow, y8[bi])
        s_sc[...] = s.reshape(nbb, RW_ROWS, 128)
        for bi in range(nbb):
            y_sc[bi, pl.ds(t0, 8), :] = y8[bi]
        return carry

    lax.fori_loop(0, -(-t_valid // 8), group, 0)

    inv = 1.0 / RW_HEAD
    for bi in range(nbb):
        y = y_sc[bi]
        yc = y - head_sums_wide(y) * inv
        var = head_sums_wide(yc * yc) * inv
        yn = yc * lax.rsqrt(var + RW_LN_EPS) * lnw_ref[...] + lnb_ref[...]
        bonus = head_sums_wide(r_sc[bi] * k_sc[bi] * rk_ref[...]) * v_sc[bi]
        y_ref[bi] = (yn + bonus) * g_sc[bi]

    @pl.when(c == pl.num_programs(1) - 1)
    def _():
        sfin_ref[...] = s_sc[...]


def _wkv_to_rows(s):
    b = s.shape[0]
    return s.reshape(b, RW_PAIRS, 2, RW_HEAD, RW_HEAD).transpose(0, 1, 3, 2, 4).reshape(b, RW_ROWS, 128)


def _rows_to_wkv(s):
    b = s.shape[0]
    return s.reshape(b, RW_PAIRS, RW_HEAD, 2, RW_HEAD).transpose(0, 1, 3, 2, 4).reshape(b, RW_HEADS, RW_HEAD, RW_HEAD)


def rwkv7(proj, shift0, wkv0, prm, nbb, tc, t_valid):
    b, t, _ = proj.shape
    vec = lambda n: pl.BlockSpec((1, n), lambda i, c: (0, 0))
    full = lambda a: pl.BlockSpec(a.shape, lambda i, c: (0,) * a.ndim)
    seq = pltpu.VMEM((nbb, tc, RW_DIM), F32)
    y, s_fin = pl.pallas_call(
        functools.partial(_rwkv_body, nbb=nbb, tc=tc, t_valid=t_valid),
        grid=(b // nbb, t // tc),
        in_specs=[pl.BlockSpec((nbb, tc, RW_COLS), lambda i, c: (i, c, 0)),
                  pl.BlockSpec((nbb, 1, RW_COLS), lambda i, c: (i, 0, 0)),
                  pl.BlockSpec((nbb, RW_ROWS, 128), lambda i, c: (i, 0, 0)),
                  vec(RW_COLS), vec(RW_DIM), vec(RW_DIM), vec(RW_DIM), vec(RW_DIM), vec(RW_DIM), vec(RW_DIM),
                  vec(RW_DIM), full(prm["wl"]), full(prm["g2"]), full(prm["bones2"])],
        out_specs=[pl.BlockSpec((nbb, tc, RW_DIM), lambda i, c: (i, c, 0)),
                   pl.BlockSpec((nbb, RW_ROWS, 128), lambda i, c: (i, 0, 0))],
        out_shape=[jax.ShapeDtypeStruct((b, t, RW_DIM), F32),
                   jax.ShapeDtypeStruct((b, RW_ROWS, 128), F32)],
        scratch_shapes=[pltpu.VMEM((nbb, RW_ROWS, 128), F32), pltpu.VMEM((nbb, 1, RW_COLS), F32),
                        seq, seq, seq, seq, seq, seq, seq, seq],
        compiler_params=_cparams(("parallel", "arbitrary")),
    )(proj, shift0, _wkv_to_rows(wkv0), prm["mu"], prm["w0"], prm["a0"], prm["k_k"], prm["k_a"], prm["r_k"],
      prm["ln_w"], prm["ln_b"], prm["wl"], prm["g2"], prm["bones2"])
    return y, _rows_to_wkv(s_fin)


def _ssd_body(z_ref, xbc_ref, dt_ref, conv0_ref, h0_ref, cw_ref, cb_ref, dtb_ref, alog_ref, dsk_ref, nw_ref,
              e_ref, et_ref, y_ref, hfin_ref, h_sc, ext_sc, *, t_valid):
    c = pl.program_id(1)
    ln = SSD_CHUNK

    @pl.when(c == 0)
    def _():
        h_sc[...] = h0_ref[0]
        ext_sc[0:8, :] = conv0_ref[0]

    xbc = xbc_ref[0]
    ext_sc[8:8 + ln, :] = xbc
    cw = cw_ref[...]
    conv = cb_ref[...] + ext_sc[5:5 + ln, :] * cw[0:1]
    conv = conv + ext_sc[6:6 + ln, :] * cw[1:2]
    conv = conv + ext_sc[7:7 + ln, :] * cw[2:3]
    conv = conv + xbc * cw[3:4]
    ext_sc[0:8, :] = xbc[ln - 8:ln, :]
    act = _silu(conv)
    xs = act[:, :SSM_INNER]
    bm = act[:, SSM_INNER:SSM_INNER + SSM_GROUPS * SSM_STATE].astype(BF16)
    cm = act[:, SSM_INNER + SSM_GROUPS * SSM_STATE:].astype(BF16)

    lane = lax.broadcasted_iota(jnp.int32, (ln, 128), 1)
    rowi = lax.broadcasted_iota(jnp.int32, (ln, 128), 0)
    dt = _softplus(dt_ref[0] + dtb_ref[...])
    if t_valid < ln:
        dt = jnp.where(rowi < t_valid, dt, 0.0)
    a = dt * jnp.where(lane[0:1] < SSM_HEADS, -jnp.exp(alog_ref[...]), 0.0)
    causal = lane <= rowi
    tri = causal.astype(F32)
    eye = (lane == rowi).astype(F32)
    acum = _dot(tri, a, HI)
    acum_t = _nt(_nt(eye, a, HI), tri, HI)
    e = e_ref[...]
    dt_full = _dot(dt, e, HI)
    ac_full = _dot(acum, e, HI)
    xdt = xs * dt_full
    xw = xdt * jnp.exp(ac_full[ln - 1:ln, :] - ac_full)
    eac = jnp.exp(ac_full)
    cd = jnp.broadcast_to(jnp.exp(acum_t[:, ln - 1:ln]), (128, 128))
    cd_rows = _dot(et_ref[...], cd, HI)

    ys = []
    for g in range(SSM_GROUPS):
        bm_g = bm[:, g * SSM_STATE:(g + 1) * SSM_STATE]
        cm_g = cm[:, g * SSM_STATE:(g + 1) * SSM_STATE]
        cbm = _nt(cm_g, bm_g)
        pairs = SSM_HEADS // SSM_GROUPS // 2
        for q in range(g * pairs, (g + 1) * pairs):
            sl = slice(q * 128, (q + 1) * 128)
            xdt_p = xdt[:, sl].astype(BF16)
            yd = []
            for h in (2 * q, 2 * q + 1):
                seg = jnp.where(causal, acum[:, h:h + 1] - acum_t[h:h + 1, :], NEG_INF)
                yd.append(_dot((cbm * jnp.exp(seg)).astype(BF16), xdt_p))
            hp = h_sc[sl, :]
            y_off = _nt(cm_g, hp.astype(BF16)) * eac[:, sl]
            st = _dot(xw[:, sl].T.astype(BF16), bm_g)
            h_sc[sl, :] = cd_rows[sl, :] * hp + st
            ys.append(jnp.where(lane < SSM_HEAD, yd[0], yd[1]) + y_off)
    y = jnp.concatenate(ys, axis=1) + dsk_ref[...] * xs
    y = y * _silu(z_ref[0])
    gw = SSM_INNER // SSM_GROUPS
    outs = []
    for g in range(SSM_GROUPS):
        yg = y[:, g * gw:(g + 1) * gw]
        outs.append(yg * lax.rsqrt(jnp.mean(yg * yg, axis=1, keepdims=True) + NORM_EPS))
    y_ref[0] = jnp.concatenate(outs, axis=1) * nw_ref[...]

    @pl.when(c == pl.num_programs(1) - 1)
    def _():
        hfin_ref[0] = h_sc[...]


def mamba2(proj, conv0, ssm0, prm, t_valid):
    b, t, _ = proj.shape
    ln = SSD_CHUNK
    vec = lambda n: pl.BlockSpec((1, n), lambda i, c: (0, 0))
    full = lambda a: pl.BlockSpec(a.shape, lambda i, c: (0,) * a.ndim)
    return pl.pallas_call(
        functools.partial(_ssd_body, t_valid=t_valid),
        grid=(b, t // ln),
        in_specs=[pl.BlockSpec((1, ln, SSM_INNER), lambda i, c: (i, c, PROJ_Z // SSM_INNER)),
                  pl.BlockSpec((1, ln, SSM_CONV_DIM), lambda i, c: (i, c, PROJ_XBC // SSM_CONV_DIM)),
                  pl.BlockSpec((1, ln, 128), lambda i, c: (i, c, PROJ_DT // 128)),
                  pl.BlockSpec((1, 8, SSM_CONV_DIM), lambda i, c: (i, 0, 0)),
                  pl.BlockSpec((1, SSM_INNER, SSM_STATE), lambda i, c: (i, 0, 0)),
                  full(prm["conv_w"]), vec(SSM_CONV_DIM), vec(128), vec(128), vec(SSM_INNER), vec(SSM_INNER),
                  full(prm["expand"]), full(prm["expand_t"])],
        out_specs=[pl.BlockSpec((1, ln, SSM_INNER), lambda i, c: (i, c, 0)),
                   pl.BlockSpec((1, SSM_INNER, SSM_STATE), lambda i, c: (i, 0, 0))],
        out_shape=[jax.ShapeDtypeStruct((b, t, SSM_INNER), F32),
                   jax.ShapeDtypeStruct((b, SSM_INNER, SSM_STATE), F32)],
        scratch_shapes=[pltpu.VMEM((SSM_INNER, SSM_STATE), F32), pltpu.VMEM((8 + ln, SSM_CONV_DIM), F32)],
        compiler_params=_cparams(("parallel", "arbitrary")),
    )(proj, proj, proj, conv0, ssm0, prm["conv_w"], prm["conv_b"], prm["dt_bias"], prm["a_log"],
      prm["d_full"], prm["norm_w"], prm["expand"], prm["expand_t"])


def _alibi_slopes():
    return (2.0 ** (-8.0 * np.arange(1, ATT_HEADS + 1) / ATT_HEADS)).astype(np.float32)


def _slope_features():
    s = _alibi_slopes()
    bf = lambda a: a.astype(jnp.bfloat16).astype(np.float32)
    hi = bf(s)
    mid = bf(s - hi)
    lo = bf(s - hi - mid)
    out = np.zeros((ATT_HEADS, 32), np.float32)
    for i, piece in enumerate((hi, mid, lo)):
        out[:, i] = piece
        out[:, 3 + i] = piece
    return out


def _top3_threshold(gate):
    v = gate
    for _ in range(MOBA_TOPK - 1):
        v = jnp.where(v == jnp.max(v, axis=1, keepdims=True), NEG_INF, v)
    return jnp.max(v, axis=1, keepdims=True)


def _blockmean_body(k_ref, o_ref):
    o_ref[0] = jnp.mean(k_ref[...], axis=0, keepdims=True)


def block_means(qkv, nblocks):
    return pl.pallas_call(
        _blockmean_body,
        grid=(nblocks,),
        in_specs=[pl.BlockSpec((MOBA_BLOCK, ATT_KV_COLS), lambda i: (i, ATT_Q_COLS // ATT_KV_COLS))],
        out_specs=pl.BlockSpec((1, 1, ATT_KV_COLS), lambda i: (i, 0, 0)),
        out_shape=jax.ShapeDtypeStruct((nblocks, 1, ATT_KV_COLS), F32),
        compiler_params=_cparams(("parallel",)),
    )(qkv)


def _qkv_body(x_ref, g_ref, w_ref, qkv_ref, kf_ref, vf_ref, *, tiles_per_seq):
    tm = x_ref.shape[0]
    qkv = _dot(_rms(x_ref[...], g_ref[...]).astype(BF16), w_ref[...])
    qkv_ref[...] = qkv
    pos = (pl.program_id(0) % tiles_per_seq) * tm + lax.broadcasted_iota(jnp.int32, (tm, ATT_HEAD), 0)
    lane = lax.broadcasted_iota(jnp.int32, (tm, ATT_HEAD), 1)
    blk_id, off = pos // MOBA_BLOCK, pos % MOBA_BLOCK
    kfeat = jnp.where(lane == blk_id, 1, 0)
    kfeat = jnp.where((lane >= 32) & (lane < 35), blk_id * MOBA_BLOCK, kfeat)
    kfeat = jnp.where((lane >= 35) & (lane < 38), off, kfeat).astype(F32).astype(BF16)
    vfeat = jnp.where(lane == 0, 1.0, 0.0).astype(BF16)
    for g in range(ATT_KV_HEADS):
        k0 = ATT_Q_COLS + g * ATT_HEAD
        v0 = ATT_Q_COLS + ATT_KV_COLS + g * ATT_HEAD
        kf_ref[0, g] = jnp.concatenate([qkv[:, k0:k0 + ATT_HEAD].astype(BF16), kfeat], axis=1)
        vf_ref[0, g] = jnp.concatenate([qkv[:, v0:v0 + ATT_HEAD].astype(BF16), vfeat], axis=1)


def qkv_proj(x, g, w, b, t, tm):
    m, d = x.shape
    n = w.shape[1]
    tps = t // tm
    ext = jax.ShapeDtypeStruct((b, ATT_KV_HEADS, t, 128), BF16)
    ext_spec = pl.BlockSpec((1, ATT_KV_HEADS, tm, 128), lambda i: (i // tps, 0, i % tps, 0))
    return pl.pallas_call(
        functools.partial(_qkv_body, tiles_per_seq=tps),
        grid=(m // tm,),
        in_specs=[pl.BlockSpec((tm, d), lambda i: (i, 0)), pl.BlockSpec((1, d), lambda i: (0, 0)),
                  pl.BlockSpec((d, n), lambda i: (0, 0))],
        out_specs=[pl.BlockSpec((tm, n), lambda i: (i, 0)), ext_spec, ext_spec],
        out_shape=[jax.ShapeDtypeStruct((m, n), F32), ext, ext],
        compiler_params=_cparams(("parallel",)),
    )(x, g, w)


def _moba_prompt_body(q_ref, kf_ref, vf_ref, mean_ref, slf_ref, o_ref, qf_sc, m_sc, acc_sc, s0_sc, s1_sc):
    qb = pl.program_id(2)
    blk = MOBA_BLOCK
    rows = ATT_GROUP * blk
    scale = ATT_HEAD ** -0.5

    q4 = q_ref[0]
    qg = jnp.concatenate([q4[:, hh * ATT_HEAD:(hh + 1) * ATT_HEAD] for hh in range(ATT_GROUP)], axis=0)
    past_blk = lax.broadcasted_iota(jnp.int32, (128, rows), 0) < qb
    gate = jnp.where(past_blk, _nt(_cat3w(mean_ref[0, 0], 1), _cat3(qg, 1)), NEG_INF)
    v = gate
    for _ in range(MOBA_TOPK - 1):
        v = jnp.where(v == jnp.max(v, axis=0, keepdims=True), NEG_INF, v)
    thr = jnp.max(v, axis=0, keepdims=True)
    selneg = jnp.where(past_blk & (gate < thr), MASK_NEG, 0.0).T[:, :32]
    slf = jnp.concatenate([jnp.broadcast_to(slf_ref[0, hh:hh + 1, :], (blk, 32)) for hh in range(ATT_GROUP)], axis=0)
    qf_sc[...] = jnp.concatenate([(qg * scale).astype(BF16), selneg.astype(BF16), slf.astype(BF16)], axis=1)

    def kv_tiles(j):
        start = pl.multiple_of(j * blk, blk)
        return kf_ref[0, 0, pl.ds(start, blk), :], vf_ref[0, 0, pl.ds(start, blk), :]

    qi = lax.broadcasted_iota(jnp.int32, (rows, blk), 0) % blk
    ki = lax.broadcasted_iota(jnp.int32, (rows, blk), 1)
    kf, vf = kv_tiles(qb)
    s = jnp.where(ki <= qi, _nt(qf_sc[...], kf), NEG_INF)
    m = jnp.max(s, axis=1, keepdims=True)
    m_sc[...] = jnp.broadcast_to(m, (rows, 128))
    acc_sc[...] = _dot(jnp.exp(s - m).astype(BF16), vf)

    def fold(j, s_cur, s_next):
        if s_next is not None:
            s_next[...] = _nt(qf_sc[...], kv_tiles(jnp.minimum(j + 1, qb))[0])
        s = s_cur[...]
        m_old = m_sc[...]
        m_new = jnp.maximum(m_old, jnp.max(s, axis=1, keepdims=True))
        p = jnp.exp(s - jnp.concatenate([m_new, m_new], axis=1))
        acc_sc[...] = acc_sc[...] * jnp.exp(m_old - m_new) + _dot(p.astype(BF16), kv_tiles(j)[1])
        m_sc[...] = m_new

    s0_sc[...] = _nt(qf_sc[...], kv_tiles(0)[0])

    def pair(i, carry):
        fold(2 * i, s0_sc, s1_sc)
        fold(2 * i + 1, s1_sc, s0_sc)
        return carry

    lax.fori_loop(0, qb // 2, pair, 0)

    @pl.when(qb % 2 == 1)
    def _():
        fold(qb - 1, s0_sc, None)

    acc = acc_sc[...]
    out = acc[:, :ATT_HEAD] / acc[:, ATT_HEAD:ATT_HEAD + 1]
    o_ref[0] = jnp.concatenate([out[hh * blk:(hh + 1) * blk] for hh in range(ATT_GROUP)], axis=1)


def moba_prompt(qkv, kf, vf, means, slf):
    b, t, _ = qkv.shape
    nb = t // MOBA_BLOCK
    assert nb <= 32
    nbl = 128
    means = jnp.pad(means, ((0, 0), (0, 0), (0, nbl - nb), (0, 0)))
    rows = ATT_GROUP * MOBA_BLOCK
    gw = ATT_GROUP * ATT_HEAD
    return pl.pallas_call(
        _moba_prompt_body,
        grid=(b, ATT_KV_HEADS, nb),
        in_specs=[pl.BlockSpec((1, MOBA_BLOCK, gw), lambda i, g, j: (i, j, g)),
                  pl.BlockSpec((1, 1, t, 128), lambda i, g, j: (i, g, 0, 0)),
                  pl.BlockSpec((1, 1, t, 128), lambda i, g, j: (i, g, 0, 0)),
                  pl.BlockSpec((1, 1, nbl, ATT_HEAD), lambda i, g, j: (i, g, 0, 0)),
                  pl.BlockSpec((1, ATT_GROUP, 32), lambda i, g, j: (g, 0, 0))],
        out_specs=pl.BlockSpec((1, MOBA_BLOCK, gw), lambda i, g, j: (i, j, g)),
        out_shape=jax.ShapeDtypeStruct((b, t, ATT_Q_COLS), F32),
        scratch_shapes=[pltpu.VMEM((rows, 128), BF16), pltpu.VMEM((rows, 128), F32), pltpu.VMEM((rows, 128), F32),
                        pltpu.VMEM((rows, MOBA_BLOCK), F32), pltpu.VMEM((rows, MOBA_BLOCK), F32)],
        compiler_params=_cparams(("parallel", "parallel", "arbitrary")),
    )(qkv, kf, vf, means, slf)


def _moba_sample_body(pt_ref, *refs, nbp, bps, nq):
    npg = 2 * bps
    k_refs, v_refs = refs[:npg], refs[npg:2 * npg]
    q_ref, kn_ref, vn_ref, slope_ref, o_ref, mean_sc, m_sc, l_sc, acc_sc = refs[2 * npg:]
    j = pl.program_id(1)
    blk = MOBA_BLOCK
    rows = ATT_HEADS * nq
    grows = ATT_GROUP * nq
    past = nbp * blk
    scale = ATT_HEAD ** -0.5
    lane = lax.broadcasted_iota(jnp.int32, (rows, 128), 1)
    qi = (lax.broadcasted_iota(jnp.int32, (rows, 1), 0) % nq).astype(F32)
    qf = q_ref[0]
    qs = (qf * scale).astype(BF16)
    slope = slope_ref[...]

    @pl.when(j == 0)
    def _():
        mean_sc[...] = jnp.zeros_like(mean_sc)
        m_sc[...] = jnp.full(m_sc.shape, NEG_INF, F32)
        l_sc[...] = jnp.zeros_like(l_sc)

    off = lax.broadcasted_iota(jnp.int32, (1, blk), 1).astype(F32)
    for s in range(bps):
        bi = j * bps + s
        kblk = jnp.concatenate([k_refs[2 * s][0], k_refs[2 * s + 1][0]], axis=0)
        vblk = jnp.concatenate([v_refs[2 * s][0], v_refs[2 * s + 1][0]], axis=0)
        mean_sc[pl.ds(bi, 1), :] = jnp.sum(kblk, axis=0, keepdims=True) * (1.0 / blk)
        sc = _nt(qs, kblk.astype(BF16)) - slope * ((past - bi * blk).astype(F32) + qi - off)
        m = jnp.max(sc, axis=1, keepdims=True)
        p = jnp.exp(sc - m)
        m_sc[...] = jnp.where(lane == bi, m, m_sc[...])
        l_sc[...] = jnp.where(lane == bi, jnp.sum(p, axis=1, keepdims=True), l_sc[...])
        acc_sc[bi] = _dot(p.astype(BF16), vblk.astype(BF16))

    @pl.when(j == pl.num_programs(1) - 1)
    def _():
        gate = jnp.where(lane < nbp, _nt(_cat3(qf, 1), _cat3w(mean_sc[...], 1)), NEG_INF)
        sel = gate >= _top3_threshold(gate)
        ki = lax.broadcasted_iota(jnp.int32, (rows, 8), 1).astype(F32)
        s_own = jnp.where(ki <= qi, _nt(qs, kn_ref[0].astype(BF16)) - slope * (qi - ki), NEG_INF)
        m_all = jnp.where(sel, m_sc[...], NEG_INF)
        mx = jnp.maximum(jnp.max(m_all, axis=1, keepdims=True), jnp.max(s_own, axis=1, keepdims=True))
        w = jnp.where(sel, jnp.exp(m_all - mx), 0.0)
        p_own = jnp.exp(s_own - mx)
        den = jnp.sum(w * l_sc[...], axis=1, keepdims=True) + jnp.sum(p_own, axis=1, keepdims=True)
        acc = _dot(p_own.astype(BF16), vn_ref[0].astype(BF16))
        for n in range(nbp):
            acc = acc + w[:, n:n + 1] * acc_sc[n]
        out = acc / den
        for g in range(ATT_KV_HEADS):
            o_ref[0, g] = out[g * grows:(g + 1) * grows, g * ATT_HEAD:(g + 1) * ATT_HEAD]


SAMPLE_BLOCKS_PER_STEP = 4


def _pages2d(cache):
    return cache.reshape(cache.shape[0], PAGE_SIZE, ATT_KV_COLS)


def _block_diag_queries(q):
    db, nq, _ = q.shape
    qh = q.reshape(db, nq, ATT_KV_HEADS, ATT_GROUP, ATT_HEAD).transpose(0, 2, 3, 1, 4)
    eye = jnp.eye(ATT_KV_HEADS, dtype=q.dtype)
    return (qh[:, :, :, :, None, :] * eye[None, :, None, None, :, None]).reshape(db, ATT_HEADS * nq, ATT_KV_COLS)


def _new_rows(a):
    db, nq = a.shape[:2]
    return jnp.pad(a.reshape(db, nq, ATT_KV_COLS), ((0, 0), (0, 8 - nq), (0, 0)))


def moba_sample(page_table, cache_k, cache_v, q_bd, k_new, v_new, slope_rows, nq):
    db, n_pages = page_table.shape
    nbp = n_pages * PAGE_SIZE // MOBA_BLOCK
    bps = min(SAMPLE_BLOCKS_PER_STEP, nbp)
    rows = ATT_HEADS * nq
    grows = ATT_GROUP * nq

    def page_spec(slot):
        return pl.BlockSpec((1, PAGE_SIZE, ATT_KV_COLS),
                            lambda b, j, pt: (pt[b * n_pages + j * 2 * bps + slot], 0, 0))

    small = lambda shp: pl.BlockSpec((1,) + shp, lambda b, j, pt: (b,) + (0,) * len(shp))
    grid_spec = pltpu.PrefetchScalarGridSpec(
        num_scalar_prefetch=1,
        grid=(db, nbp // bps),
        in_specs=[page_spec(s) for s in range(2 * bps)] * 2
        + [small((rows, ATT_KV_COLS)), small((8, ATT_KV_COLS)), small((8, ATT_KV_COLS)),
           pl.BlockSpec((rows, 1), lambda b, j, pt: (0, 0))],
        out_specs=small((ATT_KV_HEADS, grows, ATT_HEAD)),
        scratch_shapes=[pltpu.VMEM((128, ATT_KV_COLS), F32), pltpu.VMEM((rows, 128), F32),
                        pltpu.VMEM((rows, 128), F32), pltpu.VMEM((nbp, rows, ATT_KV_COLS), F32)],
    )
    return pl.pallas_call(
        functools.partial(_moba_sample_body, nbp=nbp, bps=bps, nq=nq),
        grid_spec=grid_spec,
        out_shape=jax.ShapeDtypeStruct((db, ATT_KV_HEADS, grows, ATT_HEAD), F32),
        compiler_params=_cparams(("parallel", "arbitrary")),
    )(page_table.reshape(-1), *([cache_k] * (2 * bps)), *([cache_v] * (2 * bps)), q_bd, k_new, v_new, slope_rows)


def _row(a):
    return a.reshape(1, -1).astype(F32)


def _even_params(i, w_in_ab, rwkv_mu, rwkv_w0, rwkv_w2, rwkv_a0, rwkv_a2, rwkv_g2, rwkv_k_k, rwkv_k_a, rwkv_r_k,
                 rwkv_ln_w, rwkv_ln_b, ssm_conv_w, ssm_conv_b, ssm_dt_bias, ssm_a_log, ssm_d, ssm_norm_w):
    w = w_in_ab[i]
    ssm0 = RW_COLS
    zeros = lambda n: jnp.zeros((D_MODEL, n), w.dtype)
    w_pack = jnp.concatenate([
        w[:, :RW_COLS],
        w[:, ssm0 + SSM_INNER + SSM_CONV_DIM:], zeros(PROJ_Z - PROJ_DT - SSM_HEADS),
        w[:, ssm0:ssm0 + SSM_INNER],
        w[:, ssm0 + SSM_INNER:ssm0 + SSM_INNER + SSM_CONV_DIM]], axis=1).astype(BF16)
    half = RW_LORA // 2
    zl = jnp.zeros((half, RW_DIM), F32)
    wl = jnp.concatenate([jnp.concatenate([rwkv_w2[i], zl], axis=1),
                          jnp.concatenate([zl, rwkv_a2[i]], axis=1)], axis=0)
    half_of = np.arange(128) // RW_HEAD
    bones2 = np.tile((half_of[:, None] == half_of[None, :]).astype(np.float32), (2, 1))
    rw = dict(mu=_row(rwkv_mu[i]), w0=_row(rwkv_w0[i]), a0=_row(rwkv_a0[i]), k_k=_row(rwkv_k_k[i]),
              k_a=_row(rwkv_k_a[i]), r_k=_row(rwkv_r_k[i]), ln_w=_row(rwkv_ln_w[i]), ln_b=_row(rwkv_ln_b[i]),
              wl=_cat3w(wl, 0), g2=_cat3w(rwkv_g2[i], 0), bones2=jnp.asarray(bones2, BF16))
    pad128 = lambda a: jnp.pad(_row(a), ((0, 0), (0, 128 - a.shape[-1])))
    expand = (np.arange(128)[:, None] == (np.arange(SSM_INNER) // SSM_HEAD)[None, :]).astype(np.float32)
    ssm = dict(conv_w=ssm_conv_w[i], conv_b=_row(ssm_conv_b[i]), dt_bias=pad128(ssm_dt_bias[i]),
               a_log=pad128(ssm_a_log[i]), d_full=_row(jnp.repeat(ssm_d[i], SSM_HEAD)), norm_w=_row(ssm_norm_w[i]),
               expand=jnp.asarray(expand), expand_t=jnp.asarray(expand.T))
    return w_pack, rw, ssm


def _mixer_ab(x, b, t, norm_g, w_pack, rw, ssm, w_out, shift0, wkv0, conv0, ssm0, tm):
    tm_in = 2 * tm if (b * t) % (2 * tm) == 0 else tm
    proj = norm_matmul(x, norm_g, w_pack, tm_in, SSM_CONV_DIM).reshape(b, t, PROJ_COLS)
    shift_new = proj[:, t - 1, :RW_COLS]
    conv_new = proj[:, t - (SSM_CONV - 1):, PROJ_XBC:]
    if t % SSD_CHUNK == 0:
        proj_rw, proj_ssm, tc = proj, proj, SSD_CHUNK
    else:
        proj_rw = jnp.pad(proj, ((0, 0), (0, 8 - t), (0, 0)))
        proj_ssm = jnp.pad(proj, ((0, 0), (0, SSD_CHUNK - t), (0, 0)))
        tc = 8
    nbb = next(n for n in (4, 2, 1) if b % n == 0)
    y_rw, wkv_new = rwkv7(proj_rw, shift0.reshape(b, 1, RW_COLS), wkv0, rw, nbb, tc, min(t, tc))
    conv0p = jnp.pad(conv0, ((0, 0), (8 - (SSM_CONV - 1), 0), (0, 0)))
    y_ssm, ssm_new = mamba2(proj_ssm, conv0p, ssm0.reshape(b, SSM_INNER, SSM_STATE), ssm, min(t, SSD_CHUNK))
    x = matmul_residual(x, [y_rw[:, :t].reshape(b * t, RW_DIM), y_ssm[:, :t].reshape(b * t, SSM_INNER)],
                        [w_out[:RW_DIM], w_out[RW_DIM:]], tm)
    return x, (shift_new, wkv_new, conv_new, ssm_new.reshape(b, SSM_HEADS, SSM_HEAD, SSM_STATE))


def kernel(x_prompt, x_sample, state_rwkv_shift, state_rwkv_wkv, state_ssm_conv, state_ssm, cache_k, cache_v, page_table, norm_mix, norm_ffn, norm_final, w_in_ab, rwkv_mu, rwkv_w0, rwkv_w2, rwkv_a0, rwkv_a2, rwkv_g2, rwkv_k_k, rwkv_k_a, rwkv_r_k, rwkv_ln_w, rwkv_ln_b, ssm_conv_w, ssm_conv_b, ssm_dt_bias, ssm_a_log, ssm_d, ssm_norm_w, w_out_ab, ffn_w_gate, ffn_w_up, ffn_w_down, attn_w_qkv, attn_w_o, moe_router, moe_w_gate, moe_w_up, moe_w_down):
    bp, tp, _ = x_prompt.shape
    db, ts, _ = x_sample.shape
    depth = norm_mix.shape[0]
    tm_p, tm_s = 512, db * ts
    xp = x_prompt.reshape(bp * tp, D_MODEL)
    xs = x_sample.reshape(db * ts, D_MODEL)
    slopes = _alibi_slopes()
    slf = jnp.asarray(_slope_features().reshape(ATT_KV_HEADS, ATT_GROUP, 32))
    slope_rows = jnp.asarray(np.repeat(slopes, ts).reshape(ATT_HEADS * ts, 1))
    st = {n: [] for n in ("p_shift", "p_wkv", "p_conv", "p_ssm", "p_k", "p_v",
                          "s_shift", "s_wkv", "s_conv", "s_ssm", "s_k", "s_v")}
    for l in range(depth):
        i = l // 2
        g_mix, g_ffn = _row(norm_mix[l]), _row(norm_ffn[l])
        if l % 2 == 0:
            w_pack, rw, ssm = _even_params(i, w_in_ab, rwkv_mu, rwkv_w0, rwkv_w2, rwkv_a0, rwkv_a2, rwkv_g2,
                                           rwkv_k_k, rwkv_k_a, rwkv_r_k, rwkv_ln_w, rwkv_ln_b, ssm_conv_w,
                                           ssm_conv_b, ssm_dt_bias, ssm_a_log, ssm_d, ssm_norm_w)
            w_out = w_out_ab[i].astype(BF16)
            xp, sp = _mixer_ab(xp, bp, tp, g_mix, w_pack, rw, ssm, w_out,
                               jnp.zeros((bp, RW_COLS), F32), jnp.zeros((bp, RW_HEADS, RW_HEAD, RW_HEAD), F32),
                               jnp.zeros((bp, SSM_CONV - 1, SSM_CONV_DIM), F32),
                               jnp.zeros((bp, SSM_HEADS, SSM_HEAD, SSM_STATE), F32), tm_p)
            xs, ss = _mixer_ab(xs, db, ts, g_mix, w_pack, rw, ssm, w_out, state_rwkv_shift[i], state_rwkv_wkv[i],
                               state_ssm_conv[i], state_ssm[i], tm_s)
            for pre, new in (("p", sp), ("s", ss)):
                for name, val in zip(("shift", "wkv", "conv", "ssm"), new):
                    st[f"{pre}_{name}"].append(val)
            wg, wu, wd = ffn_w_gate[i].astype(BF16), ffn_w_up[i].astype(BF16), ffn_w_down[i].astype(BF16)
            xp = ffn_swiglu(xp, g_ffn, wg, wu, wd, 1024, 256)
            xs = ffn_swiglu(xs, g_ffn, wg, wu, wd, tm_s, 256)
        else:
            w_qkv, w_o = attn_w_qkv[i].astype(BF16), attn_w_o[i].astype(BF16)
            qkv_p, kf, vf = qkv_proj(xp, g_mix, w_qkv, bp, tp, tm_p)
            nb = tp // MOBA_BLOCK
            means = block_means(qkv_p, bp * nb).reshape(bp, nb, ATT_KV_HEADS, ATT_HEAD).transpose(0, 2, 1, 3)
            k_p = qkv_p[:, ATT_Q_COLS:ATT_Q_COLS + ATT_KV_COLS].reshape(bp, tp, ATT_KV_HEADS, ATT_HEAD)
            v_p = qkv_p[:, ATT_Q_COLS + ATT_KV_COLS:].reshape(bp, tp, ATT_KV_HEADS, ATT_HEAD)
            o_p = moba_prompt(qkv_p.reshape(bp, tp, -1), kf, vf, means, slf)
            xp = matmul_residual(xp, [o_p.reshape(bp * tp, ATT_Q_COLS)], [w_o], tm_p)
            st["p_k"].append(k_p)
            st["p_v"].append(v_p)

            qkv_s = norm_matmul(xs, g_mix, w_qkv, tm_s, 512).reshape(db, ts, -1)
            kn = qkv_s[..., ATT_Q_COLS:ATT_Q_COLS + ATT_KV_COLS].reshape(db, ts, ATT_KV_HEADS, ATT_HEAD)
            vn = qkv_s[..., ATT_Q_COLS + ATT_KV_COLS:].reshape(db, ts, ATT_KV_HEADS, ATT_HEAD)
            o_s = moba_sample(page_table, _pages2d(cache_k[i]), _pages2d(cache_v[i]),
                              _block_diag_queries(qkv_s[..., :ATT_Q_COLS]), _new_rows(kn), _new_rows(vn),
                              slope_rows, ts)
            o_s = o_s.reshape(db, ATT_HEADS, ts, ATT_HEAD).transpose(0, 2, 1, 3).reshape(db * ts, ATT_Q_COLS)
            xs = matmul_residual(xs, [o_s], [w_o], tm_s)
            st["s_k"].append(kn)
            st["s_v"].append(vn)

            router = jnp.pad(moe_router[i], ((0, 0), (0, 128 - N_EXPERTS)))
            wg, wu, wd = moe_w_gate[i].astype(BF16), moe_w_up[i].astype(BF16), moe_w_down[i].astype(BF16)
            final = l == depth - 1
            xp = moe_swiglu(xp, g_ffn, router, wg, wu, wd, _row(norm_final), tm_p, final)
            xs = moe_swiglu(xs, g_ffn, router, wg, wu, wd, _row(norm_final), tm_s, final)
    y_prompt = xp.reshape(bp, tp, D_MODEL)
    y_sample = xs.reshape(db, ts, D_MODEL)
    stack = lambda n: jnp.stack(st[n])
    return (y_prompt, y_sample,
            stack("p_shift"), stack("p_wkv"), stack("p_conv"), stack("p_ssm"), stack("p_k"), stack("p_v"),
            stack("s_shift"), stack("s_wkv"), stack("s_conv"), stack("s_ssm"), stack("s_k"), stack("s_v"))
```

```python
import functools

import numpy as np
import jax
import jax.numpy as jnp
from jax import lax
from jax.experimental import pallas as pl
from jax.experimental.pallas import tpu as pltpu

F32 = jnp.float32
BF16 = jnp.bfloat16
HI = lax.Precision.HIGHEST
NEG_INF = float("-inf")

D_MODEL = 1024
NORM_EPS = 1e-6

RW_HEAD = 64
RW_HEADS = 8
RW_DIM = RW_HEADS * RW_HEAD
RW_LORA = 128
RW_GATE = 128
RW_COLS = 3 * RW_DIM + RW_LORA + RW_GATE
RW_LN_EPS = 64e-5

SSM_INNER = 1024
SSM_HEAD = 64
SSM_HEADS = 16
SSM_GROUPS = 2
SSM_STATE = 128
SSM_CONV = 4
SSM_CONV_DIM = SSM_INNER + 2 * SSM_GROUPS * SSM_STATE
SSD_CHUNK = 128

PROJ_RW = 0
PROJ_DT = RW_COLS
PROJ_Z = 2048
PROJ_XBC = 3072
PROJ_COLS = PROJ_XBC + SSM_CONV_DIM

ATT_HEADS = 16
ATT_KV_HEADS = 4
ATT_HEAD = 64
ATT_GROUP = ATT_HEADS // ATT_KV_HEADS
ATT_Q_COLS = ATT_HEADS * ATT_HEAD
ATT_KV_COLS = ATT_KV_HEADS * ATT_HEAD
MOBA_BLOCK = 256
MOBA_TOPK = 3
MASK_NEG = -1e30
PAGE_SIZE = 128

N_EXPERTS = 8

VMEM_LIMIT = 56 * 1024 * 1024


def _cparams(sem):
    return pltpu.CompilerParams(dimension_semantics=sem, vmem_limit_bytes=VMEM_LIMIT)


def _nt(a, b, precision=None):
    return lax.dot_general(a, b, (((1,), (1,)), ((), ())), precision=precision,
                           preferred_element_type=F32)


def _dot(a, b, precision=None):
    return jnp.dot(a, b, precision=precision, preferred_element_type=F32)


def _split2(x):
    hi = x.astype(BF16)
    return hi, (x - hi.astype(F32)).astype(BF16)


def _cat3(x, axis):
    hi, lo = _split2(x)
    return jnp.concatenate([hi, lo, hi], axis=axis)


def _cat3w(w, axis):
    hi, lo = _split2(w)
    return jnp.concatenate([hi, hi, lo], axis=axis)


def _rms(x, g):
    return x * lax.rsqrt(jnp.mean(x * x, axis=-1, keepdims=True) + NORM_EPS) * g


def _softplus(x):
    return jnp.maximum(x, 0.0) + jnp.log(1.0 + jnp.exp(-jnp.abs(x)))


def _sigmoid(x):
    return 1.0 / (1.0 + jnp.exp(-x))


def _silu(x):
    return x * _sigmoid(x)


def _nm_body(x_ref, g_ref, w_ref, o_ref, h_sc):
    @pl.when(pl.program_id(1) == 0)
    def _():
        h_sc[...] = _rms(x_ref[...], g_ref[...]).astype(BF16)

    o_ref[...] = _dot(h_sc[...], w_ref[...])


def norm_matmul(x, g, w, tm, tn):
    m, k = x.shape
    n = w.shape[1]
    return pl.pallas_call(
        _nm_body,
        grid=(m // tm, n // tn),
        in_specs=[pl.BlockSpec((tm, k), lambda i, j: (i, 0)),
                  pl.BlockSpec((1, k), lambda i, j: (0, 0)),
                  pl.BlockSpec((k, tn), lambda i, j: (0, j))],
        out_specs=pl.BlockSpec((tm, tn), lambda i, j: (i, j)),
        out_shape=jax.ShapeDtypeStruct((m, n), F32),
        scratch_shapes=[pltpu.VMEM((tm, k), BF16)],
        compiler_params=_cparams(("parallel", "arbitrary")),
    )(x, g, w)


def _mmres_body(*refs, n):
    res_ref, a_refs, w_refs, o_ref = refs[0], refs[1:1 + n], refs[1 + n:1 + 2 * n], refs[1 + 2 * n]
    acc = res_ref[...]
    for a_ref, w_ref in zip(a_refs, w_refs):
        acc = acc + _dot(a_ref[...].astype(BF16), w_ref[...])
    o_ref[...] = acc


def matmul_residual(res, acts, ws, tm):
    m, d = res.shape
    n = len(acts)
    in_specs = [pl.BlockSpec((tm, d), lambda i: (i, 0))]
    in_specs += [pl.BlockSpec((tm, a.shape[1]), lambda i: (i, 0)) for a in acts]
    in_specs += [pl.BlockSpec(w.shape, lambda i: (0, 0)) for w in ws]
    return pl.pallas_call(
        functools.partial(_mmres_body, n=n),
        grid=(m // tm,),
        in_specs=in_specs,
        out_specs=pl.BlockSpec((tm, d), lambda i: (i, 0)),
        out_shape=jax.ShapeDtypeStruct((m, d), F32),
        compiler_params=_cparams(("parallel",)),
    )(res, *acts, *ws)


def _ffn_body(x_ref, g_ref, wg_ref, wu_ref, wd_ref, o_ref, h_sc, acc_sc):
    j = pl.program_id(1)

    @pl.when(j == 0)
    def _():
        h_sc[...] = _rms(x_ref[...], g_ref[...]).astype(BF16)
        acc_sc[...] = jnp.zeros_like(acc_sc)

    h = h_sc[...]
    act = _silu(_dot(h, wg_ref[...])) * _dot(h, wu_ref[...])
    acc_sc[...] += _dot(act.astype(BF16), wd_ref[...])

    @pl.when(j == pl.num_programs(1) - 1)
    def _():
        o_ref[...] = x_ref[...] + acc_sc[...]


def ffn_swiglu(x, g, wg, wu, wd, tm, tf):
    m, d = x.shape
    f = wg.shape[1]
    return pl.pallas_call(
        _ffn_body,
        grid=(m // tm, f // tf),
        in_specs=[pl.BlockSpec((tm, d), lambda i, j: (i, 0)),
                  pl.BlockSpec((1, d), lambda i, j: (0, 0)),
                  pl.BlockSpec((d, tf), lambda i, j: (0, j)),
                  pl.BlockSpec((d, tf), lambda i, j: (0, j)),
                  pl.BlockSpec((tf, d), lambda i, j: (j, 0))],
        out_specs=pl.BlockSpec((tm, d), lambda i, j: (i, 0)),
        out_shape=jax.ShapeDtypeStruct((m, d), F32),
        scratch_shapes=[pltpu.VMEM((tm, d), BF16), pltpu.VMEM((tm, d), F32)],
        compiler_params=_cparams(("parallel", "arbitrary")),
    )(x, g, wg, wu, wd)


MOE_SUB = 512
MOE_CHUNK = 160


def _moe_body(x_ref, g_ref, r_ref, wg_ref, wu_ref, wd_ref, gf_ref, o_ref, h_sc, comb_sc, acc_sc, rk_sc, rkt_sc, *,
              final_norm, chunk):
    e = pl.program_id(1)
    tm = x_ref.shape[0]
    lane = lax.broadcasted_iota(jnp.int32, (tm, 128), 1)

    @pl.when(e == 0)
    def _():
        hf = _rms(x_ref[...], g_ref[...])
        h_sc[...] = hf.astype(BF16)
        lanef = lane.astype(F32)
        logits = jnp.where(lane < N_EXPERTS, _dot(_cat3(hf, 1), r_ref[...]), NEG_INF)
        m1 = jnp.max(logits, axis=1, keepdims=True)
        i1 = jnp.min(jnp.where(logits == m1, lanef, 128.0), axis=1, keepdims=True)
        mask1 = lanef == i1
        rest = jnp.where(mask1, NEG_INF, logits)
        m2 = jnp.max(rest, axis=1, keepdims=True)
        i2 = jnp.min(jnp.where(rest == m2, lanef, 128.0), axis=1, keepdims=True)
        mask2 = lanef == i2
        e2 = jnp.exp(m2 - m1)
        den = 1.0 + e2
        comb_sc[...] = jnp.where(mask1, 1.0 / den, 0.0) + jnp.where(mask2, e2 / den, 0.0)
        acc_sc[...] = jnp.zeros_like(acc_sc)
        if chunk:
            earlier = (lax.broadcasted_iota(jnp.int32, (MOE_SUB, MOE_SUB), 1)
                       < lax.broadcasted_iota(jnp.int32, (MOE_SUB, MOE_SUB), 0))
            earlier = jnp.where(earlier, 1.0, 0.0).astype(BF16)
            routed = jnp.where(mask1 | mask2, 1.0, 0.0)
            for st in range(tm // MOE_SUB):
                sl = slice(st * MOE_SUB, (st + 1) * MOE_SUB)
                rank = _dot(earlier, routed[sl].astype(BF16))
                rank = jnp.where(routed[sl] > 0.0, rank, -1.0)
                rk_sc[sl, :] = rank
                rkt_sc[st] = rank.T

    def expert(rows):
        act = _silu(_dot(rows, wg_ref[0])) * _dot(rows, wu_ref[0])
        return _dot(act.astype(BF16), wd_ref[0])

    this = lane == e
    if not chunk:
        c = jnp.sum(jnp.where(this, comb_sc[...], 0.0), axis=1, keepdims=True)
        acc_sc[...] += c * expert(h_sc[...])
    else:
        wide = -(-chunk // 128) * 128
        slot = lax.broadcasted_iota(jnp.int32, (chunk, MOE_SUB), 0).astype(F32)
        lane_w = lax.broadcasted_iota(jnp.int32, (MOE_SUB, wide), 1)
        slot_t = jnp.where(lane_w < chunk, lane_w.astype(F32), jnp.nan)
        sub8 = lax.broadcasted_iota(jnp.int32, (8, MOE_SUB), 0)
        this_sub = lax.broadcasted_iota(jnp.int32, (MOE_SUB, 128), 1) == e
        pad_rows = jnp.zeros((wide - chunk, x_ref.shape[1]), BF16)
        for st in range(tm // MOE_SUB):
            sl = slice(st * MOE_SUB, (st + 1) * MOE_SUB)
            rank_col = jnp.sum(jnp.where(this_sub, rk_sc[sl, :], 0.0), axis=1, keepdims=True)
            rank_row = jnp.sum(jnp.where(sub8 == e, rkt_sc[st, 0:8, :], 0.0), axis=0, keepdims=True)
            c = jnp.sum(jnp.where(this_sub, comb_sc[sl, :], 0.0), axis=1, keepdims=True)
            count = jnp.max(rank_row, axis=1, keepdims=True)[0, 0].astype(jnp.int32) + 1

            def one_chunk(k, carry):
                base = (k * chunk).astype(F32)
                gather = jnp.where(rank_row - base == slot, 1.0, 0.0).astype(BF16)
                y_hi, y_lo = _split2(expert(_dot(gather, h_sc[sl, :]).astype(BF16)))
                scatter = jnp.where(rank_col - base == slot_t, 1.0, 0.0).astype(BF16)
                back = (_dot(scatter, jnp.concatenate([y_hi, pad_rows], axis=0))
                        + _dot(scatter, jnp.concatenate([y_lo, pad_rows], axis=0)))
                acc_sc[sl, :] += c * back
                return carry

            lax.fori_loop(0, (count + chunk - 1) // chunk, one_chunk, 0)

    @pl.when(e == pl.num_programs(1) - 1)
    def _():
        y = x_ref[...] + acc_sc[...]
        if final_norm:
            y = _rms(y, gf_ref[...])
        o_ref[...] = y


def moe_swiglu(x, g, router, wg, wu, wd, gfinal, tm, final_norm):
    m, d = x.shape
    ne, _, fe = wg.shape
    chunk = MOE_CHUNK if tm % MOE_SUB == 0 else 0
    nsub = max(tm // MOE_SUB, 1)
    return pl.pallas_call(
        functools.partial(_moe_body, final_norm=final_norm, chunk=chunk),
        grid=(m // tm, ne),
        in_specs=[pl.BlockSpec((tm, d), lambda i, e: (i, 0)),
                  pl.BlockSpec((1, d), lambda i, e: (0, 0)),
                  pl.BlockSpec((3 * d, 128), lambda i, e: (0, 0)),
                  pl.BlockSpec((1, d, fe), lambda i, e: (e, 0, 0)),
                  pl.BlockSpec((1, d, fe), lambda i, e: (e, 0, 0)),
                  pl.BlockSpec((1, fe, d), lambda i, e: (e, 0, 0)),
                  pl.BlockSpec((1, d), lambda i, e: (0, 0))],
        out_specs=pl.BlockSpec((tm, d), lambda i, e: (i, 0)),
        out_shape=jax.ShapeDtypeStruct((m, d), F32),
        scratch_shapes=[pltpu.VMEM((tm, d), BF16), pltpu.VMEM((tm, 128), F32), pltpu.VMEM((tm, d), F32),
                        pltpu.VMEM((tm, 128), F32), pltpu.VMEM((nsub, 128, min(tm, MOE_SUB)), F32)],
        compiler_params=_cparams(("parallel", "arbitrary")),
    )(x, g, router, wg, wu, wd, gfinal)


RW_PAIRS = RW_HEADS // 2
RW_ROWS = RW_PAIRS * RW_HEAD


def _rwkv_body(p_ref, sh0_ref, s0_ref, mu_ref, w0_ref, a0_ref, kk_ref, ka_ref, rk_ref, lnw_ref, lnb_ref,
               wl_ref, g2_ref, b2_ref, pw_ref, y_ref, sfin_ref,
               s_sc, sk_sc, prev_sc, kk_sc, w_sc, b_sc, k_sc, r_sc, v_sc, g_sc, y_sc, *, nbb, tc, t_valid):
    c = pl.program_id(1)

    @pl.when(c == 0)
    def _():
        s_sc[...] = s0_ref[...]
        prev_sc[...] = sh0_ref[...]

    b2 = b2_ref[...]

    def head_sums(x, low=True):
        if low:
            hi, lo = _split2(x)
        else:
            hi = x.astype(BF16)
            lo = jnp.zeros_like(hi)
        return _dot(jnp.concatenate([hi, lo], axis=1), b2)

    def head_sums_wide(x):
        return jnp.concatenate([head_sums(x[:, q * 128:(q + 1) * 128]) for q in range(RW_PAIRS)], axis=1)

    for bi in range(nbb):
        p = p_ref[bi]
        row = lax.broadcasted_iota(jnp.int32, p.shape, 0)
        prev = jnp.where(row == 0, prev_sc[bi], pltpu.roll(p, 1, axis=0))
        prev_sc[bi] = p[tc - 1:tc, :]
        u = p + (prev - p) * mu_ref[...]
        r = u[:, 0:RW_DIM]
        k = u[:, RW_DIM:2 * RW_DIM]
        v = u[:, 2 * RW_DIM:3 * RW_DIM]
        lo = u[:, 3 * RW_DIM:3 * RW_DIM + RW_LORA]
        gd = u[:, 3 * RW_DIM + RW_LORA:]
        lane = lax.broadcasted_iota(jnp.int32, lo.shape, 1)
        lora = _dot(_cat3(jnp.where(lane < RW_LORA // 2, jnp.tanh(lo), lo), 1), wl_ref[...])
        wlog = -_softplus(-(w0_ref[...] + lora[:, :RW_DIM])) - 0.5
        a = _sigmoid(a0_ref[...] + lora[:, RW_DIM:])
        g_sc[bi] = _dot(_cat3(_sigmoid(gd), 1), g2_ref[...])
        kk = k * kk_ref[...]
        kk = kk / jnp.maximum(jnp.sqrt(head_sums_wide(kk * kk)), 1e-12)
        kk_sc[bi] = kk
        w_sc[bi] = jnp.exp(-jnp.exp(wlog))
        b_sc[bi] = kk * a
        k_sc[bi] = k * (1.0 + (a - 1.0) * ka_ref[...])
        r_sc[bi] = r
        v_sc[bi] = v

    rows = nbb * RW_ROWS
    vi = lax.broadcasted_iota(jnp.int32, (rows, 128), 0) % RW_HEAD
    li = lax.broadcasted_iota(jnp.int32, (rows, 128), 1)
    diag = (li % RW_HEAD == vi).astype(F32)
    row8 = lax.broadcasted_iota(jnp.int32, (8, RW_DIM), 0)
    steps = 8 if t_valid % 8 == 0 else t_valid

    def rows_of(x8, j):
        return jnp.concatenate([jnp.broadcast_to(x8[bi][j:j + 1, q * 128:(q + 1) * 128], (RW_HEAD, 128))
                                for bi in range(nbb) for q in range(RW_PAIRS)], axis=0)

    pair_w = pw_ref[...]
    zero_half = jnp.zeros((rows, 128), BF16)

    def pair_sums(a, b):
        lhs = jnp.concatenate([a.astype(BF16), zero_half if b is None else b.astype(BF16)], axis=1)
        out = _dot(lhs, pair_w)
        return out[:, :128], out[:, 128:]

    first8 = [kk_sc[bi, pl.ds(0, 8), :] for bi in range(nbb)]
    sk_sc[...] = pair_sums(s_sc[...].reshape(rows, 128) * rows_of(first8, 0), None)[0]

    def group(t8, carry):
        t0 = pl.multiple_of(t8 * 8, 8)
        tn = pl.multiple_of(jnp.minimum(t0 + 8, tc - 8), 8)
        kk8, w8, b8, k8, r8, v8 = ([sc[bi, pl.ds(t0, 8), :] for bi in range(nbb)]
                                   for sc in (kk_sc, w_sc, b_sc, k_sc, r_sc, v_sc))
        kk_next = [kk_sc[bi, pl.ds(tn, 8), :] for bi in range(nbb)]
        s = s_sc[...].reshape(rows, 128)
        sk = sk_sc[...]
        y8 = [jnp.zeros((8, RW_DIM), F32) for _ in range(nbb)]
        for j in range(steps):
            if j % 2 == 0:
                vcols = pair_sums(rows_of(v8, j) * diag, rows_of(v8, j + 1) * diag if j + 1 < steps else None)
            s = s * rows_of(w8, j) - sk * rows_of(b8, j) + vcols[j % 2] * rows_of(k8, j)
            kk_after = rows_of(kk8, j + 1) if j + 1 < 8 else rows_of(kk_next, 0)
            sk, yb = pair_sums(s * kk_after, s * rows_of(r8, j))
            yd = yb * diag
            for bi in range(nbb):
                yrow = jnp.concatenate(
                    [jnp.sum(yd[(bi * RW_PAIRS + q) * RW_HEAD:(bi * RW_PAIRS + q + 1) * RW_HEAD], axis=0, keepdims=True)
                     for q in range(RW_PAIRS)], axis=1)
                y8[bi] = jnp.where(row8 == j, yrow, y8[bi])
        s_sc[...] = s.reshape(nbb, RW_ROWS, 128)
        sk_sc[...] = sk
        for bi in range(nbb):
            y_sc[bi, pl.ds(t0, 8), :] = y8[bi]
        return carry

    lax.fori_loop(0, -(-t_valid // 8), group, 0)

    inv = 1.0 / RW_HEAD
    for bi in range(nbb):
        y = y_sc[bi]
        yc = y - head_sums_wide(y) * inv
        var = head_sums_wide(yc * yc) * inv
        yn = yc * lax.rsqrt(var + RW_LN_EPS) * lnw_ref[...] + lnb_ref[...]
        bonus = head_sums_wide(r_sc[bi] * k_sc[bi] * rk_ref[...]) * v_sc[bi]
        y_ref[bi] = (yn + bonus) * g_sc[bi]

    @pl.when(c == pl.num_programs(1) - 1)
    def _():
        sfin_ref[...] = s_sc[...]


def _wkv_to_rows(s):
    b = s.shape[0]
    return s.reshape(b, RW_PAIRS, 2, RW_HEAD, RW_HEAD).transpose(0, 1, 3, 2, 4).reshape(b, RW_ROWS, 128)


def _rows_to_wkv(s):
    b = s.shape[0]
    return s.reshape(b, RW_PAIRS, RW_HEAD, 2, RW_HEAD).transpose(0, 1, 3, 2, 4).reshape(b, RW_HEADS, RW_HEAD, RW_HEAD)


def rwkv7(proj, shift0, wkv0, prm, nbb, tc, t_valid):
    b, t, _ = proj.shape
    vec = lambda n: pl.BlockSpec((1, n), lambda i, c: (0, 0))
    full = lambda a: pl.BlockSpec(a.shape, lambda i, c: (0,) * a.ndim)
    seq = pltpu.VMEM((nbb, tc, RW_DIM), F32)
    y, s_fin = pl.pallas_call(
        functools.partial(_rwkv_body, nbb=nbb, tc=tc, t_valid=t_valid),
        grid=(b // nbb, t // tc),
        in_specs=[pl.BlockSpec((nbb, tc, RW_COLS), lambda i, c: (i, c, 0)),
                  pl.BlockSpec((nbb, 1, RW_COLS), lambda i, c: (i, 0, 0)),
                  pl.BlockSpec((nbb, RW_ROWS, 128), lambda i, c: (i, 0, 0)),
                  vec(RW_COLS), vec(RW_DIM), vec(RW_DIM), vec(RW_DIM), vec(RW_DIM), vec(RW_DIM), vec(RW_DIM),
                  vec(RW_DIM), full(prm["wl"]), full(prm["g2"]), full(prm["bones2"]), full(prm["pair_w"])],
        out_specs=[pl.BlockSpec((nbb, tc, RW_DIM), lambda i, c: (i, c, 0)),
                   pl.BlockSpec((nbb, RW_ROWS, 128), lambda i, c: (i, 0, 0))],
        out_shape=[jax.ShapeDtypeStruct((b, t, RW_DIM), F32),
                   jax.ShapeDtypeStruct((b, RW_ROWS, 128), F32)],
        scratch_shapes=[pltpu.VMEM((nbb, RW_ROWS, 128), F32), pltpu.VMEM((nbb * RW_ROWS, 128), F32),
                        pltpu.VMEM((nbb, 1, RW_COLS), F32),
                        seq, seq, seq, seq, seq, seq, seq, seq],
        compiler_params=_cparams(("parallel", "arbitrary")),
    )(proj, shift0, _wkv_to_rows(wkv0), prm["mu"], prm["w0"], prm["a0"], prm["k_k"], prm["k_a"], prm["r_k"],
      prm["ln_w"], prm["ln_b"], prm["wl"], prm["g2"], prm["bones2"], prm["pair_w"])
    return y, _rows_to_wkv(s_fin)


def _ssd_body(z_ref, xbc_ref, dt_ref, conv0_ref, h0_ref, cw_ref, cb_ref, dtb_ref, alog_ref, dsk_ref, nw_ref,
              e_ref, et_ref, y_ref, hfin_ref, h_sc, ext_sc, *, t_valid):
    c = pl.program_id(1)
    ln = SSD_CHUNK

    @pl.when(c == 0)
    def _():
        h_sc[...] = h0_ref[0]
        ext_sc[0:8, :] = conv0_ref[0]

    xbc = xbc_ref[0]
    ext_sc[8:8 + ln, :] = xbc
    cw = cw_ref[...]
    conv = cb_ref[...] + ext_sc[5:5 + ln, :] * cw[0:1]
    conv = conv + ext_sc[6:6 + ln, :] * cw[1:2]
    conv = conv + ext_sc[7:7 + ln, :] * cw[2:3]
    conv = conv + xbc * cw[3:4]
    ext_sc[0:8, :] = xbc[ln - 8:ln, :]
    act = _silu(conv)
    xs = act[:, :SSM_INNER]
    bm = act[:, SSM_INNER:SSM_INNER + SSM_GROUPS * SSM_STATE].astype(BF16)
    cm = act[:, SSM_INNER + SSM_GROUPS * SSM_STATE:].astype(BF16)

    lane = lax.broadcasted_iota(jnp.int32, (ln, 128), 1)
    rowi = lax.broadcasted_iota(jnp.int32, (ln, 128), 0)
    dt = _softplus(dt_ref[0] + dtb_ref[...])
    if t_valid < ln:
        dt = jnp.where(rowi < t_valid, dt, 0.0)
    a = dt * jnp.where(lane[0:1] < SSM_HEADS, -jnp.exp(alog_ref[...]), 0.0)
    causal = lane <= rowi
    tri = causal.astype(F32)
    eye = (lane == rowi).astype(F32)
    acum = _dot(tri, a, HI)
    acum_t = _nt(_nt(eye, a, HI), tri, HI)
    e = e_ref[...]
    dt_full = _dot(dt, e, HI)
    ac_full = _dot(acum, e, HI)
    xdt = xs * dt_full
    xw = xdt * jnp.exp(ac_full[ln - 1:ln, :] - ac_full)
    eac = jnp.exp(ac_full)
    cd = jnp.broadcast_to(jnp.exp(acum_t[:, ln - 1:ln]), (128, 128))
    cd_rows = _dot(et_ref[...], cd, HI)

    ys = []
    for g in range(SSM_GROUPS):
        bm_g = bm[:, g * SSM_STATE:(g + 1) * SSM_STATE]
        cm_g = cm[:, g * SSM_STATE:(g + 1) * SSM_STATE]
        cbm = _nt(cm_g, bm_g)
        pairs = SSM_HEADS // SSM_GROUPS // 2
        for q in range(g * pairs, (g + 1) * pairs):
            sl = slice(q * 128, (q + 1) * 128)
            xdt_p = xdt[:, sl].astype(BF16)
            yd = []
            for h in (2 * q, 2 * q + 1):
                seg = jnp.where(causal, acum[:, h:h + 1] - acum_t[h:h + 1, :], NEG_INF)
                yd.append(_dot((cbm * jnp.exp(seg)).astype(BF16), xdt_p))
            hp = h_sc[sl, :]
            y_off = _nt(cm_g, hp.astype(BF16)) * eac[:, sl]
            st = _dot(xw[:, sl].T.astype(BF16), bm_g)
            h_sc[sl, :] = cd_rows[sl, :] * hp + st
            ys.append(jnp.where(lane < SSM_HEAD, yd[0], yd[1]) + y_off)
    y = jnp.concatenate(ys, axis=1) + dsk_ref[...] * xs
    y = y * _silu(z_ref[0])
    gw = SSM_INNER // SSM_GROUPS
    outs = []
    for g in range(SSM_GROUPS):
        yg = y[:, g * gw:(g + 1) * gw]
        outs.append(yg * lax.rsqrt(jnp.mean(yg * yg, axis=1, keepdims=True) + NORM_EPS))
    y_ref[0] = jnp.concatenate(outs, axis=1) * nw_ref[...]

    @pl.when(c == pl.num_programs(1) - 1)
    def _():
        hfin_ref[0] = h_sc[...]


def mamba2(proj, conv0, ssm0, prm, t_valid):
    b, t, _ = proj.shape
    ln = SSD_CHUNK
    vec = lambda n: pl.BlockSpec((1, n), lambda i, c: (0, 0))
    full = lambda a: pl.BlockSpec(a.shape, lambda i, c: (0,) * a.ndim)
    return pl.pallas_call(
        functools.partial(_ssd_body, t_valid=t_valid),
        grid=(b, t // ln),
        in_specs=[pl.BlockSpec((1, ln, SSM_INNER), lambda i, c: (i, c, PROJ_Z // SSM_INNER)),
                  pl.BlockSpec((1, ln, SSM_CONV_DIM), lambda i, c: (i, c, PROJ_XBC // SSM_CONV_DIM)),
                  pl.BlockSpec((1, ln, 128), lambda i, c: (i, c, PROJ_DT // 128)),
                  pl.BlockSpec((1, 8, SSM_CONV_DIM), lambda i, c: (i, 0, 0)),
                  pl.BlockSpec((1, SSM_INNER, SSM_STATE), lambda i, c: (i, 0, 0)),
                  full(prm["conv_w"]), vec(SSM_CONV_DIM), vec(128), vec(128), vec(SSM_INNER), vec(SSM_INNER),
                  full(prm["expand"]), full(prm["expand_t"])],
        out_specs=[pl.BlockSpec((1, ln, SSM_INNER), lambda i, c: (i, c, 0)),
                   pl.BlockSpec((1, SSM_INNER, SSM_STATE), lambda i, c: (i, 0, 0))],
        out_shape=[jax.ShapeDtypeStruct((b, t, SSM_INNER), F32),
                   jax.ShapeDtypeStruct((b, SSM_INNER, SSM_STATE), F32)],
        scratch_shapes=[pltpu.VMEM((SSM_INNER, SSM_STATE), F32), pltpu.VMEM((8 + ln, SSM_CONV_DIM), F32)],
        compiler_params=_cparams(("parallel", "arbitrary")),
    )(proj, proj, proj, conv0, ssm0, prm["conv_w"], prm["conv_b"], prm["dt_bias"], prm["a_log"],
      prm["d_full"], prm["norm_w"], prm["expand"], prm["expand_t"])


def _alibi_slopes():
    return (2.0 ** (-8.0 * np.arange(1, ATT_HEADS + 1) / ATT_HEADS)).astype(np.float32)


def _slope_features():
    s = _alibi_slopes()
    bf = lambda a: a.astype(jnp.bfloat16).astype(np.float32)
    hi = bf(s)
    mid = bf(s - hi)
    lo = bf(s - hi - mid)
    out = np.zeros((ATT_HEADS, 32), np.float32)
    for i, piece in enumerate((hi, mid, lo)):
        out[:, i] = piece
        out[:, 3 + i] = piece
    return out


def _top3_threshold(gate):
    v = gate
    for _ in range(MOBA_TOPK - 1):
        v = jnp.where(v == jnp.max(v, axis=1, keepdims=True), NEG_INF, v)
    return jnp.max(v, axis=1, keepdims=True)


def _blockmean_body(k_ref, o_ref):
    o_ref[0] = jnp.mean(k_ref[...], axis=0, keepdims=True)


def block_means(qkv, nblocks):
    return pl.pallas_call(
        _blockmean_body,
        grid=(nblocks,),
        in_specs=[pl.BlockSpec((MOBA_BLOCK, ATT_KV_COLS), lambda i: (i, ATT_Q_COLS // ATT_KV_COLS))],
        out_specs=pl.BlockSpec((1, 1, ATT_KV_COLS), lambda i: (i, 0, 0)),
        out_shape=jax.ShapeDtypeStruct((nblocks, 1, ATT_KV_COLS), F32),
        compiler_params=_cparams(("parallel",)),
    )(qkv)


def _qkv_body(x_ref, g_ref, w_ref, qkv_ref, kf_ref, vf_ref, *, tiles_per_seq):
    tm = x_ref.shape[0]
    qkv = _dot(_rms(x_ref[...], g_ref[...]).astype(BF16), w_ref[...])
    qkv_ref[...] = qkv
    pos = (pl.program_id(0) % tiles_per_seq) * tm + lax.broadcasted_iota(jnp.int32, (tm, ATT_HEAD), 0)
    lane = lax.broadcasted_iota(jnp.int32, (tm, ATT_HEAD), 1)
    blk_id, off = pos // MOBA_BLOCK, pos % MOBA_BLOCK
    kfeat = jnp.where(lane == blk_id, 1, 0)
    kfeat = jnp.where((lane >= 32) & (lane < 35), blk_id * MOBA_BLOCK, kfeat)
    kfeat = jnp.where((lane >= 35) & (lane < 38), off, kfeat).astype(F32).astype(BF16)
    vfeat = jnp.where(lane == 0, 1.0, 0.0).astype(BF16)
    for g in range(ATT_KV_HEADS):
        k0 = ATT_Q_COLS + g * ATT_HEAD
        v0 = ATT_Q_COLS + ATT_KV_COLS + g * ATT_HEAD
        kf_ref[0, g] = jnp.concatenate([qkv[:, k0:k0 + ATT_HEAD].astype(BF16), kfeat], axis=1)
        vf_ref[0, g] = jnp.concatenate([qkv[:, v0:v0 + ATT_HEAD].astype(BF16), vfeat], axis=1)


def qkv_proj(x, g, w, b, t, tm):
    m, d = x.shape
    n = w.shape[1]
    tps = t // tm
    ext = jax.ShapeDtypeStruct((b, ATT_KV_HEADS, t, 128), BF16)
    ext_spec = pl.BlockSpec((1, ATT_KV_HEADS, tm, 128), lambda i: (i // tps, 0, i % tps, 0))
    return pl.pallas_call(
        functools.partial(_qkv_body, tiles_per_seq=tps),
        grid=(m // tm,),
        in_specs=[pl.BlockSpec((tm, d), lambda i: (i, 0)), pl.BlockSpec((1, d), lambda i: (0, 0)),
                  pl.BlockSpec((d, n), lambda i: (0, 0))],
        out_specs=[pl.BlockSpec((tm, n), lambda i: (i, 0)), ext_spec, ext_spec],
        out_shape=[jax.ShapeDtypeStruct((m, n), F32), ext, ext],
        compiler_params=_cparams(("parallel",)),
    )(x, g, w)


def _moba_prompt_body(q_ref, kf_ref, vf_ref, mean_ref, slf_ref, o_ref, qf_sc, m_sc, acc_sc, s0_sc, s1_sc):
    qb = pl.program_id(2)
    blk = MOBA_BLOCK
    rows = ATT_GROUP * blk
    scale = ATT_HEAD ** -0.5

    q4 = q_ref[0]
    qg = jnp.concatenate([q4[:, hh * ATT_HEAD:(hh + 1) * ATT_HEAD] for hh in range(ATT_GROUP)], axis=0)
    past_blk = lax.broadcasted_iota(jnp.int32, (128, rows), 0) < qb
    gate = jnp.where(past_blk, _nt(_cat3w(mean_ref[0, 0], 1), _cat3(qg, 1)), NEG_INF)
    v = gate
    for _ in range(MOBA_TOPK - 1):
        v = jnp.where(v == jnp.max(v, axis=0, keepdims=True), NEG_INF, v)
    thr = jnp.max(v, axis=0, keepdims=True)
    selneg = jnp.where(past_blk & (gate < thr), MASK_NEG, 0.0).T[:, :32]
    slf = jnp.concatenate([jnp.broadcast_to(slf_ref[0, hh:hh + 1, :], (blk, 32)) for hh in range(ATT_GROUP)], axis=0)
    qf_sc[...] = jnp.concatenate([(qg * scale).astype(BF16), selneg.astype(BF16), slf.astype(BF16)], axis=1)

    def kv_tiles(j):
        start = pl.multiple_of(j * blk, blk)
        return kf_ref[0, 0, pl.ds(start, blk), :], vf_ref[0, 0, pl.ds(start, blk), :]

    qi = lax.broadcasted_iota(jnp.int32, (rows, blk), 0) % blk
    ki = lax.broadcasted_iota(jnp.int32, (rows, blk), 1)
    kf, vf = kv_tiles(qb)
    s = jnp.where(ki <= qi, _nt(qf_sc[...], kf), NEG_INF)
    m = jnp.max(s, axis=1, keepdims=True)
    m_sc[...] = jnp.broadcast_to(m, (rows, 128))
    acc_sc[...] = _dot(jnp.exp(s - m).astype(BF16), vf)

    def fold(j, s_cur, s_next):
        if s_next is not None:
            s_next[...] = _nt(qf_sc[...], kv_tiles(jnp.minimum(j + 1, qb))[0])
        s = s_cur[...]
        m_old = m_sc[...]
        m_new = jnp.maximum(m_old, jnp.max(s, axis=1, keepdims=True))
        p = jnp.exp(s - jnp.concatenate([m_new, m_new], axis=1))
        acc_sc[...] = acc_sc[...] * jnp.exp(m_old - m_new) + _dot(p.astype(BF16), kv_tiles(j)[1])
        m_sc[...] = m_new

    s0_sc[...] = _nt(qf_sc[...], kv_tiles(0)[0])

    def pair(i, carry):
        fold(2 * i, s0_sc, s1_sc)
        fold(2 * i + 1, s1_sc, s0_sc)
        return carry

    lax.fori_loop(0, qb // 2, pair, 0)

    @pl.when(qb % 2 == 1)
    def _():
        fold(qb - 1, s0_sc, None)

    acc = acc_sc[...]
    out = acc[:, :ATT_HEAD] / acc[:, ATT_HEAD:ATT_HEAD + 1]
    o_ref[0] = jnp.concatenate([out[hh * blk:(hh + 1) * blk] for hh in range(ATT_GROUP)], axis=1)


def moba_prompt(qkv, kf, vf, means, slf):
    b, t, _ = qkv.shape
    nb = t // MOBA_BLOCK
    assert nb <= 32
    nbl = 128
    means = jnp.pad(means, ((0, 0), (0, 0), (0, nbl - nb), (0, 0)))
    rows = ATT_GROUP * MOBA_BLOCK
    gw = ATT_GROUP * ATT_HEAD
    return pl.pallas_call(
        _moba_prompt_body,
        grid=(b, ATT_KV_HEADS, nb),
        in_specs=[pl.BlockSpec((1, MOBA_BLOCK, gw), lambda i, g, j: (i, j, g)),
                  pl.BlockSpec((1, 1, t, 128), lambda i, g, j: (i, g, 0, 0)),
                  pl.BlockSpec((1, 1, t, 128), lambda i, g, j: (i, g, 0, 0)),
                  pl.BlockSpec((1, 1, nbl, ATT_HEAD), lambda i, g, j: (i, g, 0, 0)),
                  pl.BlockSpec((1, ATT_GROUP, 32), lambda i, g, j: (g, 0, 0))],
        out_specs=pl.BlockSpec((1, MOBA_BLOCK, gw), lambda i, g, j: (i, j, g)),
        out_shape=jax.ShapeDtypeStruct((b, t, ATT_Q_COLS), F32),
        scratch_shapes=[pltpu.VMEM((rows, 128), BF16), pltpu.VMEM((rows, 128), F32), pltpu.VMEM((rows, 128), F32),
                        pltpu.VMEM((rows, MOBA_BLOCK), F32), pltpu.VMEM((rows, MOBA_BLOCK), F32)],
        compiler_params=_cparams(("parallel", "parallel", "arbitrary")),
    )(qkv, kf, vf, means, slf)


def _moba_sample_body(pt_ref, *refs, nbp, bps, nq):
    npg = 2 * bps
    k_refs, v_refs = refs[:npg], refs[npg:2 * npg]
    q_ref, kn_ref, vn_ref, slope_ref, o_ref, mean_sc, m_sc, l_sc, acc_sc = refs[2 * npg:]
    j = pl.program_id(1)
    blk = MOBA_BLOCK
    rows = ATT_HEADS * nq
    grows = ATT_GROUP * nq
    past = nbp * blk
    scale = ATT_HEAD ** -0.5
    lane = lax.broadcasted_iota(jnp.int32, (rows, 128), 1)
    qi = (lax.broadcasted_iota(jnp.int32, (rows, 1), 0) % nq).astype(F32)
    qf = q_ref[0]
    qs = (qf * scale).astype(BF16)
    slope = slope_ref[...]

    @pl.when(j == 0)
    def _():
        mean_sc[...] = jnp.zeros_like(mean_sc)
        m_sc[...] = jnp.full(m_sc.shape, NEG_INF, F32)
        l_sc[...] = jnp.zeros_like(l_sc)

    off = lax.broadcasted_iota(jnp.int32, (1, blk), 1).astype(F32)
    for s in range(bps):
        bi = j * bps + s
        kblk = jnp.concatenate([k_refs[2 * s][0], k_refs[2 * s + 1][0]], axis=0)
        vblk = jnp.concatenate([v_refs[2 * s][0], v_refs[2 * s + 1][0]], axis=0)
        mean_sc[pl.ds(bi, 1), :] = jnp.sum(kblk, axis=0, keepdims=True) * (1.0 / blk)
        sc = _nt(qs, kblk.astype(BF16)) - slope * ((past - bi * blk).astype(F32) + qi - off)
        m = jnp.max(sc, axis=1, keepdims=True)
        p = jnp.exp(sc - m)
        m_sc[...] = jnp.where(lane == bi, m, m_sc[...])
        l_sc[...] = jnp.where(lane == bi, jnp.sum(p, axis=1, keepdims=True), l_sc[...])
        acc_sc[bi] = _dot(p.astype(BF16), vblk.astype(BF16))

    @pl.when(j == pl.num_programs(1) - 1)
    def _():
        gate = jnp.where(lane < nbp, _nt(_cat3(qf, 1), _cat3w(mean_sc[...], 1)), NEG_INF)
        sel = gate >= _top3_threshold(gate)
        ki = lax.broadcasted_iota(jnp.int32, (rows, 8), 1).astype(F32)
        s_own = jnp.where(ki <= qi, _nt(qs, kn_ref[0].astype(BF16)) - slope * (qi - ki), NEG_INF)
        m_all = jnp.where(sel, m_sc[...], NEG_INF)
        mx = jnp.maximum(jnp.max(m_all, axis=1, keepdims=True), jnp.max(s_own, axis=1, keepdims=True))
        w = jnp.where(sel, jnp.exp(m_all - mx), 0.0)
        p_own = jnp.exp(s_own - mx)
        den = jnp.sum(w * l_sc[...], axis=1, keepdims=True) + jnp.sum(p_own, axis=1, keepdims=True)
        acc = _dot(p_own.astype(BF16), vn_ref[0].astype(BF16))
        for n in range(nbp):
            acc = acc + w[:, n:n + 1] * acc_sc[n]
        out = acc / den
        for g in range(ATT_KV_HEADS):
            o_ref[0, g] = out[g * grows:(g + 1) * grows, g * ATT_HEAD:(g + 1) * ATT_HEAD]


SAMPLE_BLOCKS_PER_STEP = 4


def _pages2d(cache):
    return cache.reshape(cache.shape[0], PAGE_SIZE, ATT_KV_COLS)


def _block_diag_queries(q):
    db, nq, _ = q.shape
    qh = q.reshape(db, nq, ATT_KV_HEADS, ATT_GROUP, ATT_HEAD).transpose(0, 2, 3, 1, 4)
    eye = jnp.eye(ATT_KV_HEADS, dtype=q.dtype)
    return (qh[:, :, :, :, None, :] * eye[None, :, None, None, :, None]).reshape(db, ATT_HEADS * nq, ATT_KV_COLS)


def _new_rows(a):
    db, nq = a.shape[:2]
    return jnp.pad(a.reshape(db, nq, ATT_KV_COLS), ((0, 0), (0, 8 - nq), (0, 0)))


def moba_sample(page_table, cache_k, cache_v, q_bd, k_new, v_new, slope_rows, nq):
    db, n_pages = page_table.shape
    nbp = n_pages * PAGE_SIZE // MOBA_BLOCK
    bps = min(SAMPLE_BLOCKS_PER_STEP, nbp)
    rows = ATT_HEADS * nq
    grows = ATT_GROUP * nq

    def page_spec(slot):
        return pl.BlockSpec((1, PAGE_SIZE, ATT_KV_COLS),
                            lambda b, j, pt: (pt[b * n_pages + j * 2 * bps + slot], 0, 0))

    small = lambda shp: pl.BlockSpec((1,) + shp, lambda b, j, pt: (b,) + (0,) * len(shp))
    grid_spec = pltpu.PrefetchScalarGridSpec(
        num_scalar_prefetch=1,
        grid=(db, nbp // bps),
        in_specs=[page_spec(s) for s in range(2 * bps)] * 2
        + [small((rows, ATT_KV_COLS)), small((8, ATT_KV_COLS)), small((8, ATT_KV_COLS)),
           pl.BlockSpec((rows, 1), lambda b, j, pt: (0, 0))],
        out_specs=small((ATT_KV_HEADS, grows, ATT_HEAD)),
        scratch_shapes=[pltpu.VMEM((128, ATT_KV_COLS), F32), pltpu.VMEM((rows, 128), F32),
                        pltpu.VMEM((rows, 128), F32), pltpu.VMEM((nbp, rows, ATT_KV_COLS), F32)],
    )
    return pl.pallas_call(
        functools.partial(_moba_sample_body, nbp=nbp, bps=bps, nq=nq),
        grid_spec=grid_spec,
        out_shape=jax.ShapeDtypeStruct((db, ATT_KV_HEADS, grows, ATT_HEAD), F32),
        compiler_params=_cparams(("parallel", "arbitrary")),
    )(page_table.reshape(-1), *([cache_k] * (2 * bps)), *([cache_v] * (2 * bps)), q_bd, k_new, v_new, slope_rows)


def _row(a):
    return a.reshape(1, -1).astype(F32)


def _even_params(i, w_in_ab, rwkv_mu, rwkv_w0, rwkv_w2, rwkv_a0, rwkv_a2, rwkv_g2, rwkv_k_k, rwkv_k_a, rwkv_r_k,
                 rwkv_ln_w, rwkv_ln_b, ssm_conv_w, ssm_conv_b, ssm_dt_bias, ssm_a_log, ssm_d, ssm_norm_w):
    w = w_in_ab[i]
    ssm0 = RW_COLS
    zeros = lambda n: jnp.zeros((D_MODEL, n), w.dtype)
    w_pack = jnp.concatenate([
        w[:, :RW_COLS],
        w[:, ssm0 + SSM_INNER + SSM_CONV_DIM:], zeros(PROJ_Z - PROJ_DT - SSM_HEADS),
        w[:, ssm0:ssm0 + SSM_INNER],
        w[:, ssm0 + SSM_INNER:ssm0 + SSM_INNER + SSM_CONV_DIM]], axis=1).astype(BF16)
    half = RW_LORA // 2
    zl = jnp.zeros((half, RW_DIM), F32)
    wl = jnp.concatenate([jnp.concatenate([rwkv_w2[i], zl], axis=1),
                          jnp.concatenate([zl, rwkv_a2[i]], axis=1)], axis=0)
    half_of = np.arange(128) // RW_HEAD
    bones2 = np.tile((half_of[:, None] == half_of[None, :]).astype(np.float32), (2, 1))
    rw = dict(mu=_row(rwkv_mu[i]), w0=_row(rwkv_w0[i]), a0=_row(rwkv_a0[i]), k_k=_row(rwkv_k_k[i]),
              k_a=_row(rwkv_k_a[i]), r_k=_row(rwkv_r_k[i]), ln_w=_row(rwkv_ln_w[i]), ln_b=_row(rwkv_ln_b[i]),
              wl=_cat3w(wl, 0), g2=_cat3w(rwkv_g2[i], 0), bones2=jnp.asarray(bones2, BF16),
              pair_w=jnp.asarray(np.kron(np.eye(4, dtype=np.float32), np.ones((RW_HEAD, RW_HEAD), np.float32)), BF16))
    pad128 = lambda a: jnp.pad(_row(a), ((0, 0), (0, 128 - a.shape[-1])))
    expand = (np.arange(128)[:, None] == (np.arange(SSM_INNER) // SSM_HEAD)[None, :]).astype(np.float32)
    ssm = dict(conv_w=ssm_conv_w[i], conv_b=_row(ssm_conv_b[i]), dt_bias=pad128(ssm_dt_bias[i]),
               a_log=pad128(ssm_a_log[i]), d_full=_row(jnp.repeat(ssm_d[i], SSM_HEAD)), norm_w=_row(ssm_norm_w[i]),
               expand=jnp.asarray(expand), expand_t=jnp.asarray(expand.T))
    return w_pack, rw, ssm


def _mixer_ab(x, b, t, norm_g, w_pack, rw, ssm, w_out, shift0, wkv0, conv0, ssm0, tm):
    tm_in = 2 * tm if (b * t) % (2 * tm) == 0 else tm
    proj = norm_matmul(x, norm_g, w_pack, tm_in, SSM_CONV_DIM).reshape(b, t, PROJ_COLS)
    shift_new = proj[:, t - 1, :RW_COLS]
    conv_new = proj[:, t - (SSM_CONV - 1):, PROJ_XBC:]
    if t % SSD_CHUNK == 0:
        proj_rw, proj_ssm, tc = proj, proj, SSD_CHUNK
    else:
        proj_rw = jnp.pad(proj, ((0, 0), (0, 8 - t), (0, 0)))
        proj_ssm = jnp.pad(proj, ((0, 0), (0, SSD_CHUNK - t), (0, 0)))
        tc = 8
    nbb = next(n for n in (4, 2, 1) if b % n == 0)
    y_rw, wkv_new = rwkv7(proj_rw, shift0.reshape(b, 1, RW_COLS), wkv0, rw, nbb, tc, min(t, tc))
    conv0p = jnp.pad(conv0, ((0, 0), (8 - (SSM_CONV - 1), 0), (0, 0)))
    y_ssm, ssm_new = mamba2(proj_ssm, conv0p, ssm0.reshape(b, SSM_INNER, SSM_STATE), ssm, min(t, SSD_CHUNK))
    x = matmul_residual(x, [y_rw[:, :t].reshape(b * t, RW_DIM), y_ssm[:, :t].reshape(b * t, SSM_INNER)],
                        [w_out[:RW_DIM], w_out[RW_DIM:]], tm)
    return x, (shift_new, wkv_new, conv_new, ssm_new.reshape(b, SSM_HEADS, SSM_HEAD, SSM_STATE))


def kernel(x_prompt, x_sample, state_rwkv_shift, state_rwkv_wkv, state_ssm_conv, state_ssm, cache_k, cache_v, page_table, norm_mix, norm_ffn, norm_final, w_in_ab, rwkv_mu, rwkv_w0, rwkv_w2, rwkv_a0, rwkv_a2, rwkv_g2, rwkv_k_k, rwkv_k_a, rwkv_r_k, rwkv_ln_w, rwkv_ln_b, ssm_conv_w, ssm_conv_b, ssm_dt_bias, ssm_a_log, ssm_d, ssm_norm_w, w_out_ab, ffn_w_gate, ffn_w_up, ffn_w_down, attn_w_qkv, attn_w_o, moe_router, moe_w_gate, moe_w_up, moe_w_down):
    bp, tp, _ = x_prompt.shape
    db, ts, _ = x_sample.shape
    depth = norm_mix.shape[0]
    tm_p, tm_s = 512, db * ts
    xp = x_prompt.reshape(bp * tp, D_MODEL)
    xs = x_sample.reshape(db * ts, D_MODEL)
    slopes = _alibi_slopes()
    slf = jnp.asarray(_slope_features().reshape(ATT_KV_HEADS, ATT_GROUP, 32))
    slope_rows = jnp.asarray(np.repeat(slopes, ts).reshape(ATT_HEADS * ts, 1))
    st = {n: [] for n in ("p_shift", "p_wkv", "p_conv", "p_ssm", "p_k", "p_v",
                          "s_shift", "s_wkv", "s_conv", "s_ssm", "s_k", "s_v")}
    for l in range(depth):
        i = l // 2
        g_mix, g_ffn = _row(norm_mix[l]), _row(norm_ffn[l])
        if l % 2 == 0:
            w_pack, rw, ssm = _even_params(i, w_in_ab, rwkv_mu, rwkv_w0, rwkv_w2, rwkv_a0, rwkv_a2, rwkv_g2,
                                           rwkv_k_k, rwkv_k_a, rwkv_r_k, rwkv_ln_w, rwkv_ln_b, ssm_conv_w,
                                           ssm_conv_b, ssm_dt_bias, ssm_a_log, ssm_d, ssm_norm_w)
            w_out = w_out_ab[i].astype(BF16)
            xp, sp = _mixer_ab(xp, bp, tp, g_mix, w_pack, rw, ssm, w_out,
                               jnp.zeros((bp, RW_COLS), F32), jnp.zeros((bp, RW_HEADS, RW_HEAD, RW_HEAD), F32),
                               jnp.zeros((bp, SSM_CONV - 1, SSM_CONV_DIM), F32),
                               jnp.zeros((bp, SSM_HEADS, SSM_HEAD, SSM_STATE), F32), tm_p)
            xs, ss = _mixer_ab(xs, db, ts, g_mix, w_pack, rw, ssm, w_out, state_rwkv_shift[i], state_rwkv_wkv[i],
                               state_ssm_conv[i], state_ssm[i], tm_s)
            for pre, new in (("p", sp), ("s", ss)):
                for name, val in zip(("shift", "wkv", "conv", "ssm"), new):
                    st[f"{pre}_{name}"].append(val)
            wg, wu, wd = ffn_w_gate[i].astype(BF16), ffn_w_up[i].astype(BF16), ffn_w_down[i].astype(BF16)
            xp = ffn_swiglu(xp, g_ffn, wg, wu, wd, 1024, 256)
            xs = ffn_swiglu(xs, g_ffn, wg, wu, wd, tm_s, 256)
        else:
            w_qkv, w_o = attn_w_qkv[i].astype(BF16), attn_w_o[i].astype(BF16)
            qkv_p, kf, vf = qkv_proj(xp, g_mix, w_qkv, bp, tp, tm_p)
            nb = tp // MOBA_BLOCK
            means = block_means(qkv_p, bp * nb).reshape(bp, nb, ATT_KV_HEADS, ATT_HEAD).transpose(0, 2, 1, 3)
            k_p = qkv_p[:, ATT_Q_COLS:ATT_Q_COLS + ATT_KV_COLS].reshape(bp, tp, ATT_KV_HEADS, ATT_HEAD)
            v_p = qkv_p[:, ATT_Q_COLS + ATT_KV_COLS:].reshape(bp, tp, ATT_KV_HEADS, ATT_HEAD)
            o_p = moba_prompt(qkv_p.reshape(bp, tp, -1), kf, vf, means, slf)
            xp = matmul_residual(xp, [o_p.reshape(bp * tp, ATT_Q_COLS)], [w_o], tm_p)
            st["p_k"].append(k_p)
            st["p_v"].append(v_p)

            qkv_s = norm_matmul(xs, g_mix, w_qkv, tm_s, 512).reshape(db, ts, -1)
            kn = qkv_s[..., ATT_Q_COLS:ATT_Q_COLS + ATT_KV_COLS].reshape(db, ts, ATT_KV_HEADS, ATT_HEAD)
            vn = qkv_s[..., ATT_Q_COLS + ATT_KV_COLS:].reshape(db, ts, ATT_KV_HEADS, ATT_HEAD)
            o_s = moba_sample(page_table, _pages2d(cache_k[i]), _pages2d(cache_v[i]),
                              _block_diag_queries(qkv_s[..., :ATT_Q_COLS]), _new_rows(kn), _new_rows(vn),
                              slope_rows, ts)
            o_s = o_s.reshape(db, ATT_HEADS, ts, ATT_HEAD).transpose(0, 2, 1, 3).reshape(db * ts, ATT_Q_COLS)
            xs = matmul_residual(xs, [o_s], [w_o], tm_s)
            st["s_k"].append(kn)
            st["s_v"].append(vn)

            router = _cat3w(jnp.pad(moe_router[i], ((0, 0), (0, 128 - N_EXPERTS))), 0)
            wg, wu, wd = moe_w_gate[i].astype(BF16), moe_w_up[i].astype(BF16), moe_w_down[i].astype(BF16)
            final = l == depth - 1
            xp = moe_swiglu(xp, g_ffn, router, wg, wu, wd, _row(norm_final), 2 * tm_p, final)
            xs = moe_swiglu(xs, g_ffn, router, wg, wu, wd, _row(norm_final), tm_s, final)
    y_prompt = xp.reshape(bp, tp, D_MODEL)
    y_sample = xs.reshape(db, ts, D_MODEL)
    stack = lambda n: jnp.stack(st[n])
    return (y_prompt, y_sample,
            stack("p_shift"), stack("p_wkv"), stack("p_conv"), stack("p_ssm"), stack("p_k"), stack("p_v"),
            stack("s_shift"), stack("s_wkv"), stack("s_conv"), stack("s_ssm"), stack("s_k"), stack("s_v"))
```

```python
import functools

import numpy as np
import jax
import jax.numpy as jnp
from jax import lax
from jax.experimental import pallas as pl
from jax.experimental.pallas import tpu as pltpu

F32 = jnp.float32
BF16 = jnp.bfloat16
HI = lax.Precision.HIGHEST
NEG_INF = float("-inf")

D_MODEL = 1024
NORM_EPS = 1e-6

RW_HEAD = 64
RW_HEADS = 8
RW_DIM = RW_HEADS * RW_HEAD
RW_LORA = 128
RW_GATE = 128
RW_COLS = 3 * RW_DIM + RW_LORA + RW_GATE
RW_LN_EPS = 64e-5

SSM_INNER = 1024
SSM_HEAD = 64
SSM_HEADS = 16
SSM_GROUPS = 2
SSM_STATE = 128
SSM_CONV = 4
SSM_CONV_DIM = SSM_INNER + 2 * SSM_GROUPS * SSM_STATE
SSD_CHUNK = 128

PROJ_RW = 0
PROJ_DT = RW_COLS
PROJ_Z = 2048
PROJ_XBC = 3072
PROJ_COLS = PROJ_XBC + SSM_CONV_DIM

ATT_HEADS = 16
ATT_KV_HEADS = 4
ATT_HEAD = 64
ATT_GROUP = ATT_HEADS // ATT_KV_HEADS
ATT_Q_COLS = ATT_HEADS * ATT_HEAD
ATT_KV_COLS = ATT_KV_HEADS * ATT_HEAD
MOBA_BLOCK = 256
MOBA_TOPK = 3
MASK_NEG = -1e30
PAGE_SIZE = 128

N_EXPERTS = 8

VMEM_LIMIT = 56 * 1024 * 1024


def _cparams(sem):
    return pltpu.CompilerParams(dimension_semantics=sem, vmem_limit_bytes=VMEM_LIMIT)


def _nt(a, b, precision=None):
    return lax.dot_general(a, b, (((1,), (1,)), ((), ())), precision=precision,
                           preferred_element_type=F32)


def _dot(a, b, precision=None):
    return jnp.dot(a, b, precision=precision, preferred_element_type=F32)


def _split2(x):
    hi = x.astype(BF16)
    return hi, (x - hi.astype(F32)).astype(BF16)


def _split3(x):
    hi = x.astype(BF16)
    r = x - hi.astype(F32)
    mid = r.astype(BF16)
    return hi, mid, (r - mid.astype(F32)).astype(BF16)


def _cat3(x, axis):
    hi, lo = _split2(x)
    return jnp.concatenate([hi, lo, hi], axis=axis)


def _cat3w(w, axis):
    hi, lo = _split2(w)
    return jnp.concatenate([hi, hi, lo], axis=axis)


def _rms(x, g):
    return x * lax.rsqrt(jnp.mean(x * x, axis=-1, keepdims=True) + NORM_EPS) * g


def _softplus(x):
    return jnp.maximum(x, 0.0) + jnp.log(1.0 + jnp.exp(-jnp.abs(x)))


def _sigmoid(x):
    return 1.0 / (1.0 + jnp.exp(-x))


def _silu(x):
    return x * _sigmoid(x)


def _nm_body(x_ref, g_ref, w_ref, o_ref, h_sc):
    @pl.when(pl.program_id(1) == 0)
    def _():
        h_sc[...] = _rms(x_ref[...], g_ref[...]).astype(BF16)

    o_ref[...] = _dot(h_sc[...], w_ref[...])


def norm_matmul(x, g, w, tm, tn):
    m, k = x.shape
    n = w.shape[1]
    return pl.pallas_call(
        _nm_body,
        grid=(m // tm, n // tn),
        in_specs=[pl.BlockSpec((tm, k), lambda i, j: (i, 0)),
                  pl.BlockSpec((1, k), lambda i, j: (0, 0)),
                  pl.BlockSpec((k, tn), lambda i, j: (0, j))],
        out_specs=pl.BlockSpec((tm, tn), lambda i, j: (i, j)),
        out_shape=jax.ShapeDtypeStruct((m, n), F32),
        scratch_shapes=[pltpu.VMEM((tm, k), BF16)],
        compiler_params=_cparams(("parallel", "arbitrary")),
    )(x, g, w)


def _mmres_body(*refs, n):
    res_ref, a_refs, w_refs, o_ref = refs[0], refs[1:1 + n], refs[1 + n:1 + 2 * n], refs[1 + 2 * n]
    acc = res_ref[...]
    for a_ref, w_ref in zip(a_refs, w_refs):
        acc = acc + _dot(a_ref[...].astype(BF16), w_ref[...])
    o_ref[...] = acc


def matmul_residual(res, acts, ws, tm):
    m, d = res.shape
    n = len(acts)
    in_specs = [pl.BlockSpec((tm, d), lambda i: (i, 0))]
    in_specs += [pl.BlockSpec((tm, a.shape[1]), lambda i: (i, 0)) for a in acts]
    in_specs += [pl.BlockSpec(w.shape, lambda i: (0, 0)) for w in ws]
    return pl.pallas_call(
        functools.partial(_mmres_body, n=n),
        grid=(m // tm,),
        in_specs=in_specs,
        out_specs=pl.BlockSpec((tm, d), lambda i: (i, 0)),
        out_shape=jax.ShapeDtypeStruct((m, d), F32),
        compiler_params=_cparams(("parallel",)),
    )(res, *acts, *ws)


def _ffn_body(x_ref, g_ref, wg_ref, wu_ref, wd_ref, o_ref, h_sc, acc_sc):
    j = pl.program_id(1)

    @pl.when(j == 0)
    def _():
        h_sc[...] = _rms(x_ref[...], g_ref[...]).astype(BF16)
        acc_sc[...] = jnp.zeros_like(acc_sc)

    h = h_sc[...]
    act = _silu(_dot(h, wg_ref[...])) * _dot(h, wu_ref[...])
    acc_sc[...] += _dot(act.astype(BF16), wd_ref[...])

    @pl.when(j == pl.num_programs(1) - 1)
    def _():
        o_ref[...] = x_ref[...] + acc_sc[...]


def ffn_swiglu(x, g, wg, wu, wd, tm, tf):
    m, d = x.shape
    f = wg.shape[1]
    return pl.pallas_call(
        _ffn_body,
        grid=(m // tm, f // tf),
        in_specs=[pl.BlockSpec((tm, d), lambda i, j: (i, 0)),
                  pl.BlockSpec((1, d), lambda i, j: (0, 0)),
                  pl.BlockSpec((d, tf), lambda i, j: (0, j)),
                  pl.BlockSpec((d, tf), lambda i, j: (0, j)),
                  pl.BlockSpec((tf, d), lambda i, j: (j, 0))],
        out_specs=pl.BlockSpec((tm, d), lambda i, j: (i, 0)),
        out_shape=jax.ShapeDtypeStruct((m, d), F32),
        scratch_shapes=[pltpu.VMEM((tm, d), BF16), pltpu.VMEM((tm, d), F32)],
        compiler_params=_cparams(("parallel", "arbitrary")),
    )(x, g, wg, wu, wd)


MOE_SUB = 512
MOE_CHUNK = 160


def _moe_body(x_ref, g_ref, r_ref, wg_ref, wu_ref, wd_ref, gf_ref, o_ref, h_sc, comb_sc, acc_sc, rk_sc, rkt_sc, *,
              final_norm, chunk):
    e = pl.program_id(1)
    tm = x_ref.shape[0]
    lane = lax.broadcasted_iota(jnp.int32, (tm, 128), 1)

    @pl.when(e == 0)
    def _():
        hf = _rms(x_ref[...], g_ref[...])
        h_sc[...] = hf.astype(BF16)
        lanef = lane.astype(F32)
        logits = jnp.where(lane < N_EXPERTS, _dot(_cat3(hf, 1), r_ref[...]), NEG_INF)
        m1 = jnp.max(logits, axis=1, keepdims=True)
        i1 = jnp.min(jnp.where(logits == m1, lanef, 128.0), axis=1, keepdims=True)
        mask1 = lanef == i1
        rest = jnp.where(mask1, NEG_INF, logits)
        m2 = jnp.max(rest, axis=1, keepdims=True)
        i2 = jnp.min(jnp.where(rest == m2, lanef, 128.0), axis=1, keepdims=True)
        mask2 = lanef == i2
        e2 = jnp.exp(m2 - m1)
        den = 1.0 + e2
        comb_sc[...] = jnp.where(mask1, 1.0 / den, 0.0) + jnp.where(mask2, e2 / den, 0.0)
        acc_sc[...] = jnp.zeros_like(acc_sc)
        if chunk:
            earlier = (lax.broadcasted_iota(jnp.int32, (MOE_SUB, MOE_SUB), 1)
                       < lax.broadcasted_iota(jnp.int32, (MOE_SUB, MOE_SUB), 0))
            earlier = jnp.where(earlier, 1.0, 0.0).astype(BF16)
            routed = jnp.where(mask1 | mask2, 1.0, 0.0)
            for st in range(tm // MOE_SUB):
                sl = slice(st * MOE_SUB, (st + 1) * MOE_SUB)
                rank = _dot(earlier, routed[sl].astype(BF16))
                rank = jnp.where(routed[sl] > 0.0, rank, -1.0)
                rk_sc[sl, :] = rank
                rkt_sc[st] = rank.T

    def expert(rows):
        act = _silu(_dot(rows, wg_ref[0])) * _dot(rows, wu_ref[0])
        return _dot(act.astype(BF16), wd_ref[0])

    this = lane == e
    if not chunk:
        c = jnp.sum(jnp.where(this, comb_sc[...], 0.0), axis=1, keepdims=True)
        acc_sc[...] += c * expert(h_sc[...])
    else:
        wide = -(-chunk // 128) * 128
        slot = lax.broadcasted_iota(jnp.int32, (chunk, MOE_SUB), 0).astype(F32)
        lane_w = lax.broadcasted_iota(jnp.int32, (MOE_SUB, wide), 1)
        slot_t = jnp.where(lane_w < chunk, lane_w.astype(F32), jnp.nan)
        sub8 = lax.broadcasted_iota(jnp.int32, (8, MOE_SUB), 0)
        this_sub = lax.broadcasted_iota(jnp.int32, (MOE_SUB, 128), 1) == e
        pad_rows = jnp.zeros((wide - chunk, x_ref.shape[1]), BF16)
        for st in range(tm // MOE_SUB):
            sl = slice(st * MOE_SUB, (st + 1) * MOE_SUB)
            rank_col = jnp.sum(jnp.where(this_sub, rk_sc[sl, :], 0.0), axis=1, keepdims=True)
            rank_row = jnp.sum(jnp.where(sub8 == e, rkt_sc[st, 0:8, :], 0.0), axis=0, keepdims=True)
            c = jnp.sum(jnp.where(this_sub, comb_sc[sl, :], 0.0), axis=1, keepdims=True)
            count = jnp.max(rank_row, axis=1, keepdims=True)[0, 0].astype(jnp.int32) + 1

            def one_chunk(k, carry):
                base = (k * chunk).astype(F32)
                gather = jnp.where(rank_row - base == slot, 1.0, 0.0).astype(BF16)
                y_hi, y_lo = _split2(expert(_dot(gather, h_sc[sl, :]).astype(BF16)))
                scatter = jnp.where(rank_col - base == slot_t, 1.0, 0.0).astype(BF16)
                back = (_dot(scatter, jnp.concatenate([y_hi, pad_rows], axis=0))
                        + _dot(scatter, jnp.concatenate([y_lo, pad_rows], axis=0)))
                acc_sc[sl, :] += c * back
                return carry

            lax.fori_loop(0, (count + chunk - 1) // chunk, one_chunk, 0)

    @pl.when(e == pl.num_programs(1) - 1)
    def _():
        y = x_ref[...] + acc_sc[...]
        if final_norm:
            y = _rms(y, gf_ref[...])
        o_ref[...] = y


def moe_swiglu(x, g, router, wg, wu, wd, gfinal, tm, final_norm):
    m, d = x.shape
    ne, _, fe = wg.shape
    chunk = MOE_CHUNK if tm % MOE_SUB == 0 else 0
    nsub = max(tm // MOE_SUB, 1)
    return pl.pallas_call(
        functools.partial(_moe_body, final_norm=final_norm, chunk=chunk),
        grid=(m // tm, ne),
        in_specs=[pl.BlockSpec((tm, d), lambda i, e: (i, 0)),
                  pl.BlockSpec((1, d), lambda i, e: (0, 0)),
                  pl.BlockSpec((3 * d, 128), lambda i, e: (0, 0)),
                  pl.BlockSpec((1, d, fe), lambda i, e: (e, 0, 0)),
                  pl.BlockSpec((1, d, fe), lambda i, e: (e, 0, 0)),
                  pl.BlockSpec((1, fe, d), lambda i, e: (e, 0, 0)),
                  pl.BlockSpec((1, d), lambda i, e: (0, 0))],
        out_specs=pl.BlockSpec((tm, d), lambda i, e: (i, 0)),
        out_shape=jax.ShapeDtypeStruct((m, d), F32),
        scratch_shapes=[pltpu.VMEM((tm, d), BF16), pltpu.VMEM((tm, 128), F32), pltpu.VMEM((tm, d), F32),
                        pltpu.VMEM((tm, 128), F32), pltpu.VMEM((nsub, 128, min(tm, MOE_SUB)), F32)],
        compiler_params=_cparams(("parallel", "arbitrary")),
    )(x, g, router, wg, wu, wd, gfinal)


RW_PAIRS = RW_HEADS // 2
RW_ROWS = RW_PAIRS * RW_HEAD


def _rwkv_body(p_ref, sh0_ref, s0_ref, mu_ref, w0_ref, a0_ref, kk_ref, ka_ref, rk_ref, lnw_ref, lnb_ref,
               wl_ref, g2_ref, b2_ref, pw_ref, y_ref, sfin_ref,
               s_sc, sk_sc, prev_sc, kk_sc, w_sc, b_sc, k_sc, r_sc, v_sc, g_sc, y_sc, *, nbb, tc, t_valid):
    c = pl.program_id(1)

    @pl.when(c == 0)
    def _():
        s_sc[...] = s0_ref[...]
        prev_sc[...] = sh0_ref[...]

    b2 = b2_ref[...]

    def head_sums(x, low=True):
        if low:
            hi, lo = _split2(x)
        else:
            hi = x.astype(BF16)
            lo = jnp.zeros_like(hi)
        return _dot(jnp.concatenate([hi, lo], axis=1), b2)

    def head_sums_wide(x):
        return jnp.concatenate([head_sums(x[:, q * 128:(q + 1) * 128]) for q in range(RW_PAIRS)], axis=1)

    for bi in range(nbb):
        p = p_ref[bi]
        row = lax.broadcasted_iota(jnp.int32, p.shape, 0)
        prev = jnp.where(row == 0, prev_sc[bi], pltpu.roll(p, 1, axis=0))
        prev_sc[bi] = p[tc - 1:tc, :]
        u = p + (prev - p) * mu_ref[...]
        r = u[:, 0:RW_DIM]
        k = u[:, RW_DIM:2 * RW_DIM]
        v = u[:, 2 * RW_DIM:3 * RW_DIM]
        lo = u[:, 3 * RW_DIM:3 * RW_DIM + RW_LORA]
        gd = u[:, 3 * RW_DIM + RW_LORA:]
        lane = lax.broadcasted_iota(jnp.int32, lo.shape, 1)
        lora = _dot(_cat3(jnp.where(lane < RW_LORA // 2, jnp.tanh(lo), lo), 1), wl_ref[...])
        wlog = -_softplus(-(w0_ref[...] + lora[:, :RW_DIM])) - 0.5
        a = _sigmoid(a0_ref[...] + lora[:, RW_DIM:])
        g_sc[bi] = _dot(_cat3(_sigmoid(gd), 1), g2_ref[...])
        kk = k * kk_ref[...]
        kk = kk / jnp.maximum(jnp.sqrt(head_sums_wide(kk * kk)), 1e-12)
        kk_sc[bi] = kk
        w_sc[bi] = jnp.exp(-jnp.exp(wlog))
        b_sc[bi] = kk * a
        k_sc[bi] = k * (1.0 + (a - 1.0) * ka_ref[...])
        r_sc[bi] = r
        v_sc[bi] = v

    rows = nbb * RW_ROWS
    vi = lax.broadcasted_iota(jnp.int32, (rows, 128), 0) % RW_HEAD
    li = lax.broadcasted_iota(jnp.int32, (rows, 128), 1)
    key_lane = li % RW_HEAD
    diag = (key_lane == vi).astype(F32)
    steps = 8 if t_valid % 8 == 0 else t_valid

    def rows_of(x8, j):
        return jnp.concatenate([jnp.broadcast_to(x8[bi][j:j + 1, q * 128:(q + 1) * 128], (RW_HEAD, 128))
                                for bi in range(nbb) for q in range(RW_PAIRS)], axis=0)

    pair_w = pw_ref[...]
    zero_half = jnp.zeros((rows, 128), BF16)

    def pair_sums(a, b):
        lhs = jnp.concatenate([a.astype(BF16), zero_half if b is None else b.astype(BF16)], axis=1)
        out = _dot(lhs, pair_w)
        return out[:, :128], out[:, 128:]

    first8 = [kk_sc[bi, pl.ds(0, 8), :] for bi in range(nbb)]
    sk_sc[...] = pair_sums(s_sc[...].reshape(rows, 128) * rows_of(first8, 0), None)[0]

    def group(t8, carry):
        t0 = pl.multiple_of(t8 * 8, 8)
        tn = pl.multiple_of(jnp.minimum(t0 + 8, tc - 8), 8)
        kk8, w8, b8, k8, r8, v8 = ([sc[bi, pl.ds(t0, 8), :] for bi in range(nbb)]
                                   for sc in (kk_sc, w_sc, b_sc, k_sc, r_sc, v_sc))
        kk_next = [kk_sc[bi, pl.ds(tn, 8), :] for bi in range(nbb)]
        s = s_sc[...].reshape(rows, 128)
        sk = sk_sc[...]
        ycols = jnp.zeros((rows, 128), F32)
        for j in range(steps):
            if j % 2 == 0:
                vcols = pair_sums(rows_of(v8, j) * diag, rows_of(v8, j + 1) * diag if j + 1 < steps else None)
            s = s * rows_of(w8, j) - sk * rows_of(b8, j) + vcols[j % 2] * rows_of(k8, j)
            kk_after = rows_of(kk8, j + 1) if j + 1 < 8 else rows_of(kk_next, 0)
            sk, yb = pair_sums(s * kk_after, s * rows_of(r8, j))
            ycols = jnp.where(key_lane == j, yb, ycols)
        s_sc[...] = s.reshape(nbb, RW_ROWS, 128)
        sk_sc[...] = sk
        h64 = RW_HEAD
        for bi in range(nbb):
            pieces = []
            for q in range(0, RW_PAIRS, 2):
                r0 = (bi * RW_PAIRS + q) * h64
                tr = ycols[r0:r0 + 2 * h64].T
                pieces += [tr[0:8, 0:h64], tr[h64:h64 + 8, 0:h64], tr[0:8, h64:], tr[h64:h64 + 8, h64:]]
            y_sc[bi, pl.ds(t0, 8), :] = jnp.concatenate(pieces, axis=1)
        return carry

    lax.fori_loop(0, -(-t_valid // 8), group, 0)

    inv = 1.0 / RW_HEAD
    for bi in range(nbb):
        y = y_sc[bi]
        yc = y - head_sums_wide(y) * inv
        var = head_sums_wide(yc * yc) * inv
        yn = yc * lax.rsqrt(var + RW_LN_EPS) * lnw_ref[...] + lnb_ref[...]
        bonus = head_sums_wide(r_sc[bi] * k_sc[bi] * rk_ref[...]) * v_sc[bi]
        y_ref[bi] = (yn + bonus) * g_sc[bi]

    @pl.when(c == pl.num_programs(1) - 1)
    def _():
        sfin_ref[...] = s_sc[...]


def _wkv_to_rows(s):
    b = s.shape[0]
    return s.reshape(b, RW_PAIRS, 2, RW_HEAD, RW_HEAD).transpose(0, 1, 3, 2, 4).reshape(b, RW_ROWS, 128)


def _rows_to_wkv(s):
    b = s.shape[0]
    return s.reshape(b, RW_PAIRS, RW_HEAD, 2, RW_HEAD).transpose(0, 1, 3, 2, 4).reshape(b, RW_HEADS, RW_HEAD, RW_HEAD)


def rwkv7(proj, shift0, wkv0, prm, nbb, tc, t_valid):
    b, t, _ = proj.shape
    vec = lambda n: pl.BlockSpec((1, n), lambda i, c: (0, 0))
    full = lambda a: pl.BlockSpec(a.shape, lambda i, c: (0,) * a.ndim)
    seq = pltpu.VMEM((nbb, tc, RW_DIM), F32)
    y, s_fin = pl.pallas_call(
        functools.partial(_rwkv_body, nbb=nbb, tc=tc, t_valid=t_valid),
        grid=(b // nbb, t // tc),
        in_specs=[pl.BlockSpec((nbb, tc, RW_COLS), lambda i, c: (i, c, 0)),
                  pl.BlockSpec((nbb, 1, RW_COLS), lambda i, c: (i, 0, 0)),
                  pl.BlockSpec((nbb, RW_ROWS, 128), lambda i, c: (i, 0, 0)),
                  vec(RW_COLS), vec(RW_DIM), vec(RW_DIM), vec(RW_DIM), vec(RW_DIM), vec(RW_DIM), vec(RW_DIM),
                  vec(RW_DIM), full(prm["wl"]), full(prm["g2"]), full(prm["bones2"]), full(prm["pair_w"])],
        out_specs=[pl.BlockSpec((nbb, tc, RW_DIM), lambda i, c: (i, c, 0)),
                   pl.BlockSpec((nbb, RW_ROWS, 128), lambda i, c: (i, 0, 0))],
        out_shape=[jax.ShapeDtypeStruct((b, t, RW_DIM), F32),
                   jax.ShapeDtypeStruct((b, RW_ROWS, 128), F32)],
        scratch_shapes=[pltpu.VMEM((nbb, RW_ROWS, 128), F32), pltpu.VMEM((nbb * RW_ROWS, 128), F32),
                        pltpu.VMEM((nbb, 1, RW_COLS), F32),
                        seq, seq, seq, seq, seq, seq, seq, seq],
        compiler_params=_cparams(("parallel", "arbitrary")),
    )(proj, shift0, _wkv_to_rows(wkv0), prm["mu"], prm["w0"], prm["a0"], prm["k_k"], prm["k_a"], prm["r_k"],
      prm["ln_w"], prm["ln_b"], prm["wl"], prm["g2"], prm["bones2"], prm["pair_w"])
    return y, _rows_to_wkv(s_fin)


def _ssd_body(z_ref, xbc_ref, dt_ref, conv0_ref, h0_ref, cw_ref, cb_ref, dtb_ref, alog_ref, dsk_ref, nw_ref,
              e_ref, y_ref, hfin_ref, h_sc, ext_sc, *, t_valid):
    c = pl.program_id(1)
    ln = SSD_CHUNK

    @pl.when(c == 0)
    def _():
        h_sc[...] = h0_ref[0]
        ext_sc[0:8, :] = conv0_ref[0]

    xbc = xbc_ref[0]
    ext_sc[8:8 + ln, :] = xbc
    cw = cw_ref[...]
    conv = cb_ref[...] + ext_sc[5:5 + ln, :] * cw[0:1]
    conv = conv + ext_sc[6:6 + ln, :] * cw[1:2]
    conv = conv + ext_sc[7:7 + ln, :] * cw[2:3]
    conv = conv + xbc * cw[3:4]
    ext_sc[0:8, :] = xbc[ln - 8:ln, :]
    act = _silu(conv)
    xs = act[:, :SSM_INNER]
    bm = act[:, SSM_INNER:SSM_INNER + SSM_GROUPS * SSM_STATE].astype(BF16)
    cm = act[:, SSM_INNER + SSM_GROUPS * SSM_STATE:].astype(BF16)

    lane = lax.broadcasted_iota(jnp.int32, (ln, 128), 1)
    rowi = lax.broadcasted_iota(jnp.int32, (ln, 128), 0)
    dt = _softplus(dt_ref[0] + dtb_ref[...])
    if t_valid < ln:
        dt = jnp.where(rowi < t_valid, dt, 0.0)
    a = dt * jnp.where(lane[0:1] < SSM_HEADS, -jnp.exp(alog_ref[...]), 0.0)
    causal = lane <= rowi
    tri = jnp.where(causal, 1.0, 0.0).astype(BF16)
    tri3 = jnp.concatenate([tri, tri, tri], axis=1)
    acum = _dot(tri3, jnp.concatenate(_split3(a), axis=0))
    acum_t = _nt(jnp.concatenate(_split3(a.T), axis=1), tri3)
    e3 = e_ref[...]
    dt_full = _dot(jnp.concatenate(_split3(dt), axis=1), e3)
    ac_full = _dot(jnp.concatenate(_split3(acum), axis=1), e3)
    xdt = xs * dt_full
    xw = xdt * jnp.exp(ac_full[ln - 1:ln, :] - ac_full)
    eac = jnp.exp(ac_full)
    cd = jnp.broadcast_to(jnp.exp(acum_t[:, ln - 1:ln]), (128, 128))
    cd_rows = jnp.concatenate([jnp.broadcast_to(cd[h:h + 1, :], (SSM_HEAD, SSM_STATE)) for h in range(SSM_HEADS)],
                              axis=0)

    ys = []
    for g in range(SSM_GROUPS):
        bm_g = bm[:, g * SSM_STATE:(g + 1) * SSM_STATE]
        cm_g = cm[:, g * SSM_STATE:(g + 1) * SSM_STATE]
        cbm = _nt(cm_g, bm_g)
        pairs = SSM_HEADS // SSM_GROUPS // 2
        for q in range(g * pairs, (g + 1) * pairs):
            sl = slice(q * 128, (q + 1) * 128)
            xdt_p = xdt[:, sl].astype(BF16)
            yd = []
            for h in (2 * q, 2 * q + 1):
                seg = jnp.where(causal, acum[:, h:h + 1] - acum_t[h:h + 1, :], NEG_INF)
                yd.append(_dot((cbm * jnp.exp(seg)).astype(BF16), xdt_p))
            hp = h_sc[sl, :]
            y_off = _nt(cm_g, hp.astype(BF16)) * eac[:, sl]
            st = _dot(xw[:, sl].T.astype(BF16), bm_g)
            h_sc[sl, :] = cd_rows[sl, :] * hp + st
            ys.append(jnp.where(lane < SSM_HEAD, yd[0], yd[1]) + y_off)
    y = jnp.concatenate(ys, axis=1) + dsk_ref[...] * xs
    y = y * _silu(z_ref[0])
    gw = SSM_INNER // SSM_GROUPS
    outs = []
    for g in range(SSM_GROUPS):
        yg = y[:, g * gw:(g + 1) * gw]
        outs.append(yg * lax.rsqrt(jnp.mean(yg * yg, axis=1, keepdims=True) + NORM_EPS))
    y_ref[0] = jnp.concatenate(outs, axis=1) * nw_ref[...]

    @pl.when(c == pl.num_programs(1) - 1)
    def _():
        hfin_ref[0] = h_sc[...]


def mamba2(proj, conv0, ssm0, prm, t_valid):
    b, t, _ = proj.shape
    ln = SSD_CHUNK
    vec = lambda n: pl.BlockSpec((1, n), lambda i, c: (0, 0))
    full = lambda a: pl.BlockSpec(a.shape, lambda i, c: (0,) * a.ndim)
    return pl.pallas_call(
        functools.partial(_ssd_body, t_valid=t_valid),
        grid=(b, t // ln),
        in_specs=[pl.BlockSpec((1, ln, SSM_INNER), lambda i, c: (i, c, PROJ_Z // SSM_INNER)),
                  pl.BlockSpec((1, ln, SSM_CONV_DIM), lambda i, c: (i, c, PROJ_XBC // SSM_CONV_DIM)),
                  pl.BlockSpec((1, ln, 128), lambda i, c: (i, c, PROJ_DT // 128)),
                  pl.BlockSpec((1, 8, SSM_CONV_DIM), lambda i, c: (i, 0, 0)),
                  pl.BlockSpec((1, SSM_INNER, SSM_STATE), lambda i, c: (i, 0, 0)),
                  full(prm["conv_w"]), vec(SSM_CONV_DIM), vec(128), vec(128), vec(SSM_INNER), vec(SSM_INNER),
                  full(prm["expand"])],
        out_specs=[pl.BlockSpec((1, ln, SSM_INNER), lambda i, c: (i, c, 0)),
                   pl.BlockSpec((1, SSM_INNER, SSM_STATE), lambda i, c: (i, 0, 0))],
        out_shape=[jax.ShapeDtypeStruct((b, t, SSM_INNER), F32),
                   jax.ShapeDtypeStruct((b, SSM_INNER, SSM_STATE), F32)],
        scratch_shapes=[pltpu.VMEM((SSM_INNER, SSM_STATE), F32), pltpu.VMEM((8 + ln, SSM_CONV_DIM), F32)],
        compiler_params=_cparams(("parallel", "arbitrary")),
    )(proj, proj, proj, conv0, ssm0, prm["conv_w"], prm["conv_b"], prm["dt_bias"], prm["a_log"],
      prm["d_full"], prm["norm_w"], prm["expand"])


def _alibi_slopes():
    return (2.0 ** (-8.0 * np.arange(1, ATT_HEADS + 1) / ATT_HEADS)).astype(np.float32)


def _slope_features():
    s = _alibi_slopes()
    bf = lambda a: a.astype(jnp.bfloat16).astype(np.float32)
    hi = bf(s)
    mid = bf(s - hi)
    lo = bf(s - hi - mid)
    out = np.zeros((ATT_HEADS, 32), np.float32)
    for i, piece in enumerate((hi, mid, lo)):
        out[:, i] = piece
        out[:, 3 + i] = piece
    return out


def _top3_threshold(gate):
    v = gate
    for _ in range(MOBA_TOPK - 1):
        v = jnp.where(v == jnp.max(v, axis=1, keepdims=True), NEG_INF, v)
    return jnp.max(v, axis=1, keepdims=True)


def _blockmean_body(k_ref, o_ref):
    o_ref[0] = jnp.mean(k_ref[...], axis=0, keepdims=True)


def block_means(qkv, nblocks):
    return pl.pallas_call(
        _blockmean_body,
        grid=(nblocks,),
        in_specs=[pl.BlockSpec((MOBA_BLOCK, ATT_KV_COLS), lambda i: (i, ATT_Q_COLS // ATT_KV_COLS))],
        out_specs=pl.BlockSpec((1, 1, ATT_KV_COLS), lambda i: (i, 0, 0)),
        out_shape=jax.ShapeDtypeStruct((nblocks, 1, ATT_KV_COLS), F32),
        compiler_params=_cparams(("parallel",)),
    )(qkv)


def _qkv_body(x_ref, g_ref, w_ref, qkv_ref, kf_ref, vf_ref, *, tiles_per_seq):
    tm = x_ref.shape[0]
    qkv = _dot(_rms(x_ref[...], g_ref[...]).astype(BF16), w_ref[...])
    qkv_ref[...] = qkv
    pos = (pl.program_id(0) % tiles_per_seq) * tm + lax.broadcasted_iota(jnp.int32, (tm, ATT_HEAD), 0)
    lane = lax.broadcasted_iota(jnp.int32, (tm, ATT_HEAD), 1)
    blk_id, off = pos // MOBA_BLOCK, pos % MOBA_BLOCK
    kfeat = jnp.where(lane == blk_id, 1, 0)
    kfeat = jnp.where((lane >= 32) & (lane < 35), blk_id * MOBA_BLOCK, kfeat)
    kfeat = jnp.where((lane >= 35) & (lane < 38), off, kfeat).astype(F32).astype(BF16)
    vfeat = jnp.where(lane == 0, 1.0, 0.0).astype(BF16)
    for g in range(ATT_KV_HEADS):
        k0 = ATT_Q_COLS + g * ATT_HEAD
        v0 = ATT_Q_COLS + ATT_KV_COLS + g * ATT_HEAD
        kf_ref[0, g] = jnp.concatenate([qkv[:, k0:k0 + ATT_HEAD].astype(BF16), kfeat], axis=1)
        vf_ref[0, g] = jnp.concatenate([qkv[:, v0:v0 + ATT_HEAD].astype(BF16), vfeat], axis=1)


def qkv_proj(x, g, w, b, t, tm):
    m, d = x.shape
    n = w.shape[1]
    tps = t // tm
    ext = jax.ShapeDtypeStruct((b, ATT_KV_HEADS, t, 128), BF16)
    ext_spec = pl.BlockSpec((1, ATT_KV_HEADS, tm, 128), lambda i: (i // tps, 0, i % tps, 0))
    return pl.pallas_call(
        functools.partial(_qkv_body, tiles_per_seq=tps),
        grid=(m // tm,),
        in_specs=[pl.BlockSpec((tm, d), lambda i: (i, 0)), pl.BlockSpec((1, d), lambda i: (0, 0)),
                  pl.BlockSpec((d, n), lambda i: (0, 0))],
        out_specs=[pl.BlockSpec((tm, n), lambda i: (i, 0)), ext_spec, ext_spec],
        out_shape=[jax.ShapeDtypeStruct((m, n), F32), ext, ext],
        compiler_params=_cparams(("parallel",)),
    )(x, g, w)


def _moba_prompt_body(q_ref, kf_ref, vf_ref, mean_ref, slf_ref, o_ref, qf_sc, m_sc, acc_sc, sa0_sc, sa1_sc, sb0_sc, sb1_sc):
    qb = pl.program_id(2)
    blk = MOBA_BLOCK
    rows = ATT_GROUP * blk
    scale = ATT_HEAD ** -0.5

    q4 = q_ref[0]
    qg = jnp.concatenate([q4[:, hh * ATT_HEAD:(hh + 1) * ATT_HEAD] for hh in range(ATT_GROUP)], axis=0)
    past_blk = lax.broadcasted_iota(jnp.int32, (128, rows), 0) < qb
    gate = jnp.where(past_blk, _nt(_cat3w(mean_ref[0, 0], 1), _cat3(qg, 1)), NEG_INF)
    v = gate
    for _ in range(MOBA_TOPK - 1):
        v = jnp.where(v == jnp.max(v, axis=0, keepdims=True), NEG_INF, v)
    thr = jnp.max(v, axis=0, keepdims=True)
    selneg = jnp.where(past_blk & (gate < thr), MASK_NEG, 0.0).T[:, :32]
    slf = jnp.concatenate([jnp.broadcast_to(slf_ref[0, hh:hh + 1, :], (blk, 32)) for hh in range(ATT_GROUP)], axis=0)
    qf_sc[...] = jnp.concatenate([(qg * scale).astype(BF16), selneg.astype(BF16), slf.astype(BF16)], axis=1)

    def kv_tiles(j):
        start = pl.multiple_of(j * blk, blk)
        return kf_ref[0, 0, pl.ds(start, blk), :], vf_ref[0, 0, pl.ds(start, blk), :]

    qi = lax.broadcasted_iota(jnp.int32, (rows, blk), 0) % blk
    ki = lax.broadcasted_iota(jnp.int32, (rows, blk), 1)
    kf, vf = kv_tiles(qb)
    s = jnp.where(ki <= qi, _nt(qf_sc[...], kf), NEG_INF)
    m = jnp.max(s, axis=1, keepdims=True)
    m_sc[...] = jnp.broadcast_to(m, (rows, 128))
    acc_sc[...] = _dot(jnp.exp(s - m).astype(BF16), vf)

    def scores(j):
        return _nt(qf_sc[...], kv_tiles(jnp.minimum(j, qb))[0])

    def fold(blocks, s_cur, s_next=()):
        for ref, j in zip(s_next, (blocks[-1] + 1, blocks[-1] + 2)):
            ref[...] = scores(j)
        ss = [ref[...] for ref in s_cur]
        m_old = m_sc[...]
        m_new = m_old
        for s in ss:
            m_new = jnp.maximum(m_new, jnp.max(s, axis=1, keepdims=True))
        m2 = jnp.concatenate([m_new, m_new], axis=1)
        acc = acc_sc[...] * jnp.exp(m_old - m_new)
        for s, j in zip(ss, blocks):
            acc = acc + _dot(jnp.exp(s - m2).astype(BF16), kv_tiles(j)[1])
        acc_sc[...] = acc
        m_sc[...] = m_new

    set_a, set_b = (sa0_sc, sa1_sc), (sb0_sc, sb1_sc)
    sa0_sc[...] = scores(0)
    sa1_sc[...] = scores(1)

    def quad(i, carry):
        fold((4 * i, 4 * i + 1), set_a, set_b)
        fold((4 * i + 2, 4 * i + 3), set_b, set_a)
        return carry

    lax.fori_loop(0, qb // 4, quad, 0)
    done = qb // 4 * 4

    @pl.when(qb % 4 >= 2)
    def _():
        fold((done, done + 1), set_a)

    @pl.when(qb % 4 == 1)
    def _():
        fold((done,), set_a[:1])

    @pl.when(qb % 4 == 3)
    def _():
        sb0_sc[...] = scores(done + 2)
        fold((done + 2,), set_b[:1])

    acc = acc_sc[...]
    out = acc[:, :ATT_HEAD] / acc[:, ATT_HEAD:ATT_HEAD + 1]
    o_ref[0] = jnp.concatenate([out[hh * blk:(hh + 1) * blk] for hh in range(ATT_GROUP)], axis=1)


def moba_prompt(qkv, kf, vf, means, slf):
    b, t, _ = qkv.shape
    nb = t // MOBA_BLOCK
    assert nb <= 32
    nbl = 128
    means = jnp.pad(means, ((0, 0), (0, 0), (0, nbl - nb), (0, 0)))
    rows = ATT_GROUP * MOBA_BLOCK
    gw = ATT_GROUP * ATT_HEAD
    return pl.pallas_call(
        _moba_prompt_body,
        grid=(b, ATT_KV_HEADS, nb),
        in_specs=[pl.BlockSpec((1, MOBA_BLOCK, gw), lambda i, g, j: (i, j, g)),
                  pl.BlockSpec((1, 1, t, 128), lambda i, g, j: (i, g, 0, 0)),
                  pl.BlockSpec((1, 1, t, 128), lambda i, g, j: (i, g, 0, 0)),
                  pl.BlockSpec((1, 1, nbl, ATT_HEAD), lambda i, g, j: (i, g, 0, 0)),
                  pl.BlockSpec((1, ATT_GROUP, 32), lambda i, g, j: (g, 0, 0))],
        out_specs=pl.BlockSpec((1, MOBA_BLOCK, gw), lambda i, g, j: (i, j, g)),
        out_shape=jax.ShapeDtypeStruct((b, t, ATT_Q_COLS), F32),
        scratch_shapes=[pltpu.VMEM((rows, 128), BF16), pltpu.VMEM((rows, 128), F32), pltpu.VMEM((rows, 128), F32),
                        *([pltpu.VMEM((rows, MOBA_BLOCK), F32)] * 4)],
        compiler_params=_cparams(("parallel", "parallel", "arbitrary")),
    )(qkv, kf, vf, means, slf)


def _moba_sample_body(pt_ref, *refs, nbp, bps, nq):
    npg = 2 * bps
    k_refs, v_refs = refs[:npg], refs[npg:2 * npg]
    q_ref, kn_ref, vn_ref, slope_ref, o_ref, mean_sc, m_sc, l_sc, acc_sc = refs[2 * npg:]
    j = pl.program_id(1)
    blk = MOBA_BLOCK
    rows = ATT_HEADS * nq
    grows = ATT_GROUP * nq
    past = nbp * blk
    scale = ATT_HEAD ** -0.5
    lane = lax.broadcasted_iota(jnp.int32, (rows, 128), 1)
    qi = (lax.broadcasted_iota(jnp.int32, (rows, 1), 0) % nq).astype(F32)
    qf = q_ref[0]
    qs = (qf * scale).astype(BF16)
    slope = slope_ref[...]

    @pl.when(j == 0)
    def _():
        mean_sc[...] = jnp.zeros_like(mean_sc)
        m_sc[...] = jnp.full(m_sc.shape, NEG_INF, F32)
        l_sc[...] = jnp.zeros_like(l_sc)

    off = lax.broadcasted_iota(jnp.int32, (1, blk), 1).astype(F32)
    for s in range(bps):
        bi = j * bps + s
        kblk = jnp.concatenate([k_refs[2 * s][0], k_refs[2 * s + 1][0]], axis=0)
        vblk = jnp.concatenate([v_refs[2 * s][0], v_refs[2 * s + 1][0]], axis=0)
        mean_sc[pl.ds(bi, 1), :] = jnp.sum(kblk, axis=0, keepdims=True) * (1.0 / blk)
        sc = _nt(qs, kblk.astype(BF16)) - slope * ((past - bi * blk).astype(F32) + qi - off)
        m = jnp.max(sc, axis=1, keepdims=True)
        p = jnp.exp(sc - m)
        m_sc[...] = jnp.where(lane == bi, m, m_sc[...])
        l_sc[...] = jnp.where(lane == bi, jnp.sum(p, axis=1, keepdims=True), l_sc[...])
        acc_sc[bi] = _dot(p.astype(BF16), vblk.astype(BF16))

    @pl.when(j == pl.num_programs(1) - 1)
    def _():
        gate = jnp.where(lane < nbp, _nt(_cat3(qf, 1), _cat3w(mean_sc[...], 1)), NEG_INF)
        sel = gate >= _top3_threshold(gate)
        ki = lax.broadcasted_iota(jnp.int32, (rows, 8), 1).astype(F32)
        s_own = jnp.where(ki <= qi, _nt(qs, kn_ref[0].astype(BF16)) - slope * (qi - ki), NEG_INF)
        m_all = jnp.where(sel, m_sc[...], NEG_INF)
        mx = jnp.maximum(jnp.max(m_all, axis=1, keepdims=True), jnp.max(s_own, axis=1, keepdims=True))
        w = jnp.where(sel, jnp.exp(m_all - mx), 0.0)
        p_own = jnp.exp(s_own - mx)
        den = jnp.sum(w * l_sc[...], axis=1, keepdims=True) + jnp.sum(p_own, axis=1, keepdims=True)
        acc = _dot(p_own.astype(BF16), vn_ref[0].astype(BF16))
        for n in range(nbp):
            acc = acc + w[:, n:n + 1] * acc_sc[n]
        out = acc / den
        for g in range(ATT_KV_HEADS):
            o_ref[0, g] = out[g * grows:(g + 1) * grows, g * ATT_HEAD:(g + 1) * ATT_HEAD]


SAMPLE_BLOCKS_PER_STEP = 4


def _pages2d(cache):
    return cache.reshape(cache.shape[0], PAGE_SIZE, ATT_KV_COLS)


def _block_diag_queries(q):
    db, nq, _ = q.shape
    qh = q.reshape(db, nq, ATT_KV_HEADS, ATT_GROUP, ATT_HEAD).transpose(0, 2, 3, 1, 4)
    eye = jnp.eye(ATT_KV_HEADS, dtype=q.dtype)
    return (qh[:, :, :, :, None, :] * eye[None, :, None, None, :, None]).reshape(db, ATT_HEADS * nq, ATT_KV_COLS)


def _new_rows(a):
    db, nq = a.shape[:2]
    return jnp.pad(a.reshape(db, nq, ATT_KV_COLS), ((0, 0), (0, 8 - nq), (0, 0)))


def moba_sample(page_table, cache_k, cache_v, q_bd, k_new, v_new, slope_rows, nq):
    db, n_pages = page_table.shape
    nbp = n_pages * PAGE_SIZE // MOBA_BLOCK
    bps = min(SAMPLE_BLOCKS_PER_STEP, nbp)
    rows = ATT_HEADS * nq
    grows = ATT_GROUP * nq

    def page_spec(slot):
        return pl.BlockSpec((1, PAGE_SIZE, ATT_KV_COLS),
                            lambda b, j, pt: (pt[b * n_pages + j * 2 * bps + slot], 0, 0))

    small = lambda shp: pl.BlockSpec((1,) + shp, lambda b, j, pt: (b,) + (0,) * len(shp))
    grid_spec = pltpu.PrefetchScalarGridSpec(
        num_scalar_prefetch=1,
        grid=(db, nbp // bps),
        in_specs=[page_spec(s) for s in range(2 * bps)] * 2
        + [small((rows, ATT_KV_COLS)), small((8, ATT_KV_COLS)), small((8, ATT_KV_COLS)),
           pl.BlockSpec((rows, 1), lambda b, j, pt: (0, 0))],
        out_specs=small((ATT_KV_HEADS, grows, ATT_HEAD)),
        scratch_shapes=[pltpu.VMEM((128, ATT_KV_COLS), F32), pltpu.VMEM((rows, 128), F32),
                        pltpu.VMEM((rows, 128), F32), pltpu.VMEM((nbp, rows, ATT_KV_COLS), F32)],
    )
    return pl.pallas_call(
        functools.partial(_moba_sample_body, nbp=nbp, bps=bps, nq=nq),
        grid_spec=grid_spec,
        out_shape=jax.ShapeDtypeStruct((db, ATT_KV_HEADS, grows, ATT_HEAD), F32),
        compiler_params=_cparams(("parallel", "arbitrary")),
    )(page_table.reshape(-1), *([cache_k] * (2 * bps)), *([cache_v] * (2 * bps)), q_bd, k_new, v_new, slope_rows)


def _row(a):
    return a.reshape(1, -1).astype(F32)


def _even_params(i, w_in_ab, rwkv_mu, rwkv_w0, rwkv_w2, rwkv_a0, rwkv_a2, rwkv_g2, rwkv_k_k, rwkv_k_a, rwkv_r_k,
                 rwkv_ln_w, rwkv_ln_b, ssm_conv_w, ssm_conv_b, ssm_dt_bias, ssm_a_log, ssm_d, ssm_norm_w):
    w = w_in_ab[i]
    ssm0 = RW_COLS
    zeros = lambda n: jnp.zeros((D_MODEL, n), w.dtype)
    w_pack = jnp.concatenate([
        w[:, :RW_COLS],
        w[:, ssm0 + SSM_INNER + SSM_CONV_DIM:], zeros(PROJ_Z - PROJ_DT - SSM_HEADS),
        w[:, ssm0:ssm0 + SSM_INNER],
        w[:, ssm0 + SSM_INNER:ssm0 + SSM_INNER + SSM_CONV_DIM]], axis=1).astype(BF16)
    half = RW_LORA // 2
    zl = jnp.zeros((half, RW_DIM), F32)
    wl = jnp.concatenate([jnp.concatenate([rwkv_w2[i], zl], axis=1),
                          jnp.concatenate([zl, rwkv_a2[i]], axis=1)], axis=0)
    half_of = np.arange(128) // RW_HEAD
    bones2 = np.tile((half_of[:, None] == half_of[None, :]).astype(np.float32), (2, 1))
    rw = dict(mu=_row(rwkv_mu[i]), w0=_row(rwkv_w0[i]), a0=_row(rwkv_a0[i]), k_k=_row(rwkv_k_k[i]),
              k_a=_row(rwkv_k_a[i]), r_k=_row(rwkv_r_k[i]), ln_w=_row(rwkv_ln_w[i]), ln_b=_row(rwkv_ln_b[i]),
              wl=_cat3w(wl, 0), g2=_cat3w(rwkv_g2[i], 0), bones2=jnp.asarray(bones2, BF16),
              pair_w=jnp.asarray(np.kron(np.eye(4, dtype=np.float32), np.ones((RW_HEAD, RW_HEAD), np.float32)), BF16))
    pad128 = lambda a: jnp.pad(_row(a), ((0, 0), (0, 128 - a.shape[-1])))
    expand = (np.arange(128)[:, None] == (np.arange(SSM_INNER) // SSM_HEAD)[None, :]).astype(np.float32)
    ssm = dict(conv_w=ssm_conv_w[i], conv_b=_row(ssm_conv_b[i]), dt_bias=pad128(ssm_dt_bias[i]),
               a_log=pad128(ssm_a_log[i]), d_full=_row(jnp.repeat(ssm_d[i], SSM_HEAD)), norm_w=_row(ssm_norm_w[i]),
               expand=jnp.asarray(np.tile(expand, (3, 1)), BF16))
    return w_pack, rw, ssm


def _mixer_ab(x, b, t, norm_g, w_pack, rw, ssm, w_out, shift0, wkv0, conv0, ssm0, tm):
    tm_in = 2 * tm if (b * t) % (2 * tm) == 0 else tm
    proj = norm_matmul(x, norm_g, w_pack, tm_in, SSM_CONV_DIM).reshape(b, t, PROJ_COLS)
    shift_new = proj[:, t - 1, :RW_COLS]
    conv_new = proj[:, t - (SSM_CONV - 1):, PROJ_XBC:]
    if t % SSD_CHUNK == 0:
        proj_rw, proj_ssm, tc = proj, proj, SSD_CHUNK
    else:
        proj_rw = jnp.pad(proj, ((0, 0), (0, 8 - t), (0, 0)))
        proj_ssm = jnp.pad(proj, ((0, 0), (0, SSD_CHUNK - t), (0, 0)))
        tc = 8
    nbb = next(n for n in (4, 2, 1) if b % n == 0)
    y_rw, wkv_new = rwkv7(proj_rw, shift0.reshape(b, 1, RW_COLS), wkv0, rw, nbb, tc, min(t, tc))
    conv0p = jnp.pad(conv0, ((0, 0), (8 - (SSM_CONV - 1), 0), (0, 0)))
    y_ssm, ssm_new = mamba2(proj_ssm, conv0p, ssm0.reshape(b, SSM_INNER, SSM_STATE), ssm, min(t, SSD_CHUNK))
    x = matmul_residual(x, [y_rw[:, :t].reshape(b * t, RW_DIM), y_ssm[:, :t].reshape(b * t, SSM_INNER)],
                        [w_out[:RW_DIM], w_out[RW_DIM:]], tm)
    return x, (shift_new, wkv_new, conv_new, ssm_new.reshape(b, SSM_HEADS, SSM_HEAD, SSM_STATE))


def kernel(x_prompt, x_sample, state_rwkv_shift, state_rwkv_wkv, state_ssm_conv, state_ssm, cache_k, cache_v, page_table, norm_mix, norm_ffn, norm_final, w_in_ab, rwkv_mu, rwkv_w0, rwkv_w2, rwkv_a0, rwkv_a2, rwkv_g2, rwkv_k_k, rwkv_k_a, rwkv_r_k, rwkv_ln_w, rwkv_ln_b, ssm_conv_w, ssm_conv_b, ssm_dt_bias, ssm_a_log, ssm_d, ssm_norm_w, w_out_ab, ffn_w_gate, ffn_w_up, ffn_w_down, attn_w_qkv, attn_w_o, moe_router, moe_w_gate, moe_w_up, moe_w_down):
    bp, tp, _ = x_prompt.shape
    db, ts, _ = x_sample.shape
    depth = norm_mix.shape[0]
    tm_p, tm_s = 512, db * ts
    xp = x_prompt.reshape(bp * tp, D_MODEL)
    xs = x_sample.reshape(db * ts, D_MODEL)
    slopes = _alibi_slopes()
    slf = jnp.asarray(_slope_features().reshape(ATT_KV_HEADS, ATT_GROUP, 32))
    slope_rows = jnp.asarray(np.repeat(slopes, ts).reshape(ATT_HEADS * ts, 1))
    st = {n: [] for n in ("p_shift", "p_wkv", "p_conv", "p_ssm", "p_k", "p_v",
                          "s_shift", "s_wkv", "s_conv", "s_ssm", "s_k", "s_v")}
    for l in range(depth):
        i = l // 2
        g_mix, g_ffn = _row(norm_mix[l]), _row(norm_ffn[l])
        if l % 2 == 0:
            w_pack, rw, ssm = _even_params(i, w_in_ab, rwkv_mu, rwkv_w0, rwkv_w2, rwkv_a0, rwkv_a2, rwkv_g2,
                                           rwkv_k_k, rwkv_k_a, rwkv_r_k, rwkv_ln_w, rwkv_ln_b, ssm_conv_w,
                                           ssm_conv_b, ssm_dt_bias, ssm_a_log, ssm_d, ssm_norm_w)
            w_out = w_out_ab[i].astype(BF16)
            xp, sp = _mixer_ab(xp, bp, tp, g_mix, w_pack, rw, ssm, w_out,
                               jnp.zeros((bp, RW_COLS), F32), jnp.zeros((bp, RW_HEADS, RW_HEAD, RW_HEAD), F32),
                               jnp.zeros((bp, SSM_CONV - 1, SSM_CONV_DIM), F32),
                               jnp.zeros((bp, SSM_HEADS, SSM_HEAD, SSM_STATE), F32), tm_p)
            xs, ss = _mixer_ab(xs, db, ts, g_mix, w_pack, rw, ssm, w_out, state_rwkv_shift[i], state_rwkv_wkv[i],
                               state_ssm_conv[i], state_ssm[i], tm_s)
            for pre, new in (("p", sp), ("s", ss)):
                for name, val in zip(("shift", "wkv", "conv", "ssm"), new):
                    st[f"{pre}_{name}"].append(val)
            wg, wu, wd = ffn_w_gate[i].astype(BF16), ffn_w_up[i].astype(BF16), ffn_w_down[i].astype(BF16)
            xp = ffn_swiglu(xp, g_ffn, wg, wu, wd, 1024, 256)
            xs = ffn_swiglu(xs, g_ffn, wg, wu, wd, tm_s, 256)
        else:
            w_qkv, w_o = attn_w_qkv[i].astype(BF16), attn_w_o[i].astype(BF16)
            qkv_p, kf, vf = qkv_proj(xp, g_mix, w_qkv, bp, tp, tm_p)
            nb = tp // MOBA_BLOCK
            means = block_means(qkv_p, bp * nb).reshape(bp, nb, ATT_KV_HEADS, ATT_HEAD).transpose(0, 2, 1, 3)
            k_p = qkv_p[:, ATT_Q_COLS:ATT_Q_COLS + ATT_KV_COLS].reshape(bp, tp, ATT_KV_HEADS, ATT_HEAD)
            v_p = qkv_p[:, ATT_Q_COLS + ATT_KV_COLS:].reshape(bp, tp, ATT_KV_HEADS, ATT_HEAD)
            o_p = moba_prompt(qkv_p.reshape(bp, tp, -1), kf, vf, means, slf)
            xp = matmul_residual(xp, [o_p.reshape(bp * tp, ATT_Q_COLS)], [w_o], tm_p)
            st["p_k"].append(k_p)
            st["p_v"].append(v_p)

            qkv_s = norm_matmul(xs, g_mix, w_qkv, tm_s, 512).reshape(db, ts, -1)
            kn = qkv_s[..., ATT_Q_COLS:ATT_Q_COLS + ATT_KV_COLS].reshape(db, ts, ATT_KV_HEADS, ATT_HEAD)
            vn = qkv_s[..., ATT_Q_COLS + ATT_KV_COLS:].reshape(db, ts, ATT_KV_HEADS, ATT_HEAD)
            o_s = moba_sample(page_table, _pages2d(cache_k[i]), _pages2d(cache_v[i]),
                              _block_diag_queries(qkv_s[..., :ATT_Q_COLS]), _new_rows(kn), _new_rows(vn),
                              slope_rows, ts)
            o_s = o_s.reshape(db, ATT_HEADS, ts, ATT_HEAD).transpose(0, 2, 1, 3).reshape(db * ts, ATT_Q_COLS)
            xs = matmul_residual(xs, [o_s], [w_o], tm_s)
            st["s_k"].append(kn)
            st["s_v"].append(vn)

            router = _cat3w(jnp.pad(moe_router[i], ((0, 0), (0, 128 - N_EXPERTS))), 0)
            wg, wu, wd = moe_w_gate[i].astype(BF16), moe_w_up[i].astype(BF16), moe_w_down[i].astype(BF16)
            final = l == depth - 1
            xp = moe_swiglu(xp, g_ffn, router, wg, wu, wd, _row(norm_final), 2 * tm_p, final)
            xs = moe_swiglu(xs, g_ffn, router, wg, wu, wd, _row(norm_final), tm_s, final)
    y_prompt = xp.reshape(bp, tp, D_MODEL)
    y_sample = xs.reshape(db, ts, D_MODEL)
    stack = lambda n: jnp.stack(st[n])
    return (y_prompt, y_sample,
            stack("p_shift"), stack("p_wkv"), stack("p_conv"), stack("p_ssm"), stack("p_k"), stack("p_v"),
            stack("s_shift"), stack("s_wkv"), stack("s_conv"), stack("s_ssm"), stack("s_k"), stack("s_v"))
```

```python
import functools

import numpy as np
import jax
import jax.numpy as jnp
from jax import lax
from jax.experimental import pallas as pl
from jax.experimental.pallas import tpu as pltpu

F32 = jnp.float32
BF16 = jnp.bfloat16
HI = lax.Precision.HIGHEST
NEG_INF = float("-inf")

D_MODEL = 1024
NORM_EPS = 1e-6

RW_HEAD = 64
RW_HEADS = 8
RW_DIM = RW_HEADS * RW_HEAD
RW_LORA = 128
RW_GATE = 128
RW_COLS = 3 * RW_DIM + RW_LORA + RW_GATE
RW_LN_EPS = 64e-5

SSM_INNER = 1024
SSM_HEAD = 64
SSM_HEADS = 16
SSM_GROUPS = 2
SSM_STATE = 128
SSM_CONV = 4
SSM_CONV_DIM = SSM_INNER + 2 * SSM_GROUPS * SSM_STATE
SSD_CHUNK = 128

PROJ_RW = 0
PROJ_DT = RW_COLS
PROJ_Z = 2048
PROJ_XBC = 3072
PROJ_COLS = PROJ_XBC + SSM_CONV_DIM

ATT_HEADS = 16
ATT_KV_HEADS = 4
ATT_HEAD = 64
ATT_GROUP = ATT_HEADS // ATT_KV_HEADS
ATT_Q_COLS = ATT_HEADS * ATT_HEAD
ATT_KV_COLS = ATT_KV_HEADS * ATT_HEAD
MOBA_BLOCK = 256
MOBA_TOPK = 3
MASK_NEG = -1e30
PAGE_SIZE = 128

N_EXPERTS = 8

VMEM_LIMIT = 56 * 1024 * 1024


def _cparams(sem):
    return pltpu.CompilerParams(dimension_semantics=sem, vmem_limit_bytes=VMEM_LIMIT)


def _nt(a, b, precision=None):
    return lax.dot_general(a, b, (((1,), (1,)), ((), ())), precision=precision,
                           preferred_element_type=F32)


def _dot(a, b, precision=None):
    return jnp.dot(a, b, precision=precision, preferred_element_type=F32)


def _split2(x):
    hi = x.astype(BF16)
    return hi, (x - hi.astype(F32)).astype(BF16)


def _split3(x):
    hi = x.astype(BF16)
    r = x - hi.astype(F32)
    mid = r.astype(BF16)
    return hi, mid, (r - mid.astype(F32)).astype(BF16)


def _cat3(x, axis):
    hi, lo = _split2(x)
    return jnp.concatenate([hi, lo, hi], axis=axis)


def _cat3w(w, axis):
    hi, lo = _split2(w)
    return jnp.concatenate([hi, hi, lo], axis=axis)


def _rms(x, g):
    return x * lax.rsqrt(jnp.mean(x * x, axis=-1, keepdims=True) + NORM_EPS) * g


def _softplus(x):
    return jnp.maximum(x, 0.0) + jnp.log(1.0 + jnp.exp(-jnp.abs(x)))


def _sigmoid(x):
    return 1.0 / (1.0 + jnp.exp(-x))


def _silu(x):
    return x * _sigmoid(x)


def _nm_body(x_ref, g_ref, w_ref, o_ref, h_sc):
    @pl.when(pl.program_id(1) == 0)
    def _():
        h_sc[...] = _rms(x_ref[...], g_ref[...]).astype(BF16)

    o_ref[...] = _dot(h_sc[...], w_ref[...])


def norm_matmul(x, g, w, tm, tn):
    m, k = x.shape
    n = w.shape[1]
    return pl.pallas_call(
        _nm_body,
        grid=(m // tm, n // tn),
        in_specs=[pl.BlockSpec((tm, k), lambda i, j: (i, 0)),
                  pl.BlockSpec((1, k), lambda i, j: (0, 0)),
                  pl.BlockSpec((k, tn), lambda i, j: (0, j))],
        out_specs=pl.BlockSpec((tm, tn), lambda i, j: (i, j)),
        out_shape=jax.ShapeDtypeStruct((m, n), F32),
        scratch_shapes=[pltpu.VMEM((tm, k), BF16)],
        compiler_params=_cparams(("parallel", "arbitrary")),
    )(x, g, w)


def _mmres_body(*refs, n):
    res_ref, a_refs, w_refs, o_ref = refs[0], refs[1:1 + n], refs[1 + n:1 + 2 * n], refs[1 + 2 * n]
    acc = res_ref[...]
    for a_ref, w_ref in zip(a_refs, w_refs):
        acc = acc + _dot(a_ref[...].astype(BF16), w_ref[...])
    o_ref[...] = acc


def matmul_residual(res, acts, ws, tm):
    m, d = res.shape
    n = len(acts)
    in_specs = [pl.BlockSpec((tm, d), lambda i: (i, 0))]
    in_specs += [pl.BlockSpec((tm, a.shape[1]), lambda i: (i, 0)) for a in acts]
    in_specs += [pl.BlockSpec(w.shape, lambda i: (0, 0)) for w in ws]
    return pl.pallas_call(
        functools.partial(_mmres_body, n=n),
        grid=(m // tm,),
        in_specs=in_specs,
        out_specs=pl.BlockSpec((tm, d), lambda i: (i, 0)),
        out_shape=jax.ShapeDtypeStruct((m, d), F32),
        compiler_params=_cparams(("parallel",)),
    )(res, *acts, *ws)


def _ffn_body(x_ref, g_ref, wg_ref, wu_ref, wd_ref, o_ref, h_sc, acc_sc):
    j = pl.program_id(1)

    @pl.when(j == 0)
    def _():
        h_sc[...] = _rms(x_ref[...], g_ref[...]).astype(BF16)
        acc_sc[...] = jnp.zeros_like(acc_sc)

    h = h_sc[...]
    act = _silu(_dot(h, wg_ref[...])) * _dot(h, wu_ref[...])
    acc_sc[...] += _dot(act.astype(BF16), wd_ref[...])

    @pl.when(j == pl.num_programs(1) - 1)
    def _():
        o_ref[...] = x_ref[...] + acc_sc[...]


def ffn_swiglu(x, g, wg, wu, wd, tm, tf):
    m, d = x.shape
    f = wg.shape[1]
    return pl.pallas_call(
        _ffn_body,
        grid=(m // tm, f // tf),
        in_specs=[pl.BlockSpec((tm, d), lambda i, j: (i, 0)),
                  pl.BlockSpec((1, d), lambda i, j: (0, 0)),
                  pl.BlockSpec((d, tf), lambda i, j: (0, j)),
                  pl.BlockSpec((d, tf), lambda i, j: (0, j)),
                  pl.BlockSpec((tf, d), lambda i, j: (j, 0))],
        out_specs=pl.BlockSpec((tm, d), lambda i, j: (i, 0)),
        out_shape=jax.ShapeDtypeStruct((m, d), F32),
        scratch_shapes=[pltpu.VMEM((tm, d), BF16), pltpu.VMEM((tm, d), F32)],
        compiler_params=_cparams(("parallel", "arbitrary")),
    )(x, g, wg, wu, wd)


MOE_SUB = 512
MOE_CHUNK = 160


def _moe_body(x_ref, g_ref, r_ref, wg_ref, wu_ref, wd_ref, gf_ref, o_ref, h_sc, comb_sc, acc_sc, rk_sc, rkt_sc, *,
              final_norm, chunk):
    e = pl.program_id(1)
    tm = x_ref.shape[0]
    lane = lax.broadcasted_iota(jnp.int32, (tm, 128), 1)

    @pl.when(e == 0)
    def _():
        hf = _rms(x_ref[...], g_ref[...])
        h_sc[...] = hf.astype(BF16)
        lanef = lane.astype(F32)
        logits = jnp.where(lane < N_EXPERTS, _dot(_cat3(hf, 1), r_ref[...]), NEG_INF)
        m1 = jnp.max(logits, axis=1, keepdims=True)
        i1 = jnp.min(jnp.where(logits == m1, lanef, 128.0), axis=1, keepdims=True)
        mask1 = lanef == i1
        rest = jnp.where(mask1, NEG_INF, logits)
        m2 = jnp.max(rest, axis=1, keepdims=True)
        i2 = jnp.min(jnp.where(rest == m2, lanef, 128.0), axis=1, keepdims=True)
        mask2 = lanef == i2
        e2 = jnp.exp(m2 - m1)
        den = 1.0 + e2
        comb_sc[...] = jnp.where(mask1, 1.0 / den, 0.0) + jnp.where(mask2, e2 / den, 0.0)
        acc_sc[...] = jnp.zeros_like(acc_sc)
        if chunk:
            earlier = (lax.broadcasted_iota(jnp.int32, (MOE_SUB, MOE_SUB), 1)
                       < lax.broadcasted_iota(jnp.int32, (MOE_SUB, MOE_SUB), 0))
            earlier = jnp.where(earlier, 1.0, 0.0).astype(BF16)
            routed = jnp.where(mask1 | mask2, 1.0, 0.0)
            for st in range(tm // MOE_SUB):
                sl = slice(st * MOE_SUB, (st + 1) * MOE_SUB)
                rank = _dot(earlier, routed[sl].astype(BF16))
                rank = jnp.where(routed[sl] > 0.0, rank, -1.0)
                rk_sc[sl, :] = rank
                rkt_sc[st] = rank.T

    def expert(rows):
        act = _silu(_dot(rows, wg_ref[0])) * _dot(rows, wu_ref[0])
        return _dot(act.astype(BF16), wd_ref[0])

    this = lane == e
    if not chunk:
        c = jnp.sum(jnp.where(this, comb_sc[...], 0.0), axis=1, keepdims=True)
        acc_sc[...] += c * expert(h_sc[...])
    else:
        wide = -(-chunk // 128) * 128
        slot = lax.broadcasted_iota(jnp.int32, (chunk, MOE_SUB), 0).astype(F32)
        lane_w = lax.broadcasted_iota(jnp.int32, (MOE_SUB, wide), 1)
        slot_t = jnp.where(lane_w < chunk, lane_w.astype(F32), jnp.nan)
        sub8 = lax.broadcasted_iota(jnp.int32, (8, MOE_SUB), 0)
        this_sub = lax.broadcasted_iota(jnp.int32, (MOE_SUB, 128), 1) == e
        pad_rows = jnp.zeros((wide - chunk, x_ref.shape[1]), BF16)
        for st in range(tm // MOE_SUB):
            sl = slice(st * MOE_SUB, (st + 1) * MOE_SUB)
            rank_col = jnp.sum(jnp.where(this_sub, rk_sc[sl, :], 0.0), axis=1, keepdims=True)
            rank_row = jnp.sum(jnp.where(sub8 == e, rkt_sc[st, 0:8, :], 0.0), axis=0, keepdims=True)
            c = jnp.sum(jnp.where(this_sub, comb_sc[sl, :], 0.0), axis=1, keepdims=True)
            count = jnp.max(rank_row, axis=1, keepdims=True)[0, 0].astype(jnp.int32) + 1

            def one_chunk(k, carry):
                base = (k * chunk).astype(F32)
                gather = jnp.where(rank_row - base == slot, 1.0, 0.0).astype(BF16)
                y_hi, y_lo = _split2(expert(_dot(gather, h_sc[sl, :]).astype(BF16)))
                scatter = jnp.where(rank_col - base == slot_t, 1.0, 0.0).astype(BF16)
                back = (_dot(scatter, jnp.concatenate([y_hi, pad_rows], axis=0))
                        + _dot(scatter, jnp.concatenate([y_lo, pad_rows], axis=0)))
                acc_sc[sl, :] += c * back
                return carry

            lax.fori_loop(0, (count + chunk - 1) // chunk, one_chunk, 0)

    @pl.when(e == pl.num_programs(1) - 1)
    def _():
        y = x_ref[...] + acc_sc[...]
        if final_norm:
            y = _rms(y, gf_ref[...])
        o_ref[...] = y


def moe_swiglu(x, g, router, wg, wu, wd, gfinal, tm, final_norm):
    m, d = x.shape
    ne, _, fe = wg.shape
    chunk = MOE_CHUNK if tm % MOE_SUB == 0 else 0
    nsub = max(tm // MOE_SUB, 1)
    return pl.pallas_call(
        functools.partial(_moe_body, final_norm=final_norm, chunk=chunk),
        grid=(m // tm, ne),
        in_specs=[pl.BlockSpec((tm, d), lambda i, e: (i, 0)),
                  pl.BlockSpec((1, d), lambda i, e: (0, 0)),
                  pl.BlockSpec((3 * d, 128), lambda i, e: (0, 0)),
                  pl.BlockSpec((1, d, fe), lambda i, e: (e, 0, 0)),
                  pl.BlockSpec((1, d, fe), lambda i, e: (e, 0, 0)),
                  pl.BlockSpec((1, fe, d), lambda i, e: (e, 0, 0)),
                  pl.BlockSpec((1, d), lambda i, e: (0, 0))],
        out_specs=pl.BlockSpec((tm, d), lambda i, e: (i, 0)),
        out_shape=jax.ShapeDtypeStruct((m, d), F32),
        scratch_shapes=[pltpu.VMEM((tm, d), BF16), pltpu.VMEM((tm, 128), F32), pltpu.VMEM((tm, d), F32),
                        pltpu.VMEM((tm, 128), F32), pltpu.VMEM((nsub, 128, min(tm, MOE_SUB)), F32)],
        compiler_params=_cparams(("parallel", "arbitrary")),
    )(x, g, router, wg, wu, wd, gfinal)


RW_PAIRS = RW_HEADS // 2
RW_ROWS = RW_PAIRS * RW_HEAD


def _rwkv_body(p_ref, sh0_ref, s0_ref, mu_ref, w0_ref, a0_ref, kk_ref, ka_ref, rk_ref, lnw_ref, lnb_ref,
               wl_ref, g2_ref, b2_ref, pw_ref, y_ref, sfin_ref,
               s_sc, sk_sc, prev_sc, kk_sc, w_sc, b_sc, k_sc, r_sc, v_sc, g_sc, y_sc, *, nbb, tc, t_valid):
    c = pl.program_id(1)

    @pl.when(c == 0)
    def _():
        s_sc[...] = s0_ref[...]
        prev_sc[...] = sh0_ref[...]

    b2 = b2_ref[...]

    def head_sums(x, low=True):
        if low:
            hi, lo = _split2(x)
        else:
            hi = x.astype(BF16)
            lo = jnp.zeros_like(hi)
        return _dot(jnp.concatenate([hi, lo], axis=1), b2)

    def head_sums_wide(x):
        return jnp.concatenate([head_sums(x[:, q * 128:(q + 1) * 128]) for q in range(RW_PAIRS)], axis=1)

    for bi in range(nbb):
        p = p_ref[bi]
        row = lax.broadcasted_iota(jnp.int32, p.shape, 0)
        prev = jnp.where(row == 0, prev_sc[bi], pltpu.roll(p, 1, axis=0))
        prev_sc[bi] = p[tc - 1:tc, :]
        u = p + (prev - p) * mu_ref[...]
        r = u[:, 0:RW_DIM]
        k = u[:, RW_DIM:2 * RW_DIM]
        v = u[:, 2 * RW_DIM:3 * RW_DIM]
        lo = u[:, 3 * RW_DIM:3 * RW_DIM + RW_LORA]
        gd = u[:, 3 * RW_DIM + RW_LORA:]
        lane = lax.broadcasted_iota(jnp.int32, lo.shape, 1)
        lora = _dot(_cat3(jnp.where(lane < RW_LORA // 2, jnp.tanh(lo), lo), 1), wl_ref[...])
        wlog = -_softplus(-(w0_ref[...] + lora[:, :RW_DIM])) - 0.5
        a = _sigmoid(a0_ref[...] + lora[:, RW_DIM:])
        g_sc[bi] = _dot(_cat3(_sigmoid(gd), 1), g2_ref[...])
        kk = k * kk_ref[...]
        kk = kk / jnp.maximum(jnp.sqrt(head_sums_wide(kk * kk)), 1e-12)
        kk_sc[bi] = kk
        w_sc[bi] = jnp.exp(-jnp.exp(wlog))
        b_sc[bi] = kk * a
        k_sc[bi] = k * (1.0 + (a - 1.0) * ka_ref[...])
        r_sc[bi] = r
        v_sc[bi] = v

    rows = nbb * RW_ROWS
    vi = lax.broadcasted_iota(jnp.int32, (rows, 128), 0) % RW_HEAD
    li = lax.broadcasted_iota(jnp.int32, (rows, 128), 1)
    key_lane = li % RW_HEAD
    diag = (key_lane == vi).astype(F32)
    steps = 8 if t_valid % 8 == 0 else t_valid

    def rows_of(x8, j):
        return jnp.concatenate([jnp.broadcast_to(x8[bi][j:j + 1, q * 128:(q + 1) * 128], (RW_HEAD, 128))
                                for bi in range(nbb) for q in range(RW_PAIRS)], axis=0)

    pair_w = pw_ref[...]
    zero_half = jnp.zeros((rows, 128), BF16)

    def pair_sums(a, b):
        lhs = jnp.concatenate([a.astype(BF16), zero_half if b is None else b.astype(BF16)], axis=1)
        out = _dot(lhs, pair_w)
        return out[:, :128], out[:, 128:]

    first8 = [kk_sc[bi, pl.ds(0, 8), :] for bi in range(nbb)]
    sk_sc[...] = pair_sums(s_sc[...].reshape(rows, 128) * rows_of(first8, 0), None)[0]

    def group(t8, carry):
        t0 = pl.multiple_of(t8 * 8, 8)
        tn = pl.multiple_of(jnp.minimum(t0 + 8, tc - 8), 8)
        kk8, w8, b8, k8, r8, v8 = ([sc[bi, pl.ds(t0, 8), :] for bi in range(nbb)]
                                   for sc in (kk_sc, w_sc, b_sc, k_sc, r_sc, v_sc))
        kk_next = [kk_sc[bi, pl.ds(tn, 8), :] for bi in range(nbb)]
        s = s_sc[...].reshape(rows, 128)
        sk = sk_sc[...]
        ycols = jnp.zeros((rows, 128), F32)
        for j in range(steps):
            if j % 2 == 0:
                vcols = pair_sums(rows_of(v8, j) * diag, rows_of(v8, j + 1) * diag if j + 1 < steps else None)
            s = s * rows_of(w8, j) - sk * rows_of(b8, j) + vcols[j % 2] * rows_of(k8, j)
            kk_after = rows_of(kk8, j + 1) if j + 1 < 8 else rows_of(kk_next, 0)
            sk, yb = pair_sums(s * kk_after, s * rows_of(r8, j))
            ycols = jnp.where(key_lane == j, yb, ycols)
        s_sc[...] = s.reshape(nbb, RW_ROWS, 128)
        sk_sc[...] = sk
        h64 = RW_HEAD
        for bi in range(nbb):
            pieces = []
            for q in range(0, RW_PAIRS, 2):
                r0 = (bi * RW_PAIRS + q) * h64
                tr = ycols[r0:r0 + 2 * h64].T
                pieces += [tr[0:8, 0:h64], tr[h64:h64 + 8, 0:h64], tr[0:8, h64:], tr[h64:h64 + 8, h64:]]
            y_sc[bi, pl.ds(t0, 8), :] = jnp.concatenate(pieces, axis=1)
        return carry

    lax.fori_loop(0, -(-t_valid // 8), group, 0)

    inv = 1.0 / RW_HEAD
    for bi in range(nbb):
        y = y_sc[bi]
        yc = y - head_sums_wide(y) * inv
        var = head_sums_wide(yc * yc) * inv
        yn = yc * lax.rsqrt(var + RW_LN_EPS) * lnw_ref[...] + lnb_ref[...]
        bonus = head_sums_wide(r_sc[bi] * k_sc[bi] * rk_ref[...]) * v_sc[bi]
        y_ref[bi] = (yn + bonus) * g_sc[bi]

    @pl.when(c == pl.num_programs(1) - 1)
    def _():
        sfin_ref[...] = s_sc[...]


def _wkv_to_rows(s):
    b = s.shape[0]
    return s.reshape(b, RW_PAIRS, 2, RW_HEAD, RW_HEAD).transpose(0, 1, 3, 2, 4).reshape(b, RW_ROWS, 128)


def _rows_to_wkv(s):
    b = s.shape[0]
    return s.reshape(b, RW_PAIRS, RW_HEAD, 2, RW_HEAD).transpose(0, 1, 3, 2, 4).reshape(b, RW_HEADS, RW_HEAD, RW_HEAD)


def rwkv7(proj, shift0, wkv0, prm, nbb, tc, t_valid):
    b, t, _ = proj.shape
    vec = lambda n: pl.BlockSpec((1, n), lambda i, c: (0, 0))
    full = lambda a: pl.BlockSpec(a.shape, lambda i, c: (0,) * a.ndim)
    seq = pltpu.VMEM((nbb, tc, RW_DIM), F32)
    y, s_fin = pl.pallas_call(
        functools.partial(_rwkv_body, nbb=nbb, tc=tc, t_valid=t_valid),
        grid=(b // nbb, t // tc),
        in_specs=[pl.BlockSpec((nbb, tc, RW_COLS), lambda i, c: (i, c, 0)),
                  pl.BlockSpec((nbb, 1, RW_COLS), lambda i, c: (i, 0, 0)),
                  pl.BlockSpec((nbb, RW_ROWS, 128), lambda i, c: (i, 0, 0)),
                  vec(RW_COLS), vec(RW_DIM), vec(RW_DIM), vec(RW_DIM), vec(RW_DIM), vec(RW_DIM), vec(RW_DIM),
                  vec(RW_DIM), full(prm["wl"]), full(prm["g2"]), full(prm["bones2"]), full(prm["pair_w"])],
        out_specs=[pl.BlockSpec((nbb, tc, RW_DIM), lambda i, c: (i, c, 0)),
                   pl.BlockSpec((nbb, RW_ROWS, 128), lambda i, c: (i, 0, 0))],
        out_shape=[jax.ShapeDtypeStruct((b, t, RW_DIM), F32),
                   jax.ShapeDtypeStruct((b, RW_ROWS, 128), F32)],
        scratch_shapes=[pltpu.VMEM((nbb, RW_ROWS, 128), F32), pltpu.VMEM((nbb * RW_ROWS, 128), F32),
                        pltpu.VMEM((nbb, 1, RW_COLS), F32),
                        seq, seq, seq, seq, seq, seq, seq, seq],
        compiler_params=_cparams(("parallel", "arbitrary")),
    )(proj, shift0, _wkv_to_rows(wkv0), prm["mu"], prm["w0"], prm["a0"], prm["k_k"], prm["k_a"], prm["r_k"],
      prm["ln_w"], prm["ln_b"], prm["wl"], prm["g2"], prm["bones2"], prm["pair_w"])
    return y, _rows_to_wkv(s_fin)


def _ssd_body(z_ref, xbc_ref, dt_ref, conv0_ref, h0_ref, cw_ref, cb_ref, dtb_ref, alog_ref, dsk_ref, nw_ref,
              e_ref, y_ref, hfin_ref, h_sc, ext_sc, *, t_valid):
    c = pl.program_id(1)
    ln = SSD_CHUNK

    @pl.when(c == 0)
    def _():
        h_sc[...] = h0_ref[0]
        ext_sc[0:8, :] = conv0_ref[0]

    xbc = xbc_ref[0]
    ext_sc[8:8 + ln, :] = xbc
    cw = cw_ref[...]
    conv = cb_ref[...] + ext_sc[5:5 + ln, :] * cw[0:1]
    conv = conv + ext_sc[6:6 + ln, :] * cw[1:2]
    conv = conv + ext_sc[7:7 + ln, :] * cw[2:3]
    conv = conv + xbc * cw[3:4]
    ext_sc[0:8, :] = xbc[ln - 8:ln, :]
    act = _silu(conv)
    xs = act[:, :SSM_INNER]
    bm = act[:, SSM_INNER:SSM_INNER + SSM_GROUPS * SSM_STATE].astype(BF16)
    cm = act[:, SSM_INNER + SSM_GROUPS * SSM_STATE:].astype(BF16)

    lane = lax.broadcasted_iota(jnp.int32, (ln, 128), 1)
    rowi = lax.broadcasted_iota(jnp.int32, (ln, 128), 0)
    dt = _softplus(dt_ref[0] + dtb_ref[...])
    if t_valid < ln:
        dt = jnp.where(rowi < t_valid, dt, 0.0)
    a = dt * jnp.where(lane[0:1] < SSM_HEADS, -jnp.exp(alog_ref[...]), 0.0)
    causal = lane <= rowi
    tri = jnp.where(causal, 1.0, 0.0).astype(BF16)
    tri3 = jnp.concatenate([tri, tri, tri], axis=1)
    acum = _dot(tri3, jnp.concatenate(_split3(a), axis=0))
    acum_t = _nt(jnp.concatenate(_split3(a.T), axis=1), tri3)
    e3 = e_ref[...]
    dt_full = _dot(jnp.concatenate(_split3(dt), axis=1), e3)
    ac_full = _dot(jnp.concatenate(_split3(acum), axis=1), e3)
    xdt = xs * dt_full
    xw = xdt * jnp.exp(ac_full[ln - 1:ln, :] - ac_full)
    eac = jnp.exp(ac_full)
    cd = jnp.broadcast_to(jnp.exp(acum_t[:, ln - 1:ln]), (128, 128))
    cd_rows = jnp.concatenate([jnp.broadcast_to(cd[h:h + 1, :], (SSM_HEAD, SSM_STATE)) for h in range(SSM_HEADS)],
                              axis=0)

    ys = []
    for g in range(SSM_GROUPS):
        bm_g = bm[:, g * SSM_STATE:(g + 1) * SSM_STATE]
        cm_g = cm[:, g * SSM_STATE:(g + 1) * SSM_STATE]
        cbm = _nt(cm_g, bm_g)
        pairs = SSM_HEADS // SSM_GROUPS // 2
        for q in range(g * pairs, (g + 1) * pairs):
            sl = slice(q * 128, (q + 1) * 128)
            xdt_p = xdt[:, sl].astype(BF16)
            yd = []
            for h in (2 * q, 2 * q + 1):
                seg = jnp.where(causal, acum[:, h:h + 1] - acum_t[h:h + 1, :], NEG_INF)
                yd.append(_dot((cbm * jnp.exp(seg)).astype(BF16), xdt_p))
            hp = h_sc[sl, :]
            y_off = _nt(cm_g, hp.astype(BF16)) * eac[:, sl]
            st = _dot(xw[:, sl].T.astype(BF16), bm_g)
            h_sc[sl, :] = cd_rows[sl, :] * hp + st
            ys.append(jnp.where(lane < SSM_HEAD, yd[0], yd[1]) + y_off)
    y = jnp.concatenate(ys, axis=1) + dsk_ref[...] * xs
    y = y * _silu(z_ref[0])
    gw = SSM_INNER // SSM_GROUPS
    outs = []
    for g in range(SSM_GROUPS):
        yg = y[:, g * gw:(g + 1) * gw]
        outs.append(yg * lax.rsqrt(jnp.mean(yg * yg, axis=1, keepdims=True) + NORM_EPS))
    y_ref[0] = jnp.concatenate(outs, axis=1) * nw_ref[...]

    @pl.when(c == pl.num_programs(1) - 1)
    def _():
        hfin_ref[0] = h_sc[...]


def mamba2(proj, conv0, ssm0, prm, t_valid):
    b, t, _ = proj.shape
    ln = SSD_CHUNK
    vec = lambda n: pl.BlockSpec((1, n), lambda i, c: (0, 0))
    full = lambda a: pl.BlockSpec(a.shape, lambda i, c: (0,) * a.ndim)
    return pl.pallas_call(
        functools.partial(_ssd_body, t_valid=t_valid),
        grid=(b, t // ln),
        in_specs=[pl.BlockSpec((1, ln, SSM_INNER), lambda i, c: (i, c, PROJ_Z // SSM_INNER)),
                  pl.BlockSpec((1, ln, SSM_CONV_DIM), lambda i, c: (i, c, PROJ_XBC // SSM_CONV_DIM)),
                  pl.BlockSpec((1, ln, 128), lambda i, c: (i, c, PROJ_DT // 128)),
                  pl.BlockSpec((1, 8, SSM_CONV_DIM), lambda i, c: (i, 0, 0)),
                  pl.BlockSpec((1, SSM_INNER, SSM_STATE), lambda i, c: (i, 0, 0)),
                  full(prm["conv_w"]), vec(SSM_CONV_DIM), vec(128), vec(128), vec(SSM_INNER), vec(SSM_INNER),
                  full(prm["expand"])],
        out_specs=[pl.BlockSpec((1, ln, SSM_INNER), lambda i, c: (i, c, 0)),
                   pl.BlockSpec((1, SSM_INNER, SSM_STATE), lambda i, c: (i, 0, 0))],
        out_shape=[jax.ShapeDtypeStruct((b, t, SSM_INNER), F32),
                   jax.ShapeDtypeStruct((b, SSM_INNER, SSM_STATE), F32)],
        scratch_shapes=[pltpu.VMEM((SSM_INNER, SSM_STATE), F32), pltpu.VMEM((8 + ln, SSM_CONV_DIM), F32)],
        compiler_params=_cparams(("parallel", "arbitrary")),
    )(proj, proj, proj, conv0, ssm0, prm["conv_w"], prm["conv_b"], prm["dt_bias"], prm["a_log"],
      prm["d_full"], prm["norm_w"], prm["expand"])


def _alibi_slopes():
    return (2.0 ** (-8.0 * np.arange(1, ATT_HEADS + 1) / ATT_HEADS)).astype(np.float32)


def _slope_features():
    s = _alibi_slopes()
    bf = lambda a: a.astype(jnp.bfloat16).astype(np.float32)
    hi = bf(s)
    mid = bf(s - hi)
    lo = bf(s - hi - mid)
    out = np.zeros((ATT_HEADS, 32), np.float32)
    for i, piece in enumerate((hi, mid, lo)):
        out[:, i] = piece
        out[:, 3 + i] = piece
    return out


def _top3_threshold(gate):
    v = gate
    for _ in range(MOBA_TOPK - 1):
        v = jnp.where(v == jnp.max(v, axis=1, keepdims=True), NEG_INF, v)
    return jnp.max(v, axis=1, keepdims=True)


def _blockmean_body(k_ref, o_ref):
    o_ref[0] = jnp.mean(k_ref[...], axis=0, keepdims=True)


def block_means(qkv, nblocks):
    return pl.pallas_call(
        _blockmean_body,
        grid=(nblocks,),
        in_specs=[pl.BlockSpec((MOBA_BLOCK, ATT_KV_COLS), lambda i: (i, ATT_Q_COLS // ATT_KV_COLS))],
        out_specs=pl.BlockSpec((1, 1, ATT_KV_COLS), lambda i: (i, 0, 0)),
        out_shape=jax.ShapeDtypeStruct((nblocks, 1, ATT_KV_COLS), F32),
        compiler_params=_cparams(("parallel",)),
    )(qkv)


def _qkv_body(x_ref, g_ref, w_ref, qkv_ref, kf_ref, vf_ref, *, tiles_per_seq):
    tm = x_ref.shape[0]
    qkv = _dot(_rms(x_ref[...], g_ref[...]).astype(BF16), w_ref[...])
    qkv_ref[...] = qkv
    pos = (pl.program_id(0) % tiles_per_seq) * tm + lax.broadcasted_iota(jnp.int32, (tm, ATT_HEAD), 0)
    lane = lax.broadcasted_iota(jnp.int32, (tm, ATT_HEAD), 1)
    blk_id, off = pos // MOBA_BLOCK, pos % MOBA_BLOCK
    kfeat = jnp.where(lane == blk_id, 1, 0)
    kfeat = jnp.where((lane >= 32) & (lane < 35), blk_id * MOBA_BLOCK, kfeat)
    kfeat = jnp.where((lane >= 35) & (lane < 38), off, kfeat).astype(F32).astype(BF16)
    vfeat = jnp.where(lane == 0, 1.0, 0.0).astype(BF16)
    for g in range(ATT_KV_HEADS):
        k0 = ATT_Q_COLS + g * ATT_HEAD
        v0 = ATT_Q_COLS + ATT_KV_COLS + g * ATT_HEAD
        kf_ref[0, g] = jnp.concatenate([qkv[:, k0:k0 + ATT_HEAD].astype(BF16), kfeat], axis=1)
        vf_ref[0, g] = jnp.concatenate([qkv[:, v0:v0 + ATT_HEAD].astype(BF16), vfeat], axis=1)


def qkv_proj(x, g, w, b, t, tm):
    m, d = x.shape
    n = w.shape[1]
    tps = t // tm
    ext = jax.ShapeDtypeStruct((b, ATT_KV_HEADS, t, 128), BF16)
    ext_spec = pl.BlockSpec((1, ATT_KV_HEADS, tm, 128), lambda i: (i // tps, 0, i % tps, 0))
    return pl.pallas_call(
        functools.partial(_qkv_body, tiles_per_seq=tps),
        grid=(m // tm,),
        in_specs=[pl.BlockSpec((tm, d), lambda i: (i, 0)), pl.BlockSpec((1, d), lambda i: (0, 0)),
                  pl.BlockSpec((d, n), lambda i: (0, 0))],
        out_specs=[pl.BlockSpec((tm, n), lambda i: (i, 0)), ext_spec, ext_spec],
        out_shape=[jax.ShapeDtypeStruct((m, n), F32), ext, ext],
        compiler_params=_cparams(("parallel",)),
    )(x, g, w)


def _moba_prompt_body(q_ref, kf_ref, vf_ref, mean_ref, slf_ref, o_ref, qf_sc, m_sc, acc_sc, sa0_sc, sa1_sc, sb0_sc, sb1_sc):
    qb = pl.program_id(2)
    blk = MOBA_BLOCK
    rows = ATT_GROUP * blk
    scale = ATT_HEAD ** -0.5

    q4 = q_ref[0]
    qg = jnp.concatenate([q4[:, hh * ATT_HEAD:(hh + 1) * ATT_HEAD] for hh in range(ATT_GROUP)], axis=0)
    past_blk = lax.broadcasted_iota(jnp.int32, (128, rows), 0) < qb
    gate = jnp.where(past_blk, _nt(_cat3w(mean_ref[0, 0], 1), _cat3(qg, 1)), NEG_INF)
    v = gate
    for _ in range(MOBA_TOPK - 1):
        v = jnp.where(v == jnp.max(v, axis=0, keepdims=True), NEG_INF, v)
    thr = jnp.max(v, axis=0, keepdims=True)
    selneg = jnp.where(past_blk & (gate < thr), MASK_NEG, 0.0).T[:, :32]
    slf = jnp.concatenate([jnp.broadcast_to(slf_ref[0, hh:hh + 1, :], (blk, 32)) for hh in range(ATT_GROUP)], axis=0)
    qf_sc[...] = jnp.concatenate([(qg * scale).astype(BF16), selneg.astype(BF16), slf.astype(BF16)], axis=1)

    def kv_tiles(j):
        start = pl.multiple_of(j * blk, blk)
        return kf_ref[0, 0, pl.ds(start, blk), :], vf_ref[0, 0, pl.ds(start, blk), :]

    def block_at(pos):
        return jnp.where(pos == 0, qb, jnp.minimum(pos - 1, qb))

    def scores(pos):
        return _nt(qf_sc[...], kv_tiles(block_at(pos))[0])

    def fold(positions, s_cur, s_next=()):
        for ref, pos in zip(s_next, (positions[-1] + 1, positions[-1] + 2)):
            ref[...] = scores(pos)
        ss = [ref[...] for ref in s_cur]
        m_old = m_sc[...]
        m_new = m_old
        for s in ss:
            m_new = jnp.maximum(m_new, jnp.max(s, axis=1, keepdims=True))
        m2 = jnp.concatenate([m_new, m_new], axis=1)
        acc = acc_sc[...] * jnp.exp(m_old - m_new)
        for s, pos in zip(ss, positions):
            acc = acc + _dot(jnp.exp(s - m2).astype(BF16), kv_tiles(block_at(pos))[1])
        acc_sc[...] = acc
        m_sc[...] = m_new

    m_sc[...] = jnp.full(m_sc.shape, NEG_INF, F32)
    acc_sc[...] = jnp.zeros_like(acc_sc)
    set_a, set_b = (sa0_sc, sa1_sc), (sb0_sc, sb1_sc)
    qi = lax.broadcasted_iota(jnp.int32, (rows, blk), 0) % blk
    ki = lax.broadcasted_iota(jnp.int32, (rows, blk), 1)
    sa0_sc[...] = jnp.where(ki <= qi, scores(0), NEG_INF)
    sa1_sc[...] = scores(1)
    n = qb + 1

    def quad(i, carry):
        fold((4 * i, 4 * i + 1), set_a, set_b)
        fold((4 * i + 2, 4 * i + 3), set_b, set_a)
        return carry

    lax.fori_loop(0, n // 4, quad, 0)
    done = n // 4 * 4

    @pl.when(n % 4 >= 2)
    def _():
        fold((done, done + 1), set_a)

    @pl.when(n % 4 == 1)
    def _():
        fold((done,), set_a[:1])

    @pl.when(n % 4 == 3)
    def _():
        sb0_sc[...] = scores(done + 2)
        fold((done + 2,), set_b[:1])

    acc = acc_sc[...]
    out = acc[:, :ATT_HEAD] / acc[:, ATT_HEAD:ATT_HEAD + 1]
    o_ref[0] = jnp.concatenate([out[hh * blk:(hh + 1) * blk] for hh in range(ATT_GROUP)], axis=1)


def moba_prompt(qkv, kf, vf, means, slf):
    b, t, _ = qkv.shape
    nb = t // MOBA_BLOCK
    assert nb <= 32
    nbl = 128
    means = jnp.pad(means, ((0, 0), (0, 0), (0, nbl - nb), (0, 0)))
    rows = ATT_GROUP * MOBA_BLOCK
    gw = ATT_GROUP * ATT_HEAD
    return pl.pallas_call(
        _moba_prompt_body,
        grid=(b, ATT_KV_HEADS, nb),
        in_specs=[pl.BlockSpec((1, MOBA_BLOCK, gw), lambda i, g, j: (i, j, g)),
                  pl.BlockSpec((1, 1, t, 128), lambda i, g, j: (i, g, 0, 0)),
                  pl.BlockSpec((1, 1, t, 128), lambda i, g, j: (i, g, 0, 0)),
                  pl.BlockSpec((1, 1, nbl, ATT_HEAD), lambda i, g, j: (i, g, 0, 0)),
                  pl.BlockSpec((1, ATT_GROUP, 32), lambda i, g, j: (g, 0, 0))],
        out_specs=pl.BlockSpec((1, MOBA_BLOCK, gw), lambda i, g, j: (i, j, g)),
        out_shape=jax.ShapeDtypeStruct((b, t, ATT_Q_COLS), F32),
        scratch_shapes=[pltpu.VMEM((rows, 128), BF16), pltpu.VMEM((rows, 128), F32), pltpu.VMEM((rows, 128), F32),
                        *([pltpu.VMEM((rows, MOBA_BLOCK), F32)] * 4)],
        compiler_params=_cparams(("parallel", "parallel", "arbitrary")),
    )(qkv, kf, vf, means, slf)


def _moba_sample_body(pt_ref, *refs, nbp, bps, nq):
    npg = 2 * bps
    k_refs, v_refs = refs[:npg], refs[npg:2 * npg]
    q_ref, kn_ref, vn_ref, slope_ref, o_ref, mean_sc, m_sc, l_sc, acc_sc = refs[2 * npg:]
    j = pl.program_id(1)
    blk = MOBA_BLOCK
    rows = ATT_HEADS * nq
    grows = ATT_GROUP * nq
    past = nbp * blk
    scale = ATT_HEAD ** -0.5
    lane = lax.broadcasted_iota(jnp.int32, (rows, 128), 1)
    qi = (lax.broadcasted_iota(jnp.int32, (rows, 1), 0) % nq).astype(F32)
    qf = q_ref[0]
    qs = (qf * scale).astype(BF16)
    slope = slope_ref[...]

    @pl.when(j == 0)
    def _():
        mean_sc[...] = jnp.zeros_like(mean_sc)
        m_sc[...] = jnp.full(m_sc.shape, NEG_INF, F32)
        l_sc[...] = jnp.zeros_like(l_sc)

    kcat = jnp.concatenate([ref[0] for ref in k_refs], axis=0)
    off = lax.broadcasted_iota(jnp.int32, (1, bps * blk), 1).astype(F32)
    sc = _nt(qs, kcat.astype(BF16)) - slope * ((past - j * bps * blk).astype(F32) + qi - off)
    m_new, l_new = m_sc[...], l_sc[...]
    for s in range(bps):
        bi = j * bps + s
        ksum = jnp.sum(kcat[s * blk:(s + 1) * blk], axis=0, keepdims=True)
        mean_sc[pl.ds(bi, 1), :] = ksum * (1.0 / blk)
        sc_b = sc[:, s * blk:(s + 1) * blk]
        m = jnp.max(sc_b, axis=1, keepdims=True)
        p = jnp.exp(sc_b - m)
        m_new = jnp.where(lane == bi, m, m_new)
        l_new = jnp.where(lane == bi, jnp.sum(p, axis=1, keepdims=True), l_new)
        vblk = jnp.concatenate([v_refs[2 * s][0], v_refs[2 * s + 1][0]], axis=0)
        acc_sc[bi] = _dot(p.astype(BF16), vblk.astype(BF16))
    m_sc[...] = m_new
    l_sc[...] = l_new

    @pl.when(j == pl.num_programs(1) - 1)
    def _():
        gate = jnp.where(lane < nbp, _nt(_cat3(qf, 1), _cat3w(mean_sc[...], 1)), NEG_INF)
        sel = gate >= _top3_threshold(gate)
        ki = lax.broadcasted_iota(jnp.int32, (rows, 8), 1).astype(F32)
        s_own = jnp.where(ki <= qi, _nt(qs, kn_ref[0].astype(BF16)) - slope * (qi - ki), NEG_INF)
        m_all = jnp.where(sel, m_sc[...], NEG_INF)
        mx = jnp.maximum(jnp.max(m_all, axis=1, keepdims=True), jnp.max(s_own, axis=1, keepdims=True))
        w = jnp.where(sel, jnp.exp(m_all - mx), 0.0)
        p_own = jnp.exp(s_own - mx)
        den = jnp.sum(w * l_sc[...], axis=1, keepdims=True) + jnp.sum(p_own, axis=1, keepdims=True)
        acc = _dot(p_own.astype(BF16), vn_ref[0].astype(BF16))
        for n in range(nbp):
            acc = acc + w[:, n:n + 1] * acc_sc[n]
        out = acc / den
        for g in range(ATT_KV_HEADS):
            o_ref[0, g] = out[g * grows:(g + 1) * grows, g * ATT_HEAD:(g + 1) * ATT_HEAD]


SAMPLE_BLOCKS_PER_STEP = 8


def _pages2d(cache):
    return cache.reshape(cache.shape[0], PAGE_SIZE, ATT_KV_COLS)


def _block_diag_queries(q):
    db, nq, _ = q.shape
    qh = q.reshape(db, nq, ATT_KV_HEADS, ATT_GROUP, ATT_HEAD).transpose(0, 2, 3, 1, 4)
    eye = jnp.eye(ATT_KV_HEADS, dtype=q.dtype)
    return (qh[:, :, :, :, None, :] * eye[None, :, None, None, :, None]).reshape(db, ATT_HEADS * nq, ATT_KV_COLS)


def _new_rows(a):
    db, nq = a.shape[:2]
    return jnp.pad(a.reshape(db, nq, ATT_KV_COLS), ((0, 0), (0, 8 - nq), (0, 0)))


def moba_sample(page_table, cache_k, cache_v, q_bd, k_new, v_new, slope_rows, nq):
    db, n_pages = page_table.shape
    nbp = n_pages * PAGE_SIZE // MOBA_BLOCK
    bps = min(SAMPLE_BLOCKS_PER_STEP, nbp)
    rows = ATT_HEADS * nq
    grows = ATT_GROUP * nq

    def page_spec(slot):
        return pl.BlockSpec((1, PAGE_SIZE, ATT_KV_COLS),
                            lambda b, j, pt: (pt[b * n_pages + j * 2 * bps + slot], 0, 0))

    small = lambda shp: pl.BlockSpec((1,) + shp, lambda b, j, pt: (b,) + (0,) * len(shp))
    grid_spec = pltpu.PrefetchScalarGridSpec(
        num_scalar_prefetch=1,
        grid=(db, nbp // bps),
        in_specs=[page_spec(s) for s in range(2 * bps)] * 2
        + [small((rows, ATT_KV_COLS)), small((8, ATT_KV_COLS)), small((8, ATT_KV_COLS)),
           pl.BlockSpec((rows, 1), lambda b, j, pt: (0, 0))],
        out_specs=small((ATT_KV_HEADS, grows, ATT_HEAD)),
        scratch_shapes=[pltpu.VMEM((128, ATT_KV_COLS), F32), pltpu.VMEM((rows, 128), F32),
                        pltpu.VMEM((rows, 128), F32), pltpu.VMEM((nbp, rows, ATT_KV_COLS), F32)],
    )
    return pl.pallas_call(
        functools.partial(_moba_sample_body, nbp=nbp, bps=bps, nq=nq),
        grid_spec=grid_spec,
        out_shape=jax.ShapeDtypeStruct((db, ATT_KV_HEADS, grows, ATT_HEAD), F32),
        compiler_params=_cparams(("parallel", "arbitrary")),
    )(page_table.reshape(-1), *([cache_k] * (2 * bps)), *([cache_v] * (2 * bps)), q_bd, k_new, v_new, slope_rows)


def _row(a):
    return a.reshape(1, -1).astype(F32)


def _even_params(i, w_in_ab, rwkv_mu, rwkv_w0, rwkv_w2, rwkv_a0, rwkv_a2, rwkv_g2, rwkv_k_k, rwkv_k_a, rwkv_r_k,
                 rwkv_ln_w, rwkv_ln_b, ssm_conv_w, ssm_conv_b, ssm_dt_bias, ssm_a_log, ssm_d, ssm_norm_w):
    w = w_in_ab[i]
    ssm0 = RW_COLS
    zeros = lambda n: jnp.zeros((D_MODEL, n), w.dtype)
    w_pack = jnp.concatenate([
        w[:, :RW_COLS],
        w[:, ssm0 + SSM_INNER + SSM_CONV_DIM:], zeros(PROJ_Z - PROJ_DT - SSM_HEADS),
        w[:, ssm0:ssm0 + SSM_INNER],
        w[:, ssm0 + SSM_INNER:ssm0 + SSM_INNER + SSM_CONV_DIM]], axis=1).astype(BF16)
    half = RW_LORA // 2
    zl = jnp.zeros((half, RW_DIM), F32)
    wl = jnp.concatenate([jnp.concatenate([rwkv_w2[i], zl], axis=1),
                          jnp.concatenate([zl, rwkv_a2[i]], axis=1)], axis=0)
    half_of = np.arange(128) // RW_HEAD
    bones2 = np.tile((half_of[:, None] == half_of[None, :]).astype(np.float32), (2, 1))
    rw = dict(mu=_row(rwkv_mu[i]), w0=_row(rwkv_w0[i]), a0=_row(rwkv_a0[i]), k_k=_row(rwkv_k_k[i]),
              k_a=_row(rwkv_k_a[i]), r_k=_row(rwkv_r_k[i]), ln_w=_row(rwkv_ln_w[i]), ln_b=_row(rwkv_ln_b[i]),
              wl=_cat3w(wl, 0), g2=_cat3w(rwkv_g2[i], 0), bones2=jnp.asarray(bones2, BF16),
              pair_w=jnp.asarray(np.kron(np.eye(4, dtype=np.float32), np.ones((RW_HEAD, RW_HEAD), np.float32)), BF16))
    pad128 = lambda a: jnp.pad(_row(a), ((0, 0), (0, 128 - a.shape[-1])))
    expand = (np.arange(128)[:, None] == (np.arange(SSM_INNER) // SSM_HEAD)[None, :]).astype(np.float32)
    ssm = dict(conv_w=ssm_conv_w[i], conv_b=_row(ssm_conv_b[i]), dt_bias=pad128(ssm_dt_bias[i]),
               a_log=pad128(ssm_a_log[i]), d_full=_row(jnp.repeat(ssm_d[i], SSM_HEAD)), norm_w=_row(ssm_norm_w[i]),
               expand=jnp.asarray(np.tile(expand, (3, 1)), BF16))
    return w_pack, rw, ssm


def _mixer_ab(x, b, t, norm_g, w_pack, rw, ssm, w_out, shift0, wkv0, conv0, ssm0, tm):
    tm_in = 2 * tm if (b * t) % (2 * tm) == 0 else tm
    proj = norm_matmul(x, norm_g, w_pack, tm_in, SSM_CONV_DIM).reshape(b, t, PROJ_COLS)
    shift_new = proj[:, t - 1, :RW_COLS]
    conv_new = proj[:, t - (SSM_CONV - 1):, PROJ_XBC:]
    if t % SSD_CHUNK == 0:
        proj_rw, proj_ssm, tc = proj, proj, SSD_CHUNK
    else:
        proj_rw = jnp.pad(proj, ((0, 0), (0, 8 - t), (0, 0)))
        proj_ssm = jnp.pad(proj, ((0, 0), (0, SSD_CHUNK - t), (0, 0)))
        tc = 8
    nbb = next(n for n in (4, 2, 1) if b % n == 0)
    y_rw, wkv_new = rwkv7(proj_rw, shift0.reshape(b, 1, RW_COLS), wkv0, rw, nbb, tc, min(t, tc))
    conv0p = jnp.pad(conv0, ((0, 0), (8 - (SSM_CONV - 1), 0), (0, 0)))
    y_ssm, ssm_new = mamba2(proj_ssm, conv0p, ssm0.reshape(b, SSM_INNER, SSM_STATE), ssm, min(t, SSD_CHUNK))
    x = matmul_residual(x, [y_rw[:, :t].reshape(b * t, RW_DIM), y_ssm[:, :t].reshape(b * t, SSM_INNER)],
                        [w_out[:RW_DIM], w_out[RW_DIM:]], tm)
    return x, (shift_new, wkv_new, conv_new, ssm_new.reshape(b, SSM_HEADS, SSM_HEAD, SSM_STATE))


def kernel(x_prompt, x_sample, state_rwkv_shift, state_rwkv_wkv, state_ssm_conv, state_ssm, cache_k, cache_v, page_table, norm_mix, norm_ffn, norm_final, w_in_ab, rwkv_mu, rwkv_w0, rwkv_w2, rwkv_a0, rwkv_a2, rwkv_g2, rwkv_k_k, rwkv_k_a, rwkv_r_k, rwkv_ln_w, rwkv_ln_b, ssm_conv_w, ssm_conv_b, ssm_dt_bias, ssm_a_log, ssm_d, ssm_norm_w, w_out_ab, ffn_w_gate, ffn_w_up, ffn_w_down, attn_w_qkv, attn_w_o, moe_router, moe_w_gate, moe_w_up, moe_w_down):
    bp, tp, _ = x_prompt.shape
    db, ts, _ = x_sample.shape
    depth = norm_mix.shape[0]
    tm_p, tm_s = 512, db * ts
    xp = x_prompt.reshape(bp * tp, D_MODEL)
    xs = x_sample.reshape(db * ts, D_MODEL)
    slopes = _alibi_slopes()
    slf = jnp.asarray(_slope_features().reshape(ATT_KV_HEADS, ATT_GROUP, 32))
    slope_rows = jnp.asarray(np.repeat(slopes, ts).reshape(ATT_HEADS * ts, 1))
    st = {n: [] for n in ("p_shift", "p_wkv", "p_conv", "p_ssm", "p_k", "p_v",
                          "s_shift", "s_wkv", "s_conv", "s_ssm", "s_k", "s_v")}
    for l in range(depth):
        i = l // 2
        g_mix, g_ffn = _row(norm_mix[l]), _row(norm_ffn[l])
        if l % 2 == 0:
            w_pack, rw, ssm = _even_params(i, w_in_ab, rwkv_mu, rwkv_w0, rwkv_w2, rwkv_a0, rwkv_a2, rwkv_g2,
                                           rwkv_k_k, rwkv_k_a, rwkv_r_k, rwkv_ln_w, rwkv_ln_b, ssm_conv_w,
                                           ssm_conv_b, ssm_dt_bias, ssm_a_log, ssm_d, ssm_norm_w)
            w_out = w_out_ab[i].astype(BF16)
            xp, sp = _mixer_ab(xp, bp, tp, g_mix, w_pack, rw, ssm, w_out,
                               jnp.zeros((bp, RW_COLS), F32), jnp.zeros((bp, RW_HEADS, RW_HEAD, RW_HEAD), F32),
                               jnp.zeros((bp, SSM_CONV - 1, SSM_CONV_DIM), F32),
                               jnp.zeros((bp, SSM_HEADS, SSM_HEAD, SSM_STATE), F32), tm_p)
            xs, ss = _mixer_ab(xs, db, ts, g_mix, w_pack, rw, ssm, w_out, state_rwkv_shift[i], state_rwkv_wkv[i],
                               state_ssm_conv[i], state_ssm[i], tm_s)
            for pre, new in (("p", sp), ("s", ss)):
                for name, val in zip(("shift", "wkv", "conv", "ssm"), new):
                    st[f"{pre}_{name}"].append(val)
            wg, wu, wd = ffn_w_gate[i].astype(BF16), ffn_w_up[i].astype(BF16), ffn_w_down[i].astype(BF16)
            xp = ffn_swiglu(xp, g_ffn, wg, wu, wd, 1024, 256)
            xs = ffn_swiglu(xs, g_ffn, wg, wu, wd, tm_s, 256)
        else:
            w_qkv, w_o = attn_w_qkv[i].astype(BF16), attn_w_o[i].astype(BF16)
            qkv_p, kf, vf = qkv_proj(xp, g_mix, w_qkv, bp, tp, tm_p)
            nb = tp // MOBA_BLOCK
            means = block_means(qkv_p, bp * nb).reshape(bp, nb, ATT_KV_HEADS, ATT_HEAD).transpose(0, 2, 1, 3)
            k_p = qkv_p[:, ATT_Q_COLS:ATT_Q_COLS + ATT_KV_COLS].reshape(bp, tp, ATT_KV_HEADS, ATT_HEAD)
            v_p = qkv_p[:, ATT_Q_COLS + ATT_KV_COLS:].reshape(bp, tp, ATT_KV_HEADS, ATT_HEAD)
            o_p = moba_prompt(qkv_p.reshape(bp, tp, -1), kf, vf, means, slf)
            xp = matmul_residual(xp, [o_p.reshape(bp * tp, ATT_Q_COLS)], [w_o], tm_p)
            st["p_k"].append(k_p)
            st["p_v"].append(v_p)

            qkv_s = norm_matmul(xs, g_mix, w_qkv, tm_s, 512).reshape(db, ts, -1)
            kn = qkv_s[..., ATT_Q_COLS:ATT_Q_COLS + ATT_KV_COLS].reshape(db, ts, ATT_KV_HEADS, ATT_HEAD)
            vn = qkv_s[..., ATT_Q_COLS + ATT_KV_COLS:].reshape(db, ts, ATT_KV_HEADS, ATT_HEAD)
            o_s = moba_sample(page_table, _pages2d(cache_k[i]), _pages2d(cache_v[i]),
                              _block_diag_queries(qkv_s[..., :ATT_Q_COLS]), _new_rows(kn), _new_rows(vn),
                              slope_rows, ts)
            o_s = o_s.reshape(db, ATT_HEADS, ts, ATT_HEAD).transpose(0, 2, 1, 3).reshape(db * ts, ATT_Q_COLS)
            xs = matmul_residual(xs, [o_s], [w_o], tm_s)
            st["s_k"].append(kn)
            st["s_v"].append(vn)

            router = _cat3w(jnp.pad(moe_router[i], ((0, 0), (0, 128 - N_EXPERTS))), 0)
            wg, wu, wd = moe_w_gate[i].astype(BF16), moe_w_up[i].astype(BF16), moe_w_down[i].astype(BF16)
            final = l == depth - 1
            xp = moe_swiglu(xp, g_ffn, router, wg, wu, wd, _row(norm_final), 2 * tm_p, final)
            xs = moe_swiglu(xs, g_ffn, router, wg, wu, wd, _row(norm_final), tm_s, final)
    y_prompt = xp.reshape(bp, tp, D_MODEL)
    y_sample = xs.reshape(db, ts, D_MODEL)
    stack = lambda n: jnp.stack(st[n])
    return (y_prompt, y_sample,
            stack("p_shift"), stack("p_wkv"), stack("p_conv"), stack("p_ssm"), stack("p_k"), stack("p_v"),
            stack("s_shift"), stack("s_wkv"), stack("s_conv"), stack("s_ssm"), stack("s_k"), stack("s_v"))
```

```python
import functools

import numpy as np
import jax
import jax.numpy as jnp
from jax import lax
from jax.experimental import pallas as pl
from jax.experimental.pallas import tpu as pltpu

F32 = jnp.float32
BF16 = jnp.bfloat16
NEG_INF = float("-inf")

D_MODEL = 1024
NORM_EPS = 1e-6

RW_HEAD = 64
RW_HEADS = 8
RW_DIM = RW_HEADS * RW_HEAD
RW_LORA = 128
RW_GATE = 128
RW_COLS = 3 * RW_DIM + RW_LORA + RW_GATE
RW_LN_EPS = 64e-5

SSM_INNER = 1024
SSM_HEAD = 64
SSM_HEADS = 16
SSM_GROUPS = 2
SSM_STATE = 128
SSM_CONV = 4
SSM_CONV_DIM = SSM_INNER + 2 * SSM_GROUPS * SSM_STATE
SSD_CHUNK = 128

PROJ_RW = 0
PROJ_DT = RW_COLS
PROJ_Z = 2048
PROJ_XBC = 3072
PROJ_COLS = PROJ_XBC + SSM_CONV_DIM

ATT_HEADS = 16
ATT_KV_HEADS = 4
ATT_HEAD = 64
ATT_GROUP = ATT_HEADS // ATT_KV_HEADS
ATT_Q_COLS = ATT_HEADS * ATT_HEAD
ATT_KV_COLS = ATT_KV_HEADS * ATT_HEAD
MOBA_BLOCK = 256
MOBA_TOPK = 3
MASK_NEG = -1e30
PAGE_SIZE = 128
FEAT_BLOCKS = 32
FEAT_START = FEAT_BLOCKS
FEAT_OFFSET = FEAT_START + 3
SLOPE_PARTS = 3

N_EXPERTS = 8

V7X_VMEM_BYTES = 64 * 1024 * 1024
VMEM_LIMIT = V7X_VMEM_BYTES - 8 * 1024 * 1024


def _cparams(sem):
    return pltpu.CompilerParams(dimension_semantics=sem, vmem_limit_bytes=VMEM_LIMIT)


def _nt(a, b, precision=None):
    return lax.dot_general(a, b, (((1,), (1,)), ((), ())), precision=precision,
                           preferred_element_type=F32)


def _dot(a, b, precision=None):
    return jnp.dot(a, b, precision=precision, preferred_element_type=F32)


def _split2(x):
    hi = x.astype(BF16)
    return hi, (x - hi.astype(F32)).astype(BF16)


def _split3(x):
    hi = x.astype(BF16)
    r = x - hi.astype(F32)
    mid = r.astype(BF16)
    return hi, mid, (r - mid.astype(F32)).astype(BF16)


def _cat3(x, axis):
    hi, lo = _split2(x)
    return jnp.concatenate([hi, lo, hi], axis=axis)


def _cat3w(w, axis):
    hi, lo = _split2(w)
    return jnp.concatenate([hi, hi, lo], axis=axis)


def _rms(x, g):
    return x * lax.rsqrt(jnp.mean(x * x, axis=-1, keepdims=True) + NORM_EPS) * g


def _softplus(x):
    return jnp.maximum(x, 0.0) + jnp.log(1.0 + jnp.exp(-jnp.abs(x)))


def _sigmoid(x):
    return 1.0 / (1.0 + jnp.exp(-x))


def _silu(x):
    return x * _sigmoid(x)


def _nm_body(x_ref, g_ref, w_ref, o_ref, h_sc):
    @pl.when(pl.program_id(1) == 0)
    def _():
        h_sc[...] = _rms(x_ref[...], g_ref[...]).astype(BF16)

    o_ref[...] = _dot(h_sc[...], w_ref[...])


def norm_matmul(x, g, w, tm, tn):
    m, k = x.shape
    n = w.shape[1]
    return pl.pallas_call(
        _nm_body,
        grid=(m // tm, n // tn),
        in_specs=[pl.BlockSpec((tm, k), lambda i, j: (i, 0)),
                  pl.BlockSpec((1, k), lambda i, j: (0, 0)),
                  pl.BlockSpec((k, tn), lambda i, j: (0, j))],
        out_specs=pl.BlockSpec((tm, tn), lambda i, j: (i, j)),
        out_shape=jax.ShapeDtypeStruct((m, n), F32),
        scratch_shapes=[pltpu.VMEM((tm, k), BF16)],
        compiler_params=_cparams(("parallel", "arbitrary")),
    )(x, g, w)


def _mmres_body(*refs, n):
    res_ref, a_refs, w_refs, o_ref = refs[0], refs[1:1 + n], refs[1 + n:1 + 2 * n], refs[1 + 2 * n]
    acc = res_ref[...]
    for a_ref, w_ref in zip(a_refs, w_refs):
        acc = acc + _dot(a_ref[...].astype(BF16), w_ref[...])
    o_ref[...] = acc


def matmul_residual(res, acts, ws, tm):
    m, d = res.shape
    n = len(acts)
    in_specs = [pl.BlockSpec((tm, d), lambda i: (i, 0))]
    in_specs += [pl.BlockSpec((tm, a.shape[1]), lambda i: (i, 0)) for a in acts]
    in_specs += [pl.BlockSpec(w.shape, lambda i: (0, 0)) for w in ws]
    return pl.pallas_call(
        functools.partial(_mmres_body, n=n),
        grid=(m // tm,),
        in_specs=in_specs,
        out_specs=pl.BlockSpec((tm, d), lambda i: (i, 0)),
        out_shape=jax.ShapeDtypeStruct((m, d), F32),
        compiler_params=_cparams(("parallel",)),
    )(res, *acts, *ws)


def _ffn_body(x_ref, g_ref, wg_ref, wu_ref, wd_ref, o_ref, h_sc, acc_sc):
    j = pl.program_id(1)

    @pl.when(j == 0)
    def _():
        h_sc[...] = _rms(x_ref[...], g_ref[...]).astype(BF16)
        acc_sc[...] = jnp.zeros_like(acc_sc)

    h = h_sc[...]
    act = _silu(_dot(h, wg_ref[...])) * _dot(h, wu_ref[...])
    acc_sc[...] += _dot(act.astype(BF16), wd_ref[...])

    @pl.when(j == pl.num_programs(1) - 1)
    def _():
        o_ref[...] = x_ref[...] + acc_sc[...]


def ffn_swiglu(x, g, wg, wu, wd, tm, tf):
    m, d = x.shape
    f = wg.shape[1]
    return pl.pallas_call(
        _ffn_body,
        grid=(m // tm, f // tf),
        in_specs=[pl.BlockSpec((tm, d), lambda i, j: (i, 0)),
                  pl.BlockSpec((1, d), lambda i, j: (0, 0)),
                  pl.BlockSpec((d, tf), lambda i, j: (0, j)),
                  pl.BlockSpec((d, tf), lambda i, j: (0, j)),
                  pl.BlockSpec((tf, d), lambda i, j: (j, 0))],
        out_specs=pl.BlockSpec((tm, d), lambda i, j: (i, 0)),
        out_shape=jax.ShapeDtypeStruct((m, d), F32),
        scratch_shapes=[pltpu.VMEM((tm, d), BF16), pltpu.VMEM((tm, d), F32)],
        compiler_params=_cparams(("parallel", "arbitrary")),
    )(x, g, wg, wu, wd)


MOE_SUB = 512
MOE_CHUNK = 160


def _moe_body(x_ref, g_ref, r_ref, wg_ref, wu_ref, wd_ref, gf_ref, o_ref, h_sc, comb_sc, acc_sc, rk_sc, rkt_sc, *,
              final_norm, chunk):
    e = pl.program_id(1)
    tm = x_ref.shape[0]
    lane = lax.broadcasted_iota(jnp.int32, (tm, 128), 1)

    @pl.when(e == 0)
    def _():
        hf = _rms(x_ref[...], g_ref[...])
        h_sc[...] = hf.astype(BF16)
        lanef = lane.astype(F32)
        logits = jnp.where(lane < N_EXPERTS, _dot(_cat3(hf, 1), r_ref[...]), NEG_INF)
        m1 = jnp.max(logits, axis=1, keepdims=True)
        i1 = jnp.min(jnp.where(logits == m1, lanef, 128.0), axis=1, keepdims=True)
        mask1 = lanef == i1
        rest = jnp.where(mask1, NEG_INF, logits)
        m2 = jnp.max(rest, axis=1, keepdims=True)
        i2 = jnp.min(jnp.where(rest == m2, lanef, 128.0), axis=1, keepdims=True)
        mask2 = lanef == i2
        e2 = jnp.exp(m2 - m1)
        den = 1.0 + e2
        comb_sc[...] = jnp.where(mask1, 1.0 / den, 0.0) + jnp.where(mask2, e2 / den, 0.0)
        acc_sc[...] = jnp.zeros_like(acc_sc)
        if chunk:
            earlier = (lax.broadcasted_iota(jnp.int32, (MOE_SUB, MOE_SUB), 1)
                       < lax.broadcasted_iota(jnp.int32, (MOE_SUB, MOE_SUB), 0))
            earlier = jnp.where(earlier, 1.0, 0.0).astype(BF16)
            routed = jnp.where(mask1 | mask2, 1.0, 0.0)
            for st in range(tm // MOE_SUB):
                sl = slice(st * MOE_SUB, (st + 1) * MOE_SUB)
                rank = _dot(earlier, routed[sl].astype(BF16))
                rank = jnp.where(routed[sl] > 0.0, rank, -1.0)
                rk_sc[sl, :] = rank
                rkt_sc[st] = rank.T

    def expert(rows):
        act = _silu(_dot(rows, wg_ref[0])) * _dot(rows, wu_ref[0])
        return _dot(act.astype(BF16), wd_ref[0])

    this = lane == e
    if not chunk:
        c = jnp.sum(jnp.where(this, comb_sc[...], 0.0), axis=1, keepdims=True)
        acc_sc[...] += c * expert(h_sc[...])
    else:
        wide = -(-chunk // 128) * 128
        slot = lax.broadcasted_iota(jnp.int32, (chunk, MOE_SUB), 0).astype(F32)
        lane_w = lax.broadcasted_iota(jnp.int32, (MOE_SUB, wide), 1)
        slot_t = jnp.where(lane_w < chunk, lane_w.astype(F32), jnp.nan)
        sub8 = lax.broadcasted_iota(jnp.int32, (8, MOE_SUB), 0)
        this_sub = lax.broadcasted_iota(jnp.int32, (MOE_SUB, 128), 1) == e
        pad_rows = jnp.zeros((wide - chunk, x_ref.shape[1]), BF16)
        for st in range(tm // MOE_SUB):
            sl = slice(st * MOE_SUB, (st + 1) * MOE_SUB)
            rank_col = jnp.sum(jnp.where(this_sub, rk_sc[sl, :], 0.0), axis=1, keepdims=True)
            rank_row = jnp.sum(jnp.where(sub8 == e, rkt_sc[st, 0:8, :], 0.0), axis=0, keepdims=True)
            c = jnp.sum(jnp.where(this_sub, comb_sc[sl, :], 0.0), axis=1, keepdims=True)
            count = jnp.max(rank_row, axis=1, keepdims=True)[0, 0].astype(jnp.int32) + 1

            def one_chunk(k, carry):
                base = (k * chunk).astype(F32)
                gather = jnp.where(rank_row - base == slot, 1.0, 0.0).astype(BF16)
                y_hi, y_lo = _split2(expert(_dot(gather, h_sc[sl, :]).astype(BF16)))
                scatter = jnp.where(rank_col - base == slot_t, 1.0, 0.0).astype(BF16)
                back = (_dot(scatter, jnp.concatenate([y_hi, pad_rows], axis=0))
                        + _dot(scatter, jnp.concatenate([y_lo, pad_rows], axis=0)))
                acc_sc[sl, :] += c * back
                return carry

            lax.fori_loop(0, (count + chunk - 1) // chunk, one_chunk, 0)

    @pl.when(e == pl.num_programs(1) - 1)
    def _():
        y = x_ref[...] + acc_sc[...]
        if final_norm:
            y = _rms(y, gf_ref[...])
        o_ref[...] = y


def moe_swiglu(x, g, router, wg, wu, wd, gfinal, tm, final_norm):
    m, d = x.shape
    ne, _, fe = wg.shape
    chunk = MOE_CHUNK if tm % MOE_SUB == 0 else 0
    nsub = max(tm // MOE_SUB, 1)
    return pl.pallas_call(
        functools.partial(_moe_body, final_norm=final_norm, chunk=chunk),
        grid=(m // tm, ne),
        in_specs=[pl.BlockSpec((tm, d), lambda i, e: (i, 0)),
                  pl.BlockSpec((1, d), lambda i, e: (0, 0)),
                  pl.BlockSpec((3 * d, 128), lambda i, e: (0, 0)),
                  pl.BlockSpec((1, d, fe), lambda i, e: (e, 0, 0)),
                  pl.BlockSpec((1, d, fe), lambda i, e: (e, 0, 0)),
                  pl.BlockSpec((1, fe, d), lambda i, e: (e, 0, 0)),
                  pl.BlockSpec((1, d), lambda i, e: (0, 0))],
        out_specs=pl.BlockSpec((tm, d), lambda i, e: (i, 0)),
        out_shape=jax.ShapeDtypeStruct((m, d), F32),
        scratch_shapes=[pltpu.VMEM((tm, d), BF16), pltpu.VMEM((tm, 128), F32), pltpu.VMEM((tm, d), F32),
                        pltpu.VMEM((tm, 128), F32), pltpu.VMEM((nsub, 128, min(tm, MOE_SUB)), F32)],
        compiler_params=_cparams(("parallel", "arbitrary")),
    )(x, g, router, wg, wu, wd, gfinal)


RW_PAIRS = RW_HEADS // 2
RW_ROWS = RW_PAIRS * RW_HEAD


def _rwkv_body(p_ref, sh0_ref, s0_ref, mu_ref, w0_ref, a0_ref, kk_ref, ka_ref, rk_ref, lnw_ref, lnb_ref,
               wl_ref, g2_ref, b2_ref, pw_ref, y_ref, sfin_ref,
               s_sc, sk_sc, prev_sc, kk_sc, w_sc, b_sc, k_sc, r_sc, v_sc, g_sc, y_sc, *, nbb, tc, t_valid):
    c = pl.program_id(1)

    @pl.when(c == 0)
    def _():
        s_sc[...] = s0_ref[...]
        prev_sc[...] = sh0_ref[...]

    b2 = b2_ref[...]

    def head_sums(x, low=True):
        if low:
            hi, lo = _split2(x)
        else:
            hi = x.astype(BF16)
            lo = jnp.zeros_like(hi)
        return _dot(jnp.concatenate([hi, lo], axis=1), b2)

    def head_sums_wide(x):
        return jnp.concatenate([head_sums(x[:, q * 128:(q + 1) * 128]) for q in range(RW_PAIRS)], axis=1)

    for bi in range(nbb):
        p = p_ref[bi]
        row = lax.broadcasted_iota(jnp.int32, p.shape, 0)
        prev = jnp.where(row == 0, prev_sc[bi], pltpu.roll(p, 1, axis=0))
        prev_sc[bi] = p[tc - 1:tc, :]
        u = p + (prev - p) * mu_ref[...]
        r = u[:, 0:RW_DIM]
        k = u[:, RW_DIM:2 * RW_DIM]
        v = u[:, 2 * RW_DIM:3 * RW_DIM]
        lo = u[:, 3 * RW_DIM:3 * RW_DIM + RW_LORA]
        gd = u[:, 3 * RW_DIM + RW_LORA:]
        lane = lax.broadcasted_iota(jnp.int32, lo.shape, 1)
        lora = _dot(_cat3(jnp.where(lane < RW_LORA // 2, jnp.tanh(lo), lo), 1), wl_ref[...])
        wlog = -_softplus(-(w0_ref[...] + lora[:, :RW_DIM])) - 0.5
        a = _sigmoid(a0_ref[...] + lora[:, RW_DIM:])
        g_sc[bi] = _dot(_cat3(_sigmoid(gd), 1), g2_ref[...])
        kk = k * kk_ref[...]
        kk = kk / jnp.maximum(jnp.sqrt(head_sums_wide(kk * kk)), 1e-12)
        kk_sc[bi] = kk
        w_sc[bi] = jnp.exp(-jnp.exp(wlog))
        b_sc[bi] = kk * a
        k_sc[bi] = k * (1.0 + (a - 1.0) * ka_ref[...])
        r_sc[bi] = r
        v_sc[bi] = v

    rows = nbb * RW_ROWS
    vi = lax.broadcasted_iota(jnp.int32, (rows, 128), 0) % RW_HEAD
    li = lax.broadcasted_iota(jnp.int32, (rows, 128), 1)
    key_lane = li % RW_HEAD
    diag = (key_lane == vi).astype(F32)
    steps = 8 if t_valid % 8 == 0 else t_valid

    def rows_of(x8, j):
        return jnp.concatenate([jnp.broadcast_to(x8[bi][j:j + 1, q * 128:(q + 1) * 128], (RW_HEAD, 128))
                                for bi in range(nbb) for q in range(RW_PAIRS)], axis=0)

    pair_w = pw_ref[...]
    zero_half = jnp.zeros((rows, 128), BF16)

    def pair_sums(a, b):
        lhs = jnp.concatenate([a.astype(BF16), zero_half if b is None else b.astype(BF16)], axis=1)
        out = _dot(lhs, pair_w)
        return out[:, :128], out[:, 128:]

    first8 = [kk_sc[bi, pl.ds(0, 8), :] for bi in range(nbb)]
    sk_sc[...] = pair_sums(s_sc[...].reshape(rows, 128) * rows_of(first8, 0), None)[0]

    def group(t8, carry):
        t0 = pl.multiple_of(t8 * 8, 8)
        tn = pl.multiple_of(jnp.minimum(t0 + 8, tc - 8), 8)
        kk8, w8, b8, k8, r8, v8 = ([sc[bi, pl.ds(t0, 8), :] for bi in range(nbb)]
                                   for sc in (kk_sc, w_sc, b_sc, k_sc, r_sc, v_sc))
        kk_next = [kk_sc[bi, pl.ds(tn, 8), :] for bi in range(nbb)]
        s = s_sc[...].reshape(rows, 128)
        sk = sk_sc[...]
        ycols = jnp.zeros((rows, 128), F32)
        for j in range(steps):
            if j % 2 == 0:
                vcols = pair_sums(rows_of(v8, j) * diag, rows_of(v8, j + 1) * diag if j + 1 < steps else None)
            s = s * rows_of(w8, j) - sk * rows_of(b8, j) + vcols[j % 2] * rows_of(k8, j)
            kk_after = rows_of(kk8, j + 1) if j + 1 < 8 else rows_of(kk_next, 0)
            sk, yb = pair_sums(s * kk_after, s * rows_of(r8, j))
            ycols = jnp.where(key_lane == j, yb, ycols)
        s_sc[...] = s.reshape(nbb, RW_ROWS, 128)
        sk_sc[...] = sk
        h64 = RW_HEAD
        for bi in range(nbb):
            pieces = []
            for q in range(0, RW_PAIRS, 2):
                r0 = (bi * RW_PAIRS + q) * h64
                tr = ycols[r0:r0 + 2 * h64].T
                pieces += [tr[0:8, 0:h64], tr[h64:h64 + 8, 0:h64], tr[0:8, h64:], tr[h64:h64 + 8, h64:]]
            y_sc[bi, pl.ds(t0, 8), :] = jnp.concatenate(pieces, axis=1)
        return carry

    lax.fori_loop(0, -(-t_valid // 8), group, 0)

    inv = 1.0 / RW_HEAD
    for bi in range(nbb):
        y = y_sc[bi]
        yc = y - head_sums_wide(y) * inv
        var = head_sums_wide(yc * yc) * inv
        yn = yc * lax.rsqrt(var + RW_LN_EPS) * lnw_ref[...] + lnb_ref[...]
        bonus = head_sums_wide(r_sc[bi] * k_sc[bi] * rk_ref[...]) * v_sc[bi]
        y_ref[bi] = (yn + bonus) * g_sc[bi]

    @pl.when(c == pl.num_programs(1) - 1)
    def _():
        sfin_ref[...] = s_sc[...]


def _wkv_to_rows(s):
    b = s.shape[0]
    return s.reshape(b, RW_PAIRS, 2, RW_HEAD, RW_HEAD).transpose(0, 1, 3, 2, 4).reshape(b, RW_ROWS, 128)


def _rows_to_wkv(s):
    b = s.shape[0]
    return s.reshape(b, RW_PAIRS, RW_HEAD, 2, RW_HEAD).transpose(0, 1, 3, 2, 4).reshape(b, RW_HEADS, RW_HEAD, RW_HEAD)


def rwkv7(proj, shift0, wkv0, prm, nbb, tc, t_valid):
    b, t, _ = proj.shape
    vec = lambda n: pl.BlockSpec((1, n), lambda i, c: (0, 0))
    full = lambda a: pl.BlockSpec(a.shape, lambda i, c: (0,) * a.ndim)
    seq = pltpu.VMEM((nbb, tc, RW_DIM), F32)
    y, s_fin = pl.pallas_call(
        functools.partial(_rwkv_body, nbb=nbb, tc=tc, t_valid=t_valid),
        grid=(b // nbb, t // tc),
        in_specs=[pl.BlockSpec((nbb, tc, RW_COLS), lambda i, c: (i, c, 0)),
                  pl.BlockSpec((nbb, 1, RW_COLS), lambda i, c: (i, 0, 0)),
                  pl.BlockSpec((nbb, RW_ROWS, 128), lambda i, c: (i, 0, 0)),
                  vec(RW_COLS), vec(RW_DIM), vec(RW_DIM), vec(RW_DIM), vec(RW_DIM), vec(RW_DIM), vec(RW_DIM),
                  vec(RW_DIM), full(prm["wl"]), full(prm["g2"]), full(prm["bones2"]), full(prm["pair_w"])],
        out_specs=[pl.BlockSpec((nbb, tc, RW_DIM), lambda i, c: (i, c, 0)),
                   pl.BlockSpec((nbb, RW_ROWS, 128), lambda i, c: (i, 0, 0))],
        out_shape=[jax.ShapeDtypeStruct((b, t, RW_DIM), F32),
                   jax.ShapeDtypeStruct((b, RW_ROWS, 128), F32)],
        scratch_shapes=[pltpu.VMEM((nbb, RW_ROWS, 128), F32), pltpu.VMEM((nbb * RW_ROWS, 128), F32),
                        pltpu.VMEM((nbb, 1, RW_COLS), F32),
                        seq, seq, seq, seq, seq, seq, seq, seq],
        compiler_params=_cparams(("parallel", "arbitrary")),
    )(proj, shift0, _wkv_to_rows(wkv0), prm["mu"], prm["w0"], prm["a0"], prm["k_k"], prm["k_a"], prm["r_k"],
      prm["ln_w"], prm["ln_b"], prm["wl"], prm["g2"], prm["bones2"], prm["pair_w"])
    return y, _rows_to_wkv(s_fin)


def _ssd_body(z_ref, xbc_ref, dt_ref, conv0_ref, h0_ref, cw_ref, cb_ref, dtb_ref, alog_ref, dsk_ref, nw_ref,
              e_ref, y_ref, hfin_ref, h_sc, ext_sc, *, t_valid):
    c = pl.program_id(1)
    ln = SSD_CHUNK

    @pl.when(c == 0)
    def _():
        h_sc[...] = h0_ref[0]
        ext_sc[0:8, :] = conv0_ref[0]

    xbc = xbc_ref[0]
    ext_sc[8:8 + ln, :] = xbc
    cw = cw_ref[...]
    conv = cb_ref[...] + ext_sc[5:5 + ln, :] * cw[0:1]
    conv = conv + ext_sc[6:6 + ln, :] * cw[1:2]
    conv = conv + ext_sc[7:7 + ln, :] * cw[2:3]
    conv = conv + xbc * cw[3:4]
    ext_sc[0:8, :] = xbc[ln - 8:ln, :]
    act = _silu(conv)
    xs = act[:, :SSM_INNER]
    bm = act[:, SSM_INNER:SSM_INNER + SSM_GROUPS * SSM_STATE].astype(BF16)
    cm = act[:, SSM_INNER + SSM_GROUPS * SSM_STATE:].astype(BF16)

    lane = lax.broadcasted_iota(jnp.int32, (ln, 128), 1)
    rowi = lax.broadcasted_iota(jnp.int32, (ln, 128), 0)
    dt = _softplus(dt_ref[0] + dtb_ref[...])
    if t_valid < ln:
        dt = jnp.where(rowi < t_valid, dt, 0.0)
    a = dt * jnp.where(lane[0:1] < SSM_HEADS, -jnp.exp(alog_ref[...]), 0.0)
    causal = lane <= rowi
    tri = jnp.where(causal, 1.0, 0.0).astype(BF16)
    tri3 = jnp.concatenate([tri, tri, tri], axis=1)
    acum = _dot(tri3, jnp.concatenate(_split3(a), axis=0))
    acum_t = _nt(jnp.concatenate(_split3(a.T), axis=1), tri3)
    e3 = e_ref[...]
    dt_full = _dot(jnp.concatenate(_split3(dt), axis=1), e3)
    ac_full = _dot(jnp.concatenate(_split3(acum), axis=1), e3)
    xdt = xs * dt_full
    xw = xdt * jnp.exp(ac_full[ln - 1:ln, :] - ac_full)
    eac = jnp.exp(ac_full)
    cd = jnp.broadcast_to(jnp.exp(acum_t[:, ln - 1:ln]), (128, 128))
    cd_rows = jnp.concatenate([jnp.broadcast_to(cd[h:h + 1, :], (SSM_HEAD, SSM_STATE)) for h in range(SSM_HEADS)],
                              axis=0)

    ys = []
    for g in range(SSM_GROUPS):
        bm_g = bm[:, g * SSM_STATE:(g + 1) * SSM_STATE]
        cm_g = cm[:, g * SSM_STATE:(g + 1) * SSM_STATE]
        cbm = _nt(cm_g, bm_g)
        pairs = SSM_HEADS // SSM_GROUPS // 2
        for q in range(g * pairs, (g + 1) * pairs):
            sl = slice(q * 128, (q + 1) * 128)
            xdt_p = xdt[:, sl].astype(BF16)
            yd = []
            for h in (2 * q, 2 * q + 1):
                seg = jnp.where(causal, acum[:, h:h + 1] - acum_t[h:h + 1, :], NEG_INF)
                yd.append(_dot((cbm * jnp.exp(seg)).astype(BF16), xdt_p))
            hp = h_sc[sl, :]
            y_off = _nt(cm_g, hp.astype(BF16)) * eac[:, sl]
            st = _dot(xw[:, sl].T.astype(BF16), bm_g)
            h_sc[sl, :] = cd_rows[sl, :] * hp + st
            ys.append(jnp.where(lane < SSM_HEAD, yd[0], yd[1]) + y_off)
    y = jnp.concatenate(ys, axis=1) + dsk_ref[...] * xs
    y = y * _silu(z_ref[0])
    gw = SSM_INNER // SSM_GROUPS
    outs = []
    for g in range(SSM_GROUPS):
        yg = y[:, g * gw:(g + 1) * gw]
        outs.append(yg * lax.rsqrt(jnp.mean(yg * yg, axis=1, keepdims=True) + NORM_EPS))
    y_ref[0] = jnp.concatenate(outs, axis=1) * nw_ref[...]

    @pl.when(c == pl.num_programs(1) - 1)
    def _():
        hfin_ref[0] = h_sc[...]


def mamba2(proj, conv0, ssm0, prm, t_valid):
    b, t, _ = proj.shape
    ln = SSD_CHUNK
    vec = lambda n: pl.BlockSpec((1, n), lambda i, c: (0, 0))
    full = lambda a: pl.BlockSpec(a.shape, lambda i, c: (0,) * a.ndim)
    return pl.pallas_call(
        functools.partial(_ssd_body, t_valid=t_valid),
        grid=(b, t // ln),
        in_specs=[pl.BlockSpec((1, ln, SSM_INNER), lambda i, c: (i, c, PROJ_Z // SSM_INNER)),
                  pl.BlockSpec((1, ln, SSM_CONV_DIM), lambda i, c: (i, c, PROJ_XBC // SSM_CONV_DIM)),
                  pl.BlockSpec((1, ln, 128), lambda i, c: (i, c, PROJ_DT // 128)),
                  pl.BlockSpec((1, 8, SSM_CONV_DIM), lambda i, c: (i, 0, 0)),
                  pl.BlockSpec((1, SSM_INNER, SSM_STATE), lambda i, c: (i, 0, 0)),
                  full(prm["conv_w"]), vec(SSM_CONV_DIM), vec(128), vec(128), vec(SSM_INNER), vec(SSM_INNER),
                  full(prm["expand"])],
        out_specs=[pl.BlockSpec((1, ln, SSM_INNER), lambda i, c: (i, c, 0)),
                   pl.BlockSpec((1, SSM_INNER, SSM_STATE), lambda i, c: (i, 0, 0))],
        out_shape=[jax.ShapeDtypeStruct((b, t, SSM_INNER), F32),
                   jax.ShapeDtypeStruct((b, SSM_INNER, SSM_STATE), F32)],
        scratch_shapes=[pltpu.VMEM((SSM_INNER, SSM_STATE), F32), pltpu.VMEM((8 + ln, SSM_CONV_DIM), F32)],
        compiler_params=_cparams(("parallel", "arbitrary")),
    )(proj, proj, proj, conv0, ssm0, prm["conv_w"], prm["conv_b"], prm["dt_bias"], prm["a_log"],
      prm["d_full"], prm["norm_w"], prm["expand"])


def _alibi_slopes():
    return (2.0 ** (-8.0 * np.arange(1, ATT_HEADS + 1) / ATT_HEADS)).astype(np.float32)


def _slope_features():
    s = _alibi_slopes()
    bf = lambda a: a.astype(jnp.bfloat16).astype(np.float32)
    hi = bf(s)
    mid = bf(s - hi)
    lo = bf(s - hi - mid)
    out = np.zeros((ATT_HEADS, FEAT_BLOCKS), np.float32)
    for i, piece in enumerate((hi, mid, lo)):
        out[:, i] = piece
        out[:, SLOPE_PARTS + i] = piece
    return out


def _top3_threshold(gate):
    v = gate
    for _ in range(MOBA_TOPK - 1):
        v = jnp.where(v == jnp.max(v, axis=1, keepdims=True), NEG_INF, v)
    return jnp.max(v, axis=1, keepdims=True)


def _blockmean_body(k_ref, o_ref):
    o_ref[0] = jnp.mean(k_ref[...], axis=0, keepdims=True)


def block_means(qkv, nblocks):
    return pl.pallas_call(
        _blockmean_body,
        grid=(nblocks,),
        in_specs=[pl.BlockSpec((MOBA_BLOCK, ATT_KV_COLS), lambda i: (i, ATT_Q_COLS // ATT_KV_COLS))],
        out_specs=pl.BlockSpec((1, 1, ATT_KV_COLS), lambda i: (i, 0, 0)),
        out_shape=jax.ShapeDtypeStruct((nblocks, 1, ATT_KV_COLS), F32),
        compiler_params=_cparams(("parallel",)),
    )(qkv)


def _qkv_body(x_ref, g_ref, w_ref, qkv_ref, kf_ref, vf_ref, *, tiles_per_seq):
    tm = x_ref.shape[0]
    qkv = _dot(_rms(x_ref[...], g_ref[...]).astype(BF16), w_ref[...])
    qkv_ref[...] = qkv
    pos = (pl.program_id(0) % tiles_per_seq) * tm + lax.broadcasted_iota(jnp.int32, (tm, ATT_HEAD), 0)
    lane = lax.broadcasted_iota(jnp.int32, (tm, ATT_HEAD), 1)
    blk_id, off = pos // MOBA_BLOCK, pos % MOBA_BLOCK
    kfeat = jnp.where(lane == blk_id, 1, 0)
    kfeat = jnp.where((lane >= FEAT_START) & (lane < FEAT_START + SLOPE_PARTS), blk_id * MOBA_BLOCK, kfeat)
    kfeat = jnp.where((lane >= FEAT_OFFSET) & (lane < FEAT_OFFSET + SLOPE_PARTS), off, kfeat)
    kfeat = kfeat.astype(F32).astype(BF16)
    vfeat = jnp.where(lane == 0, 1.0, 0.0).astype(BF16)
    for g in range(ATT_KV_HEADS):
        k0 = ATT_Q_COLS + g * ATT_HEAD
        v0 = ATT_Q_COLS + ATT_KV_COLS + g * ATT_HEAD
        kf_ref[0, g] = jnp.concatenate([qkv[:, k0:k0 + ATT_HEAD].astype(BF16), kfeat], axis=1)
        vf_ref[0, g] = jnp.concatenate([qkv[:, v0:v0 + ATT_HEAD].astype(BF16), vfeat], axis=1)


def qkv_proj(x, g, w, b, t, tm):
    m, d = x.shape
    n = w.shape[1]
    tps = t // tm
    ext = jax.ShapeDtypeStruct((b, ATT_KV_HEADS, t, 128), BF16)
    ext_spec = pl.BlockSpec((1, ATT_KV_HEADS, tm, 128), lambda i: (i // tps, 0, i % tps, 0))
    return pl.pallas_call(
        functools.partial(_qkv_body, tiles_per_seq=tps),
        grid=(m // tm,),
        in_specs=[pl.BlockSpec((tm, d), lambda i: (i, 0)), pl.BlockSpec((1, d), lambda i: (0, 0)),
                  pl.BlockSpec((d, n), lambda i: (0, 0))],
        out_specs=[pl.BlockSpec((tm, n), lambda i: (i, 0)), ext_spec, ext_spec],
        out_shape=[jax.ShapeDtypeStruct((m, n), F32), ext, ext],
        compiler_params=_cparams(("parallel",)),
    )(x, g, w)


def _moba_prompt_body(q_ref, kf_ref, vf_ref, mean_ref, slf_ref, o_ref, qf_sc, m_sc, acc_sc, sa0_sc, sa1_sc, sb0_sc, sb1_sc):
    qb = pl.program_id(2)
    blk = MOBA_BLOCK
    rows = ATT_GROUP * blk
    scale = ATT_HEAD ** -0.5

    q4 = q_ref[0]
    qg = jnp.concatenate([q4[:, hh * ATT_HEAD:(hh + 1) * ATT_HEAD] for hh in range(ATT_GROUP)], axis=0)
    past_blk = lax.broadcasted_iota(jnp.int32, (128, rows), 0) < qb
    gate = jnp.where(past_blk, _nt(_cat3w(mean_ref[0, 0], 1), _cat3(qg, 1)), NEG_INF)
    v = gate
    for _ in range(MOBA_TOPK - 1):
        v = jnp.where(v == jnp.max(v, axis=0, keepdims=True), NEG_INF, v)
    thr = jnp.max(v, axis=0, keepdims=True)
    selneg = jnp.where(past_blk & (gate < thr), MASK_NEG, 0.0).T[:, :FEAT_BLOCKS]
    slf = jnp.concatenate([jnp.broadcast_to(slf_ref[0, hh:hh + 1, :], (blk, FEAT_BLOCKS))
                           for hh in range(ATT_GROUP)], axis=0)
    qf_sc[...] = jnp.concatenate([(qg * scale).astype(BF16), selneg.astype(BF16), slf.astype(BF16)], axis=1)

    def kv_tiles(j):
        start = pl.multiple_of(j * blk, blk)
        return kf_ref[0, 0, pl.ds(start, blk), :], vf_ref[0, 0, pl.ds(start, blk), :]

    qi = lax.broadcasted_iota(jnp.int32, (rows, blk), 0) % blk
    ki = lax.broadcasted_iota(jnp.int32, (rows, blk), 1)
    kf, vf = kv_tiles(qb)
    s = jnp.where(ki <= qi, _nt(qf_sc[...], kf), NEG_INF)
    m = jnp.max(s, axis=1, keepdims=True)
    m_sc[...] = jnp.broadcast_to(m, (rows, 128))
    acc_sc[...] = _dot(jnp.exp(s - m).astype(BF16), vf)

    def scores(j):
        return _nt(qf_sc[...], kv_tiles(jnp.minimum(j, qb))[0])

    def fold(blocks, s_cur, s_next=()):
        for ref, j in zip(s_next, (blocks[-1] + 1, blocks[-1] + 2)):
            ref[...] = scores(j)
        ss = [ref[...] for ref in s_cur]
        m_old = m_sc[...]
        m_new = m_old
        for s in ss:
            m_new = jnp.maximum(m_new, jnp.max(s, axis=1, keepdims=True))
        m2 = jnp.concatenate([m_new, m_new], axis=1)
        acc = acc_sc[...] * jnp.exp(m_old - m_new)
        for s, j in zip(ss, blocks):
            acc = acc + _dot(jnp.exp(s - m2).astype(BF16), kv_tiles(j)[1])
        acc_sc[...] = acc
        m_sc[...] = m_new

    set_a, set_b = (sa0_sc, sa1_sc), (sb0_sc, sb1_sc)
    sa0_sc[...] = scores(0)
    sa1_sc[...] = scores(1)

    def quad(i, carry):
        fold((4 * i, 4 * i + 1), set_a, set_b)
        fold((4 * i + 2, 4 * i + 3), set_b, set_a)
        return carry

    lax.fori_loop(0, qb // 4, quad, 0)
    done = qb // 4 * 4

    @pl.when(qb % 4 >= 2)
    def _():
        fold((done, done + 1), set_a)

    @pl.when(qb % 4 == 1)
    def _():
        fold((done,), set_a[:1])

    @pl.when(qb % 4 == 3)
    def _():
        sb0_sc[...] = scores(done + 2)
        fold((done + 2,), set_b[:1])

    acc = acc_sc[...]
    out = acc[:, :ATT_HEAD] / acc[:, ATT_HEAD:ATT_HEAD + 1]
    o_ref[0] = jnp.concatenate([out[hh * blk:(hh + 1) * blk] for hh in range(ATT_GROUP)], axis=1)


def moba_prompt(qkv, kf, vf, means, slf):
    b, t, _ = qkv.shape
    nb = t // MOBA_BLOCK
    assert nb <= 32
    nbl = 128
    means = jnp.pad(means, ((0, 0), (0, 0), (0, nbl - nb), (0, 0)))
    rows = ATT_GROUP * MOBA_BLOCK
    gw = ATT_GROUP * ATT_HEAD
    return pl.pallas_call(
        _moba_prompt_body,
        grid=(b, ATT_KV_HEADS, nb),
        in_specs=[pl.BlockSpec((1, MOBA_BLOCK, gw), lambda i, g, j: (i, j, g)),
                  pl.BlockSpec((1, 1, t, 128), lambda i, g, j: (i, g, 0, 0)),
                  pl.BlockSpec((1, 1, t, 128), lambda i, g, j: (i, g, 0, 0)),
                  pl.BlockSpec((1, 1, nbl, ATT_HEAD), lambda i, g, j: (i, g, 0, 0)),
                  pl.BlockSpec((1, ATT_GROUP, FEAT_BLOCKS), lambda i, g, j: (g, 0, 0))],
        out_specs=pl.BlockSpec((1, MOBA_BLOCK, gw), lambda i, g, j: (i, j, g)),
        out_shape=jax.ShapeDtypeStruct((b, t, ATT_Q_COLS), F32),
        scratch_shapes=[pltpu.VMEM((rows, 128), BF16), pltpu.VMEM((rows, 128), F32), pltpu.VMEM((rows, 128), F32),
                        *([pltpu.VMEM((rows, MOBA_BLOCK), F32)] * 4)],
        compiler_params=_cparams(("parallel", "parallel", "arbitrary")),
    )(qkv, kf, vf, means, slf)


def _moba_sample_body(pt_ref, *refs, nbp, bps, nq):
    npg = 2 * bps
    k_refs, v_refs = refs[:npg], refs[npg:2 * npg]
    q_ref, kn_ref, vn_ref, slope_ref, o_ref, mean_sc, m_sc, l_sc, acc_sc = refs[2 * npg:]
    j = pl.program_id(1)
    blk = MOBA_BLOCK
    rows = ATT_HEADS * nq
    grows = ATT_GROUP * nq
    past = nbp * blk
    scale = ATT_HEAD ** -0.5
    lane = lax.broadcasted_iota(jnp.int32, (rows, 128), 1)
    qi = (lax.broadcasted_iota(jnp.int32, (rows, 1), 0) % nq).astype(F32)
    qf = q_ref[0]
    qs = (qf * scale).astype(BF16)
    slope = slope_ref[...]

    @pl.when(j == 0)
    def _():
        mean_sc[...] = jnp.zeros_like(mean_sc)
        m_sc[...] = jnp.full(m_sc.shape, NEG_INF, F32)
        l_sc[...] = jnp.zeros_like(l_sc)

    kcat = jnp.concatenate([ref[0] for ref in k_refs], axis=0)
    off = lax.broadcasted_iota(jnp.int32, (1, bps * blk), 1).astype(F32)
    sc = _nt(qs, kcat.astype(BF16)) - slope * ((past - j * bps * blk).astype(F32) + qi - off)
    m_new, l_new = m_sc[...], l_sc[...]
    for s in range(bps):
        bi = j * bps + s
        ksum = jnp.sum(kcat[s * blk:(s + 1) * blk], axis=0, keepdims=True)
        mean_sc[pl.ds(bi, 1), :] = ksum * (1.0 / blk)
        sc_b = sc[:, s * blk:(s + 1) * blk]
        m = jnp.max(sc_b, axis=1, keepdims=True)
        p = jnp.exp(sc_b - m)
        m_new = jnp.where(lane == bi, m, m_new)
        l_new = jnp.where(lane == bi, jnp.sum(p, axis=1, keepdims=True), l_new)
        vblk = jnp.concatenate([v_refs[2 * s][0], v_refs[2 * s + 1][0]], axis=0)
        acc_sc[bi] = _dot(p.astype(BF16), vblk.astype(BF16))
    m_sc[...] = m_new
    l_sc[...] = l_new

    @pl.when(j == pl.num_programs(1) - 1)
    def _():
        gate = jnp.where(lane < nbp, _nt(_cat3(qf, 1), _cat3w(mean_sc[...], 1)), NEG_INF)
        sel = gate >= _top3_threshold(gate)
        ki = lax.broadcasted_iota(jnp.int32, (rows, 8), 1).astype(F32)
        s_own = jnp.where(ki <= qi, _nt(qs, kn_ref[0].astype(BF16)) - slope * (qi - ki), NEG_INF)
        m_all = jnp.where(sel, m_sc[...], NEG_INF)
        mx = jnp.maximum(jnp.max(m_all, axis=1, keepdims=True), jnp.max(s_own, axis=1, keepdims=True))
        w = jnp.where(sel, jnp.exp(m_all - mx), 0.0)
        p_own = jnp.exp(s_own - mx)
        den = jnp.sum(w * l_sc[...], axis=1, keepdims=True) + jnp.sum(p_own, axis=1, keepdims=True)
        acc = _dot(p_own.astype(BF16), vn_ref[0].astype(BF16))
        for n in range(nbp):
            acc = acc + w[:, n:n + 1] * acc_sc[n]
        out = acc / den
        for g in range(ATT_KV_HEADS):
            o_ref[0, g] = out[g * grows:(g + 1) * grows, g * ATT_HEAD:(g + 1) * ATT_HEAD]


SAMPLE_BLOCKS_PER_STEP = 8


def _pages2d(cache):
    return cache.reshape(cache.shape[0], PAGE_SIZE, ATT_KV_COLS)


def _block_diag_queries(q):
    db, nq, _ = q.shape
    qh = q.reshape(db, nq, ATT_KV_HEADS, ATT_GROUP, ATT_HEAD).transpose(0, 2, 3, 1, 4)
    eye = jnp.eye(ATT_KV_HEADS, dtype=q.dtype)
    return (qh[:, :, :, :, None, :] * eye[None, :, None, None, :, None]).reshape(db, ATT_HEADS * nq, ATT_KV_COLS)


def _new_rows(a):
    db, nq = a.shape[:2]
    return jnp.pad(a.reshape(db, nq, ATT_KV_COLS), ((0, 0), (0, 8 - nq), (0, 0)))


def moba_sample(page_table, cache_k, cache_v, q_bd, k_new, v_new, slope_rows, nq):
    db, n_pages = page_table.shape
    nbp = n_pages * PAGE_SIZE // MOBA_BLOCK
    bps = min(SAMPLE_BLOCKS_PER_STEP, nbp)
    rows = ATT_HEADS * nq
    grows = ATT_GROUP * nq

    def page_spec(slot):
        return pl.BlockSpec((1, PAGE_SIZE, ATT_KV_COLS),
                            lambda b, j, pt: (pt[b * n_pages + j * 2 * bps + slot], 0, 0))

    small = lambda shp: pl.BlockSpec((1,) + shp, lambda b, j, pt: (b,) + (0,) * len(shp))
    grid_spec = pltpu.PrefetchScalarGridSpec(
        num_scalar_prefetch=1,
        grid=(db, nbp // bps),
        in_specs=[page_spec(s) for s in range(2 * bps)] * 2
        + [small((rows, ATT_KV_COLS)), small((8, ATT_KV_COLS)), small((8, ATT_KV_COLS)),
           pl.BlockSpec((rows, 1), lambda b, j, pt: (0, 0))],
        out_specs=small((ATT_KV_HEADS, grows, ATT_HEAD)),
        scratch_shapes=[pltpu.VMEM((128, ATT_KV_COLS), F32), pltpu.VMEM((rows, 128), F32),
                        pltpu.VMEM((rows, 128), F32), pltpu.VMEM((nbp, rows, ATT_KV_COLS), F32)],
    )
    return pl.pallas_call(
        functools.partial(_moba_sample_body, nbp=nbp, bps=bps, nq=nq),
        grid_spec=grid_spec,
        out_shape=jax.ShapeDtypeStruct((db, ATT_KV_HEADS, grows, ATT_HEAD), F32),
        compiler_params=_cparams(("parallel", "arbitrary")),
    )(page_table.reshape(-1), *([cache_k] * (2 * bps)), *([cache_v] * (2 * bps)), q_bd, k_new, v_new, slope_rows)


def _row(a):
    return a.reshape(1, -1).astype(F32)


def _even_params(i, w_in_ab, rwkv_mu, rwkv_w0, rwkv_w2, rwkv_a0, rwkv_a2, rwkv_g2, rwkv_k_k, rwkv_k_a, rwkv_r_k,
                 rwkv_ln_w, rwkv_ln_b, ssm_conv_w, ssm_conv_b, ssm_dt_bias, ssm_a_log, ssm_d, ssm_norm_w):
    w = w_in_ab[i]
    ssm0 = RW_COLS
    zeros = lambda n: jnp.zeros((D_MODEL, n), w.dtype)
    w_pack = jnp.concatenate([
        w[:, :RW_COLS],
        w[:, ssm0 + SSM_INNER + SSM_CONV_DIM:], zeros(PROJ_Z - PROJ_DT - SSM_HEADS),
        w[:, ssm0:ssm0 + SSM_INNER],
        w[:, ssm0 + SSM_INNER:ssm0 + SSM_INNER + SSM_CONV_DIM]], axis=1).astype(BF16)
    half = RW_LORA // 2
    zl = jnp.zeros((half, RW_DIM), F32)
    wl = jnp.concatenate([jnp.concatenate([rwkv_w2[i], zl], axis=1),
                          jnp.concatenate([zl, rwkv_a2[i]], axis=1)], axis=0)
    half_of = np.arange(128) // RW_HEAD
    bones2 = np.tile((half_of[:, None] == half_of[None, :]).astype(np.float32), (2, 1))
    rw = dict(mu=_row(rwkv_mu[i]), w0=_row(rwkv_w0[i]), a0=_row(rwkv_a0[i]), k_k=_row(rwkv_k_k[i]),
              k_a=_row(rwkv_k_a[i]), r_k=_row(rwkv_r_k[i]), ln_w=_row(rwkv_ln_w[i]), ln_b=_row(rwkv_ln_b[i]),
              wl=_cat3w(wl, 0), g2=_cat3w(rwkv_g2[i], 0), bones2=jnp.asarray(bones2, BF16),
              pair_w=jnp.asarray(np.kron(np.eye(4, dtype=np.float32), np.ones((RW_HEAD, RW_HEAD), np.float32)), BF16))
    pad128 = lambda a: jnp.pad(_row(a), ((0, 0), (0, 128 - a.shape[-1])))
    expand = (np.arange(128)[:, None] == (np.arange(SSM_INNER) // SSM_HEAD)[None, :]).astype(np.float32)
    ssm = dict(conv_w=ssm_conv_w[i], conv_b=_row(ssm_conv_b[i]), dt_bias=pad128(ssm_dt_bias[i]),
               a_log=pad128(ssm_a_log[i]), d_full=_row(jnp.repeat(ssm_d[i], SSM_HEAD)), norm_w=_row(ssm_norm_w[i]),
               expand=jnp.asarray(np.tile(expand, (3, 1)), BF16))
    return w_pack, rw, ssm


def _mixer_ab(x, b, t, norm_g, w_pack, rw, ssm, w_out, shift0, wkv0, conv0, ssm0, tm):
    tm_in = 2 * tm if (b * t) % (2 * tm) == 0 else tm
    proj = norm_matmul(x, norm_g, w_pack, tm_in, SSM_CONV_DIM).reshape(b, t, PROJ_COLS)
    shift_new = proj[:, t - 1, :RW_COLS]
    conv_new = proj[:, t - (SSM_CONV - 1):, PROJ_XBC:]
    if t % SSD_CHUNK == 0:
        proj_rw, proj_ssm, tc = proj, proj, SSD_CHUNK
    else:
        proj_rw = jnp.pad(proj, ((0, 0), (0, 8 - t), (0, 0)))
        proj_ssm = jnp.pad(proj, ((0, 0), (0, SSD_CHUNK - t), (0, 0)))
        tc = 8
    nbb = next(n for n in (4, 2, 1) if b % n == 0)
    y_rw, wkv_new = rwkv7(proj_rw, shift0.reshape(b, 1, RW_COLS), wkv0, rw, nbb, tc, min(t, tc))
    conv0p = jnp.pad(conv0, ((0, 0), (8 - (SSM_CONV - 1), 0), (0, 0)))
    y_ssm, ssm_new = mamba2(proj_ssm, conv0p, ssm0.reshape(b, SSM_INNER, SSM_STATE), ssm, min(t, SSD_CHUNK))
    x = matmul_residual(x, [y_rw[:, :t].reshape(b * t, RW_DIM), y_ssm[:, :t].reshape(b * t, SSM_INNER)],
                        [w_out[:RW_DIM], w_out[RW_DIM:]], tm)
    return x, (shift_new, wkv_new, conv_new, ssm_new.reshape(b, SSM_HEADS, SSM_HEAD, SSM_STATE))


def kernel(x_prompt, x_sample, state_rwkv_shift, state_rwkv_wkv, state_ssm_conv, state_ssm, cache_k, cache_v, page_table, norm_mix, norm_ffn, norm_final, w_in_ab, rwkv_mu, rwkv_w0, rwkv_w2, rwkv_a0, rwkv_a2, rwkv_g2, rwkv_k_k, rwkv_k_a, rwkv_r_k, rwkv_ln_w, rwkv_ln_b, ssm_conv_w, ssm_conv_b, ssm_dt_bias, ssm_a_log, ssm_d, ssm_norm_w, w_out_ab, ffn_w_gate, ffn_w_up, ffn_w_down, attn_w_qkv, attn_w_o, moe_router, moe_w_gate, moe_w_up, moe_w_down):
    bp, tp, _ = x_prompt.shape
    db, ts, _ = x_sample.shape
    depth = norm_mix.shape[0]
    tm_p, tm_s = 512, db * ts
    xp = x_prompt.reshape(bp * tp, D_MODEL)
    xs = x_sample.reshape(db * ts, D_MODEL)
    slopes = _alibi_slopes()
    slf = jnp.asarray(_slope_features().reshape(ATT_KV_HEADS, ATT_GROUP, FEAT_BLOCKS))
    slope_rows = jnp.asarray(np.repeat(slopes, ts).reshape(ATT_HEADS * ts, 1))
    st = {n: [] for n in ("p_shift", "p_wkv", "p_conv", "p_ssm", "p_k", "p_v",
                          "s_shift", "s_wkv", "s_conv", "s_ssm", "s_k", "s_v")}
    for l in range(depth):
        i = l // 2
        g_mix, g_ffn = _row(norm_mix[l]), _row(norm_ffn[l])
        if l % 2 == 0:
            w_pack, rw, ssm = _even_params(i, w_in_ab, rwkv_mu, rwkv_w0, rwkv_w2, rwkv_a0, rwkv_a2, rwkv_g2,
                                           rwkv_k_k, rwkv_k_a, rwkv_r_k, rwkv_ln_w, rwkv_ln_b, ssm_conv_w,
                                           ssm_conv_b, ssm_dt_bias, ssm_a_log, ssm_d, ssm_norm_w)
            w_out = w_out_ab[i].astype(BF16)
            xp, sp = _mixer_ab(xp, bp, tp, g_mix, w_pack, rw, ssm, w_out,
                               jnp.zeros((bp, RW_COLS), F32), jnp.zeros((bp, RW_HEADS, RW_HEAD, RW_HEAD), F32),
                               jnp.zeros((bp, SSM_CONV - 1, SSM_CONV_DIM), F32),
                               jnp.zeros((bp, SSM_HEADS, SSM_HEAD, SSM_STATE), F32), tm_p)
            xs, ss = _mixer_ab(xs, db, ts, g_mix, w_pack, rw, ssm, w_out, state_rwkv_shift[i], state_rwkv_wkv[i],
                               state_ssm_conv[i], state_ssm[i], tm_s)
            for pre, new in (("p", sp), ("s", ss)):
                for name, val in zip(("shift", "wkv", "conv", "ssm"), new):
                    st[f"{pre}_{name}"].append(val)
            wg, wu, wd = ffn_w_gate[i].astype(BF16), ffn_w_up[i].astype(BF16), ffn_w_down[i].astype(BF16)
            xp = ffn_swiglu(xp, g_ffn, wg, wu, wd, 1024, 256)
            xs = ffn_swiglu(xs, g_ffn, wg, wu, wd, tm_s, 256)
        else:
            w_qkv, w_o = attn_w_qkv[i].astype(BF16), attn_w_o[i].astype(BF16)
            qkv_p, kf, vf = qkv_proj(xp, g_mix, w_qkv, bp, tp, tm_p)
            nb = tp // MOBA_BLOCK
            means = block_means(qkv_p, bp * nb).reshape(bp, nb, ATT_KV_HEADS, ATT_HEAD).transpose(0, 2, 1, 3)
            k_p = qkv_p[:, ATT_Q_COLS:ATT_Q_COLS + ATT_KV_COLS].reshape(bp, tp, ATT_KV_HEADS, ATT_HEAD)
            v_p = qkv_p[:, ATT_Q_COLS + ATT_KV_COLS:].reshape(bp, tp, ATT_KV_HEADS, ATT_HEAD)
            o_p = moba_prompt(qkv_p.reshape(bp, tp, -1), kf, vf, means, slf)
            xp = matmul_residual(xp, [o_p.reshape(bp * tp, ATT_Q_COLS)], [w_o], tm_p)
            st["p_k"].append(k_p)
            st["p_v"].append(v_p)

            qkv_s = norm_matmul(xs, g_mix, w_qkv, tm_s, 512).reshape(db, ts, -1)
            kn = qkv_s[..., ATT_Q_COLS:ATT_Q_COLS + ATT_KV_COLS].reshape(db, ts, ATT_KV_HEADS, ATT_HEAD)
            vn = qkv_s[..., ATT_Q_COLS + ATT_KV_COLS:].reshape(db, ts, ATT_KV_HEADS, ATT_HEAD)
            o_s = moba_sample(page_table, _pages2d(cache_k[i]), _pages2d(cache_v[i]),
                              _block_diag_queries(qkv_s[..., :ATT_Q_COLS]), _new_rows(kn), _new_rows(vn),
                              slope_rows, ts)
            o_s = o_s.reshape(db, ATT_HEADS, ts, ATT_HEAD).transpose(0, 2, 1, 3).reshape(db * ts, ATT_Q_COLS)
            xs = matmul_residual(xs, [o_s], [w_o], tm_s)
            st["s_k"].append(kn)
            st["s_v"].append(vn)

            router = _cat3w(jnp.pad(moe_router[i], ((0, 0), (0, 128 - N_EXPERTS))), 0)
            wg, wu, wd = moe_w_gate[i].astype(BF16), moe_w_up[i].astype(BF16), moe_w_down[i].astype(BF16)
            final = l == depth - 1
            xp = moe_swiglu(xp, g_ffn, router, wg, wu, wd, _row(norm_final), 2 * tm_p, final)
            xs = moe_swiglu(xs, g_ffn, router, wg, wu, wd, _row(norm_final), tm_s, final)
    y_prompt = xp.reshape(bp, tp, D_MODEL)
    y_sample = xs.reshape(db, ts, D_MODEL)
    stack = lambda n: jnp.stack(st[n])
    return (y_prompt, y_sample,
            stack("p_shift"), stack("p_wkv"), stack("p_conv"), stack("p_ssm"), stack("p_k"), stack("p_v"),
            stack("s_shift"), stack("s_wkv"), stack("s_conv"), stack("s_ssm"), stack("s_k"), stack("s_v"))
```

```python
import functools

import numpy as np
import jax
import jax.numpy as jnp
from jax import lax
from jax.experimental import pallas as pl
from jax.experimental.pallas import tpu as pltpu

F32 = jnp.float32
BF16 = jnp.bfloat16
NEG_INF = float("-inf")

D_MODEL = 1024
NORM_EPS = 1e-6

RW_HEAD = 64
RW_HEADS = 8
RW_DIM = RW_HEADS * RW_HEAD
RW_LORA = 128
RW_GATE = 128
RW_COLS = 3 * RW_DIM + RW_LORA + RW_GATE
RW_LN_EPS = 64e-5

SSM_INNER = 1024
SSM_HEAD = 64
SSM_HEADS = 16
SSM_GROUPS = 2
SSM_STATE = 128
SSM_CONV = 4
SSM_CONV_DIM = SSM_INNER + 2 * SSM_GROUPS * SSM_STATE
SSD_CHUNK = 128

PROJ_RW = 0
PROJ_DT = RW_COLS
PROJ_Z = 2048
PROJ_XBC = 3072
PROJ_COLS = PROJ_XBC + SSM_CONV_DIM

ATT_HEADS = 16
ATT_KV_HEADS = 4
ATT_HEAD = 64
ATT_GROUP = ATT_HEADS // ATT_KV_HEADS
ATT_Q_COLS = ATT_HEADS * ATT_HEAD
ATT_KV_COLS = ATT_KV_HEADS * ATT_HEAD
MOBA_BLOCK = 256
MOBA_TOPK = 3
MASK_NEG = -1e30
PAGE_SIZE = 128
FEAT_BLOCKS = 32
FEAT_START = FEAT_BLOCKS
FEAT_OFFSET = FEAT_START + 3
SLOPE_PARTS = 3

N_EXPERTS = 8

V7X_VMEM_BYTES = 64 * 1024 * 1024
VMEM_LIMIT = V7X_VMEM_BYTES - 8 * 1024 * 1024


def _cparams(sem):
    return pltpu.CompilerParams(dimension_semantics=sem, vmem_limit_bytes=VMEM_LIMIT)


def _nt(a, b, precision=None):
    return lax.dot_general(a, b, (((1,), (1,)), ((), ())), precision=precision,
                           preferred_element_type=F32)


def _dot(a, b, precision=None):
    return jnp.dot(a, b, precision=precision, preferred_element_type=F32)


def _split2(x):
    hi = x.astype(BF16)
    return hi, (x - hi.astype(F32)).astype(BF16)


def _split3(x):
    hi = x.astype(BF16)
    r = x - hi.astype(F32)
    mid = r.astype(BF16)
    return hi, mid, (r - mid.astype(F32)).astype(BF16)


def _cat3(x, axis):
    hi, lo = _split2(x)
    return jnp.concatenate([hi, lo, hi], axis=axis)


def _cat3w(w, axis):
    hi, lo = _split2(w)
    return jnp.concatenate([hi, hi, lo], axis=axis)


def _rms(x, g):
    return x * lax.rsqrt(jnp.mean(x * x, axis=-1, keepdims=True) + NORM_EPS) * g


def _softplus(x):
    return jnp.maximum(x, 0.0) + jnp.log(1.0 + jnp.exp(-jnp.abs(x)))


def _sigmoid(x):
    return 1.0 / (1.0 + jnp.exp(-x))


def _silu(x):
    return x * _sigmoid(x)


def _nm_body(x_ref, g_ref, w_ref, o_ref, h_sc):
    @pl.when(pl.program_id(1) == 0)
    def _():
        h_sc[...] = _rms(x_ref[...], g_ref[...]).astype(BF16)

    o_ref[...] = _dot(h_sc[...], w_ref[...])


def norm_matmul(x, g, w, tm, tn):
    m, k = x.shape
    n = w.shape[1]
    return pl.pallas_call(
        _nm_body,
        grid=(m // tm, n // tn),
        in_specs=[pl.BlockSpec((tm, k), lambda i, j: (i, 0)),
                  pl.BlockSpec((1, k), lambda i, j: (0, 0)),
                  pl.BlockSpec((k, tn), lambda i, j: (0, j))],
        out_specs=pl.BlockSpec((tm, tn), lambda i, j: (i, j)),
        out_shape=jax.ShapeDtypeStruct((m, n), F32),
        scratch_shapes=[pltpu.VMEM((tm, k), BF16)],
        compiler_params=_cparams(("parallel", "arbitrary")),
    )(x, g, w)


def _mmres_body(*refs, n):
    res_ref, a_refs, w_refs, o_ref = refs[0], refs[1:1 + n], refs[1 + n:1 + 2 * n], refs[1 + 2 * n]
    acc = res_ref[...]
    for a_ref, w_ref in zip(a_refs, w_refs):
        acc = acc + _dot(a_ref[...].astype(BF16), w_ref[...])
    o_ref[...] = acc


def matmul_residual(res, acts, ws, tm):
    m, d = res.shape
    n = len(acts)
    in_specs = [pl.BlockSpec((tm, d), lambda i: (i, 0))]
    in_specs += [pl.BlockSpec((tm, a.shape[1]), lambda i: (i, 0)) for a in acts]
    in_specs += [pl.BlockSpec(w.shape, lambda i: (0, 0)) for w in ws]
    return pl.pallas_call(
        functools.partial(_mmres_body, n=n),
        grid=(m // tm,),
        in_specs=in_specs,
        out_specs=pl.BlockSpec((tm, d), lambda i: (i, 0)),
        out_shape=jax.ShapeDtypeStruct((m, d), F32),
        compiler_params=_cparams(("parallel",)),
    )(res, *acts, *ws)


def _ffn_body(*refs, n_mix):
    x_ref, mix_refs, wmix_refs = refs[0], refs[1:1 + n_mix], refs[1 + n_mix:1 + 2 * n_mix]
    g_ref, wg_ref, wu_ref, wd_ref, o_ref, x1_sc, h_sc, acc_sc = refs[1 + 2 * n_mix:]
    j = pl.program_id(1)

    @pl.when(j == 0)
    def _():
        x1 = x_ref[...]
        for a_ref, w_ref in zip(mix_refs, wmix_refs):
            x1 = x1 + _dot(a_ref[...].astype(BF16), w_ref[...])
        x1_sc[...] = x1
        h_sc[...] = _rms(x1, g_ref[...]).astype(BF16)
        acc_sc[...] = jnp.zeros_like(acc_sc)

    h = h_sc[...]
    act = _silu(_dot(h, wg_ref[...])) * _dot(h, wu_ref[...])
    acc_sc[...] += _dot(act.astype(BF16), wd_ref[...])

    @pl.when(j == pl.num_programs(1) - 1)
    def _():
        o_ref[...] = x1_sc[...] + acc_sc[...]


def ffn_swiglu(x, mixes, wmixes, g, wg, wu, wd, tm, tf):
    m, d = x.shape
    f = wg.shape[1]
    row_tile = lambda a: pl.BlockSpec((tm, a.shape[1]), lambda i, j: (i, 0))
    whole = lambda a: pl.BlockSpec(a.shape, lambda i, j: (0, 0))
    return pl.pallas_call(
        functools.partial(_ffn_body, n_mix=len(mixes)),
        grid=(m // tm, f // tf),
        in_specs=[row_tile(x)] + [row_tile(a) for a in mixes] + [whole(w) for w in wmixes]
        + [pl.BlockSpec((1, d), lambda i, j: (0, 0)),
           pl.BlockSpec((d, tf), lambda i, j: (0, j)),
           pl.BlockSpec((d, tf), lambda i, j: (0, j)),
           pl.BlockSpec((tf, d), lambda i, j: (j, 0))],
        out_specs=pl.BlockSpec((tm, d), lambda i, j: (i, 0)),
        out_shape=jax.ShapeDtypeStruct((m, d), F32),
        scratch_shapes=[pltpu.VMEM((tm, d), F32), pltpu.VMEM((tm, d), BF16), pltpu.VMEM((tm, d), F32)],
        compiler_params=_cparams(("parallel", "arbitrary")),
    )(x, *mixes, *wmixes, g, wg, wu, wd)


MOE_SUB = 512
MOE_CHUNK = 160


def _moe_body(x_ref, g_ref, r_ref, wg_ref, wu_ref, wd_ref, gf_ref, o_ref, h_sc, comb_sc, acc_sc, rk_sc, rkt_sc, *,
              final_norm, chunk):
    e = pl.program_id(1)
    tm = x_ref.shape[0]
    lane = lax.broadcasted_iota(jnp.int32, (tm, 128), 1)

    @pl.when(e == 0)
    def _():
        hf = _rms(x_ref[...], g_ref[...])
        h_sc[...] = hf.astype(BF16)
        lanef = lane.astype(F32)
        logits = jnp.where(lane < N_EXPERTS, _dot(_cat3(hf, 1), r_ref[...]), NEG_INF)
        m1 = jnp.max(logits, axis=1, keepdims=True)
        i1 = jnp.min(jnp.where(logits == m1, lanef, 128.0), axis=1, keepdims=True)
        mask1 = lanef == i1
        rest = jnp.where(mask1, NEG_INF, logits)
        m2 = jnp.max(rest, axis=1, keepdims=True)
        i2 = jnp.min(jnp.where(rest == m2, lanef, 128.0), axis=1, keepdims=True)
        mask2 = lanef == i2
        e2 = jnp.exp(m2 - m1)
        den = 1.0 + e2
        comb_sc[...] = jnp.where(mask1, 1.0 / den, 0.0) + jnp.where(mask2, e2 / den, 0.0)
        acc_sc[...] = jnp.zeros_like(acc_sc)
        if chunk:
            earlier = (lax.broadcasted_iota(jnp.int32, (MOE_SUB, MOE_SUB), 1)
                       < lax.broadcasted_iota(jnp.int32, (MOE_SUB, MOE_SUB), 0))
            earlier = jnp.where(earlier, 1.0, 0.0).astype(BF16)
            routed = jnp.where(mask1 | mask2, 1.0, 0.0)
            for st in range(tm // MOE_SUB):
                sl = slice(st * MOE_SUB, (st + 1) * MOE_SUB)
                rank = _dot(earlier, routed[sl].astype(BF16))
                rank = jnp.where(routed[sl] > 0.0, rank, -1.0)
                rk_sc[sl, :] = rank
                rkt_sc[st] = rank.T

    def expert(rows):
        act = _silu(_dot(rows, wg_ref[0])) * _dot(rows, wu_ref[0])
        return _dot(act.astype(BF16), wd_ref[0])

    this = lane == e
    if not chunk:
        c = jnp.sum(jnp.where(this, comb_sc[...], 0.0), axis=1, keepdims=True)
        acc_sc[...] += c * expert(h_sc[...])
    else:
        wide = -(-chunk // 128) * 128
        slot = lax.broadcasted_iota(jnp.int32, (chunk, MOE_SUB), 0).astype(F32)
        lane_w = lax.broadcasted_iota(jnp.int32, (MOE_SUB, wide), 1)
        slot_t = jnp.where(lane_w < chunk, lane_w.astype(F32), jnp.nan)
        sub8 = lax.broadcasted_iota(jnp.int32, (8, MOE_SUB), 0)
        this_sub = lax.broadcasted_iota(jnp.int32, (MOE_SUB, 128), 1) == e
        pad_rows = jnp.zeros((wide - chunk, x_ref.shape[1]), BF16)
        for st in range(tm // MOE_SUB):
            sl = slice(st * MOE_SUB, (st + 1) * MOE_SUB)
            rank_col = jnp.sum(jnp.where(this_sub, rk_sc[sl, :], 0.0), axis=1, keepdims=True)
            rank_row = jnp.sum(jnp.where(sub8 == e, rkt_sc[st, 0:8, :], 0.0), axis=0, keepdims=True)
            c = jnp.sum(jnp.where(this_sub, comb_sc[sl, :], 0.0), axis=1, keepdims=True)
            count = jnp.max(rank_row, axis=1, keepdims=True)[0, 0].astype(jnp.int32) + 1

            def one_chunk(k, carry):
                base = (k * chunk).astype(F32)
                gather = jnp.where(rank_row - base == slot, 1.0, 0.0).astype(BF16)
                y_hi, y_lo = _split2(expert(_dot(gather, h_sc[sl, :]).astype(BF16)))
                scatter = jnp.where(rank_col - base == slot_t, 1.0, 0.0).astype(BF16)
                back = (_dot(scatter, jnp.concatenate([y_hi, pad_rows], axis=0))
                        + _dot(scatter, jnp.concatenate([y_lo, pad_rows], axis=0)))
                acc_sc[sl, :] += c * back
                return carry

            lax.fori_loop(0, (count + chunk - 1) // chunk, one_chunk, 0)

    @pl.when(e == pl.num_programs(1) - 1)
    def _():
        y = x_ref[...] + acc_sc[...]
        if final_norm:
            y = _rms(y, gf_ref[...])
        o_ref[...] = y


def moe_swiglu(x, g, router, wg, wu, wd, gfinal, tm, final_norm):
    m, d = x.shape
    ne, _, fe = wg.shape
    chunk = MOE_CHUNK if tm % MOE_SUB == 0 else 0
    nsub = max(tm // MOE_SUB, 1)
    return pl.pallas_call(
        functools.partial(_moe_body, final_norm=final_norm, chunk=chunk),
        grid=(m // tm, ne),
        in_specs=[pl.BlockSpec((tm, d), lambda i, e: (i, 0)),
                  pl.BlockSpec((1, d), lambda i, e: (0, 0)),
                  pl.BlockSpec((3 * d, 128), lambda i, e: (0, 0)),
                  pl.BlockSpec((1, d, fe), lambda i, e: (e, 0, 0)),
                  pl.BlockSpec((1, d, fe), lambda i, e: (e, 0, 0)),
                  pl.BlockSpec((1, fe, d), lambda i, e: (e, 0, 0)),
                  pl.BlockSpec((1, d), lambda i, e: (0, 0))],
        out_specs=pl.BlockSpec((tm, d), lambda i, e: (i, 0)),
        out_shape=jax.ShapeDtypeStruct((m, d), F32),
        scratch_shapes=[pltpu.VMEM((tm, d), BF16), pltpu.VMEM((tm, 128), F32), pltpu.VMEM((tm, d), F32),
                        pltpu.VMEM((tm, 128), F32), pltpu.VMEM((nsub, 128, min(tm, MOE_SUB)), F32)],
        compiler_params=_cparams(("parallel", "arbitrary")),
    )(x, g, router, wg, wu, wd, gfinal)


RW_PAIRS = RW_HEADS // 2
RW_ROWS = RW_PAIRS * RW_HEAD


def _rwkv_body(p_ref, sh0_ref, s0_ref, mu_ref, w0_ref, a0_ref, kk_ref, ka_ref, rk_ref, lnw_ref, lnb_ref,
               wl_ref, g2_ref, b2_ref, pw_ref, y_ref, sfin_ref,
               s_sc, sk_sc, prev_sc, kk_sc, w_sc, b_sc, k_sc, r_sc, v_sc, g_sc, y_sc, *, nbb, tc, t_valid):
    c = pl.program_id(1)

    @pl.when(c == 0)
    def _():
        s_sc[...] = s0_ref[...]
        prev_sc[...] = sh0_ref[...]

    b2 = b2_ref[...]

    def head_sums(x, low=True):
        if low:
            hi, lo = _split2(x)
        else:
            hi = x.astype(BF16)
            lo = jnp.zeros_like(hi)
        return _dot(jnp.concatenate([hi, lo], axis=1), b2)

    def head_sums_wide(x):
        return jnp.concatenate([head_sums(x[:, q * 128:(q + 1) * 128]) for q in range(RW_PAIRS)], axis=1)

    for bi in range(nbb):
        p = p_ref[bi]
        row = lax.broadcasted_iota(jnp.int32, p.shape, 0)
        prev = jnp.where(row == 0, prev_sc[bi], pltpu.roll(p, 1, axis=0))
        prev_sc[bi] = p[tc - 1:tc, :]
        u = p + (prev - p) * mu_ref[...]
        r = u[:, 0:RW_DIM]
        k = u[:, RW_DIM:2 * RW_DIM]
        v = u[:, 2 * RW_DIM:3 * RW_DIM]
        lo = u[:, 3 * RW_DIM:3 * RW_DIM + RW_LORA]
        gd = u[:, 3 * RW_DIM + RW_LORA:]
        lane = lax.broadcasted_iota(jnp.int32, lo.shape, 1)
        lora = _dot(_cat3(jnp.where(lane < RW_LORA // 2, jnp.tanh(lo), lo), 1), wl_ref[...])
        wlog = -_softplus(-(w0_ref[...] + lora[:, :RW_DIM])) - 0.5
        a = _sigmoid(a0_ref[...] + lora[:, RW_DIM:])
        g_sc[bi] = _dot(_cat3(_sigmoid(gd), 1), g2_ref[...])
        kk = k * kk_ref[...]
        kk = kk / jnp.maximum(jnp.sqrt(head_sums_wide(kk * kk)), 1e-12)
        kk_sc[bi] = kk
        w_sc[bi] = jnp.exp(-jnp.exp(wlog))
        b_sc[bi] = kk * a
        k_sc[bi] = k * (1.0 + (a - 1.0) * ka_ref[...])
        r_sc[bi] = r
        v_sc[bi] = v

    rows = nbb * RW_ROWS
    vi = lax.broadcasted_iota(jnp.int32, (rows, 128), 0) % RW_HEAD
    li = lax.broadcasted_iota(jnp.int32, (rows, 128), 1)
    key_lane = li % RW_HEAD
    diag = (key_lane == vi).astype(F32)
    steps = 8 if t_valid % 8 == 0 else t_valid

    def rows_of(x8, j):
        return jnp.concatenate([jnp.broadcast_to(x8[bi][j:j + 1, q * 128:(q + 1) * 128], (RW_HEAD, 128))
                                for bi in range(nbb) for q in range(RW_PAIRS)], axis=0)

    pair_w = pw_ref[...]
    zero_half = jnp.zeros((rows, 128), BF16)

    def pair_sums(a, b):
        lhs = jnp.concatenate([a.astype(BF16), zero_half if b is None else b.astype(BF16)], axis=1)
        out = _dot(lhs, pair_w)
        return out[:, :128], out[:, 128:]

    first8 = [kk_sc[bi, pl.ds(0, 8), :] for bi in range(nbb)]
    sk_sc[...] = pair_sums(s_sc[...].reshape(rows, 128) * rows_of(first8, 0), None)[0]

    def group(t8, carry):
        t0 = pl.multiple_of(t8 * 8, 8)
        tn = pl.multiple_of(jnp.minimum(t0 + 8, tc - 8), 8)
        kk8, w8, b8, k8, r8, v8 = ([sc[bi, pl.ds(t0, 8), :] for bi in range(nbb)]
                                   for sc in (kk_sc, w_sc, b_sc, k_sc, r_sc, v_sc))
        kk_next = [kk_sc[bi, pl.ds(tn, 8), :] for bi in range(nbb)]
        s = s_sc[...].reshape(rows, 128)
        sk = sk_sc[...]
        ycols = jnp.zeros((rows, 128), F32)
        for j in range(steps):
            if j % 2 == 0:
                vcols = pair_sums(rows_of(v8, j) * diag, rows_of(v8, j + 1) * diag if j + 1 < steps else None)
            s = s * rows_of(w8, j) - sk * rows_of(b8, j) + vcols[j % 2] * rows_of(k8, j)
            kk_after = rows_of(kk8, j + 1) if j + 1 < 8 else rows_of(kk_next, 0)
            sk, yb = pair_sums(s * kk_after, s * rows_of(r8, j))
            ycols = jnp.where(key_lane == j, yb, ycols)
        s_sc[...] = s.reshape(nbb, RW_ROWS, 128)
        sk_sc[...] = sk
        h64 = RW_HEAD
        for bi in range(nbb):
            pieces = []
            for q in range(0, RW_PAIRS, 2):
                r0 = (bi * RW_PAIRS + q) * h64
                tr = ycols[r0:r0 + 2 * h64].T
                pieces += [tr[0:8, 0:h64], tr[h64:h64 + 8, 0:h64], tr[0:8, h64:], tr[h64:h64 + 8, h64:]]
            y_sc[bi, pl.ds(t0, 8), :] = jnp.concatenate(pieces, axis=1)
        return carry

    lax.fori_loop(0, -(-t_valid // 8), group, 0)

    inv = 1.0 / RW_HEAD
    for bi in range(nbb):
        y = y_sc[bi]
        yc = y - head_sums_wide(y) * inv
        var = head_sums_wide(yc * yc) * inv
        yn = yc * lax.rsqrt(var + RW_LN_EPS) * lnw_ref[...] + lnb_ref[...]
        bonus = head_sums_wide(r_sc[bi] * k_sc[bi] * rk_ref[...]) * v_sc[bi]
        y_ref[bi] = ((yn + bonus) * g_sc[bi]).astype(y_ref.dtype)

    @pl.when(c == pl.num_programs(1) - 1)
    def _():
        sfin_ref[...] = s_sc[...]


def _wkv_to_rows(s):
    b = s.shape[0]
    return s.reshape(b, RW_PAIRS, 2, RW_HEAD, RW_HEAD).transpose(0, 1, 3, 2, 4).reshape(b, RW_ROWS, 128)


def _rows_to_wkv(s):
    b = s.shape[0]
    return s.reshape(b, RW_PAIRS, RW_HEAD, 2, RW_HEAD).transpose(0, 1, 3, 2, 4).reshape(b, RW_HEADS, RW_HEAD, RW_HEAD)


def rwkv7(proj, shift0, wkv0, prm, nbb, tc, t_valid):
    b, t, _ = proj.shape
    vec = lambda n: pl.BlockSpec((1, n), lambda i, c: (0, 0))
    full = lambda a: pl.BlockSpec(a.shape, lambda i, c: (0,) * a.ndim)
    seq = pltpu.VMEM((nbb, tc, RW_DIM), F32)
    y, s_fin = pl.pallas_call(
        functools.partial(_rwkv_body, nbb=nbb, tc=tc, t_valid=t_valid),
        grid=(b // nbb, t // tc),
        in_specs=[pl.BlockSpec((nbb, tc, RW_COLS), lambda i, c: (i, c, 0)),
                  pl.BlockSpec((nbb, 1, RW_COLS), lambda i, c: (i, 0, 0)),
                  pl.BlockSpec((nbb, RW_ROWS, 128), lambda i, c: (i, 0, 0)),
                  vec(RW_COLS), vec(RW_DIM), vec(RW_DIM), vec(RW_DIM), vec(RW_DIM), vec(RW_DIM), vec(RW_DIM),
                  vec(RW_DIM), full(prm["wl"]), full(prm["g2"]), full(prm["bones2"]), full(prm["pair_w"])],
        out_specs=[pl.BlockSpec((nbb, tc, RW_DIM), lambda i, c: (i, c, 0)),
                   pl.BlockSpec((nbb, RW_ROWS, 128), lambda i, c: (i, 0, 0))],
        out_shape=[jax.ShapeDtypeStruct((b, t, RW_DIM), BF16),
                   jax.ShapeDtypeStruct((b, RW_ROWS, 128), F32)],
        scratch_shapes=[pltpu.VMEM((nbb, RW_ROWS, 128), F32), pltpu.VMEM((nbb * RW_ROWS, 128), F32),
                        pltpu.VMEM((nbb, 1, RW_COLS), F32),
                        seq, seq, seq, seq, seq, seq, seq, seq],
        compiler_params=_cparams(("parallel", "arbitrary")),
    )(proj, shift0, _wkv_to_rows(wkv0), prm["mu"], prm["w0"], prm["a0"], prm["k_k"], prm["k_a"], prm["r_k"],
      prm["ln_w"], prm["ln_b"], prm["wl"], prm["g2"], prm["bones2"], prm["pair_w"])
    return y, _rows_to_wkv(s_fin)


def _ssd_body(z_ref, xbc_ref, dt_ref, conv0_ref, h0_ref, cw_ref, cb_ref, dtb_ref, alog_ref, dsk_ref, nw_ref,
              e_ref, y_ref, hfin_ref, h_sc, ext_sc, *, t_valid):
    c = pl.program_id(1)
    ln = SSD_CHUNK

    @pl.when(c == 0)
    def _():
        h_sc[...] = h0_ref[0]
        ext_sc[0:8, :] = conv0_ref[0]

    xbc = xbc_ref[0]
    ext_sc[8:8 + ln, :] = xbc
    cw = cw_ref[...]
    conv = cb_ref[...] + ext_sc[5:5 + ln, :] * cw[0:1]
    conv = conv + ext_sc[6:6 + ln, :] * cw[1:2]
    conv = conv + ext_sc[7:7 + ln, :] * cw[2:3]
    conv = conv + xbc * cw[3:4]
    ext_sc[0:8, :] = xbc[ln - 8:ln, :]
    act = _silu(conv)
    xs = act[:, :SSM_INNER]
    bm = act[:, SSM_INNER:SSM_INNER + SSM_GROUPS * SSM_STATE].astype(BF16)
    cm = act[:, SSM_INNER + SSM_GROUPS * SSM_STATE:].astype(BF16)

    lane = lax.broadcasted_iota(jnp.int32, (ln, 128), 1)
    rowi = lax.broadcasted_iota(jnp.int32, (ln, 128), 0)
    dt = _softplus(dt_ref[0] + dtb_ref[...])
    if t_valid < ln:
        dt = jnp.where(rowi < t_valid, dt, 0.0)
    a = dt * jnp.where(lane[0:1] < SSM_HEADS, -jnp.exp(alog_ref[...]), 0.0)
    causal = lane <= rowi
    tri = jnp.where(causal, 1.0, 0.0).astype(BF16)
    tri3 = jnp.concatenate([tri, tri, tri], axis=1)
    acum = _dot(tri3, jnp.concatenate(_split3(a), axis=0))
    acum_t = _nt(jnp.concatenate(_split3(a.T), axis=1), tri3)
    e3 = e_ref[...]
    dt_full = _dot(jnp.concatenate(_split3(dt), axis=1), e3)
    ac_full = _dot(jnp.concatenate(_split3(acum), axis=1), e3)
    xdt = xs * dt_full
    xw = xdt * jnp.exp(ac_full[ln - 1:ln, :] - ac_full)
    eac = jnp.exp(ac_full)
    cd = jnp.broadcast_to(jnp.exp(acum_t[:, ln - 1:ln]), (128, 128))
    cd_rows = jnp.concatenate([jnp.broadcast_to(cd[h:h + 1, :], (SSM_HEAD, SSM_STATE)) for h in range(SSM_HEADS)],
                              axis=0)

    ys = []
    for g in range(SSM_GROUPS):
        bm_g = bm[:, g * SSM_STATE:(g + 1) * SSM_STATE]
        cm_g = cm[:, g * SSM_STATE:(g + 1) * SSM_STATE]
        cbm = _nt(cm_g, bm_g)
        pairs = SSM_HEADS // SSM_GROUPS // 2
        for q in range(g * pairs, (g + 1) * pairs):
            sl = slice(q * 128, (q + 1) * 128)
            xdt_p = xdt[:, sl].astype(BF16)
            yd = []
            for h in (2 * q, 2 * q + 1):
                seg = jnp.where(causal, acum[:, h:h + 1] - acum_t[h:h + 1, :], NEG_INF)
                yd.append(_dot((cbm * jnp.exp(seg)).astype(BF16), xdt_p))
            hp = h_sc[sl, :]
            y_off = _nt(cm_g, hp.astype(BF16)) * eac[:, sl]
            st = _dot(xw[:, sl].T.astype(BF16), bm_g)
            h_sc[sl, :] = cd_rows[sl, :] * hp + st
            ys.append(jnp.where(lane < SSM_HEAD, yd[0], yd[1]) + y_off)
    y = jnp.concatenate(ys, axis=1) + dsk_ref[...] * xs
    y = y * _silu(z_ref[0])
    gw = SSM_INNER // SSM_GROUPS
    outs = []
    for g in range(SSM_GROUPS):
        yg = y[:, g * gw:(g + 1) * gw]
        outs.append(yg * lax.rsqrt(jnp.mean(yg * yg, axis=1, keepdims=True) + NORM_EPS))
    y_ref[0] = (jnp.concatenate(outs, axis=1) * nw_ref[...]).astype(y_ref.dtype)

    @pl.when(c == pl.num_programs(1) - 1)
    def _():
        hfin_ref[0] = h_sc[...]


def mamba2(proj, conv0, ssm0, prm, t_valid):
    b, t, _ = proj.shape
    ln = SSD_CHUNK
    vec = lambda n: pl.BlockSpec((1, n), lambda i, c: (0, 0))
    full = lambda a: pl.BlockSpec(a.shape, lambda i, c: (0,) * a.ndim)
    return pl.pallas_call(
        functools.partial(_ssd_body, t_valid=t_valid),
        grid=(b, t // ln),
        in_specs=[pl.BlockSpec((1, ln, SSM_INNER), lambda i, c: (i, c, PROJ_Z // SSM_INNER)),
                  pl.BlockSpec((1, ln, SSM_CONV_DIM), lambda i, c: (i, c, PROJ_XBC // SSM_CONV_DIM)),
                  pl.BlockSpec((1, ln, 128), lambda i, c: (i, c, PROJ_DT // 128)),
                  pl.BlockSpec((1, 8, SSM_CONV_DIM), lambda i, c: (i, 0, 0)),
                  pl.BlockSpec((1, SSM_INNER, SSM_STATE), lambda i, c: (i, 0, 0)),
                  full(prm["conv_w"]), vec(SSM_CONV_DIM), vec(128), vec(128), vec(SSM_INNER), vec(SSM_INNER),
                  full(prm["expand"])],
        out_specs=[pl.BlockSpec((1, ln, SSM_INNER), lambda i, c: (i, c, 0)),
                   pl.BlockSpec((1, SSM_INNER, SSM_STATE), lambda i, c: (i, 0, 0))],
        out_shape=[jax.ShapeDtypeStruct((b, t, SSM_INNER), BF16),
                   jax.ShapeDtypeStruct((b, SSM_INNER, SSM_STATE), F32)],
        scratch_shapes=[pltpu.VMEM((SSM_INNER, SSM_STATE), F32), pltpu.VMEM((8 + ln, SSM_CONV_DIM), F32)],
        compiler_params=_cparams(("parallel", "arbitrary")),
    )(proj, proj, proj, conv0, ssm0, prm["conv_w"], prm["conv_b"], prm["dt_bias"], prm["a_log"],
      prm["d_full"], prm["norm_w"], prm["expand"])


def _alibi_slopes():
    return (2.0 ** (-8.0 * np.arange(1, ATT_HEADS + 1) / ATT_HEADS)).astype(np.float32)


def _slope_features():
    s = _alibi_slopes()
    bf = lambda a: a.astype(jnp.bfloat16).astype(np.float32)
    hi = bf(s)
    mid = bf(s - hi)
    lo = bf(s - hi - mid)
    out = np.zeros((ATT_HEADS, FEAT_BLOCKS), np.float32)
    for i, piece in enumerate((hi, mid, lo)):
        out[:, i] = piece
        out[:, SLOPE_PARTS + i] = piece
    return out


def _top3_threshold(gate):
    v = gate
    for _ in range(MOBA_TOPK - 1):
        v = jnp.where(v == jnp.max(v, axis=1, keepdims=True), NEG_INF, v)
    return jnp.max(v, axis=1, keepdims=True)


def _qkv_body(x_ref, g_ref, w_ref, qkv_ref, kf_ref, vf_ref, mean_ref, *, tiles_per_seq):
    tm = x_ref.shape[0]
    qkv = _dot(_rms(x_ref[...], g_ref[...]).astype(BF16), w_ref[...])
    qkv_ref[...] = qkv
    for r in range(tm // MOBA_BLOCK):
        mean_ref[r] = jnp.mean(qkv[r * MOBA_BLOCK:(r + 1) * MOBA_BLOCK, ATT_Q_COLS:ATT_Q_COLS + ATT_KV_COLS],
                               axis=0, keepdims=True)
    pos = (pl.program_id(0) % tiles_per_seq) * tm + lax.broadcasted_iota(jnp.int32, (tm, ATT_HEAD), 0)
    lane = lax.broadcasted_iota(jnp.int32, (tm, ATT_HEAD), 1)
    blk_id, off = pos // MOBA_BLOCK, pos % MOBA_BLOCK
    kfeat = jnp.where(lane == blk_id, 1, 0)
    kfeat = jnp.where((lane >= FEAT_START) & (lane < FEAT_START + SLOPE_PARTS), blk_id * MOBA_BLOCK, kfeat)
    kfeat = jnp.where((lane >= FEAT_OFFSET) & (lane < FEAT_OFFSET + SLOPE_PARTS), off, kfeat)
    kfeat = kfeat.astype(F32).astype(BF16)
    vfeat = jnp.where(lane == 0, 1.0, 0.0).astype(BF16)
    for g in range(ATT_KV_HEADS):
        k0 = ATT_Q_COLS + g * ATT_HEAD
        v0 = ATT_Q_COLS + ATT_KV_COLS + g * ATT_HEAD
        kf_ref[0, g] = jnp.concatenate([qkv[:, k0:k0 + ATT_HEAD].astype(BF16), kfeat], axis=1)
        vf_ref[0, g] = jnp.concatenate([qkv[:, v0:v0 + ATT_HEAD].astype(BF16), vfeat], axis=1)


def qkv_proj(x, g, w, b, t, tm):
    m, d = x.shape
    n = w.shape[1]
    tps = t // tm
    ext = jax.ShapeDtypeStruct((b, ATT_KV_HEADS, t, 128), BF16)
    ext_spec = pl.BlockSpec((1, ATT_KV_HEADS, tm, 128), lambda i: (i // tps, 0, i % tps, 0))
    return pl.pallas_call(
        functools.partial(_qkv_body, tiles_per_seq=tps),
        grid=(m // tm,),
        in_specs=[pl.BlockSpec((tm, d), lambda i: (i, 0)), pl.BlockSpec((1, d), lambda i: (0, 0)),
                  pl.BlockSpec((d, n), lambda i: (0, 0))],
        out_specs=[pl.BlockSpec((tm, n), lambda i: (i, 0)), ext_spec, ext_spec,
                   pl.BlockSpec((tm // MOBA_BLOCK, 1, ATT_KV_COLS), lambda i: (i, 0, 0))],
        out_shape=[jax.ShapeDtypeStruct((m, n), F32), ext, ext,
                   jax.ShapeDtypeStruct((m // MOBA_BLOCK, 1, ATT_KV_COLS), F32)],
        compiler_params=_cparams(("parallel",)),
    )(x, g, w)


def _moba_prompt_body(q_ref, kf_ref, vf_ref, mean_ref, slf_ref, o_ref, qf_sc, m_sc, acc_sc, sa0_sc, sa1_sc, sb0_sc, sb1_sc):
    qb = pl.program_id(2)
    blk = MOBA_BLOCK
    rows = ATT_GROUP * blk
    scale = ATT_HEAD ** -0.5

    q4 = q_ref[0]
    qg = jnp.concatenate([q4[:, hh * ATT_HEAD:(hh + 1) * ATT_HEAD] for hh in range(ATT_GROUP)], axis=0)
    past_blk = lax.broadcasted_iota(jnp.int32, (128, rows), 0) < qb
    gate = jnp.where(past_blk, _nt(_cat3w(mean_ref[0, 0], 1), _cat3(qg, 1)), NEG_INF)
    v = gate
    for _ in range(MOBA_TOPK - 1):
        v = jnp.where(v == jnp.max(v, axis=0, keepdims=True), NEG_INF, v)
    thr = jnp.max(v, axis=0, keepdims=True)
    selneg = jnp.where(past_blk & (gate < thr), MASK_NEG, 0.0).T[:, :FEAT_BLOCKS]
    slf = jnp.concatenate([jnp.broadcast_to(slf_ref[0, hh:hh + 1, :], (blk, FEAT_BLOCKS))
                           for hh in range(ATT_GROUP)], axis=0)
    qf_sc[...] = jnp.concatenate([(qg * scale).astype(BF16), selneg.astype(BF16), slf.astype(BF16)], axis=1)

    def kv_tiles(j):
        start = pl.multiple_of(j * blk, blk)
        return kf_ref[0, 0, pl.ds(start, blk), :], vf_ref[0, 0, pl.ds(start, blk), :]

    qi = lax.broadcasted_iota(jnp.int32, (rows, blk), 0) % blk
    ki = lax.broadcasted_iota(jnp.int32, (rows, blk), 1)
    kf, vf = kv_tiles(qb)
    s = jnp.where(ki <= qi, _nt(qf_sc[...], kf), NEG_INF)
    m = jnp.max(s, axis=1, keepdims=True)
    m_sc[...] = jnp.broadcast_to(m, (rows, 128))
    acc_sc[...] = _dot(jnp.exp(s - m).astype(BF16), vf)

    def scores(j):
        return _nt(qf_sc[...], kv_tiles(jnp.minimum(j, qb))[0])

    def fold(blocks, s_cur, s_next=()):
        for ref, j in zip(s_next, (blocks[-1] + 1, blocks[-1] + 2)):
            ref[...] = scores(j)
        ss = [ref[...] for ref in s_cur]
        m_old = m_sc[...]
        m_new = m_old
        for s in ss:
            m_new = jnp.maximum(m_new, jnp.max(s, axis=1, keepdims=True))
        m2 = jnp.concatenate([m_new, m_new], axis=1)
        acc = acc_sc[...] * jnp.exp(m_old - m_new)
        for s, j in zip(ss, blocks):
            acc = acc + _dot(jnp.exp(s - m2).astype(BF16), kv_tiles(j)[1])
        acc_sc[...] = acc
        m_sc[...] = m_new

    set_a, set_b = (sa0_sc, sa1_sc), (sb0_sc, sb1_sc)
    sa0_sc[...] = scores(0)
    sa1_sc[...] = scores(1)

    def quad(i, carry):
        fold((4 * i, 4 * i + 1), set_a, set_b)
        fold((4 * i + 2, 4 * i + 3), set_b, set_a)
        return carry

    lax.fori_loop(0, qb // 4, quad, 0)
    done = qb // 4 * 4

    @pl.when(qb % 4 >= 2)
    def _():
        fold((done, done + 1), set_a)

    @pl.when(qb % 4 == 1)
    def _():
        fold((done,), set_a[:1])

    @pl.when(qb % 4 == 3)
    def _():
        sb0_sc[...] = scores(done + 2)
        fold((done + 2,), set_b[:1])

    acc = acc_sc[...]
    out = acc[:, :ATT_HEAD] / acc[:, ATT_HEAD:ATT_HEAD + 1]
    o_ref[0] = jnp.concatenate([out[hh * blk:(hh + 1) * blk] for hh in range(ATT_GROUP)], axis=1)


def moba_prompt(qkv, kf, vf, means, slf):
    b, t, _ = qkv.shape
    nb = t // MOBA_BLOCK
    assert nb <= 32
    nbl = 128
    means = jnp.pad(means, ((0, 0), (0, 0), (0, nbl - nb), (0, 0)))
    rows = ATT_GROUP * MOBA_BLOCK
    gw = ATT_GROUP * ATT_HEAD
    return pl.pallas_call(
        _moba_prompt_body,
        grid=(b, ATT_KV_HEADS, nb),
        in_specs=[pl.BlockSpec((1, MOBA_BLOCK, gw), lambda i, g, j: (i, j, g)),
                  pl.BlockSpec((1, 1, t, 128), lambda i, g, j: (i, g, 0, 0)),
                  pl.BlockSpec((1, 1, t, 128), lambda i, g, j: (i, g, 0, 0)),
                  pl.BlockSpec((1, 1, nbl, ATT_HEAD), lambda i, g, j: (i, g, 0, 0)),
                  pl.BlockSpec((1, ATT_GROUP, FEAT_BLOCKS), lambda i, g, j: (g, 0, 0))],
        out_specs=pl.BlockSpec((1, MOBA_BLOCK, gw), lambda i, g, j: (i, j, g)),
        out_shape=jax.ShapeDtypeStruct((b, t, ATT_Q_COLS), F32),
        scratch_shapes=[pltpu.VMEM((rows, 128), BF16), pltpu.VMEM((rows, 128), F32), pltpu.VMEM((rows, 128), F32),
                        *([pltpu.VMEM((rows, MOBA_BLOCK), F32)] * 4)],
        compiler_params=_cparams(("parallel", "parallel", "arbitrary")),
    )(qkv, kf, vf, means, slf)


def _moba_sample_body(pt_ref, *refs, nbp, bps, nq):
    npg = 2 * bps
    k_refs, v_refs = refs[:npg], refs[npg:2 * npg]
    q_ref, kn_ref, vn_ref, slope_ref, o_ref, mean_sc, m_sc, l_sc, acc_sc = refs[2 * npg:]
    j = pl.program_id(1)
    blk = MOBA_BLOCK
    rows = ATT_HEADS * nq
    grows = ATT_GROUP * nq
    past = nbp * blk
    scale = ATT_HEAD ** -0.5
    lane = lax.broadcasted_iota(jnp.int32, (rows, 128), 1)
    qi = (lax.broadcasted_iota(jnp.int32, (rows, 1), 0) % nq).astype(F32)
    qf = q_ref[0]
    qs = (qf * scale).astype(BF16)
    slope = slope_ref[...]

    @pl.when(j == 0)
    def _():
        mean_sc[...] = jnp.zeros_like(mean_sc)
        m_sc[...] = jnp.full(m_sc.shape, NEG_INF, F32)
        l_sc[...] = jnp.zeros_like(l_sc)

    kcat = jnp.concatenate([ref[0] for ref in k_refs], axis=0)
    off = lax.broadcasted_iota(jnp.int32, (1, bps * blk), 1).astype(F32)
    sc = _nt(qs, kcat.astype(BF16)) - slope * ((past - j * bps * blk).astype(F32) + qi - off)
    m_new, l_new = m_sc[...], l_sc[...]
    for s in range(bps):
        bi = j * bps + s
        ksum = jnp.sum(kcat[s * blk:(s + 1) * blk], axis=0, keepdims=True)
        mean_sc[pl.ds(bi, 1), :] = ksum * (1.0 / blk)
        sc_b = sc[:, s * blk:(s + 1) * blk]
        m = jnp.max(sc_b, axis=1, keepdims=True)
        p = jnp.exp(sc_b - m)
        m_new = jnp.where(lane == bi, m, m_new)
        l_new = jnp.where(lane == bi, jnp.sum(p, axis=1, keepdims=True), l_new)
        vblk = jnp.concatenate([v_refs[2 * s][0], v_refs[2 * s + 1][0]], axis=0)
        acc_sc[bi] = _dot(p.astype(BF16), vblk.astype(BF16))
    m_sc[...] = m_new
    l_sc[...] = l_new

    @pl.when(j == pl.num_programs(1) - 1)
    def _():
        gate = jnp.where(lane < nbp, _nt(_cat3(qf, 1), _cat3w(mean_sc[...], 1)), NEG_INF)
        sel = gate >= _top3_threshold(gate)
        ki = lax.broadcasted_iota(jnp.int32, (rows, 8), 1).astype(F32)
        s_own = jnp.where(ki <= qi, _nt(qs, kn_ref[0].astype(BF16)) - slope * (qi - ki), NEG_INF)
        m_all = jnp.where(sel, m_sc[...], NEG_INF)
        mx = jnp.maximum(jnp.max(m_all, axis=1, keepdims=True), jnp.max(s_own, axis=1, keepdims=True))
        w = jnp.where(sel, jnp.exp(m_all - mx), 0.0)
        p_own = jnp.exp(s_own - mx)
        den = jnp.sum(w * l_sc[...], axis=1, keepdims=True) + jnp.sum(p_own, axis=1, keepdims=True)
        acc = _dot(p_own.astype(BF16), vn_ref[0].astype(BF16))
        for n in range(nbp):
            acc = acc + w[:, n:n + 1] * acc_sc[n]
        out = acc / den
        for g in range(ATT_KV_HEADS):
            o_ref[0, g] = out[g * grows:(g + 1) * grows, g * ATT_HEAD:(g + 1) * ATT_HEAD]


SAMPLE_BLOCKS_PER_STEP = 8


def _pages2d(cache):
    return cache.reshape(cache.shape[0], PAGE_SIZE, ATT_KV_COLS)


def _block_diag_queries(q):
    db, nq, _ = q.shape
    qh = q.reshape(db, nq, ATT_KV_HEADS, ATT_GROUP, ATT_HEAD).transpose(0, 2, 3, 1, 4)
    eye = jnp.eye(ATT_KV_HEADS, dtype=q.dtype)
    return (qh[:, :, :, :, None, :] * eye[None, :, None, None, :, None]).reshape(db, ATT_HEADS * nq, ATT_KV_COLS)


def _new_rows(a):
    db, nq = a.shape[:2]
    return jnp.pad(a.reshape(db, nq, ATT_KV_COLS), ((0, 0), (0, 8 - nq), (0, 0)))


def moba_sample(page_table, cache_k, cache_v, q_bd, k_new, v_new, slope_rows, nq):
    db, n_pages = page_table.shape
    nbp = n_pages * PAGE_SIZE // MOBA_BLOCK
    bps = min(SAMPLE_BLOCKS_PER_STEP, nbp)
    rows = ATT_HEADS * nq
    grows = ATT_GROUP * nq

    def page_spec(slot):
        return pl.BlockSpec((1, PAGE_SIZE, ATT_KV_COLS),
                            lambda b, j, pt: (pt[b * n_pages + j * 2 * bps + slot], 0, 0))

    small = lambda shp: pl.BlockSpec((1,) + shp, lambda b, j, pt: (b,) + (0,) * len(shp))
    grid_spec = pltpu.PrefetchScalarGridSpec(
        num_scalar_prefetch=1,
        grid=(db, nbp // bps),
        in_specs=[page_spec(s) for s in range(2 * bps)] * 2
        + [small((rows, ATT_KV_COLS)), small((8, ATT_KV_COLS)), small((8, ATT_KV_COLS)),
           pl.BlockSpec((rows, 1), lambda b, j, pt: (0, 0))],
        out_specs=small((ATT_KV_HEADS, grows, ATT_HEAD)),
        scratch_shapes=[pltpu.VMEM((128, ATT_KV_COLS), F32), pltpu.VMEM((rows, 128), F32),
                        pltpu.VMEM((rows, 128), F32), pltpu.VMEM((nbp, rows, ATT_KV_COLS), F32)],
    )
    return pl.pallas_call(
        functools.partial(_moba_sample_body, nbp=nbp, bps=bps, nq=nq),
        grid_spec=grid_spec,
        out_shape=jax.ShapeDtypeStruct((db, ATT_KV_HEADS, grows, ATT_HEAD), F32),
        compiler_params=_cparams(("parallel", "arbitrary")),
    )(page_table.reshape(-1), *([cache_k] * (2 * bps)), *([cache_v] * (2 * bps)), q_bd, k_new, v_new, slope_rows)


def _row(a):
    return a.reshape(1, -1).astype(F32)


def _even_params(i, w_in_ab, rwkv_mu, rwkv_w0, rwkv_w2, rwkv_a0, rwkv_a2, rwkv_g2, rwkv_k_k, rwkv_k_a, rwkv_r_k,
                 rwkv_ln_w, rwkv_ln_b, ssm_conv_w, ssm_conv_b, ssm_dt_bias, ssm_a_log, ssm_d, ssm_norm_w):
    w = w_in_ab[i]
    ssm0 = RW_COLS
    zeros = lambda n: jnp.zeros((D_MODEL, n), w.dtype)
    w_pack = jnp.concatenate([
        w[:, :RW_COLS],
        w[:, ssm0 + SSM_INNER + SSM_CONV_DIM:], zeros(PROJ_Z - PROJ_DT - SSM_HEADS),
        w[:, ssm0:ssm0 + SSM_INNER],
        w[:, ssm0 + SSM_INNER:ssm0 + SSM_INNER + SSM_CONV_DIM]], axis=1).astype(BF16)
    half = RW_LORA // 2
    zl = jnp.zeros((half, RW_DIM), F32)
    wl = jnp.concatenate([jnp.concatenate([rwkv_w2[i], zl], axis=1),
                          jnp.concatenate([zl, rwkv_a2[i]], axis=1)], axis=0)
    half_of = np.arange(128) // RW_HEAD
    bones2 = np.tile((half_of[:, None] == half_of[None, :]).astype(np.float32), (2, 1))
    rw = dict(mu=_row(rwkv_mu[i]), w0=_row(rwkv_w0[i]), a0=_row(rwkv_a0[i]), k_k=_row(rwkv_k_k[i]),
              k_a=_row(rwkv_k_a[i]), r_k=_row(rwkv_r_k[i]), ln_w=_row(rwkv_ln_w[i]), ln_b=_row(rwkv_ln_b[i]),
              wl=_cat3w(wl, 0), g2=_cat3w(rwkv_g2[i], 0), bones2=jnp.asarray(bones2, BF16),
              pair_w=jnp.asarray(np.kron(np.eye(4, dtype=np.float32), np.ones((RW_HEAD, RW_HEAD), np.float32)), BF16))
    pad128 = lambda a: jnp.pad(_row(a), ((0, 0), (0, 128 - a.shape[-1])))
    expand = (np.arange(128)[:, None] == (np.arange(SSM_INNER) // SSM_HEAD)[None, :]).astype(np.float32)
    ssm = dict(conv_w=ssm_conv_w[i], conv_b=_row(ssm_conv_b[i]), dt_bias=pad128(ssm_dt_bias[i]),
               a_log=pad128(ssm_a_log[i]), d_full=_row(jnp.repeat(ssm_d[i], SSM_HEAD)), norm_w=_row(ssm_norm_w[i]),
               expand=jnp.asarray(np.tile(expand, (3, 1)), BF16))
    return w_pack, rw, ssm


def _mixer_ab(x, b, t, norm_g, w_pack, rw, ssm, w_out, shift0, wkv0, conv0, ssm0, tm):
    tm_in = 2 * tm if (b * t) % (2 * tm) == 0 else tm
    proj = norm_matmul(x, norm_g, w_pack, tm_in, SSM_CONV_DIM).reshape(b, t, PROJ_COLS)
    shift_new = proj[:, t - 1, :RW_COLS]
    conv_new = proj[:, t - (SSM_CONV - 1):, PROJ_XBC:]
    if t % SSD_CHUNK == 0:
        proj_rw, proj_ssm, tc = proj, proj, SSD_CHUNK
    else:
        proj_rw = jnp.pad(proj, ((0, 0), (0, 8 - t), (0, 0)))
        proj_ssm = jnp.pad(proj, ((0, 0), (0, SSD_CHUNK - t), (0, 0)))
        tc = 8
    nbb = next(n for n in (4, 2, 1) if b % n == 0)
    y_rw, wkv_new = rwkv7(proj_rw, shift0.reshape(b, 1, RW_COLS), wkv0, rw, nbb, tc, min(t, tc))
    conv0p = jnp.pad(conv0, ((0, 0), (8 - (SSM_CONV - 1), 0), (0, 0)))
    y_ssm, ssm_new = mamba2(proj_ssm, conv0p, ssm0.reshape(b, SSM_INNER, SSM_STATE), ssm, min(t, SSD_CHUNK))
    mixes = [y_rw[:, :t].reshape(b * t, RW_DIM), y_ssm[:, :t].reshape(b * t, SSM_INNER)]
    return (mixes, [w_out[:RW_DIM], w_out[RW_DIM:]],
            (shift_new, wkv_new, conv_new, ssm_new.reshape(b, SSM_HEADS, SSM_HEAD, SSM_STATE)))


def kernel(x_prompt, x_sample, state_rwkv_shift, state_rwkv_wkv, state_ssm_conv, state_ssm, cache_k, cache_v, page_table, norm_mix, norm_ffn, norm_final, w_in_ab, rwkv_mu, rwkv_w0, rwkv_w2, rwkv_a0, rwkv_a2, rwkv_g2, rwkv_k_k, rwkv_k_a, rwkv_r_k, rwkv_ln_w, rwkv_ln_b, ssm_conv_w, ssm_conv_b, ssm_dt_bias, ssm_a_log, ssm_d, ssm_norm_w, w_out_ab, ffn_w_gate, ffn_w_up, ffn_w_down, attn_w_qkv, attn_w_o, moe_router, moe_w_gate, moe_w_up, moe_w_down):
    bp, tp, _ = x_prompt.shape
    db, ts, _ = x_sample.shape
    depth = norm_mix.shape[0]
    tm_p, tm_s = 512, db * ts
    xp = x_prompt.reshape(bp * tp, D_MODEL)
    xs = x_sample.reshape(db * ts, D_MODEL)
    slopes = _alibi_slopes()
    slf = jnp.asarray(_slope_features().reshape(ATT_KV_HEADS, ATT_GROUP, FEAT_BLOCKS))
    slope_rows = jnp.asarray(np.repeat(slopes, ts).reshape(ATT_HEADS * ts, 1))
    st = {n: [] for n in ("p_shift", "p_wkv", "p_conv", "p_ssm", "p_k", "p_v",
                          "s_shift", "s_wkv", "s_conv", "s_ssm", "s_k", "s_v")}
    for l in range(depth):
        i = l // 2
        g_mix, g_ffn = _row(norm_mix[l]), _row(norm_ffn[l])
        if l % 2 == 0:
            w_pack, rw, ssm = _even_params(i, w_in_ab, rwkv_mu, rwkv_w0, rwkv_w2, rwkv_a0, rwkv_a2, rwkv_g2,
                                           rwkv_k_k, rwkv_k_a, rwkv_r_k, rwkv_ln_w, rwkv_ln_b, ssm_conv_w,
                                           ssm_conv_b, ssm_dt_bias, ssm_a_log, ssm_d, ssm_norm_w)
            w_out = w_out_ab[i].astype(BF16)
            mix_p, wmix, sp = _mixer_ab(xp, bp, tp, g_mix, w_pack, rw, ssm, w_out,
                                        jnp.zeros((bp, RW_COLS), F32),
                                        jnp.zeros((bp, RW_HEADS, RW_HEAD, RW_HEAD), F32),
                                        jnp.zeros((bp, SSM_CONV - 1, SSM_CONV_DIM), F32),
                                        jnp.zeros((bp, SSM_HEADS, SSM_HEAD, SSM_STATE), F32), tm_p)
            mix_s, _, ss = _mixer_ab(xs, db, ts, g_mix, w_pack, rw, ssm, w_out, state_rwkv_shift[i],
                                     state_rwkv_wkv[i], state_ssm_conv[i], state_ssm[i], tm_s)
            for pre, new in (("p", sp), ("s", ss)):
                for name, val in zip(("shift", "wkv", "conv", "ssm"), new):
                    st[f"{pre}_{name}"].append(val)
            wg, wu, wd = ffn_w_gate[i].astype(BF16), ffn_w_up[i].astype(BF16), ffn_w_down[i].astype(BF16)
            xp = ffn_swiglu(xp, mix_p, wmix, g_ffn, wg, wu, wd, 2 * tm_p, 256)
            xs = ffn_swiglu(xs, mix_s, wmix, g_ffn, wg, wu, wd, tm_s, 256)
        else:
            w_qkv, w_o = attn_w_qkv[i].astype(BF16), attn_w_o[i].astype(BF16)
            qkv_p, kf, vf, means = qkv_proj(xp, g_mix, w_qkv, bp, tp, tm_p)
            nb = tp // MOBA_BLOCK
            means = means.reshape(bp, nb, ATT_KV_HEADS, ATT_HEAD).transpose(0, 2, 1, 3)
            k_p = qkv_p[:, ATT_Q_COLS:ATT_Q_COLS + ATT_KV_COLS].reshape(bp, tp, ATT_KV_HEADS, ATT_HEAD)
            v_p = qkv_p[:, ATT_Q_COLS + ATT_KV_COLS:].reshape(bp, tp, ATT_KV_HEADS, ATT_HEAD)
            o_p = moba_prompt(qkv_p.reshape(bp, tp, -1), kf, vf, means, slf)
            xp = matmul_residual(xp, [o_p.reshape(bp * tp, ATT_Q_COLS)], [w_o], tm_p)
            st["p_k"].append(k_p)
            st["p_v"].append(v_p)

            qkv_s = norm_matmul(xs, g_mix, w_qkv, tm_s, 512).reshape(db, ts, -1)
            kn = qkv_s[..., ATT_Q_COLS:ATT_Q_COLS + ATT_KV_COLS].reshape(db, ts, ATT_KV_HEADS, ATT_HEAD)
            vn = qkv_s[..., ATT_Q_COLS + ATT_KV_COLS:].reshape(db, ts, ATT_KV_HEADS, ATT_HEAD)
            o_s = moba_sample(page_table, _pages2d(cache_k[i]), _pages2d(cache_v[i]),
                              _block_diag_queries(qkv_s[..., :ATT_Q_COLS]), _new_rows(kn), _new_rows(vn),
                              slope_rows, ts)
            o_s = o_s.reshape(db, ATT_HEADS, ts, ATT_HEAD).transpose(0, 2, 1, 3).reshape(db * ts, ATT_Q_COLS)
            xs = matmul_residual(xs, [o_s], [w_o], tm_s)
            st["s_k"].append(kn)
            st["s_v"].append(vn)

            router = _cat3w(jnp.pad(moe_router[i], ((0, 0), (0, 128 - N_EXPERTS))), 0)
            wg, wu, wd = moe_w_gate[i].astype(BF16), moe_w_up[i].astype(BF16), moe_w_down[i].astype(BF16)
            final = l == depth - 1
            xp = moe_swiglu(xp, g_ffn, router, wg, wu, wd, _row(norm_final), 2 * tm_p, final)
            xs = moe_swiglu(xs, g_ffn, router, wg, wu, wd, _row(norm_final), tm_s, final)
    y_prompt = xp.reshape(bp, tp, D_MODEL)
    y_sample = xs.reshape(db, ts, D_MODEL)
    stack = lambda n: jnp.stack(st[n])
    return (y_prompt, y_sample,
            stack("p_shift"), stack("p_wkv"), stack("p_conv"), stack("p_ssm"), stack("p_k"), stack("p_v"),
            stack("s_shift"), stack("s_wkv"), stack("s_conv"), stack("s_ssm"), stack("s_k"), stack("s_v"))
```

```python
import functools

import numpy as np
import jax
import jax.numpy as jnp
from jax import lax
from jax.experimental import pallas as pl
from jax.experimental.pallas import tpu as pltpu

F32 = jnp.float32
BF16 = jnp.bfloat16
NEG_INF = float("-inf")

D_MODEL = 1024
NORM_EPS = 1e-6

RW_HEAD = 64
RW_HEADS = 8
RW_DIM = RW_HEADS * RW_HEAD
RW_LORA = 128
RW_GATE = 128
RW_COLS = 3 * RW_DIM + RW_LORA + RW_GATE
RW_LN_EPS = 64e-5

SSM_INNER = 1024
SSM_HEAD = 64
SSM_HEADS = 16
SSM_GROUPS = 2
SSM_STATE = 128
SSM_CONV = 4
SSM_CONV_DIM = SSM_INNER + 2 * SSM_GROUPS * SSM_STATE
SSD_CHUNK = 128

PROJ_RW = 0
PROJ_DT = RW_COLS
PROJ_Z = 2048
PROJ_XBC = 3072
PROJ_COLS = PROJ_XBC + SSM_CONV_DIM

ATT_HEADS = 16
ATT_KV_HEADS = 4
ATT_HEAD = 64
ATT_GROUP = ATT_HEADS // ATT_KV_HEADS
ATT_Q_COLS = ATT_HEADS * ATT_HEAD
ATT_KV_COLS = ATT_KV_HEADS * ATT_HEAD
MOBA_BLOCK = 256
MOBA_TOPK = 3
MASK_NEG = -1e30
PAGE_SIZE = 128
FEAT_BLOCKS = 32
FEAT_START = FEAT_BLOCKS
FEAT_OFFSET = FEAT_START + 3
SLOPE_PARTS = 3

N_EXPERTS = 8

V7X_VMEM_BYTES = 64 * 1024 * 1024
VMEM_LIMIT = V7X_VMEM_BYTES - 8 * 1024 * 1024


def _cparams(sem):
    return pltpu.CompilerParams(dimension_semantics=sem, vmem_limit_bytes=VMEM_LIMIT)


def _nt(a, b, precision=None):
    return lax.dot_general(a, b, (((1,), (1,)), ((), ())), precision=precision,
                           preferred_element_type=F32)


def _dot(a, b, precision=None):
    return jnp.dot(a, b, precision=precision, preferred_element_type=F32)


def _split2(x):
    hi = x.astype(BF16)
    return hi, (x - hi.astype(F32)).astype(BF16)


def _split3(x):
    hi = x.astype(BF16)
    r = x - hi.astype(F32)
    mid = r.astype(BF16)
    return hi, mid, (r - mid.astype(F32)).astype(BF16)


def _cat3(x, axis):
    hi, lo = _split2(x)
    return jnp.concatenate([hi, lo, hi], axis=axis)


def _cat3w(w, axis):
    hi, lo = _split2(w)
    return jnp.concatenate([hi, hi, lo], axis=axis)


def _rms(x, g):
    return x * lax.rsqrt(jnp.mean(x * x, axis=-1, keepdims=True) + NORM_EPS) * g


def _softplus(x):
    return jnp.maximum(x, 0.0) + jnp.log(1.0 + jnp.exp(-jnp.abs(x)))


def _sigmoid(x):
    return 1.0 / (1.0 + jnp.exp(-x))


def _silu(x):
    return x * _sigmoid(x)


def _nm_body(x_ref, g_ref, w_ref, o_ref, h_sc):
    @pl.when(pl.program_id(1) == 0)
    def _():
        h_sc[...] = _rms(x_ref[...], g_ref[...]).astype(BF16)

    o_ref[...] = _dot(h_sc[...], w_ref[...])


def norm_matmul(x, g, w, tm, tn):
    m, k = x.shape
    n = w.shape[1]
    return pl.pallas_call(
        _nm_body,
        grid=(m // tm, n // tn),
        in_specs=[pl.BlockSpec((tm, k), lambda i, j: (i, 0)),
                  pl.BlockSpec((1, k), lambda i, j: (0, 0)),
                  pl.BlockSpec((k, tn), lambda i, j: (0, j))],
        out_specs=pl.BlockSpec((tm, tn), lambda i, j: (i, j)),
        out_shape=jax.ShapeDtypeStruct((m, n), F32),
        scratch_shapes=[pltpu.VMEM((tm, k), BF16)],
        compiler_params=_cparams(("parallel", "arbitrary")),
    )(x, g, w)


def _mmres_body(*refs, n):
    res_ref, a_refs, w_refs, o_ref = refs[0], refs[1:1 + n], refs[1 + n:1 + 2 * n], refs[1 + 2 * n]
    acc = res_ref[...]
    for a_ref, w_ref in zip(a_refs, w_refs):
        acc = acc + _dot(a_ref[...].astype(BF16), w_ref[...])
    o_ref[...] = acc


def matmul_residual(res, acts, ws, tm):
    m, d = res.shape
    n = len(acts)
    in_specs = [pl.BlockSpec((tm, d), lambda i: (i, 0))]
    in_specs += [pl.BlockSpec((tm, a.shape[1]), lambda i: (i, 0)) for a in acts]
    in_specs += [pl.BlockSpec(w.shape, lambda i: (0, 0)) for w in ws]
    return pl.pallas_call(
        functools.partial(_mmres_body, n=n),
        grid=(m // tm,),
        in_specs=in_specs,
        out_specs=pl.BlockSpec((tm, d), lambda i: (i, 0)),
        out_shape=jax.ShapeDtypeStruct((m, d), F32),
        compiler_params=_cparams(("parallel",)),
    )(res, *acts, *ws)


def _ffn_body(*refs, n_mix):
    x_ref, mix_refs, wmix_refs = refs[0], refs[1:1 + n_mix], refs[1 + n_mix:1 + 2 * n_mix]
    g_ref, wg_ref, wu_ref, wd_ref, o_ref, x1_sc, h_sc, acc_sc = refs[1 + 2 * n_mix:]
    j = pl.program_id(1)

    @pl.when(j == 0)
    def _():
        x1 = x_ref[...]
        for a_ref, w_ref in zip(mix_refs, wmix_refs):
            x1 = x1 + _dot(a_ref[...].astype(BF16), w_ref[...])
        x1_sc[...] = x1
        h_sc[...] = _rms(x1, g_ref[...]).astype(BF16)
        acc_sc[...] = jnp.zeros_like(acc_sc)

    h = h_sc[...]
    act = _silu(_dot(h, wg_ref[...])) * _dot(h, wu_ref[...])
    acc_sc[...] += _dot(act.astype(BF16), wd_ref[...])

    @pl.when(j == pl.num_programs(1) - 1)
    def _():
        o_ref[...] = x1_sc[...] + acc_sc[...]


def ffn_swiglu(x, mixes, wmixes, g, wg, wu, wd, tm, tf):
    m, d = x.shape
    f = wg.shape[1]
    row_tile = lambda a: pl.BlockSpec((tm, a.shape[1]), lambda i, j: (i, 0))
    whole = lambda a: pl.BlockSpec(a.shape, lambda i, j: (0, 0))
    return pl.pallas_call(
        functools.partial(_ffn_body, n_mix=len(mixes)),
        grid=(m // tm, f // tf),
        in_specs=[row_tile(x)] + [row_tile(a) for a in mixes] + [whole(w) for w in wmixes]
        + [pl.BlockSpec((1, d), lambda i, j: (0, 0)),
           pl.BlockSpec((d, tf), lambda i, j: (0, j)),
           pl.BlockSpec((d, tf), lambda i, j: (0, j)),
           pl.BlockSpec((tf, d), lambda i, j: (j, 0))],
        out_specs=pl.BlockSpec((tm, d), lambda i, j: (i, 0)),
        out_shape=jax.ShapeDtypeStruct((m, d), F32),
        scratch_shapes=[pltpu.VMEM((tm, d), F32), pltpu.VMEM((tm, d), BF16), pltpu.VMEM((tm, d), F32)],
        compiler_params=_cparams(("parallel", "arbitrary")),
    )(x, *mixes, *wmixes, g, wg, wu, wd)


MOE_SUB = 512
MOE_CHUNK = 160


def _moe_body(x_ref, g_ref, r_ref, wg_ref, wu_ref, wd_ref, gf_ref, o_ref, h_sc, comb_sc, acc_sc, rk_sc, rkt_sc, *,
              final_norm, chunk):
    e = pl.program_id(1)
    tm = x_ref.shape[0]
    lane = lax.broadcasted_iota(jnp.int32, (tm, 128), 1)

    @pl.when(e == 0)
    def _():
        hf = _rms(x_ref[...], g_ref[...])
        h_sc[...] = hf.astype(BF16)
        lanef = lane.astype(F32)
        logits = jnp.where(lane < N_EXPERTS, _dot(_cat3(hf, 1), r_ref[...]), NEG_INF)
        m1 = jnp.max(logits, axis=1, keepdims=True)
        i1 = jnp.min(jnp.where(logits == m1, lanef, 128.0), axis=1, keepdims=True)
        mask1 = lanef == i1
        rest = jnp.where(mask1, NEG_INF, logits)
        m2 = jnp.max(rest, axis=1, keepdims=True)
        i2 = jnp.min(jnp.where(rest == m2, lanef, 128.0), axis=1, keepdims=True)
        mask2 = lanef == i2
        e2 = jnp.exp(m2 - m1)
        den = 1.0 + e2
        comb_sc[...] = jnp.where(mask1, 1.0 / den, 0.0) + jnp.where(mask2, e2 / den, 0.0)
        acc_sc[...] = jnp.zeros_like(acc_sc)
        if chunk:
            earlier = (lax.broadcasted_iota(jnp.int32, (MOE_SUB, MOE_SUB), 1)
                       < lax.broadcasted_iota(jnp.int32, (MOE_SUB, MOE_SUB), 0))
            earlier = jnp.where(earlier, 1.0, 0.0).astype(BF16)
            routed = jnp.where(mask1 | mask2, 1.0, 0.0)
            for st in range(tm // MOE_SUB):
                sl = slice(st * MOE_SUB, (st + 1) * MOE_SUB)
                rank = _dot(earlier, routed[sl].astype(BF16))
                rank = jnp.where(routed[sl] > 0.0, rank, -1.0)
                rk_sc[sl, :] = rank
                rkt_sc[st] = rank.T

    def expert(rows):
        act = _silu(_dot(rows, wg_ref[0])) * _dot(rows, wu_ref[0])
        return _dot(act.astype(BF16), wd_ref[0])

    this = lane == e
    if not chunk:
        c = jnp.sum(jnp.where(this, comb_sc[...], 0.0), axis=1, keepdims=True)
        acc_sc[...] += c * expert(h_sc[...])
    else:
        wide = -(-chunk // 128) * 128
        slot = lax.broadcasted_iota(jnp.int32, (chunk, MOE_SUB), 0).astype(F32)
        lane_w = lax.broadcasted_iota(jnp.int32, (MOE_SUB, wide), 1)
        slot_t = jnp.where(lane_w < chunk, lane_w.astype(F32), jnp.nan)
        sub8 = lax.broadcasted_iota(jnp.int32, (8, MOE_SUB), 0)
        this_sub = lax.broadcasted_iota(jnp.int32, (MOE_SUB, 128), 1) == e
        pad_rows = jnp.zeros((wide - chunk, x_ref.shape[1]), BF16)
        for st in range(tm // MOE_SUB):
            sl = slice(st * MOE_SUB, (st + 1) * MOE_SUB)
            rank_col = jnp.sum(jnp.where(this_sub, rk_sc[sl, :], 0.0), axis=1, keepdims=True)
            rank_row = jnp.sum(jnp.where(sub8 == e, rkt_sc[st, 0:8, :], 0.0), axis=0, keepdims=True)
            c = jnp.sum(jnp.where(this_sub, comb_sc[sl, :], 0.0), axis=1, keepdims=True)
            count = jnp.max(rank_row, axis=1, keepdims=True)[0, 0].astype(jnp.int32) + 1

            def one_chunk(k, carry):
                base = (k * chunk).astype(F32)
                gather = jnp.where(rank_row - base == slot, 1.0, 0.0).astype(BF16)
                y_hi, y_lo = _split2(expert(_dot(gather, h_sc[sl, :]).astype(BF16)))
                scatter = jnp.where(rank_col - base == slot_t, 1.0, 0.0).astype(BF16)
                back = (_dot(scatter, jnp.concatenate([y_hi, pad_rows], axis=0))
                        + _dot(scatter, jnp.concatenate([y_lo, pad_rows], axis=0)))
                acc_sc[sl, :] += c * back
                return carry

            lax.fori_loop(0, (count + chunk - 1) // chunk, one_chunk, 0)

    @pl.when(e == pl.num_programs(1) - 1)
    def _():
        y = x_ref[...] + acc_sc[...]
        if final_norm:
            y = _rms(y, gf_ref[...])
        o_ref[...] = y


def moe_swiglu(x, g, router, wg, wu, wd, gfinal, tm, final_norm):
    m, d = x.shape
    ne, _, fe = wg.shape
    chunk = MOE_CHUNK if tm % MOE_SUB == 0 else 0
    nsub = max(tm // MOE_SUB, 1)
    return pl.pallas_call(
        functools.partial(_moe_body, final_norm=final_norm, chunk=chunk),
        grid=(m // tm, ne),
        in_specs=[pl.BlockSpec((tm, d), lambda i, e: (i, 0)),
                  pl.BlockSpec((1, d), lambda i, e: (0, 0)),
                  pl.BlockSpec((3 * d, 128), lambda i, e: (0, 0)),
                  pl.BlockSpec((1, d, fe), lambda i, e: (e, 0, 0)),
                  pl.BlockSpec((1, d, fe), lambda i, e: (e, 0, 0)),
                  pl.BlockSpec((1, fe, d), lambda i, e: (e, 0, 0)),
                  pl.BlockSpec((1, d), lambda i, e: (0, 0))],
        out_specs=pl.BlockSpec((tm, d), lambda i, e: (i, 0)),
        out_shape=jax.ShapeDtypeStruct((m, d), F32),
        scratch_shapes=[pltpu.VMEM((tm, d), BF16), pltpu.VMEM((tm, 128), F32), pltpu.VMEM((tm, d), F32),
                        pltpu.VMEM((tm, 128), F32), pltpu.VMEM((nsub, 128, min(tm, MOE_SUB)), F32)],
        compiler_params=_cparams(("parallel", "arbitrary")),
    )(x, g, router, wg, wu, wd, gfinal)


RW_PAIRS = RW_HEADS // 2
RW_ROWS = RW_PAIRS * RW_HEAD


def _rwkv_body(p_ref, sh0_ref, s0_ref, mu_ref, w0_ref, a0_ref, kk_ref, ka_ref, rk_ref, lnw_ref, lnb_ref,
               wl_ref, g2_ref, b2_ref, pw_ref, y_ref, sfin_ref,
               s_sc, sk_sc, prev_sc, kk_sc, w_sc, b_sc, k_sc, r_sc, v_sc, g_sc, y_sc, *, nbb, tc, t_valid):
    c = pl.program_id(1)

    @pl.when(c == 0)
    def _():
        s_sc[...] = s0_ref[...]
        prev_sc[...] = sh0_ref[...]

    b2 = b2_ref[...]

    def head_sums(x, low=True):
        if low:
            hi, lo = _split2(x)
        else:
            hi = x.astype(BF16)
            lo = jnp.zeros_like(hi)
        return _dot(jnp.concatenate([hi, lo], axis=1), b2)

    def head_sums_wide(x):
        return jnp.concatenate([head_sums(x[:, q * 128:(q + 1) * 128]) for q in range(RW_PAIRS)], axis=1)

    for bi in range(nbb):
        p = p_ref[bi]
        row = lax.broadcasted_iota(jnp.int32, p.shape, 0)
        prev = jnp.where(row == 0, prev_sc[bi], pltpu.roll(p, 1, axis=0))
        prev_sc[bi] = p[tc - 1:tc, :]
        u = p + (prev - p) * mu_ref[...]
        r = u[:, 0:RW_DIM]
        k = u[:, RW_DIM:2 * RW_DIM]
        v = u[:, 2 * RW_DIM:3 * RW_DIM]
        lo = u[:, 3 * RW_DIM:3 * RW_DIM + RW_LORA]
        gd = u[:, 3 * RW_DIM + RW_LORA:]
        lane = lax.broadcasted_iota(jnp.int32, lo.shape, 1)
        lora = _dot(_cat3(jnp.where(lane < RW_LORA // 2, jnp.tanh(lo), lo), 1), wl_ref[...])
        wlog = -_softplus(-(w0_ref[...] + lora[:, :RW_DIM])) - 0.5
        a = _sigmoid(a0_ref[...] + lora[:, RW_DIM:])
        g_sc[bi] = _dot(_cat3(_sigmoid(gd), 1), g2_ref[...])
        kk = k * kk_ref[...]
        kk = kk / jnp.maximum(jnp.sqrt(head_sums_wide(kk * kk)), 1e-12)
        kk_sc[bi] = kk
        w_sc[bi] = jnp.exp(-jnp.exp(wlog))
        b_sc[bi] = kk * a
        k_sc[bi] = k * (1.0 + (a - 1.0) * ka_ref[...])
        r_sc[bi] = r
        v_sc[bi] = v

    rows = nbb * RW_ROWS
    vi = lax.broadcasted_iota(jnp.int32, (rows, 128), 0) % RW_HEAD
    li = lax.broadcasted_iota(jnp.int32, (rows, 128), 1)
    key_lane = li % RW_HEAD
    diag = (key_lane == vi).astype(F32)
    steps = 8 if t_valid % 8 == 0 else t_valid

    def rows_of(x8, j):
        return jnp.concatenate([jnp.broadcast_to(x8[bi][j:j + 1, q * 128:(q + 1) * 128], (RW_HEAD, 128))
                                for bi in range(nbb) for q in range(RW_PAIRS)], axis=0)

    pair_w = pw_ref[...]
    zero_half = jnp.zeros((rows, 128), BF16)

    def pair_sums(a, b):
        lhs = jnp.concatenate([a.astype(BF16), zero_half if b is None else b.astype(BF16)], axis=1)
        out = _dot(lhs, pair_w)
        return out[:, :128], out[:, 128:]

    first8 = [kk_sc[bi, pl.ds(0, 8), :] for bi in range(nbb)]
    sk_sc[...] = pair_sums(s_sc[...].reshape(rows, 128) * rows_of(first8, 0), None)[0]

    def group(t8, carry):
        t0 = pl.multiple_of(t8 * 8, 8)
        tn = pl.multiple_of(jnp.minimum(t0 + 8, tc - 8), 8)
        kk8, w8, b8, k8, r8, v8 = ([sc[bi, pl.ds(t0, 8), :] for bi in range(nbb)]
                                   for sc in (kk_sc, w_sc, b_sc, k_sc, r_sc, v_sc))
        kk_next = [kk_sc[bi, pl.ds(tn, 8), :] for bi in range(nbb)]
        s = s_sc[...].reshape(rows, 128)
        sk = sk_sc[...]
        ycols = jnp.zeros((rows, 128), F32)
        for j in range(steps):
            if j % 2 == 0:
                vcols = pair_sums(rows_of(v8, j) * diag, rows_of(v8, j + 1) * diag if j + 1 < steps else None)
            s = s * rows_of(w8, j) - sk * rows_of(b8, j) + vcols[j % 2] * rows_of(k8, j)
            kk_after = rows_of(kk8, j + 1) if j + 1 < 8 else rows_of(kk_next, 0)
            sk, yb = pair_sums(s * kk_after, s * rows_of(r8, j))
            ycols = jnp.where(key_lane == j, yb, ycols)
        s_sc[...] = s.reshape(nbb, RW_ROWS, 128)
        sk_sc[...] = sk
        h64 = RW_HEAD
        for bi in range(nbb):
            pieces = []
            for q in range(0, RW_PAIRS, 2):
                r0 = (bi * RW_PAIRS + q) * h64
                tr = ycols[r0:r0 + 2 * h64].T
                pieces += [tr[0:8, 0:h64], tr[h64:h64 + 8, 0:h64], tr[0:8, h64:], tr[h64:h64 + 8, h64:]]
            y_sc[bi, pl.ds(t0, 8), :] = jnp.concatenate(pieces, axis=1)
        return carry

    lax.fori_loop(0, -(-t_valid // 8), group, 0)

    inv = 1.0 / RW_HEAD
    for bi in range(nbb):
        y = y_sc[bi]
        yc = y - head_sums_wide(y) * inv
        var = head_sums_wide(yc * yc) * inv
        yn = yc * lax.rsqrt(var + RW_LN_EPS) * lnw_ref[...] + lnb_ref[...]
        bonus = head_sums_wide(r_sc[bi] * k_sc[bi] * rk_ref[...]) * v_sc[bi]
        y_ref[bi] = ((yn + bonus) * g_sc[bi]).astype(y_ref.dtype)

    @pl.when(c == pl.num_programs(1) - 1)
    def _():
        sfin_ref[...] = s_sc[...]


def _wkv_to_rows(s):
    b = s.shape[0]
    return s.reshape(b, RW_PAIRS, 2, RW_HEAD, RW_HEAD).transpose(0, 1, 3, 2, 4).reshape(b, RW_ROWS, 128)


def _rows_to_wkv(s):
    b = s.shape[0]
    return s.reshape(b, RW_PAIRS, RW_HEAD, 2, RW_HEAD).transpose(0, 1, 3, 2, 4).reshape(b, RW_HEADS, RW_HEAD, RW_HEAD)


def rwkv7(proj, shift0, wkv0, prm, nbb, tc, t_valid):
    b, t, _ = proj.shape
    vec = lambda n: pl.BlockSpec((1, n), lambda i, c: (0, 0))
    full = lambda a: pl.BlockSpec(a.shape, lambda i, c: (0,) * a.ndim)
    seq = pltpu.VMEM((nbb, tc, RW_DIM), F32)
    y, s_fin = pl.pallas_call(
        functools.partial(_rwkv_body, nbb=nbb, tc=tc, t_valid=t_valid),
        grid=(b // nbb, t // tc),
        in_specs=[pl.BlockSpec((nbb, tc, RW_COLS), lambda i, c: (i, c, 0)),
                  pl.BlockSpec((nbb, 1, RW_COLS), lambda i, c: (i, 0, 0)),
                  pl.BlockSpec((nbb, RW_ROWS, 128), lambda i, c: (i, 0, 0)),
                  vec(RW_COLS), vec(RW_DIM), vec(RW_DIM), vec(RW_DIM), vec(RW_DIM), vec(RW_DIM), vec(RW_DIM),
                  vec(RW_DIM), full(prm["wl"]), full(prm["g2"]), full(prm["bones2"]), full(prm["pair_w"])],
        out_specs=[pl.BlockSpec((nbb, tc, RW_DIM), lambda i, c: (i, c, 0)),
                   pl.BlockSpec((nbb, RW_ROWS, 128), lambda i, c: (i, 0, 0))],
        out_shape=[jax.ShapeDtypeStruct((b, t, RW_DIM), BF16),
                   jax.ShapeDtypeStruct((b, RW_ROWS, 128), F32)],
        scratch_shapes=[pltpu.VMEM((nbb, RW_ROWS, 128), F32), pltpu.VMEM((nbb * RW_ROWS, 128), F32),
                        pltpu.VMEM((nbb, 1, RW_COLS), F32),
                        seq, seq, seq, seq, seq, seq, seq, seq],
        compiler_params=_cparams(("parallel", "arbitrary")),
    )(proj, shift0, _wkv_to_rows(wkv0), prm["mu"], prm["w0"], prm["a0"], prm["k_k"], prm["k_a"], prm["r_k"],
      prm["ln_w"], prm["ln_b"], prm["wl"], prm["g2"], prm["bones2"], prm["pair_w"])
    return y, _rows_to_wkv(s_fin)


def _ssd_body(z_ref, xbc_ref, dt_ref, conv0_ref, h0_ref, cw_ref, cb_ref, dtb_ref, alog_ref, dsk_ref, nw_ref,
              e_ref, y_ref, hfin_ref, h_sc, ext_sc, *, t_valid):
    c = pl.program_id(1)
    ln = SSD_CHUNK

    @pl.when(c == 0)
    def _():
        h_sc[...] = h0_ref[0]
        ext_sc[0:8, :] = conv0_ref[0]

    xbc = xbc_ref[0]
    ext_sc[8:8 + ln, :] = xbc
    cw = cw_ref[...]
    conv = cb_ref[...] + ext_sc[5:5 + ln, :] * cw[0:1]
    conv = conv + ext_sc[6:6 + ln, :] * cw[1:2]
    conv = conv + ext_sc[7:7 + ln, :] * cw[2:3]
    conv = conv + xbc * cw[3:4]
    ext_sc[0:8, :] = xbc[ln - 8:ln, :]
    act = _silu(conv)
    xs = act[:, :SSM_INNER]
    bm = act[:, SSM_INNER:SSM_INNER + SSM_GROUPS * SSM_STATE].astype(BF16)
    cm = act[:, SSM_INNER + SSM_GROUPS * SSM_STATE:].astype(BF16)

    lane = lax.broadcasted_iota(jnp.int32, (ln, 128), 1)
    rowi = lax.broadcasted_iota(jnp.int32, (ln, 128), 0)
    dt = _softplus(dt_ref[0] + dtb_ref[...])
    if t_valid < ln:
        dt = jnp.where(rowi < t_valid, dt, 0.0)
    a = dt * jnp.where(lane[0:1] < SSM_HEADS, -jnp.exp(alog_ref[...]), 0.0)
    causal = lane <= rowi
    tri = jnp.where(causal, 1.0, 0.0).astype(BF16)
    tri3 = jnp.concatenate([tri, tri, tri], axis=1)
    acum = _dot(tri3, jnp.concatenate(_split3(a), axis=0))
    acum_t = _nt(jnp.concatenate(_split3(a.T), axis=1), tri3)
    e3 = e_ref[...]
    dt_full = _dot(jnp.concatenate(_split3(dt), axis=1), e3)
    ac_full = _dot(jnp.concatenate(_split3(acum), axis=1), e3)
    xdt = xs * dt_full
    xw = xdt * jnp.exp(ac_full[ln - 1:ln, :] - ac_full)
    eac = jnp.exp(ac_full)
    cd = jnp.broadcast_to(jnp.exp(acum_t[:, ln - 1:ln]), (128, 128))
    cd_rows = jnp.concatenate([jnp.broadcast_to(cd[h:h + 1, :], (SSM_HEAD, SSM_STATE)) for h in range(SSM_HEADS)],
                              axis=0)

    ys = []
    for g in range(SSM_GROUPS):
        bm_g = bm[:, g * SSM_STATE:(g + 1) * SSM_STATE]
        cm_g = cm[:, g * SSM_STATE:(g + 1) * SSM_STATE]
        cbm = _nt(cm_g, bm_g)
        pairs = SSM_HEADS // SSM_GROUPS // 2
        for q in range(g * pairs, (g + 1) * pairs):
            sl = slice(q * 128, (q + 1) * 128)
            xdt_p = xdt[:, sl].astype(BF16)
            yd = []
            for h in (2 * q, 2 * q + 1):
                seg = jnp.where(causal, acum[:, h:h + 1] - acum_t[h:h + 1, :], NEG_INF)
                yd.append(_dot((cbm * jnp.exp(seg)).astype(BF16), xdt_p))
            hp = h_sc[sl, :]
            y_off = _nt(cm_g, hp.astype(BF16)) * eac[:, sl]
            st = _dot(xw[:, sl].T.astype(BF16), bm_g)
            h_sc[sl, :] = cd_rows[sl, :] * hp + st
            ys.append(jnp.where(lane < SSM_HEAD, yd[0], yd[1]) + y_off)
    y = jnp.concatenate(ys, axis=1) + dsk_ref[...] * xs
    y = y * _silu(z_ref[0])
    gw = SSM_INNER // SSM_GROUPS
    outs = []
    for g in range(SSM_GROUPS):
        yg = y[:, g * gw:(g + 1) * gw]
        outs.append(yg * lax.rsqrt(jnp.mean(yg * yg, axis=1, keepdims=True) + NORM_EPS))
    y_ref[0] = (jnp.concatenate(outs, axis=1) * nw_ref[...]).astype(y_ref.dtype)

    @pl.when(c == pl.num_programs(1) - 1)
    def _():
        hfin_ref[0] = h_sc[...]


def mamba2(proj, conv0, ssm0, prm, t_valid):
    b, t, _ = proj.shape
    ln = SSD_CHUNK
    vec = lambda n: pl.BlockSpec((1, n), lambda i, c: (0, 0))
    full = lambda a: pl.BlockSpec(a.shape, lambda i, c: (0,) * a.ndim)
    return pl.pallas_call(
        functools.partial(_ssd_body, t_valid=t_valid),
        grid=(b, t // ln),
        in_specs=[pl.BlockSpec((1, ln, SSM_INNER), lambda i, c: (i, c, PROJ_Z // SSM_INNER)),
                  pl.BlockSpec((1, ln, SSM_CONV_DIM), lambda i, c: (i, c, PROJ_XBC // SSM_CONV_DIM)),
                  pl.BlockSpec((1, ln, 128), lambda i, c: (i, c, PROJ_DT // 128)),
                  pl.BlockSpec((1, 8, SSM_CONV_DIM), lambda i, c: (i, 0, 0)),
                  pl.BlockSpec((1, SSM_INNER, SSM_STATE), lambda i, c: (i, 0, 0)),
                  full(prm["conv_w"]), vec(SSM_CONV_DIM), vec(128), vec(128), vec(SSM_INNER), vec(SSM_INNER),
                  full(prm["expand"])],
        out_specs=[pl.BlockSpec((1, ln, SSM_INNER), lambda i, c: (i, c, 0)),
                   pl.BlockSpec((1, SSM_INNER, SSM_STATE), lambda i, c: (i, 0, 0))],
        out_shape=[jax.ShapeDtypeStruct((b, t, SSM_INNER), BF16),
                   jax.ShapeDtypeStruct((b, SSM_INNER, SSM_STATE), F32)],
        scratch_shapes=[pltpu.VMEM((SSM_INNER, SSM_STATE), F32), pltpu.VMEM((8 + ln, SSM_CONV_DIM), F32)],
        compiler_params=_cparams(("parallel", "arbitrary")),
    )(proj, proj, proj, conv0, ssm0, prm["conv_w"], prm["conv_b"], prm["dt_bias"], prm["a_log"],
      prm["d_full"], prm["norm_w"], prm["expand"])


def _alibi_slopes():
    return (2.0 ** (-8.0 * np.arange(1, ATT_HEADS + 1) / ATT_HEADS)).astype(np.float32)


def _slope_features():
    s = _alibi_slopes()
    bf = lambda a: a.astype(jnp.bfloat16).astype(np.float32)
    hi = bf(s)
    mid = bf(s - hi)
    lo = bf(s - hi - mid)
    out = np.zeros((ATT_HEADS, FEAT_BLOCKS), np.float32)
    for i, piece in enumerate((hi, mid, lo)):
        out[:, i] = piece
        out[:, SLOPE_PARTS + i] = piece
    return out


def _top3_threshold(gate):
    v = gate
    for _ in range(MOBA_TOPK - 1):
        v = jnp.where(v == jnp.max(v, axis=1, keepdims=True), NEG_INF, v)
    return jnp.max(v, axis=1, keepdims=True)


def _qkv_body(x_ref, g_ref, w_ref, qkv_ref, kf_ref, vf_ref, mean_ref, *, tiles_per_seq):
    tm = x_ref.shape[0]
    qkv = _dot(_rms(x_ref[...], g_ref[...]).astype(BF16), w_ref[...])
    qkv_ref[...] = qkv
    for r in range(tm // MOBA_BLOCK):
        mean_ref[r] = jnp.mean(qkv[r * MOBA_BLOCK:(r + 1) * MOBA_BLOCK, ATT_Q_COLS:ATT_Q_COLS + ATT_KV_COLS],
                               axis=0, keepdims=True)
    pos = (pl.program_id(0) % tiles_per_seq) * tm + lax.broadcasted_iota(jnp.int32, (tm, ATT_HEAD), 0)
    lane = lax.broadcasted_iota(jnp.int32, (tm, ATT_HEAD), 1)
    blk_id, off = pos // MOBA_BLOCK, pos % MOBA_BLOCK
    kfeat = jnp.where(lane == blk_id, 1, 0)
    kfeat = jnp.where((lane >= FEAT_START) & (lane < FEAT_START + SLOPE_PARTS), blk_id * MOBA_BLOCK, kfeat)
    kfeat = jnp.where((lane >= FEAT_OFFSET) & (lane < FEAT_OFFSET + SLOPE_PARTS), off, kfeat)
    kfeat = kfeat.astype(F32).astype(BF16)
    vfeat = jnp.where(lane == 0, 1.0, 0.0).astype(BF16)
    for g in range(ATT_KV_HEADS):
        k0 = ATT_Q_COLS + g * ATT_HEAD
        v0 = ATT_Q_COLS + ATT_KV_COLS + g * ATT_HEAD
        kf_ref[0, g] = jnp.concatenate([qkv[:, k0:k0 + ATT_HEAD].astype(BF16), kfeat], axis=1)
        vf_ref[0, g] = jnp.concatenate([qkv[:, v0:v0 + ATT_HEAD].astype(BF16), vfeat], axis=1)


def qkv_proj(x, g, w, b, t, tm):
    m, d = x.shape
    n = w.shape[1]
    tps = t // tm
    ext = jax.ShapeDtypeStruct((b, ATT_KV_HEADS, t, 128), BF16)
    ext_spec = pl.BlockSpec((1, ATT_KV_HEADS, tm, 128), lambda i: (i // tps, 0, i % tps, 0))
    return pl.pallas_call(
        functools.partial(_qkv_body, tiles_per_seq=tps),
        grid=(m // tm,),
        in_specs=[pl.BlockSpec((tm, d), lambda i: (i, 0)), pl.BlockSpec((1, d), lambda i: (0, 0)),
                  pl.BlockSpec((d, n), lambda i: (0, 0))],
        out_specs=[pl.BlockSpec((tm, n), lambda i: (i, 0)), ext_spec, ext_spec,
                   pl.BlockSpec((tm // MOBA_BLOCK, 1, ATT_KV_COLS), lambda i: (i, 0, 0))],
        out_shape=[jax.ShapeDtypeStruct((m, n), F32), ext, ext,
                   jax.ShapeDtypeStruct((m // MOBA_BLOCK, 1, ATT_KV_COLS), F32)],
        compiler_params=_cparams(("parallel",)),
    )(x, g, w)


def _moba_prompt_body(q_ref, kf_ref, vf_ref, mean_ref, slf_ref, o_ref, qf_sc, m_sc, acc_sc, sa0_sc, sa1_sc, sb0_sc, sb1_sc):
    qb = pl.program_id(2)
    blk = MOBA_BLOCK
    rows = ATT_GROUP * blk
    scale = ATT_HEAD ** -0.5

    q4 = q_ref[0]
    qg = jnp.concatenate([q4[:, hh * ATT_HEAD:(hh + 1) * ATT_HEAD] for hh in range(ATT_GROUP)], axis=0)
    past_blk = lax.broadcasted_iota(jnp.int32, (128, rows), 0) < qb
    gate = jnp.where(past_blk, _nt(_cat3w(mean_ref[0, 0], 1), _cat3(qg, 1)), NEG_INF)
    v = gate
    for _ in range(MOBA_TOPK - 1):
        v = jnp.where(v == jnp.max(v, axis=0, keepdims=True), NEG_INF, v)
    thr = jnp.max(v, axis=0, keepdims=True)
    selneg = jnp.where(past_blk & (gate < thr), MASK_NEG, 0.0).T[:, :FEAT_BLOCKS]
    slf = jnp.concatenate([jnp.broadcast_to(slf_ref[0, hh:hh + 1, :], (blk, FEAT_BLOCKS))
                           for hh in range(ATT_GROUP)], axis=0)
    qf_sc[...] = jnp.concatenate([(qg * scale).astype(BF16), selneg.astype(BF16), slf.astype(BF16)], axis=1)

    def kv_tiles(j):
        start = pl.multiple_of(j * blk, blk)
        return kf_ref[0, 0, pl.ds(start, blk), :], vf_ref[0, 0, pl.ds(start, blk), :]

    qi = lax.broadcasted_iota(jnp.int32, (rows, blk), 0) % blk
    ki = lax.broadcasted_iota(jnp.int32, (rows, blk), 1)
    kf, vf = kv_tiles(qb)
    s = jnp.where(ki <= qi, _nt(qf_sc[...], kf), NEG_INF)
    m = jnp.max(s, axis=1, keepdims=True)
    m_sc[...] = jnp.broadcast_to(m, (rows, 128))
    acc_sc[...] = _dot(jnp.exp(s - m).astype(BF16), vf)

    def scores(j):
        return _nt(qf_sc[...], kv_tiles(jnp.minimum(j, qb))[0])

    def fold(blocks, s_cur, s_next=()):
        for ref, j in zip(s_next, (blocks[-1] + 1, blocks[-1] + 2)):
            ref[...] = scores(j)
        ss = [ref[...] for ref in s_cur]
        m_old = m_sc[...]
        m_new = m_old
        for s in ss:
            m_new = jnp.maximum(m_new, jnp.max(s, axis=1, keepdims=True))
        m2 = jnp.concatenate([m_new, m_new], axis=1)
        acc = acc_sc[...] * jnp.exp(m_old - m_new)
        for s, j in zip(ss, blocks):
            acc = acc + _dot(jnp.exp(s - m2).astype(BF16), kv_tiles(j)[1])
        acc_sc[...] = acc
        m_sc[...] = m_new

    set_a, set_b = (sa0_sc, sa1_sc), (sb0_sc, sb1_sc)
    sa0_sc[...] = scores(0)
    sa1_sc[...] = scores(1)

    def quad(i, carry):
        fold((4 * i, 4 * i + 1), set_a, set_b)
        fold((4 * i + 2, 4 * i + 3), set_b, set_a)
        return carry

    lax.fori_loop(0, qb // 4, quad, 0)
    done = qb // 4 * 4

    @pl.when(qb % 4 >= 2)
    def _():
        fold((done, done + 1), set_a)

    @pl.when(qb % 4 == 1)
    def _():
        fold((done,), set_a[:1])

    @pl.when(qb % 4 == 3)
    def _():
        sb0_sc[...] = scores(done + 2)
        fold((done + 2,), set_b[:1])

    acc = acc_sc[...]
    out = acc[:, :ATT_HEAD] / acc[:, ATT_HEAD:ATT_HEAD + 1]
    o_ref[0] = jnp.concatenate([out[hh * blk:(hh + 1) * blk] for hh in range(ATT_GROUP)], axis=1).astype(o_ref.dtype)


def moba_prompt(qkv, kf, vf, means, slf):
    b, t, _ = qkv.shape
    nb = t // MOBA_BLOCK
    assert nb <= FEAT_BLOCKS
    nbl = 128
    means = jnp.pad(means, ((0, 0), (0, 0), (0, nbl - nb), (0, 0)))
    rows = ATT_GROUP * MOBA_BLOCK
    gw = ATT_GROUP * ATT_HEAD
    return pl.pallas_call(
        _moba_prompt_body,
        grid=(b, ATT_KV_HEADS, nb),
        in_specs=[pl.BlockSpec((1, MOBA_BLOCK, gw), lambda i, g, j: (i, j, g)),
                  pl.BlockSpec((1, 1, t, 128), lambda i, g, j: (i, g, 0, 0)),
                  pl.BlockSpec((1, 1, t, 128), lambda i, g, j: (i, g, 0, 0)),
                  pl.BlockSpec((1, 1, nbl, ATT_HEAD), lambda i, g, j: (i, g, 0, 0)),
                  pl.BlockSpec((1, ATT_GROUP, FEAT_BLOCKS), lambda i, g, j: (g, 0, 0))],
        out_specs=pl.BlockSpec((1, MOBA_BLOCK, gw), lambda i, g, j: (i, j, g)),
        out_shape=jax.ShapeDtypeStruct((b, t, ATT_Q_COLS), BF16),
        scratch_shapes=[pltpu.VMEM((rows, 128), BF16), pltpu.VMEM((rows, 128), F32), pltpu.VMEM((rows, 128), F32),
                        *([pltpu.VMEM((rows, MOBA_BLOCK), F32)] * 4)],
        compiler_params=_cparams(("parallel", "parallel", "arbitrary")),
    )(qkv, kf, vf, means, slf)


def _moba_sample_body(pt_ref, *refs, nbp, bps, nq):
    npg = 2 * bps
    k_refs, v_refs = refs[:npg], refs[npg:2 * npg]
    q_ref, kn_ref, vn_ref, slope_ref, o_ref, mean_sc, m_sc, l_sc, acc_sc = refs[2 * npg:]
    j = pl.program_id(1)
    blk = MOBA_BLOCK
    rows = ATT_HEADS * nq
    grows = ATT_GROUP * nq
    past = nbp * blk
    scale = ATT_HEAD ** -0.5
    lane = lax.broadcasted_iota(jnp.int32, (rows, 128), 1)
    qi = (lax.broadcasted_iota(jnp.int32, (rows, 1), 0) % nq).astype(F32)
    qf = q_ref[0]
    qs = (qf * scale).astype(BF16)
    slope = slope_ref[...]

    @pl.when(j == 0)
    def _():
        mean_sc[...] = jnp.zeros_like(mean_sc)
        m_sc[...] = jnp.full(m_sc.shape, NEG_INF, F32)
        l_sc[...] = jnp.zeros_like(l_sc)

    kcat = jnp.concatenate([ref[0] for ref in k_refs], axis=0)
    off = lax.broadcasted_iota(jnp.int32, (1, bps * blk), 1).astype(F32)
    sc = _nt(qs, kcat.astype(BF16)) - slope * ((past - j * bps * blk).astype(F32) + qi - off)
    m_new, l_new = m_sc[...], l_sc[...]
    for s in range(bps):
        bi = j * bps + s
        ksum = jnp.sum(kcat[s * blk:(s + 1) * blk], axis=0, keepdims=True)
        mean_sc[pl.ds(bi, 1), :] = ksum * (1.0 / blk)
        sc_b = sc[:, s * blk:(s + 1) * blk]
        m = jnp.max(sc_b, axis=1, keepdims=True)
        p = jnp.exp(sc_b - m)
        m_new = jnp.where(lane == bi, m, m_new)
        l_new = jnp.where(lane == bi, jnp.sum(p, axis=1, keepdims=True), l_new)
        vblk = jnp.concatenate([v_refs[2 * s][0], v_refs[2 * s + 1][0]], axis=0)
        acc_sc[bi] = _dot(p.astype(BF16), vblk.astype(BF16))
    m_sc[...] = m_new
    l_sc[...] = l_new

    @pl.when(j == pl.num_programs(1) - 1)
    def _():
        gate = jnp.where(lane < nbp, _nt(_cat3(qf, 1), _cat3w(mean_sc[...], 1)), NEG_INF)
        sel = gate >= _top3_threshold(gate)
        ki = lax.broadcasted_iota(jnp.int32, (rows, 8), 1).astype(F32)
        s_own = jnp.where(ki <= qi, _nt(qs, kn_ref[0].astype(BF16)) - slope * (qi - ki), NEG_INF)
        m_all = jnp.where(sel, m_sc[...], NEG_INF)
        mx = jnp.maximum(jnp.max(m_all, axis=1, keepdims=True), jnp.max(s_own, axis=1, keepdims=True))
        w = jnp.where(sel, jnp.exp(m_all - mx), 0.0)
        p_own = jnp.exp(s_own - mx)
        den = jnp.sum(w * l_sc[...], axis=1, keepdims=True) + jnp.sum(p_own, axis=1, keepdims=True)
        acc = _dot(p_own.astype(BF16), vn_ref[0].astype(BF16))
        for n in range(nbp):
            acc = acc + w[:, n:n + 1] * acc_sc[n]
        out = acc / den
        for g in range(ATT_KV_HEADS):
            o_ref[0, g] = out[g * grows:(g + 1) * grows, g * ATT_HEAD:(g + 1) * ATT_HEAD]


SAMPLE_BLOCKS_PER_STEP = 8


def _pages2d(cache):
    return cache.reshape(cache.shape[0], PAGE_SIZE, ATT_KV_COLS)


def _block_diag_queries(q):
    db, nq, _ = q.shape
    qh = q.reshape(db, nq, ATT_KV_HEADS, ATT_GROUP, ATT_HEAD).transpose(0, 2, 3, 1, 4)
    eye = jnp.eye(ATT_KV_HEADS, dtype=q.dtype)
    return (qh[:, :, :, :, None, :] * eye[None, :, None, None, :, None]).reshape(db, ATT_HEADS * nq, ATT_KV_COLS)


def _new_rows(a):
    db, nq = a.shape[:2]
    return jnp.pad(a.reshape(db, nq, ATT_KV_COLS), ((0, 0), (0, 8 - nq), (0, 0)))


def moba_sample(page_table, cache_k, cache_v, q_bd, k_new, v_new, slope_rows, nq):
    db, n_pages = page_table.shape
    nbp = n_pages * PAGE_SIZE // MOBA_BLOCK
    bps = min(SAMPLE_BLOCKS_PER_STEP, nbp)
    rows = ATT_HEADS * nq
    grows = ATT_GROUP * nq

    def page_spec(slot):
        return pl.BlockSpec((1, PAGE_SIZE, ATT_KV_COLS),
                            lambda b, j, pt: (pt[b * n_pages + j * 2 * bps + slot], 0, 0))

    small = lambda shp: pl.BlockSpec((1,) + shp, lambda b, j, pt: (b,) + (0,) * len(shp))
    grid_spec = pltpu.PrefetchScalarGridSpec(
        num_scalar_prefetch=1,
        grid=(db, nbp // bps),
        in_specs=[page_spec(s) for s in range(2 * bps)] * 2
        + [small((rows, ATT_KV_COLS)), small((8, ATT_KV_COLS)), small((8, ATT_KV_COLS)),
           pl.BlockSpec((rows, 1), lambda b, j, pt: (0, 0))],
        out_specs=small((ATT_KV_HEADS, grows, ATT_HEAD)),
        scratch_shapes=[pltpu.VMEM((128, ATT_KV_COLS), F32), pltpu.VMEM((rows, 128), F32),
                        pltpu.VMEM((rows, 128), F32), pltpu.VMEM((nbp, rows, ATT_KV_COLS), F32)],
    )
    return pl.pallas_call(
        functools.partial(_moba_sample_body, nbp=nbp, bps=bps, nq=nq),
        grid_spec=grid_spec,
        out_shape=jax.ShapeDtypeStruct((db, ATT_KV_HEADS, grows, ATT_HEAD), F32),
        compiler_params=_cparams(("parallel", "arbitrary")),
    )(page_table.reshape(-1), *([cache_k] * (2 * bps)), *([cache_v] * (2 * bps)), q_bd, k_new, v_new, slope_rows)


def _row(a):
    return a.reshape(1, -1).astype(F32)


def _even_params(i, w_in_ab, rwkv_mu, rwkv_w0, rwkv_w2, rwkv_a0, rwkv_a2, rwkv_g2, rwkv_k_k, rwkv_k_a, rwkv_r_k,
                 rwkv_ln_w, rwkv_ln_b, ssm_conv_w, ssm_conv_b, ssm_dt_bias, ssm_a_log, ssm_d, ssm_norm_w):
    w = w_in_ab[i]
    ssm0 = RW_COLS
    zeros = lambda n: jnp.zeros((D_MODEL, n), w.dtype)
    w_pack = jnp.concatenate([
        w[:, :RW_COLS],
        w[:, ssm0 + SSM_INNER + SSM_CONV_DIM:], zeros(PROJ_Z - PROJ_DT - SSM_HEADS),
        w[:, ssm0:ssm0 + SSM_INNER],
        w[:, ssm0 + SSM_INNER:ssm0 + SSM_INNER + SSM_CONV_DIM]], axis=1).astype(BF16)
    half = RW_LORA // 2
    zl = jnp.zeros((half, RW_DIM), F32)
    wl = jnp.concatenate([jnp.concatenate([rwkv_w2[i], zl], axis=1),
                          jnp.concatenate([zl, rwkv_a2[i]], axis=1)], axis=0)
    half_of = np.arange(128) // RW_HEAD
    bones2 = np.tile((half_of[:, None] == half_of[None, :]).astype(np.float32), (2, 1))
    rw = dict(mu=_row(rwkv_mu[i]), w0=_row(rwkv_w0[i]), a0=_row(rwkv_a0[i]), k_k=_row(rwkv_k_k[i]),
              k_a=_row(rwkv_k_a[i]), r_k=_row(rwkv_r_k[i]), ln_w=_row(rwkv_ln_w[i]), ln_b=_row(rwkv_ln_b[i]),
              wl=_cat3w(wl, 0), g2=_cat3w(rwkv_g2[i], 0), bones2=jnp.asarray(bones2, BF16),
              pair_w=jnp.asarray(np.kron(np.eye(4, dtype=np.float32), np.ones((RW_HEAD, RW_HEAD), np.float32)), BF16))
    pad128 = lambda a: jnp.pad(_row(a), ((0, 0), (0, 128 - a.shape[-1])))
    expand = (np.arange(128)[:, None] == (np.arange(SSM_INNER) // SSM_HEAD)[None, :]).astype(np.float32)
    ssm = dict(conv_w=ssm_conv_w[i], conv_b=_row(ssm_conv_b[i]), dt_bias=pad128(ssm_dt_bias[i]),
               a_log=pad128(ssm_a_log[i]), d_full=_row(jnp.repeat(ssm_d[i], SSM_HEAD)), norm_w=_row(ssm_norm_w[i]),
               expand=jnp.asarray(np.tile(expand, (3, 1)), BF16))
    return w_pack, rw, ssm


def _mixer_ab(x, b, t, norm_g, w_pack, rw, ssm, w_out, shift0, wkv0, conv0, ssm0, tm):
    tm_in = 2 * tm if (b * t) % (2 * tm) == 0 else tm
    proj = norm_matmul(x, norm_g, w_pack, tm_in, SSM_CONV_DIM).reshape(b, t, PROJ_COLS)
    shift_new = proj[:, t - 1, :RW_COLS]
    conv_new = proj[:, t - (SSM_CONV - 1):, PROJ_XBC:]
    if t % SSD_CHUNK == 0:
        proj_rw, proj_ssm, tc = proj, proj, SSD_CHUNK
    else:
        proj_rw = jnp.pad(proj, ((0, 0), (0, 8 - t), (0, 0)))
        proj_ssm = jnp.pad(proj, ((0, 0), (0, SSD_CHUNK - t), (0, 0)))
        tc = 8
    nbb = next(n for n in (4, 2, 1) if b % n == 0)
    y_rw, wkv_new = rwkv7(proj_rw, shift0.reshape(b, 1, RW_COLS), wkv0, rw, nbb, tc, min(t, tc))
    conv0p = jnp.pad(conv0, ((0, 0), (8 - (SSM_CONV - 1), 0), (0, 0)))
    y_ssm, ssm_new = mamba2(proj_ssm, conv0p, ssm0.reshape(b, SSM_INNER, SSM_STATE), ssm, min(t, SSD_CHUNK))
    mixes = [y_rw[:, :t].reshape(b * t, RW_DIM), y_ssm[:, :t].reshape(b * t, SSM_INNER)]
    return (mixes, [w_out[:RW_DIM], w_out[RW_DIM:]],
            (shift_new, wkv_new, conv_new, ssm_new.reshape(b, SSM_HEADS, SSM_HEAD, SSM_STATE)))


def kernel(x_prompt, x_sample, state_rwkv_shift, state_rwkv_wkv, state_ssm_conv, state_ssm, cache_k, cache_v, page_table, norm_mix, norm_ffn, norm_final, w_in_ab, rwkv_mu, rwkv_w0, rwkv_w2, rwkv_a0, rwkv_a2, rwkv_g2, rwkv_k_k, rwkv_k_a, rwkv_r_k, rwkv_ln_w, rwkv_ln_b, ssm_conv_w, ssm_conv_b, ssm_dt_bias, ssm_a_log, ssm_d, ssm_norm_w, w_out_ab, ffn_w_gate, ffn_w_up, ffn_w_down, attn_w_qkv, attn_w_o, moe_router, moe_w_gate, moe_w_up, moe_w_down):
    bp, tp, _ = x_prompt.shape
    db, ts, _ = x_sample.shape
    depth = norm_mix.shape[0]
    tm_p, tm_s = 512, db * ts
    xp = x_prompt.reshape(bp * tp, D_MODEL)
    xs = x_sample.reshape(db * ts, D_MODEL)
    slopes = _alibi_slopes()
    slf = jnp.asarray(_slope_features().reshape(ATT_KV_HEADS, ATT_GROUP, FEAT_BLOCKS))
    slope_rows = jnp.asarray(np.repeat(slopes, ts).reshape(ATT_HEADS * ts, 1))
    st = {n: [] for n in ("p_shift", "p_wkv", "p_conv", "p_ssm", "p_k", "p_v",
                          "s_shift", "s_wkv", "s_conv", "s_ssm", "s_k", "s_v")}
    for l in range(depth):
        i = l // 2
        g_mix, g_ffn = _row(norm_mix[l]), _row(norm_ffn[l])
        if l % 2 == 0:
            w_pack, rw, ssm = _even_params(i, w_in_ab, rwkv_mu, rwkv_w0, rwkv_w2, rwkv_a0, rwkv_a2, rwkv_g2,
                                           rwkv_k_k, rwkv_k_a, rwkv_r_k, rwkv_ln_w, rwkv_ln_b, ssm_conv_w,
                                           ssm_conv_b, ssm_dt_bias, ssm_a_log, ssm_d, ssm_norm_w)
            w_out = w_out_ab[i].astype(BF16)
            mix_p, wmix, sp = _mixer_ab(xp, bp, tp, g_mix, w_pack, rw, ssm, w_out,
                                        jnp.zeros((bp, RW_COLS), F32),
                                        jnp.zeros((bp, RW_HEADS, RW_HEAD, RW_HEAD), F32),
                                        jnp.zeros((bp, SSM_CONV - 1, SSM_CONV_DIM), F32),
                                        jnp.zeros((bp, SSM_HEADS, SSM_HEAD, SSM_STATE), F32), tm_p)
            mix_s, _, ss = _mixer_ab(xs, db, ts, g_mix, w_pack, rw, ssm, w_out, state_rwkv_shift[i],
                                     state_rwkv_wkv[i], state_ssm_conv[i], state_ssm[i], tm_s)
            for pre, new in (("p", sp), ("s", ss)):
                for name, val in zip(("shift", "wkv", "conv", "ssm"), new):
                    st[f"{pre}_{name}"].append(val)
            wg, wu, wd = ffn_w_gate[i].astype(BF16), ffn_w_up[i].astype(BF16), ffn_w_down[i].astype(BF16)
            xp = ffn_swiglu(xp, mix_p, wmix, g_ffn, wg, wu, wd, 2 * tm_p, 256)
            xs = ffn_swiglu(xs, mix_s, wmix, g_ffn, wg, wu, wd, tm_s, 256)
        else:
            w_qkv, w_o = attn_w_qkv[i].astype(BF16), attn_w_o[i].astype(BF16)
            qkv_p, kf, vf, means = qkv_proj(xp, g_mix, w_qkv, bp, tp, tm_p)
            nb = tp // MOBA_BLOCK
            means = means.reshape(bp, nb, ATT_KV_HEADS, ATT_HEAD).transpose(0, 2, 1, 3)
            k_p = qkv_p[:, ATT_Q_COLS:ATT_Q_COLS + ATT_KV_COLS].reshape(bp, tp, ATT_KV_HEADS, ATT_HEAD)
            v_p = qkv_p[:, ATT_Q_COLS + ATT_KV_COLS:].reshape(bp, tp, ATT_KV_HEADS, ATT_HEAD)
            o_p = moba_prompt(qkv_p.reshape(bp, tp, -1), kf, vf, means, slf)
            xp = matmul_residual(xp, [o_p.reshape(bp * tp, ATT_Q_COLS)], [w_o], tm_p)
            st["p_k"].append(k_p)
            st["p_v"].append(v_p)

            qkv_s = norm_matmul(xs, g_mix, w_qkv, tm_s, 512).reshape(db, ts, -1)
            kn = qkv_s[..., ATT_Q_COLS:ATT_Q_COLS + ATT_KV_COLS].reshape(db, ts, ATT_KV_HEADS, ATT_HEAD)
            vn = qkv_s[..., ATT_Q_COLS + ATT_KV_COLS:].reshape(db, ts, ATT_KV_HEADS, ATT_HEAD)
            o_s = moba_sample(page_table, _pages2d(cache_k[i]), _pages2d(cache_v[i]),
                              _block_diag_queries(qkv_s[..., :ATT_Q_COLS]), _new_rows(kn), _new_rows(vn),
                              slope_rows, ts)
            o_s = o_s.reshape(db, ATT_HEADS, ts, ATT_HEAD).transpose(0, 2, 1, 3).reshape(db * ts, ATT_Q_COLS)
            xs = matmul_residual(xs, [o_s], [w_o], tm_s)
            st["s_k"].append(kn)
            st["s_v"].append(vn)

            router = _cat3w(jnp.pad(moe_router[i], ((0, 0), (0, 128 - N_EXPERTS))), 0)
            wg, wu, wd = moe_w_gate[i].astype(BF16), moe_w_up[i].astype(BF16), moe_w_down[i].astype(BF16)
            final = l == depth - 1
            xp = moe_swiglu(xp, g_ffn, router, wg, wu, wd, _row(norm_final), 2 * tm_p, final)
            xs = moe_swiglu(xs, g_ffn, router, wg, wu, wd, _row(norm_final), tm_s, final)
    y_prompt = xp.reshape(bp, tp, D_MODEL)
    y_sample = xs.reshape(db, ts, D_MODEL)
    stack = lambda n: jnp.stack(st[n])
    return (y_prompt, y_sample,
            stack("p_shift"), stack("p_wkv"), stack("p_conv"), stack("p_ssm"), stack("p_k"), stack("p_v"),
            stack("s_shift"), stack("s_wkv"), stack("s_conv"), stack("s_ssm"), stack("s_k"), stack("s_v"))
```

```python
import functools

import numpy as np
import jax
import jax.numpy as jnp
from jax import lax
from jax.experimental import pallas as pl
from jax.experimental.pallas import tpu as pltpu

F32 = jnp.float32
BF16 = jnp.bfloat16
NEG_INF = float("-inf")

D_MODEL = 1024
NORM_EPS = 1e-6

RW_HEAD = 64
RW_HEADS = 8
RW_DIM = RW_HEADS * RW_HEAD
RW_LORA = 128
RW_GATE = 128
RW_COLS = 3 * RW_DIM + RW_LORA + RW_GATE
RW_LN_EPS = 64e-5

SSM_INNER = 1024
SSM_HEAD = 64
SSM_HEADS = 16
SSM_GROUPS = 2
SSM_STATE = 128
SSM_CONV = 4
SSM_CONV_DIM = SSM_INNER + 2 * SSM_GROUPS * SSM_STATE
SSD_CHUNK = 128

PROJ_RW = 0
PROJ_DT = RW_COLS
PROJ_Z = 2048
PROJ_XBC = 3072
PROJ_COLS = PROJ_XBC + SSM_CONV_DIM

ATT_HEADS = 16
ATT_KV_HEADS = 4
ATT_HEAD = 64
ATT_GROUP = ATT_HEADS // ATT_KV_HEADS
ATT_Q_COLS = ATT_HEADS * ATT_HEAD
ATT_KV_COLS = ATT_KV_HEADS * ATT_HEAD
MOBA_BLOCK = 256
MOBA_TOPK = 3
MASK_NEG = -1e30
PAGE_SIZE = 128
FEAT_BLOCKS = 32
FEAT_START = FEAT_BLOCKS
FEAT_OFFSET = FEAT_START + 3
SLOPE_PARTS = 3

N_EXPERTS = 8

V7X_VMEM_BYTES = 64 * 1024 * 1024
VMEM_LIMIT = V7X_VMEM_BYTES - 8 * 1024 * 1024


def _cparams(sem):
    return pltpu.CompilerParams(dimension_semantics=sem, vmem_limit_bytes=VMEM_LIMIT)


def _nt(a, b, precision=None):
    return lax.dot_general(a, b, (((1,), (1,)), ((), ())), precision=precision,
                           preferred_element_type=F32)


def _dot(a, b, precision=None):
    return jnp.dot(a, b, precision=precision, preferred_element_type=F32)


def _split2(x):
    hi = x.astype(BF16)
    return hi, (x - hi.astype(F32)).astype(BF16)


def _split3(x):
    hi = x.astype(BF16)
    r = x - hi.astype(F32)
    mid = r.astype(BF16)
    return hi, mid, (r - mid.astype(F32)).astype(BF16)


def _cat3(x, axis):
    hi, lo = _split2(x)
    return jnp.concatenate([hi, lo, hi], axis=axis)


def _cat3w(w, axis):
    hi, lo = _split2(w)
    return jnp.concatenate([hi, hi, lo], axis=axis)


def _rms(x, g):
    return x * lax.rsqrt(jnp.mean(x * x, axis=-1, keepdims=True) + NORM_EPS) * g


def _softplus(x):
    return jnp.maximum(x, 0.0) + jnp.log(1.0 + jnp.exp(-jnp.abs(x)))


def _sigmoid(x):
    return 1.0 / (1.0 + jnp.exp(-x))


def _silu(x):
    return x * _sigmoid(x)


def _nm_body(x_ref, g_ref, w_ref, o_ref, h_sc):
    @pl.when(pl.program_id(1) == 0)
    def _():
        h_sc[...] = _rms(x_ref[...], g_ref[...]).astype(BF16)

    o_ref[...] = _dot(h_sc[...], w_ref[...])


def norm_matmul(x, g, w, tm, tn):
    m, k = x.shape
    n = w.shape[1]
    return pl.pallas_call(
        _nm_body,
        grid=(m // tm, n // tn),
        in_specs=[pl.BlockSpec((tm, k), lambda i, j: (i, 0)),
                  pl.BlockSpec((1, k), lambda i, j: (0, 0)),
                  pl.BlockSpec((k, tn), lambda i, j: (0, j))],
        out_specs=pl.BlockSpec((tm, tn), lambda i, j: (i, j)),
        out_shape=jax.ShapeDtypeStruct((m, n), F32),
        scratch_shapes=[pltpu.VMEM((tm, k), BF16)],
        compiler_params=_cparams(("parallel", "arbitrary")),
    )(x, g, w)


def _mmres_body(*refs, n):
    res_ref, a_refs, w_refs, o_ref = refs[0], refs[1:1 + n], refs[1 + n:1 + 2 * n], refs[1 + 2 * n]
    acc = res_ref[...]
    for a_ref, w_ref in zip(a_refs, w_refs):
        acc = acc + _dot(a_ref[...].astype(BF16), w_ref[...])
    o_ref[...] = acc


def matmul_residual(res, acts, ws, tm):
    m, d = res.shape
    n = len(acts)
    in_specs = [pl.BlockSpec((tm, d), lambda i: (i, 0))]
    in_specs += [pl.BlockSpec((tm, a.shape[1]), lambda i: (i, 0)) for a in acts]
    in_specs += [pl.BlockSpec(w.shape, lambda i: (0, 0)) for w in ws]
    return pl.pallas_call(
        functools.partial(_mmres_body, n=n),
        grid=(m // tm,),
        in_specs=in_specs,
        out_specs=pl.BlockSpec((tm, d), lambda i: (i, 0)),
        out_shape=jax.ShapeDtypeStruct((m, d), F32),
        compiler_params=_cparams(("parallel",)),
    )(res, *acts, *ws)


def _ffn_body(*refs, n_mix):
    x_ref, mix_refs, wmix_refs = refs[0], refs[1:1 + n_mix], refs[1 + n_mix:1 + 2 * n_mix]
    g_ref, wg_ref, wu_ref, wd_ref, o_ref, x1_sc, h_sc, acc_sc = refs[1 + 2 * n_mix:]
    j = pl.program_id(1)

    @pl.when(j == 0)
    def _():
        x1 = x_ref[...]
        for a_ref, w_ref in zip(mix_refs, wmix_refs):
            x1 = x1 + _dot(a_ref[...].astype(BF16), w_ref[...])
        x1_sc[...] = x1
        h_sc[...] = _rms(x1, g_ref[...]).astype(BF16)
        acc_sc[...] = jnp.zeros_like(acc_sc)

    h = h_sc[...]
    act = _silu(_dot(h, wg_ref[...])) * _dot(h, wu_ref[...])
    acc_sc[...] += _dot(act.astype(BF16), wd_ref[...])

    @pl.when(j == pl.num_programs(1) - 1)
    def _():
        o_ref[...] = x1_sc[...] + acc_sc[...]


def ffn_swiglu(x, mixes, wmixes, g, wg, wu, wd, tm, tf):
    m, d = x.shape
    f = wg.shape[1]
    row_tile = lambda a: pl.BlockSpec((tm, a.shape[1]), lambda i, j: (i, 0))
    whole = lambda a: pl.BlockSpec(a.shape, lambda i, j: (0, 0))
    return pl.pallas_call(
        functools.partial(_ffn_body, n_mix=len(mixes)),
        grid=(m // tm, f // tf),
        in_specs=[row_tile(x)] + [row_tile(a) for a in mixes] + [whole(w) for w in wmixes]
        + [pl.BlockSpec((1, d), lambda i, j: (0, 0)),
           pl.BlockSpec((d, tf), lambda i, j: (0, j)),
           pl.BlockSpec((d, tf), lambda i, j: (0, j)),
           pl.BlockSpec((tf, d), lambda i, j: (j, 0))],
        out_specs=pl.BlockSpec((tm, d), lambda i, j: (i, 0)),
        out_shape=jax.ShapeDtypeStruct((m, d), F32),
        scratch_shapes=[pltpu.VMEM((tm, d), F32), pltpu.VMEM((tm, d), BF16), pltpu.VMEM((tm, d), F32)],
        compiler_params=_cparams(("parallel", "arbitrary")),
    )(x, *mixes, *wmixes, g, wg, wu, wd)


MOE_SUB = 512
MOE_CHUNK = 144


def _moe_body(x_ref, g_ref, r_ref, wg_ref, wu_ref, wd_ref, gf_ref, o_ref, h_sc, comb_sc, acc_sc, rk_sc, rkt_sc, *,
              final_norm, chunk):
    e = pl.program_id(1)
    tm = x_ref.shape[0]
    lane = lax.broadcasted_iota(jnp.int32, (tm, 128), 1)

    @pl.when(e == 0)
    def _():
        hf = _rms(x_ref[...], g_ref[...])
        h_sc[...] = hf.astype(BF16)
        lanef = lane.astype(F32)
        logits = jnp.where(lane < N_EXPERTS, _dot(_cat3(hf, 1), r_ref[...]), NEG_INF)
        m1 = jnp.max(logits, axis=1, keepdims=True)
        i1 = jnp.min(jnp.where(logits == m1, lanef, 128.0), axis=1, keepdims=True)
        mask1 = lanef == i1
        rest = jnp.where(mask1, NEG_INF, logits)
        m2 = jnp.max(rest, axis=1, keepdims=True)
        i2 = jnp.min(jnp.where(rest == m2, lanef, 128.0), axis=1, keepdims=True)
        mask2 = lanef == i2
        e2 = jnp.exp(m2 - m1)
        den = 1.0 + e2
        comb_sc[...] = jnp.where(mask1, 1.0 / den, 0.0) + jnp.where(mask2, e2 / den, 0.0)
        acc_sc[...] = jnp.zeros_like(acc_sc)
        if chunk:
            earlier = (lax.broadcasted_iota(jnp.int32, (MOE_SUB, MOE_SUB), 1)
                       < lax.broadcasted_iota(jnp.int32, (MOE_SUB, MOE_SUB), 0))
            earlier = jnp.where(earlier, 1.0, 0.0).astype(BF16)
            routed = jnp.where(mask1 | mask2, 1.0, 0.0)
            for st in range(tm // MOE_SUB):
                sl = slice(st * MOE_SUB, (st + 1) * MOE_SUB)
                rank = _dot(earlier, routed[sl].astype(BF16))
                rank = jnp.where(routed[sl] > 0.0, rank, -1.0)
                rk_sc[sl, :] = rank
                rkt_sc[st] = rank.T

    def expert(rows):
        act = _silu(_dot(rows, wg_ref[0])) * _dot(rows, wu_ref[0])
        return _dot(act.astype(BF16), wd_ref[0])

    this = lane == e
    if not chunk:
        c = jnp.sum(jnp.where(this, comb_sc[...], 0.0), axis=1, keepdims=True)
        acc_sc[...] += c * expert(h_sc[...])
    else:
        wide = -(-chunk // 128) * 128
        slot = lax.broadcasted_iota(jnp.int32, (chunk, MOE_SUB), 0).astype(F32)
        lane_w = lax.broadcasted_iota(jnp.int32, (MOE_SUB, wide), 1)
        slot_t = jnp.where(lane_w < chunk, lane_w.astype(F32), jnp.nan)
        sub8 = lax.broadcasted_iota(jnp.int32, (8, MOE_SUB), 0)
        this_sub = lax.broadcasted_iota(jnp.int32, (MOE_SUB, 128), 1) == e
        pad_rows = jnp.zeros((wide - chunk, x_ref.shape[1]), BF16)
        for st in range(tm // MOE_SUB):
            sl = slice(st * MOE_SUB, (st + 1) * MOE_SUB)
            rank_col = jnp.sum(jnp.where(this_sub, rk_sc[sl, :], 0.0), axis=1, keepdims=True)
            rank_row = jnp.sum(jnp.where(sub8 == e, rkt_sc[st, 0:8, :], 0.0), axis=0, keepdims=True)
            c = jnp.sum(jnp.where(this_sub, comb_sc[sl, :], 0.0), axis=1, keepdims=True)
            count = jnp.max(rank_row, axis=1, keepdims=True)[0, 0].astype(jnp.int32) + 1

            def one_chunk(k, carry):
                base = (k * chunk).astype(F32)
                gather = jnp.where(rank_row - base == slot, 1.0, 0.0).astype(BF16)
                y_hi, y_lo = _split2(expert(_dot(gather, h_sc[sl, :]).astype(BF16)))
                scatter = jnp.where(rank_col - base == slot_t, 1.0, 0.0).astype(BF16)
                back = (_dot(scatter, jnp.concatenate([y_hi, pad_rows], axis=0))
                        + _dot(scatter, jnp.concatenate([y_lo, pad_rows], axis=0)))
                acc_sc[sl, :] += c * back
                return carry

            lax.fori_loop(0, (count + chunk - 1) // chunk, one_chunk, 0)

    @pl.when(e == pl.num_programs(1) - 1)
    def _():
        y = x_ref[...] + acc_sc[...]
        if final_norm:
            y = _rms(y, gf_ref[...])
        o_ref[...] = y


def moe_swiglu(x, g, router, wg, wu, wd, gfinal, tm, final_norm):
    m, d = x.shape
    ne, _, fe = wg.shape
    chunk = MOE_CHUNK if tm % MOE_SUB == 0 else 0
    nsub = max(tm // MOE_SUB, 1)
    return pl.pallas_call(
        functools.partial(_moe_body, final_norm=final_norm, chunk=chunk),
        grid=(m // tm, ne),
        in_specs=[pl.BlockSpec((tm, d), lambda i, e: (i, 0)),
                  pl.BlockSpec((1, d), lambda i, e: (0, 0)),
                  pl.BlockSpec((3 * d, 128), lambda i, e: (0, 0)),
                  pl.BlockSpec((1, d, fe), lambda i, e: (e, 0, 0)),
                  pl.BlockSpec((1, d, fe), lambda i, e: (e, 0, 0)),
                  pl.BlockSpec((1, fe, d), lambda i, e: (e, 0, 0)),
                  pl.BlockSpec((1, d), lambda i, e: (0, 0))],
        out_specs=pl.BlockSpec((tm, d), lambda i, e: (i, 0)),
        out_shape=jax.ShapeDtypeStruct((m, d), F32),
        scratch_shapes=[pltpu.VMEM((tm, d), BF16), pltpu.VMEM((tm, 128), F32), pltpu.VMEM((tm, d), F32),
                        pltpu.VMEM((tm, 128), F32), pltpu.VMEM((nsub, 128, min(tm, MOE_SUB)), F32)],
        compiler_params=_cparams(("parallel", "arbitrary")),
    )(x, g, router, wg, wu, wd, gfinal)


RW_PAIRS = RW_HEADS // 2
RW_ROWS = RW_PAIRS * RW_HEAD


def _rwkv_body(p_ref, sh0_ref, s0_ref, mu_ref, w0_ref, a0_ref, kk_ref, ka_ref, rk_ref, lnw_ref, lnb_ref,
               wl_ref, g2_ref, b2_ref, pw_ref, y_ref, sfin_ref,
               s_sc, sk_sc, prev_sc, kk_sc, w_sc, b_sc, k_sc, r_sc, v_sc, g_sc, y_sc, *, nbb, tc, t_valid):
    c = pl.program_id(1)

    @pl.when(c == 0)
    def _():
        s_sc[...] = s0_ref[...]
        prev_sc[...] = sh0_ref[...]

    b2 = b2_ref[...]

    def head_sums(x, low=True):
        if low:
            hi, lo = _split2(x)
        else:
            hi = x.astype(BF16)
            lo = jnp.zeros_like(hi)
        return _dot(jnp.concatenate([hi, lo], axis=1), b2)

    def head_sums_wide(x):
        return jnp.concatenate([head_sums(x[:, q * 128:(q + 1) * 128]) for q in range(RW_PAIRS)], axis=1)

    for bi in range(nbb):
        p = p_ref[bi]
        row = lax.broadcasted_iota(jnp.int32, p.shape, 0)
        prev = jnp.where(row == 0, prev_sc[bi], pltpu.roll(p, 1, axis=0))
        prev_sc[bi] = p[tc - 1:tc, :]
        u = p + (prev - p) * mu_ref[...]
        r = u[:, 0:RW_DIM]
        k = u[:, RW_DIM:2 * RW_DIM]
        v = u[:, 2 * RW_DIM:3 * RW_DIM]
        lo = u[:, 3 * RW_DIM:3 * RW_DIM + RW_LORA]
        gd = u[:, 3 * RW_DIM + RW_LORA:]
        lane = lax.broadcasted_iota(jnp.int32, lo.shape, 1)
        lora = _dot(_cat3(jnp.where(lane < RW_LORA // 2, jnp.tanh(lo), lo), 1), wl_ref[...])
        wlog = -_softplus(-(w0_ref[...] + lora[:, :RW_DIM])) - 0.5
        a = _sigmoid(a0_ref[...] + lora[:, RW_DIM:])
        g_sc[bi] = _dot(_cat3(_sigmoid(gd), 1), g2_ref[...])
        kk = k * kk_ref[...]
        kk = kk / jnp.maximum(jnp.sqrt(head_sums_wide(kk * kk)), 1e-12)
        kk_sc[bi] = kk
        w_sc[bi] = jnp.exp(-jnp.exp(wlog))
        b_sc[bi] = kk * a
        k_sc[bi] = k * (1.0 + (a - 1.0) * ka_ref[...])
        r_sc[bi] = r
        v_sc[bi] = v

    rows = nbb * RW_ROWS
    vi = lax.broadcasted_iota(jnp.int32, (rows, 128), 0) % RW_HEAD
    li = lax.broadcasted_iota(jnp.int32, (rows, 128), 1)
    key_lane = li % RW_HEAD
    diag = (key_lane == vi).astype(F32)
    steps = 8 if t_valid % 8 == 0 else t_valid

    def rows_of(x8, j):
        return jnp.concatenate([jnp.broadcast_to(x8[bi][j:j + 1, q * 128:(q + 1) * 128], (RW_HEAD, 128))
                                for bi in range(nbb) for q in range(RW_PAIRS)], axis=0)

    pair_w = pw_ref[...]
    zero_half = jnp.zeros((rows, 128), BF16)

    def pair_sums(a, b):
        lhs = jnp.concatenate([a.astype(BF16), zero_half if b is None else b.astype(BF16)], axis=1)
        out = _dot(lhs, pair_w)
        return out[:, :128], out[:, 128:]

    first8 = [kk_sc[bi, pl.ds(0, 8), :] for bi in range(nbb)]
    sk_sc[...] = pair_sums(s_sc[...].reshape(rows, 128) * rows_of(first8, 0), None)[0]

    def group(t8, carry):
        t0 = pl.multiple_of(t8 * 8, 8)
        tn = pl.multiple_of(jnp.minimum(t0 + 8, tc - 8), 8)
        kk8, w8, b8, k8, r8, v8 = ([sc[bi, pl.ds(t0, 8), :] for bi in range(nbb)]
                                   for sc in (kk_sc, w_sc, b_sc, k_sc, r_sc, v_sc))
        kk_next = [kk_sc[bi, pl.ds(tn, 8), :] for bi in range(nbb)]
        s = s_sc[...].reshape(rows, 128)
        sk = sk_sc[...]
        ycols = jnp.zeros((rows, 128), F32)
        for j in range(steps):
            if j % 2 == 0:
                vcols = pair_sums(rows_of(v8, j) * diag, rows_of(v8, j + 1) * diag if j + 1 < steps else None)
            s = s * rows_of(w8, j) - sk * rows_of(b8, j) + vcols[j % 2] * rows_of(k8, j)
            kk_after = rows_of(kk8, j + 1) if j + 1 < 8 else rows_of(kk_next, 0)
            sk, yb = pair_sums(s * kk_after, s * rows_of(r8, j))
            ycols = jnp.where(key_lane == j, yb, ycols)
        s_sc[...] = s.reshape(nbb, RW_ROWS, 128)
        sk_sc[...] = sk
        h64 = RW_HEAD
        for bi in range(nbb):
            pieces = []
            for q in range(0, RW_PAIRS, 2):
                r0 = (bi * RW_PAIRS + q) * h64
                tr = ycols[r0:r0 + 2 * h64].T
                pieces += [tr[0:8, 0:h64], tr[h64:h64 + 8, 0:h64], tr[0:8, h64:], tr[h64:h64 + 8, h64:]]
            y_sc[bi, pl.ds(t0, 8), :] = jnp.concatenate(pieces, axis=1)
        return carry

    lax.fori_loop(0, -(-t_valid // 8), group, 0)

    inv = 1.0 / RW_HEAD
    for bi in range(nbb):
        y = y_sc[bi]
        yc = y - head_sums_wide(y) * inv
        var = head_sums_wide(yc * yc) * inv
        yn = yc * lax.rsqrt(var + RW_LN_EPS) * lnw_ref[...] + lnb_ref[...]
        bonus = head_sums_wide(r_sc[bi] * k_sc[bi] * rk_ref[...]) * v_sc[bi]
        y_ref[bi] = ((yn + bonus) * g_sc[bi]).astype(y_ref.dtype)

    @pl.when(c == pl.num_programs(1) - 1)
    def _():
        sfin_ref[...] = s_sc[...]


def _wkv_to_rows(s):
    b = s.shape[0]
    return s.reshape(b, RW_PAIRS, 2, RW_HEAD, RW_HEAD).transpose(0, 1, 3, 2, 4).reshape(b, RW_ROWS, 128)


def _rows_to_wkv(s):
    b = s.shape[0]
    return s.reshape(b, RW_PAIRS, RW_HEAD, 2, RW_HEAD).transpose(0, 1, 3, 2, 4).reshape(b, RW_HEADS, RW_HEAD, RW_HEAD)


def rwkv7(proj, shift0, wkv0, prm, nbb, tc, t_valid):
    b, t, _ = proj.shape
    vec = lambda n: pl.BlockSpec((1, n), lambda i, c: (0, 0))
    full = lambda a: pl.BlockSpec(a.shape, lambda i, c: (0,) * a.ndim)
    seq = pltpu.VMEM((nbb, tc, RW_DIM), F32)
    y, s_fin = pl.pallas_call(
        functools.partial(_rwkv_body, nbb=nbb, tc=tc, t_valid=t_valid),
        grid=(b // nbb, t // tc),
        in_specs=[pl.BlockSpec((nbb, tc, RW_COLS), lambda i, c: (i, c, 0)),
                  pl.BlockSpec((nbb, 1, RW_COLS), lambda i, c: (i, 0, 0)),
                  pl.BlockSpec((nbb, RW_ROWS, 128), lambda i, c: (i, 0, 0)),
                  vec(RW_COLS), vec(RW_DIM), vec(RW_DIM), vec(RW_DIM), vec(RW_DIM), vec(RW_DIM), vec(RW_DIM),
                  vec(RW_DIM), full(prm["wl"]), full(prm["g2"]), full(prm["bones2"]), full(prm["pair_w"])],
        out_specs=[pl.BlockSpec((nbb, tc, RW_DIM), lambda i, c: (i, c, 0)),
                   pl.BlockSpec((nbb, RW_ROWS, 128), lambda i, c: (i, 0, 0))],
        out_shape=[jax.ShapeDtypeStruct((b, t, RW_DIM), BF16),
                   jax.ShapeDtypeStruct((b, RW_ROWS, 128), F32)],
        scratch_shapes=[pltpu.VMEM((nbb, RW_ROWS, 128), F32), pltpu.VMEM((nbb * RW_ROWS, 128), F32),
                        pltpu.VMEM((nbb, 1, RW_COLS), F32),
                        seq, seq, seq, seq, seq, seq, seq, seq],
        compiler_params=_cparams(("parallel", "arbitrary")),
    )(proj, shift0, _wkv_to_rows(wkv0), prm["mu"], prm["w0"], prm["a0"], prm["k_k"], prm["k_a"], prm["r_k"],
      prm["ln_w"], prm["ln_b"], prm["wl"], prm["g2"], prm["bones2"], prm["pair_w"])
    return y, _rows_to_wkv(s_fin)


def _ssd_body(z_ref, xbc_ref, dt_ref, conv0_ref, h0_ref, cw_ref, cb_ref, dtb_ref, alog_ref, dsk_ref, nw_ref,
              e_ref, y_ref, hfin_ref, h_sc, ext_sc, *, t_valid):
    c = pl.program_id(1)
    ln = SSD_CHUNK

    @pl.when(c == 0)
    def _():
        h_sc[...] = h0_ref[0]
        ext_sc[0:8, :] = conv0_ref[0]

    xbc = xbc_ref[0]
    ext_sc[8:8 + ln, :] = xbc
    cw = cw_ref[...]
    conv = cb_ref[...] + ext_sc[5:5 + ln, :] * cw[0:1]
    conv = conv + ext_sc[6:6 + ln, :] * cw[1:2]
    conv = conv + ext_sc[7:7 + ln, :] * cw[2:3]
    conv = conv + xbc * cw[3:4]
    ext_sc[0:8, :] = xbc[ln - 8:ln, :]
    act = _silu(conv)
    xs = act[:, :SSM_INNER]
    bm = act[:, SSM_INNER:SSM_INNER + SSM_GROUPS * SSM_STATE].astype(BF16)
    cm = act[:, SSM_INNER + SSM_GROUPS * SSM_STATE:].astype(BF16)

    lane = lax.broadcasted_iota(jnp.int32, (ln, 128), 1)
    rowi = lax.broadcasted_iota(jnp.int32, (ln, 128), 0)
    dt = _softplus(dt_ref[0] + dtb_ref[...])
    if t_valid < ln:
        dt = jnp.where(rowi < t_valid, dt, 0.0)
    a = dt * jnp.where(lane[0:1] < SSM_HEADS, -jnp.exp(alog_ref[...]), 0.0)
    causal = lane <= rowi
    tri = jnp.where(causal, 1.0, 0.0).astype(BF16)
    tri3 = jnp.concatenate([tri, tri, tri], axis=1)
    acum = _dot(tri3, jnp.concatenate(_split3(a), axis=0))
    acum_t = _nt(jnp.concatenate(_split3(a.T), axis=1), tri3)
    e3 = e_ref[...]
    dt_full = _dot(jnp.concatenate(_split3(dt), axis=1), e3)
    ac_full = _dot(jnp.concatenate(_split3(acum), axis=1), e3)
    xdt = xs * dt_full
    xw = xdt * jnp.exp(ac_full[ln - 1:ln, :] - ac_full)
    eac = jnp.exp(ac_full)
    cd = jnp.broadcast_to(jnp.exp(acum_t[:, ln - 1:ln]), (128, 128))
    cd_rows = jnp.concatenate([jnp.broadcast_to(cd[h:h + 1, :], (SSM_HEAD, SSM_STATE)) for h in range(SSM_HEADS)],
                              axis=0)

    ys = []
    for g in range(SSM_GROUPS):
        bm_g = bm[:, g * SSM_STATE:(g + 1) * SSM_STATE]
        cm_g = cm[:, g * SSM_STATE:(g + 1) * SSM_STATE]
        cbm = _nt(cm_g, bm_g)
        pairs = SSM_HEADS // SSM_GROUPS // 2
        for q in range(g * pairs, (g + 1) * pairs):
            sl = slice(q * 128, (q + 1) * 128)
            xdt_p = xdt[:, sl].astype(BF16)
            yd = []
            for h in (2 * q, 2 * q + 1):
                seg = jnp.where(causal, acum[:, h:h + 1] - acum_t[h:h + 1, :], NEG_INF)
                yd.append(_dot((cbm * jnp.exp(seg)).astype(BF16), xdt_p))
            hp = h_sc[sl, :]
            y_off = _nt(cm_g, hp.astype(BF16)) * eac[:, sl]
            st = _dot(xw[:, sl].T.astype(BF16), bm_g)
            h_sc[sl, :] = cd_rows[sl, :] * hp + st
            ys.append(jnp.where(lane < SSM_HEAD, yd[0], yd[1]) + y_off)
    y = jnp.concatenate(ys, axis=1) + dsk_ref[...] * xs
    y = y * _silu(z_ref[0])
    gw = SSM_INNER // SSM_GROUPS
    outs = []
    for g in range(SSM_GROUPS):
        yg = y[:, g * gw:(g + 1) * gw]
        outs.append(yg * lax.rsqrt(jnp.mean(yg * yg, axis=1, keepdims=True) + NORM_EPS))
    y_ref[0] = (jnp.concatenate(outs, axis=1) * nw_ref[...]).astype(y_ref.dtype)

    @pl.when(c == pl.num_programs(1) - 1)
    def _():
        hfin_ref[0] = h_sc[...]


def mamba2(proj, conv0, ssm0, prm, t_valid):
    b, t, _ = proj.shape
    ln = SSD_CHUNK
    vec = lambda n: pl.BlockSpec((1, n), lambda i, c: (0, 0))
    full = lambda a: pl.BlockSpec(a.shape, lambda i, c: (0,) * a.ndim)
    return pl.pallas_call(
        functools.partial(_ssd_body, t_valid=t_valid),
        grid=(b, t // ln),
        in_specs=[pl.BlockSpec((1, ln, SSM_INNER), lambda i, c: (i, c, PROJ_Z // SSM_INNER)),
                  pl.BlockSpec((1, ln, SSM_CONV_DIM), lambda i, c: (i, c, PROJ_XBC // SSM_CONV_DIM)),
                  pl.BlockSpec((1, ln, 128), lambda i, c: (i, c, PROJ_DT // 128)),
                  pl.BlockSpec((1, 8, SSM_CONV_DIM), lambda i, c: (i, 0, 0)),
                  pl.BlockSpec((1, SSM_INNER, SSM_STATE), lambda i, c: (i, 0, 0)),
                  full(prm["conv_w"]), vec(SSM_CONV_DIM), vec(128), vec(128), vec(SSM_INNER), vec(SSM_INNER),
                  full(prm["expand"])],
        out_specs=[pl.BlockSpec((1, ln, SSM_INNER), lambda i, c: (i, c, 0)),
                   pl.BlockSpec((1, SSM_INNER, SSM_STATE), lambda i, c: (i, 0, 0))],
        out_shape=[jax.ShapeDtypeStruct((b, t, SSM_INNER), BF16),
                   jax.ShapeDtypeStruct((b, SSM_INNER, SSM_STATE), F32)],
        scratch_shapes=[pltpu.VMEM((SSM_INNER, SSM_STATE), F32), pltpu.VMEM((8 + ln, SSM_CONV_DIM), F32)],
        compiler_params=_cparams(("parallel", "arbitrary")),
    )(proj, proj, proj, conv0, ssm0, prm["conv_w"], prm["conv_b"], prm["dt_bias"], prm["a_log"],
      prm["d_full"], prm["norm_w"], prm["expand"])


def _alibi_slopes():
    return (2.0 ** (-8.0 * np.arange(1, ATT_HEADS + 1) / ATT_HEADS)).astype(np.float32)


def _slope_features():
    s = _alibi_slopes()
    bf = lambda a: a.astype(jnp.bfloat16).astype(np.float32)
    hi = bf(s)
    mid = bf(s - hi)
    lo = bf(s - hi - mid)
    out = np.zeros((ATT_HEADS, FEAT_BLOCKS), np.float32)
    for i, piece in enumerate((hi, mid, lo)):
        out[:, i] = piece
        out[:, SLOPE_PARTS + i] = piece
    return out


def _top3_threshold(gate):
    v = gate
    for _ in range(MOBA_TOPK - 1):
        v = jnp.where(v == jnp.max(v, axis=1, keepdims=True), NEG_INF, v)
    return jnp.max(v, axis=1, keepdims=True)


def _qkv_body(x_ref, g_ref, w_ref, qkv_ref, kf_ref, vf_ref, mean_ref, *, tiles_per_seq):
    tm = x_ref.shape[0]
    qkv = _dot(_rms(x_ref[...], g_ref[...]).astype(BF16), w_ref[...])
    qkv_ref[...] = qkv
    for r in range(tm // MOBA_BLOCK):
        mean_ref[r] = jnp.mean(qkv[r * MOBA_BLOCK:(r + 1) * MOBA_BLOCK, ATT_Q_COLS:ATT_Q_COLS + ATT_KV_COLS],
                               axis=0, keepdims=True)
    pos = (pl.program_id(0) % tiles_per_seq) * tm + lax.broadcasted_iota(jnp.int32, (tm, ATT_HEAD), 0)
    lane = lax.broadcasted_iota(jnp.int32, (tm, ATT_HEAD), 1)
    blk_id, off = pos // MOBA_BLOCK, pos % MOBA_BLOCK
    kfeat = jnp.where(lane == blk_id, 1, 0)
    kfeat = jnp.where((lane >= FEAT_START) & (lane < FEAT_START + SLOPE_PARTS), blk_id * MOBA_BLOCK, kfeat)
    kfeat = jnp.where((lane >= FEAT_OFFSET) & (lane < FEAT_OFFSET + SLOPE_PARTS), off, kfeat)
    kfeat = kfeat.astype(F32).astype(BF16)
    vfeat = jnp.where(lane == 0, 1.0, 0.0).astype(BF16)
    for g in range(ATT_KV_HEADS):
        k0 = ATT_Q_COLS + g * ATT_HEAD
        v0 = ATT_Q_COLS + ATT_KV_COLS + g * ATT_HEAD
        kf_ref[0, g] = jnp.concatenate([qkv[:, k0:k0 + ATT_HEAD].astype(BF16), kfeat], axis=1)
        vf_ref[0, g] = jnp.concatenate([qkv[:, v0:v0 + ATT_HEAD].astype(BF16), vfeat], axis=1)


def qkv_proj(x, g, w, b, t, tm):
    m, d = x.shape
    n = w.shape[1]
    tps = t // tm
    ext = jax.ShapeDtypeStruct((b, ATT_KV_HEADS, t, 128), BF16)
    ext_spec = pl.BlockSpec((1, ATT_KV_HEADS, tm, 128), lambda i: (i // tps, 0, i % tps, 0))
    return pl.pallas_call(
        functools.partial(_qkv_body, tiles_per_seq=tps),
        grid=(m // tm,),
        in_specs=[pl.BlockSpec((tm, d), lambda i: (i, 0)), pl.BlockSpec((1, d), lambda i: (0, 0)),
                  pl.BlockSpec((d, n), lambda i: (0, 0))],
        out_specs=[pl.BlockSpec((tm, n), lambda i: (i, 0)), ext_spec, ext_spec,
                   pl.BlockSpec((tm // MOBA_BLOCK, 1, ATT_KV_COLS), lambda i: (i, 0, 0))],
        out_shape=[jax.ShapeDtypeStruct((m, n), F32), ext, ext,
                   jax.ShapeDtypeStruct((m // MOBA_BLOCK, 1, ATT_KV_COLS), F32)],
        compiler_params=_cparams(("parallel",)),
    )(x, g, w)


def _moba_prompt_body(q_ref, kf_ref, vf_ref, mean_ref, slf_ref, o_ref, qf_sc, m_sc, acc_sc, sa0_sc, sa1_sc, sb0_sc, sb1_sc):
    qb = pl.program_id(2)
    blk = MOBA_BLOCK
    rows = ATT_GROUP * blk
    scale = ATT_HEAD ** -0.5

    q4 = q_ref[0]
    qg = jnp.concatenate([q4[:, hh * ATT_HEAD:(hh + 1) * ATT_HEAD] for hh in range(ATT_GROUP)], axis=0)
    past_blk = lax.broadcasted_iota(jnp.int32, (128, rows), 0) < qb
    gate = jnp.where(past_blk, _nt(_cat3w(mean_ref[0, 0], 1), _cat3(qg, 1)), NEG_INF)
    v = gate
    for _ in range(MOBA_TOPK - 1):
        v = jnp.where(v == jnp.max(v, axis=0, keepdims=True), NEG_INF, v)
    thr = jnp.max(v, axis=0, keepdims=True)
    selneg = jnp.where(past_blk & (gate < thr), MASK_NEG, 0.0).T[:, :FEAT_BLOCKS]
    slf = jnp.concatenate([jnp.broadcast_to(slf_ref[0, hh:hh + 1, :], (blk, FEAT_BLOCKS))
                           for hh in range(ATT_GROUP)], axis=0)
    qf_sc[...] = jnp.concatenate([(qg * scale).astype(BF16), selneg.astype(BF16), slf.astype(BF16)], axis=1)

    def kv_tiles(j):
        start = pl.multiple_of(j * blk, blk)
        return kf_ref[0, 0, pl.ds(start, blk), :], vf_ref[0, 0, pl.ds(start, blk), :]

    qi = lax.broadcasted_iota(jnp.int32, (rows, blk), 0) % blk
    ki = lax.broadcasted_iota(jnp.int32, (rows, blk), 1)
    kf, vf = kv_tiles(qb)
    s = jnp.where(ki <= qi, _nt(qf_sc[...], kf), NEG_INF)
    m = jnp.max(s, axis=1, keepdims=True)
    m_sc[...] = jnp.broadcast_to(m, (rows, 128))
    acc_sc[...] = _dot(jnp.exp(s - m).astype(BF16), vf)

    def scores(j):
        return _nt(qf_sc[...], kv_tiles(jnp.minimum(j, qb))[0])

    def fold(blocks, s_cur, s_next=()):
        for ref, j in zip(s_next, (blocks[-1] + 1, blocks[-1] + 2)):
            ref[...] = scores(j)
        ss = [ref[...] for ref in s_cur]
        m_old = m_sc[...]
        m_new = m_old
        for s in ss:
            m_new = jnp.maximum(m_new, jnp.max(s, axis=1, keepdims=True))
        m2 = jnp.concatenate([m_new, m_new], axis=1)
        acc = acc_sc[...] * jnp.exp(m_old - m_new)
        for s, j in zip(ss, blocks):
            acc = acc + _dot(jnp.exp(s - m2).astype(BF16), kv_tiles(j)[1])
        acc_sc[...] = acc
        m_sc[...] = m_new

    set_a, set_b = (sa0_sc, sa1_sc), (sb0_sc, sb1_sc)
    sa0_sc[...] = scores(0)
    sa1_sc[...] = scores(1)

    def quad(i, carry):
        fold((4 * i, 4 * i + 1), set_a, set_b)
        fold((4 * i + 2, 4 * i + 3), set_b, set_a)
        return carry

    lax.fori_loop(0, qb // 4, quad, 0)
    done = qb // 4 * 4

    @pl.when(qb % 4 >= 2)
    def _():
        fold((done, done + 1), set_a)

    @pl.when(qb % 4 == 1)
    def _():
        fold((done,), set_a[:1])

    @pl.when(qb % 4 == 3)
    def _():
        sb0_sc[...] = scores(done + 2)
        fold((done + 2,), set_b[:1])

    acc = acc_sc[...]
    out = acc[:, :ATT_HEAD] / acc[:, ATT_HEAD:ATT_HEAD + 1]
    o_ref[0] = jnp.concatenate([out[hh * blk:(hh + 1) * blk] for hh in range(ATT_GROUP)], axis=1).astype(o_ref.dtype)


def moba_prompt(qkv, kf, vf, means, slf):
    b, t, _ = qkv.shape
    nb = t // MOBA_BLOCK
    assert nb <= FEAT_BLOCKS
    nbl = 128
    means = jnp.pad(means, ((0, 0), (0, 0), (0, nbl - nb), (0, 0)))
    rows = ATT_GROUP * MOBA_BLOCK
    gw = ATT_GROUP * ATT_HEAD
    return pl.pallas_call(
        _moba_prompt_body,
        grid=(b, ATT_KV_HEADS, nb),
        in_specs=[pl.BlockSpec((1, MOBA_BLOCK, gw), lambda i, g, j: (i, j, g)),
                  pl.BlockSpec((1, 1, t, 128), lambda i, g, j: (i, g, 0, 0)),
                  pl.BlockSpec((1, 1, t, 128), lambda i, g, j: (i, g, 0, 0)),
                  pl.BlockSpec((1, 1, nbl, ATT_HEAD), lambda i, g, j: (i, g, 0, 0)),
                  pl.BlockSpec((1, ATT_GROUP, FEAT_BLOCKS), lambda i, g, j: (g, 0, 0))],
        out_specs=pl.BlockSpec((1, MOBA_BLOCK, gw), lambda i, g, j: (i, j, g)),
        out_shape=jax.ShapeDtypeStruct((b, t, ATT_Q_COLS), BF16),
        scratch_shapes=[pltpu.VMEM((rows, 128), BF16), pltpu.VMEM((rows, 128), F32), pltpu.VMEM((rows, 128), F32),
                        *([pltpu.VMEM((rows, MOBA_BLOCK), F32)] * 4)],
        compiler_params=_cparams(("parallel", "parallel", "arbitrary")),
    )(qkv, kf, vf, means, slf)


def _moba_sample_body(pt_ref, *refs, nbp, bps, nq):
    npg = 2 * bps
    k_refs, v_refs = refs[:npg], refs[npg:2 * npg]
    q_ref, kn_ref, vn_ref, slope_ref, o_ref, mean_sc, m_sc, l_sc, acc_sc = refs[2 * npg:]
    j = pl.program_id(1)
    blk = MOBA_BLOCK
    rows = ATT_HEADS * nq
    grows = ATT_GROUP * nq
    past = nbp * blk
    scale = ATT_HEAD ** -0.5
    lane = lax.broadcasted_iota(jnp.int32, (rows, 128), 1)
    qi = (lax.broadcasted_iota(jnp.int32, (rows, 1), 0) % nq).astype(F32)
    qf = q_ref[0]
    qs = (qf * scale).astype(BF16)
    slope = slope_ref[...]

    @pl.when(j == 0)
    def _():
        mean_sc[...] = jnp.zeros_like(mean_sc)
        m_sc[...] = jnp.full(m_sc.shape, NEG_INF, F32)
        l_sc[...] = jnp.zeros_like(l_sc)

    kcat = jnp.concatenate([ref[0] for ref in k_refs], axis=0)
    off = lax.broadcasted_iota(jnp.int32, (1, bps * blk), 1).astype(F32)
    sc = _nt(qs, kcat.astype(BF16)) - slope * ((past - j * bps * blk).astype(F32) + qi - off)
    m_new, l_new = m_sc[...], l_sc[...]
    for s in range(bps):
        bi = j * bps + s
        ksum = jnp.sum(kcat[s * blk:(s + 1) * blk], axis=0, keepdims=True)
        mean_sc[pl.ds(bi, 1), :] = ksum * (1.0 / blk)
        sc_b = sc[:, s * blk:(s + 1) * blk]
        m = jnp.max(sc_b, axis=1, keepdims=True)
        p = jnp.exp(sc_b - m)
        m_new = jnp.where(lane == bi, m, m_new)
        l_new = jnp.where(lane == bi, jnp.sum(p, axis=1, keepdims=True), l_new)
        vblk = jnp.concatenate([v_refs[2 * s][0], v_refs[2 * s + 1][0]], axis=0)
        acc_sc[bi] = _dot(p.astype(BF16), vblk.astype(BF16))
    m_sc[...] = m_new
    l_sc[...] = l_new

    @pl.when(j == pl.num_programs(1) - 1)
    def _():
        gate = jnp.where(lane < nbp, _nt(_cat3(qf, 1), _cat3w(mean_sc[...], 1)), NEG_INF)
        sel = gate >= _top3_threshold(gate)
        ki = lax.broadcasted_iota(jnp.int32, (rows, 8), 1).astype(F32)
        s_own = jnp.where(ki <= qi, _nt(qs, kn_ref[0].astype(BF16)) - slope * (qi - ki), NEG_INF)
        m_all = jnp.where(sel, m_sc[...], NEG_INF)
        mx = jnp.maximum(jnp.max(m_all, axis=1, keepdims=True), jnp.max(s_own, axis=1, keepdims=True))
        w = jnp.where(sel, jnp.exp(m_all - mx), 0.0)
        p_own = jnp.exp(s_own - mx)
        den = jnp.sum(w * l_sc[...], axis=1, keepdims=True) + jnp.sum(p_own, axis=1, keepdims=True)
        acc = _dot(p_own.astype(BF16), vn_ref[0].astype(BF16))
        for n in range(nbp):
            acc = acc + w[:, n:n + 1] * acc_sc[n]
        out = acc / den
        for g in range(ATT_KV_HEADS):
            o_ref[0, g] = out[g * grows:(g + 1) * grows, g * ATT_HEAD:(g + 1) * ATT_HEAD]


SAMPLE_BLOCKS_PER_STEP = 8


def _pages2d(cache):
    return cache.reshape(cache.shape[0], PAGE_SIZE, ATT_KV_COLS)


def _block_diag_queries(q):
    db, nq, _ = q.shape
    qh = q.reshape(db, nq, ATT_KV_HEADS, ATT_GROUP, ATT_HEAD).transpose(0, 2, 3, 1, 4)
    eye = jnp.eye(ATT_KV_HEADS, dtype=q.dtype)
    return (qh[:, :, :, :, None, :] * eye[None, :, None, None, :, None]).reshape(db, ATT_HEADS * nq, ATT_KV_COLS)


def _new_rows(a):
    db, nq = a.shape[:2]
    return jnp.pad(a.reshape(db, nq, ATT_KV_COLS), ((0, 0), (0, 8 - nq), (0, 0)))


def moba_sample(page_table, cache_k, cache_v, q_bd, k_new, v_new, slope_rows, nq):
    db, n_pages = page_table.shape
    nbp = n_pages * PAGE_SIZE // MOBA_BLOCK
    bps = min(SAMPLE_BLOCKS_PER_STEP, nbp)
    rows = ATT_HEADS * nq
    grows = ATT_GROUP * nq

    def page_spec(slot):
        return pl.BlockSpec((1, PAGE_SIZE, ATT_KV_COLS),
                            lambda b, j, pt: (pt[b * n_pages + j * 2 * bps + slot], 0, 0))

    small = lambda shp: pl.BlockSpec((1,) + shp, lambda b, j, pt: (b,) + (0,) * len(shp))
    grid_spec = pltpu.PrefetchScalarGridSpec(
        num_scalar_prefetch=1,
        grid=(db, nbp // bps),
        in_specs=[page_spec(s) for s in range(2 * bps)] * 2
        + [small((rows, ATT_KV_COLS)), small((8, ATT_KV_COLS)), small((8, ATT_KV_COLS)),
           pl.BlockSpec((rows, 1), lambda b, j, pt: (0, 0))],
        out_specs=small((ATT_KV_HEADS, grows, ATT_HEAD)),
        scratch_shapes=[pltpu.VMEM((128, ATT_KV_COLS), F32), pltpu.VMEM((rows, 128), F32),
                        pltpu.VMEM((rows, 128), F32), pltpu.VMEM((nbp, rows, ATT_KV_COLS), F32)],
    )
    return pl.pallas_call(
        functools.partial(_moba_sample_body, nbp=nbp, bps=bps, nq=nq),
        grid_spec=grid_spec,
        out_shape=jax.ShapeDtypeStruct((db, ATT_KV_HEADS, grows, ATT_HEAD), F32),
        compiler_params=_cparams(("parallel", "arbitrary")),
    )(page_table.reshape(-1), *([cache_k] * (2 * bps)), *([cache_v] * (2 * bps)), q_bd, k_new, v_new, slope_rows)


def _row(a):
    return a.reshape(1, -1).astype(F32)


def _even_params(i, w_in_ab, rwkv_mu, rwkv_w0, rwkv_w2, rwkv_a0, rwkv_a2, rwkv_g2, rwkv_k_k, rwkv_k_a, rwkv_r_k,
                 rwkv_ln_w, rwkv_ln_b, ssm_conv_w, ssm_conv_b, ssm_dt_bias, ssm_a_log, ssm_d, ssm_norm_w):
    w = w_in_ab[i]
    ssm0 = RW_COLS
    zeros = lambda n: jnp.zeros((D_MODEL, n), w.dtype)
    w_pack = jnp.concatenate([
        w[:, :RW_COLS],
        w[:, ssm0 + SSM_INNER + SSM_CONV_DIM:], zeros(PROJ_Z - PROJ_DT - SSM_HEADS),
        w[:, ssm0:ssm0 + SSM_INNER],
        w[:, ssm0 + SSM_INNER:ssm0 + SSM_INNER + SSM_CONV_DIM]], axis=1).astype(BF16)
    half = RW_LORA // 2
    zl = jnp.zeros((half, RW_DIM), F32)
    wl = jnp.concatenate([jnp.concatenate([rwkv_w2[i], zl], axis=1),
                          jnp.concatenate([zl, rwkv_a2[i]], axis=1)], axis=0)
    half_of = np.arange(128) // RW_HEAD
    bones2 = np.tile((half_of[:, None] == half_of[None, :]).astype(np.float32), (2, 1))
    rw = dict(mu=_row(rwkv_mu[i]), w0=_row(rwkv_w0[i]), a0=_row(rwkv_a0[i]), k_k=_row(rwkv_k_k[i]),
              k_a=_row(rwkv_k_a[i]), r_k=_row(rwkv_r_k[i]), ln_w=_row(rwkv_ln_w[i]), ln_b=_row(rwkv_ln_b[i]),
              wl=_cat3w(wl, 0), g2=_cat3w(rwkv_g2[i], 0), bones2=jnp.asarray(bones2, BF16),
              pair_w=jnp.asarray(np.kron(np.eye(4, dtype=np.float32), np.ones((RW_HEAD, RW_HEAD), np.float32)), BF16))
    pad128 = lambda a: jnp.pad(_row(a), ((0, 0), (0, 128 - a.shape[-1])))
    expand = (np.arange(128)[:, None] == (np.arange(SSM_INNER) // SSM_HEAD)[None, :]).astype(np.float32)
    ssm = dict(conv_w=ssm_conv_w[i], conv_b=_row(ssm_conv_b[i]), dt_bias=pad128(ssm_dt_bias[i]),
               a_log=pad128(ssm_a_log[i]), d_full=_row(jnp.repeat(ssm_d[i], SSM_HEAD)), norm_w=_row(ssm_norm_w[i]),
               expand=jnp.asarray(np.tile(expand, (3, 1)), BF16))
    return w_pack, rw, ssm


def _mixer_ab(x, b, t, norm_g, w_pack, rw, ssm, w_out, shift0, wkv0, conv0, ssm0, tm):
    tm_in = 2 * tm if (b * t) % (2 * tm) == 0 else tm
    proj = norm_matmul(x, norm_g, w_pack, tm_in, SSM_CONV_DIM).reshape(b, t, PROJ_COLS)
    shift_new = proj[:, t - 1, :RW_COLS]
    conv_new = proj[:, t - (SSM_CONV - 1):, PROJ_XBC:]
    if t % SSD_CHUNK == 0:
        proj_rw, proj_ssm, tc = proj, proj, SSD_CHUNK
    else:
        proj_rw = jnp.pad(proj, ((0, 0), (0, 8 - t), (0, 0)))
        proj_ssm = jnp.pad(proj, ((0, 0), (0, SSD_CHUNK - t), (0, 0)))
        tc = 8
    nbb = next(n for n in (4, 2, 1) if b % n == 0)
    y_rw, wkv_new = rwkv7(proj_rw, shift0.reshape(b, 1, RW_COLS), wkv0, rw, nbb, tc, min(t, tc))
    conv0p = jnp.pad(conv0, ((0, 0), (8 - (SSM_CONV - 1), 0), (0, 0)))
    y_ssm, ssm_new = mamba2(proj_ssm, conv0p, ssm0.reshape(b, SSM_INNER, SSM_STATE), ssm, min(t, SSD_CHUNK))
    mixes = [y_rw[:, :t].reshape(b * t, RW_DIM), y_ssm[:, :t].reshape(b * t, SSM_INNER)]
    return (mixes, [w_out[:RW_DIM], w_out[RW_DIM:]],
            (shift_new, wkv_new, conv_new, ssm_new.reshape(b, SSM_HEADS, SSM_HEAD, SSM_STATE)))


def kernel(x_prompt, x_sample, state_rwkv_shift, state_rwkv_wkv, state_ssm_conv, state_ssm, cache_k, cache_v, page_table, norm_mix, norm_ffn, norm_final, w_in_ab, rwkv_mu, rwkv_w0, rwkv_w2, rwkv_a0, rwkv_a2, rwkv_g2, rwkv_k_k, rwkv_k_a, rwkv_r_k, rwkv_ln_w, rwkv_ln_b, ssm_conv_w, ssm_conv_b, ssm_dt_bias, ssm_a_log, ssm_d, ssm_norm_w, w_out_ab, ffn_w_gate, ffn_w_up, ffn_w_down, attn_w_qkv, attn_w_o, moe_router, moe_w_gate, moe_w_up, moe_w_down):
    bp, tp, _ = x_prompt.shape
    db, ts, _ = x_sample.shape
    depth = norm_mix.shape[0]
    tm_p, tm_s = 512, db * ts
    xp = x_prompt.reshape(bp * tp, D_MODEL)
    xs = x_sample.reshape(db * ts, D_MODEL)
    slopes = _alibi_slopes()
    slf = jnp.asarray(_slope_features().reshape(ATT_KV_HEADS, ATT_GROUP, FEAT_BLOCKS))
    slope_rows = jnp.asarray(np.repeat(slopes, ts).reshape(ATT_HEADS * ts, 1))
    st = {n: [] for n in ("p_shift", "p_wkv", "p_conv", "p_ssm", "p_k", "p_v",
                          "s_shift", "s_wkv", "s_conv", "s_ssm", "s_k", "s_v")}
    for l in range(depth):
        i = l // 2
        g_mix, g_ffn = _row(norm_mix[l]), _row(norm_ffn[l])
        if l % 2 == 0:
            w_pack, rw, ssm = _even_params(i, w_in_ab, rwkv_mu, rwkv_w0, rwkv_w2, rwkv_a0, rwkv_a2, rwkv_g2,
                                           rwkv_k_k, rwkv_k_a, rwkv_r_k, rwkv_ln_w, rwkv_ln_b, ssm_conv_w,
                                           ssm_conv_b, ssm_dt_bias, ssm_a_log, ssm_d, ssm_norm_w)
            w_out = w_out_ab[i].astype(BF16)
            mix_p, wmix, sp = _mixer_ab(xp, bp, tp, g_mix, w_pack, rw, ssm, w_out,
                                        jnp.zeros((bp, RW_COLS), F32),
                                        jnp.zeros((bp, RW_HEADS, RW_HEAD, RW_HEAD), F32),
                                        jnp.zeros((bp, SSM_CONV - 1, SSM_CONV_DIM), F32),
                                        jnp.zeros((bp, SSM_HEADS, SSM_HEAD, SSM_STATE), F32), tm_p)
            mix_s, _, ss = _mixer_ab(xs, db, ts, g_mix, w_pack, rw, ssm, w_out, state_rwkv_shift[i],
                                     state_rwkv_wkv[i], state_ssm_conv[i], state_ssm[i], tm_s)
            for pre, new in (("p", sp), ("s", ss)):
                for name, val in zip(("shift", "wkv", "conv", "ssm"), new):
                    st[f"{pre}_{name}"].append(val)
            wg, wu, wd = ffn_w_gate[i].astype(BF16), ffn_w_up[i].astype(BF16), ffn_w_down[i].astype(BF16)
            xp = ffn_swiglu(xp, mix_p, wmix, g_ffn, wg, wu, wd, 2 * tm_p, 256)
            xs = ffn_swiglu(xs, mix_s, wmix, g_ffn, wg, wu, wd, tm_s, 256)
        else:
            w_qkv, w_o = attn_w_qkv[i].astype(BF16), attn_w_o[i].astype(BF16)
            qkv_p, kf, vf, means = qkv_proj(xp, g_mix, w_qkv, bp, tp, 2 * tm_p)
            nb = tp // MOBA_BLOCK
            means = means.reshape(bp, nb, ATT_KV_HEADS, ATT_HEAD).transpose(0, 2, 1, 3)
            k_p = qkv_p[:, ATT_Q_COLS:ATT_Q_COLS + ATT_KV_COLS].reshape(bp, tp, ATT_KV_HEADS, ATT_HEAD)
            v_p = qkv_p[:, ATT_Q_COLS + ATT_KV_COLS:].reshape(bp, tp, ATT_KV_HEADS, ATT_HEAD)
            o_p = moba_prompt(qkv_p.reshape(bp, tp, -1), kf, vf, means, slf)
            xp = matmul_residual(xp, [o_p.reshape(bp * tp, ATT_Q_COLS)], [w_o], 2 * tm_p)
            st["p_k"].append(k_p)
            st["p_v"].append(v_p)

            qkv_s = norm_matmul(xs, g_mix, w_qkv, tm_s, 512).reshape(db, ts, -1)
            kn = qkv_s[..., ATT_Q_COLS:ATT_Q_COLS + ATT_KV_COLS].reshape(db, ts, ATT_KV_HEADS, ATT_HEAD)
            vn = qkv_s[..., ATT_Q_COLS + ATT_KV_COLS:].reshape(db, ts, ATT_KV_HEADS, ATT_HEAD)
            o_s = moba_sample(page_table, _pages2d(cache_k[i]), _pages2d(cache_v[i]),
                              _block_diag_queries(qkv_s[..., :ATT_Q_COLS]), _new_rows(kn), _new_rows(vn),
                              slope_rows, ts)
            o_s = o_s.reshape(db, ATT_HEADS, ts, ATT_HEAD).transpose(0, 2, 1, 3).reshape(db * ts, ATT_Q_COLS)
            xs = matmul_residual(xs, [o_s], [w_o], tm_s)
            st["s_k"].append(kn)
            st["s_v"].append(vn)

            router = _cat3w(jnp.pad(moe_router[i], ((0, 0), (0, 128 - N_EXPERTS))), 0)
            wg, wu, wd = moe_w_gate[i].astype(BF16), moe_w_up[i].astype(BF16), moe_w_down[i].astype(BF16)
            final = l == depth - 1
            xp = moe_swiglu(xp, g_ffn, router, wg, wu, wd, _row(norm_final), 2 * tm_p, final)
            xs = moe_swiglu(xs, g_ffn, router, wg, wu, wd, _row(norm_final), tm_s, final)
    y_prompt = xp.reshape(bp, tp, D_MODEL)
    y_sample = xs.reshape(db, ts, D_MODEL)
    stack = lambda n: jnp.stack(st[n])
    return (y_prompt, y_sample,
            stack("p_shift"), stack("p_wkv"), stack("p_conv"), stack("p_ssm"), stack("p_k"), stack("p_v"),
            stack("s_shift"), stack("s_wkv"), stack("s_conv"), stack("s_ssm"), stack("s_k"), stack("s_v"))
```

```python
import functools

import numpy as np
import jax
import jax.numpy as jnp
from jax import lax
from jax.experimental import pallas as pl
from jax.experimental.pallas import tpu as pltpu

F32 = jnp.float32
BF16 = jnp.bfloat16
NEG_INF = float("-inf")

D_MODEL = 1024
NORM_EPS = 1e-6

RW_HEAD = 64
RW_HEADS = 8
RW_DIM = RW_HEADS * RW_HEAD
RW_LORA = 128
RW_GATE = 128
RW_COLS = 3 * RW_DIM + RW_LORA + RW_GATE
RW_LN_EPS = 64e-5

SSM_INNER = 1024
SSM_HEAD = 64
SSM_HEADS = 16
SSM_GROUPS = 2
SSM_STATE = 128
SSM_CONV = 4
SSM_CONV_DIM = SSM_INNER + 2 * SSM_GROUPS * SSM_STATE
SSD_CHUNK = 128

PROJ_RW = 0
PROJ_DT = RW_COLS
PROJ_Z = 2048
PROJ_XBC = 3072
PROJ_COLS = PROJ_XBC + SSM_CONV_DIM

ATT_HEADS = 16
ATT_KV_HEADS = 4
ATT_HEAD = 64
ATT_GROUP = ATT_HEADS // ATT_KV_HEADS
ATT_Q_COLS = ATT_HEADS * ATT_HEAD
ATT_KV_COLS = ATT_KV_HEADS * ATT_HEAD
MOBA_BLOCK = 256
MOBA_TOPK = 3
MASK_NEG = -1e30
PAGE_SIZE = 128
FEAT_BLOCKS = 32
FEAT_START = FEAT_BLOCKS
FEAT_OFFSET = FEAT_START + 3
SLOPE_PARTS = 3

N_EXPERTS = 8

V7X_VMEM_BYTES = 64 * 1024 * 1024
VMEM_LIMIT = V7X_VMEM_BYTES - 8 * 1024 * 1024


def _cparams(sem):
    return pltpu.CompilerParams(dimension_semantics=sem, vmem_limit_bytes=VMEM_LIMIT)


def _nt(a, b, precision=None):
    return lax.dot_general(a, b, (((1,), (1,)), ((), ())), precision=precision,
                           preferred_element_type=F32)


def _dot(a, b, precision=None):
    return jnp.dot(a, b, precision=precision, preferred_element_type=F32)


def _split2(x):
    hi = x.astype(BF16)
    return hi, (x - hi.astype(F32)).astype(BF16)


def _split3(x):
    hi = x.astype(BF16)
    r = x - hi.astype(F32)
    mid = r.astype(BF16)
    return hi, mid, (r - mid.astype(F32)).astype(BF16)


def _cat3(x, axis):
    hi, lo = _split2(x)
    return jnp.concatenate([hi, lo, hi], axis=axis)


def _cat3w(w, axis):
    hi, lo = _split2(w)
    return jnp.concatenate([hi, hi, lo], axis=axis)


def _rms(x, g):
    return x * lax.rsqrt(jnp.mean(x * x, axis=-1, keepdims=True) + NORM_EPS) * g


def _softplus(x):
    return jnp.maximum(x, 0.0) + jnp.log(1.0 + jnp.exp(-jnp.abs(x)))


def _sigmoid(x):
    return 1.0 / (1.0 + jnp.exp(-x))


def _silu(x):
    return x * _sigmoid(x)


def _nm_body(x_ref, g_ref, w_ref, o_ref, h_sc):
    @pl.when(pl.program_id(1) == 0)
    def _():
        h_sc[...] = _rms(x_ref[...], g_ref[...]).astype(BF16)

    o_ref[...] = _dot(h_sc[...], w_ref[...])


def norm_matmul(x, g, w, tm, tn):
    m, k = x.shape
    n = w.shape[1]
    return pl.pallas_call(
        _nm_body,
        grid=(m // tm, n // tn),
        in_specs=[pl.BlockSpec((tm, k), lambda i, j: (i, 0)),
                  pl.BlockSpec((1, k), lambda i, j: (0, 0)),
                  pl.BlockSpec((k, tn), lambda i, j: (0, j))],
        out_specs=pl.BlockSpec((tm, tn), lambda i, j: (i, j)),
        out_shape=jax.ShapeDtypeStruct((m, n), F32),
        scratch_shapes=[pltpu.VMEM((tm, k), BF16)],
        compiler_params=_cparams(("parallel", "arbitrary")),
    )(x, g, w)


def _mmres_body(*refs, n):
    res_ref, a_refs, w_refs, o_ref = refs[0], refs[1:1 + n], refs[1 + n:1 + 2 * n], refs[1 + 2 * n]
    acc = res_ref[...]
    for a_ref, w_ref in zip(a_refs, w_refs):
        acc = acc + _dot(a_ref[...].astype(BF16), w_ref[...])
    o_ref[...] = acc


def matmul_residual(res, acts, ws, tm):
    m, d = res.shape
    n = len(acts)
    in_specs = [pl.BlockSpec((tm, d), lambda i: (i, 0))]
    in_specs += [pl.BlockSpec((tm, a.shape[1]), lambda i: (i, 0)) for a in acts]
    in_specs += [pl.BlockSpec(w.shape, lambda i: (0, 0)) for w in ws]
    return pl.pallas_call(
        functools.partial(_mmres_body, n=n),
        grid=(m // tm,),
        in_specs=in_specs,
        out_specs=pl.BlockSpec((tm, d), lambda i: (i, 0)),
        out_shape=jax.ShapeDtypeStruct((m, d), F32),
        compiler_params=_cparams(("parallel",)),
    )(res, *acts, *ws)


def _ffn_body(*refs, n_mix):
    x_ref, mix_refs, wmix_refs = refs[0], refs[1:1 + n_mix], refs[1 + n_mix:1 + 2 * n_mix]
    g_ref, wg_ref, wu_ref, wd_ref, o_ref, x1_sc, h_sc, acc_sc = refs[1 + 2 * n_mix:]
    j = pl.program_id(1)

    @pl.when(j == 0)
    def _():
        x1 = x_ref[...]
        for a_ref, w_ref in zip(mix_refs, wmix_refs):
            x1 = x1 + _dot(a_ref[...].astype(BF16), w_ref[...])
        x1_sc[...] = x1
        h_sc[...] = _rms(x1, g_ref[...]).astype(BF16)
        acc_sc[...] = jnp.zeros_like(acc_sc)

    h = h_sc[...]
    act = _silu(_dot(h, wg_ref[...])) * _dot(h, wu_ref[...])
    acc_sc[...] += _dot(act.astype(BF16), wd_ref[...])

    @pl.when(j == pl.num_programs(1) - 1)
    def _():
        o_ref[...] = x1_sc[...] + acc_sc[...]


def ffn_swiglu(x, mixes, wmixes, g, wg, wu, wd, tm, tf):
    m, d = x.shape
    f = wg.shape[1]
    row_tile = lambda a: pl.BlockSpec((tm, a.shape[1]), lambda i, j: (i, 0))
    whole = lambda a: pl.BlockSpec(a.shape, lambda i, j: (0, 0))
    return pl.pallas_call(
        functools.partial(_ffn_body, n_mix=len(mixes)),
        grid=(m // tm, f // tf),
        in_specs=[row_tile(x)] + [row_tile(a) for a in mixes] + [whole(w) for w in wmixes]
        + [pl.BlockSpec((1, d), lambda i, j: (0, 0)),
           pl.BlockSpec((d, tf), lambda i, j: (0, j)),
           pl.BlockSpec((d, tf), lambda i, j: (0, j)),
           pl.BlockSpec((tf, d), lambda i, j: (j, 0))],
        out_specs=pl.BlockSpec((tm, d), lambda i, j: (i, 0)),
        out_shape=jax.ShapeDtypeStruct((m, d), F32),
        scratch_shapes=[pltpu.VMEM((tm, d), F32), pltpu.VMEM((tm, d), BF16), pltpu.VMEM((tm, d), F32)],
        compiler_params=_cparams(("parallel", "arbitrary")),
    )(x, *mixes, *wmixes, g, wg, wu, wd)


MOE_SUB = 512
MOE_CHUNK = 160


def _moe_body(x_ref, g_ref, r_ref, wg_ref, wu_ref, wd_ref, gf_ref, o_ref, h_sc, comb_sc, acc_sc, rk_sc, rkt_sc, *,
              final_norm, chunk):
    e = pl.program_id(1)
    tm = x_ref.shape[0]
    lane = lax.broadcasted_iota(jnp.int32, (tm, 128), 1)

    @pl.when(e == 0)
    def _():
        hf = _rms(x_ref[...], g_ref[...])
        h_sc[...] = hf.astype(BF16)
        lanef = lane.astype(F32)
        logits = jnp.where(lane < N_EXPERTS, _dot(_cat3(hf, 1), r_ref[...]), NEG_INF)
        m1 = jnp.max(logits, axis=1, keepdims=True)
        i1 = jnp.min(jnp.where(logits == m1, lanef, 128.0), axis=1, keepdims=True)
        mask1 = lanef == i1
        rest = jnp.where(mask1, NEG_INF, logits)
        m2 = jnp.max(rest, axis=1, keepdims=True)
        i2 = jnp.min(jnp.where(rest == m2, lanef, 128.0), axis=1, keepdims=True)
        mask2 = lanef == i2
        e2 = jnp.exp(m2 - m1)
        den = 1.0 + e2
        comb_sc[...] = jnp.where(mask1, 1.0 / den, 0.0) + jnp.where(mask2, e2 / den, 0.0)
        acc_sc[...] = jnp.zeros_like(acc_sc)
        if chunk:
            earlier = (lax.broadcasted_iota(jnp.int32, (MOE_SUB, MOE_SUB), 1)
                       < lax.broadcasted_iota(jnp.int32, (MOE_SUB, MOE_SUB), 0))
            earlier = jnp.where(earlier, 1.0, 0.0).astype(BF16)
            routed = jnp.where(mask1 | mask2, 1.0, 0.0)
            for st in range(tm // MOE_SUB):
                sl = slice(st * MOE_SUB, (st + 1) * MOE_SUB)
                rank = _dot(earlier, routed[sl].astype(BF16))
                rank = jnp.where(routed[sl] > 0.0, rank, -1.0)
                rk_sc[sl, :] = rank
                rkt_sc[st] = rank.T

    def expert(rows):
        act = _silu(_dot(rows, wg_ref[0])) * _dot(rows, wu_ref[0])
        return _dot(act.astype(BF16), wd_ref[0])

    this = lane == e
    if not chunk:
        c = jnp.sum(jnp.where(this, comb_sc[...], 0.0), axis=1, keepdims=True)
        acc_sc[...] += c * expert(h_sc[...])
    else:
        wide = -(-chunk // 128) * 128
        slot = lax.broadcasted_iota(jnp.int32, (chunk, MOE_SUB), 0).astype(F32)
        lane_w = lax.broadcasted_iota(jnp.int32, (MOE_SUB, wide), 1)
        slot_t = jnp.where(lane_w < chunk, lane_w.astype(F32), jnp.nan)
        sub8 = lax.broadcasted_iota(jnp.int32, (8, MOE_SUB), 0)
        this_sub = lax.broadcasted_iota(jnp.int32, (MOE_SUB, 128), 1) == e
        pad_rows = jnp.zeros((wide - chunk, x_ref.shape[1]), BF16)
        for st in range(tm // MOE_SUB):
            sl = slice(st * MOE_SUB, (st + 1) * MOE_SUB)
            rank_col = jnp.sum(jnp.where(this_sub, rk_sc[sl, :], 0.0), axis=1, keepdims=True)
            rank_row = jnp.sum(jnp.where(sub8 == e, rkt_sc[st, 0:8, :], 0.0), axis=0, keepdims=True)
            c = jnp.sum(jnp.where(this_sub, comb_sc[sl, :], 0.0), axis=1, keepdims=True)
            count = jnp.max(rank_row, axis=1, keepdims=True)[0, 0].astype(jnp.int32) + 1

            def one_chunk(k, carry):
                base = (k * chunk).astype(F32)
                gather = jnp.where(rank_row - base == slot, 1.0, 0.0).astype(BF16)
                y_hi, y_lo = _split2(expert(_dot(gather, h_sc[sl, :]).astype(BF16)))
                scatter = jnp.where(rank_col - base == slot_t, 1.0, 0.0).astype(BF16)
                back = (_dot(scatter, jnp.concatenate([y_hi, pad_rows], axis=0))
                        + _dot(scatter, jnp.concatenate([y_lo, pad_rows], axis=0)))
                acc_sc[sl, :] += c * back
                return carry

            lax.fori_loop(0, (count + chunk - 1) // chunk, one_chunk, 0)

    @pl.when(e == pl.num_programs(1) - 1)
    def _():
        y = x_ref[...] + acc_sc[...]
        if final_norm:
            y = _rms(y, gf_ref[...])
        o_ref[...] = y


def moe_swiglu(x, g, router, wg, wu, wd, gfinal, tm, final_norm):
    m, d = x.shape
    ne, _, fe = wg.shape
    chunk = MOE_CHUNK if tm % MOE_SUB == 0 else 0
    nsub = max(tm // MOE_SUB, 1)
    return pl.pallas_call(
        functools.partial(_moe_body, final_norm=final_norm, chunk=chunk),
        grid=(m // tm, ne),
        in_specs=[pl.BlockSpec((tm, d), lambda i, e: (i, 0)),
                  pl.BlockSpec((1, d), lambda i, e: (0, 0)),
                  pl.BlockSpec((3 * d, 128), lambda i, e: (0, 0)),
                  pl.BlockSpec((1, d, fe), lambda i, e: (e, 0, 0)),
                  pl.BlockSpec((1, d, fe), lambda i, e: (e, 0, 0)),
                  pl.BlockSpec((1, fe, d), lambda i, e: (e, 0, 0)),
                  pl.BlockSpec((1, d), lambda i, e: (0, 0))],
        out_specs=pl.BlockSpec((tm, d), lambda i, e: (i, 0)),
        out_shape=jax.ShapeDtypeStruct((m, d), F32),
        scratch_shapes=[pltpu.VMEM((tm, d), BF16), pltpu.VMEM((tm, 128), F32), pltpu.VMEM((tm, d), F32),
                        pltpu.VMEM((tm, 128), F32), pltpu.VMEM((nsub, 128, min(tm, MOE_SUB)), F32)],
        compiler_params=_cparams(("parallel", "arbitrary")),
    )(x, g, router, wg, wu, wd, gfinal)


RW_PAIRS = RW_HEADS // 2
RW_ROWS = RW_PAIRS * RW_HEAD


def _rwkv_body(p_ref, sh0_ref, s0_ref, mu_ref, w0_ref, a0_ref, kk_ref, ka_ref, rk_ref, lnw_ref, lnb_ref,
               wl_ref, g2_ref, b2_ref, pw_ref, y_ref, sfin_ref,
               s_sc, sk_sc, prev_sc, kk_sc, w_sc, b_sc, k_sc, r_sc, v_sc, g_sc, y_sc, *, nbb, tc, t_valid):
    c = pl.program_id(1)

    @pl.when(c == 0)
    def _():
        s_sc[...] = s0_ref[...]
        prev_sc[...] = sh0_ref[...]

    b2 = b2_ref[...]

    def head_sums(x, low=True):
        if low:
            hi, lo = _split2(x)
        else:
            hi = x.astype(BF16)
            lo = jnp.zeros_like(hi)
        return _dot(jnp.concatenate([hi, lo], axis=1), b2)

    def head_sums_wide(x):
        return jnp.concatenate([head_sums(x[:, q * 128:(q + 1) * 128]) for q in range(RW_PAIRS)], axis=1)

    for bi in range(nbb):
        p = p_ref[bi]
        row = lax.broadcasted_iota(jnp.int32, p.shape, 0)
        prev = jnp.where(row == 0, prev_sc[bi], pltpu.roll(p, 1, axis=0))
        prev_sc[bi] = p[tc - 1:tc, :]
        u = p + (prev - p) * mu_ref[...]
        r = u[:, 0:RW_DIM]
        k = u[:, RW_DIM:2 * RW_DIM]
        v = u[:, 2 * RW_DIM:3 * RW_DIM]
        lo = u[:, 3 * RW_DIM:3 * RW_DIM + RW_LORA]
        gd = u[:, 3 * RW_DIM + RW_LORA:]
        lane = lax.broadcasted_iota(jnp.int32, lo.shape, 1)
        lora = _dot(_cat3(jnp.where(lane < RW_LORA // 2, jnp.tanh(lo), lo), 1), wl_ref[...])
        wlog = -_softplus(-(w0_ref[...] + lora[:, :RW_DIM])) - 0.5
        a = _sigmoid(a0_ref[...] + lora[:, RW_DIM:])
        g_sc[bi] = _dot(_cat3(_sigmoid(gd), 1), g2_ref[...])
        kk = k * kk_ref[...]
        kk = kk / jnp.maximum(jnp.sqrt(head_sums_wide(kk * kk)), 1e-12)
        kk_sc[bi] = kk
        w_sc[bi] = jnp.exp(-jnp.exp(wlog))
        b_sc[bi] = kk * a
        k_sc[bi] = k * (1.0 + (a - 1.0) * ka_ref[...])
        r_sc[bi] = r
        v_sc[bi] = v

    rows = nbb * RW_ROWS
    vi = lax.broadcasted_iota(jnp.int32, (rows, 128), 0) % RW_HEAD
    li = lax.broadcasted_iota(jnp.int32, (rows, 128), 1)
    key_lane = li % RW_HEAD
    diag = (key_lane == vi).astype(F32)
    steps = 8 if t_valid % 8 == 0 else t_valid

    def rows_of(x8, j):
        return jnp.concatenate([jnp.broadcast_to(x8[bi][j:j + 1, q * 128:(q + 1) * 128], (RW_HEAD, 128))
                                for bi in range(nbb) for q in range(RW_PAIRS)], axis=0)

    pair_w = pw_ref[...]
    zero_half = jnp.zeros((rows, 128), BF16)

    def pair_sums(a, b):
        lhs = jnp.concatenate([a.astype(BF16), zero_half if b is None else b.astype(BF16)], axis=1)
        out = _dot(lhs, pair_w)
        return out[:, :128], out[:, 128:]

    first8 = [kk_sc[bi, pl.ds(0, 8), :] for bi in range(nbb)]
    sk_sc[...] = pair_sums(s_sc[...].reshape(rows, 128) * rows_of(first8, 0), None)[0]

    def group(t8, carry):
        t0 = pl.multiple_of(t8 * 8, 8)
        tn = pl.multiple_of(jnp.minimum(t0 + 8, tc - 8), 8)
        kk8, w8, b8, k8, r8, v8 = ([sc[bi, pl.ds(t0, 8), :] for bi in range(nbb)]
                                   for sc in (kk_sc, w_sc, b_sc, k_sc, r_sc, v_sc))
        kk_next = [kk_sc[bi, pl.ds(tn, 8), :] for bi in range(nbb)]
        s = s_sc[...].reshape(rows, 128)
        sk = sk_sc[...]
        ycols = jnp.zeros((rows, 128), F32)
        for j in range(steps):
            if j % 2 == 0:
                vcols = pair_sums(rows_of(v8, j) * diag, rows_of(v8, j + 1) * diag if j + 1 < steps else None)
            s = s * rows_of(w8, j) - sk * rows_of(b8, j) + vcols[j % 2] * rows_of(k8, j)
            kk_after = rows_of(kk8, j + 1) if j + 1 < 8 else rows_of(kk_next, 0)
            sk, yb = pair_sums(s * kk_after, s * rows_of(r8, j))
            ycols = jnp.where(key_lane == j, yb, ycols)
        s_sc[...] = s.reshape(nbb, RW_ROWS, 128)
        sk_sc[...] = sk
        h64 = RW_HEAD
        for bi in range(nbb):
            pieces = []
            for q in range(0, RW_PAIRS, 2):
                r0 = (bi * RW_PAIRS + q) * h64
                tr = ycols[r0:r0 + 2 * h64].T
                pieces += [tr[0:8, 0:h64], tr[h64:h64 + 8, 0:h64], tr[0:8, h64:], tr[h64:h64 + 8, h64:]]
            y_sc[bi, pl.ds(t0, 8), :] = jnp.concatenate(pieces, axis=1)
        return carry

    lax.fori_loop(0, -(-t_valid // 8), group, 0)

    inv = 1.0 / RW_HEAD
    for bi in range(nbb):
        y = y_sc[bi]
        yc = y - head_sums_wide(y) * inv
        var = head_sums_wide(yc * yc) * inv
        yn = yc * lax.rsqrt(var + RW_LN_EPS) * lnw_ref[...] + lnb_ref[...]
        bonus = head_sums_wide(r_sc[bi] * k_sc[bi] * rk_ref[...]) * v_sc[bi]
        y_ref[bi] = ((yn + bonus) * g_sc[bi]).astype(y_ref.dtype)

    @pl.when(c == pl.num_programs(1) - 1)
    def _():
        sfin_ref[...] = s_sc[...]


def _wkv_to_rows(s):
    b = s.shape[0]
    return s.reshape(b, RW_PAIRS, 2, RW_HEAD, RW_HEAD).transpose(0, 1, 3, 2, 4).reshape(b, RW_ROWS, 128)


def _rows_to_wkv(s):
    b = s.shape[0]
    return s.reshape(b, RW_PAIRS, RW_HEAD, 2, RW_HEAD).transpose(0, 1, 3, 2, 4).reshape(b, RW_HEADS, RW_HEAD, RW_HEAD)


def rwkv7(proj, shift0, wkv0, prm, nbb, tc, t_valid):
    b, t, _ = proj.shape
    vec = lambda n: pl.BlockSpec((1, n), lambda i, c: (0, 0))
    full = lambda a: pl.BlockSpec(a.shape, lambda i, c: (0,) * a.ndim)
    seq = pltpu.VMEM((nbb, tc, RW_DIM), F32)
    y, s_fin = pl.pallas_call(
        functools.partial(_rwkv_body, nbb=nbb, tc=tc, t_valid=t_valid),
        grid=(b // nbb, t // tc),
        in_specs=[pl.BlockSpec((nbb, tc, RW_COLS), lambda i, c: (i, c, 0)),
                  pl.BlockSpec((nbb, 1, RW_COLS), lambda i, c: (i, 0, 0)),
                  pl.BlockSpec((nbb, RW_ROWS, 128), lambda i, c: (i, 0, 0)),
                  vec(RW_COLS), vec(RW_DIM), vec(RW_DIM), vec(RW_DIM), vec(RW_DIM), vec(RW_DIM), vec(RW_DIM),
                  vec(RW_DIM), full(prm["wl"]), full(prm["g2"]), full(prm["bones2"]), full(prm["pair_w"])],
        out_specs=[pl.BlockSpec((nbb, tc, RW_DIM), lambda i, c: (i, c, 0)),
                   pl.BlockSpec((nbb, RW_ROWS, 128), lambda i, c: (i, 0, 0))],
        out_shape=[jax.ShapeDtypeStruct((b, t, RW_DIM), BF16),
                   jax.ShapeDtypeStruct((b, RW_ROWS, 128), F32)],
        scratch_shapes=[pltpu.VMEM((nbb, RW_ROWS, 128), F32), pltpu.VMEM((nbb * RW_ROWS, 128), F32),
                        pltpu.VMEM((nbb, 1, RW_COLS), F32),
                        seq, seq, seq, seq, seq, seq, seq, seq],
        compiler_params=_cparams(("parallel", "arbitrary")),
    )(proj, shift0, _wkv_to_rows(wkv0), prm["mu"], prm["w0"], prm["a0"], prm["k_k"], prm["k_a"], prm["r_k"],
      prm["ln_w"], prm["ln_b"], prm["wl"], prm["g2"], prm["bones2"], prm["pair_w"])
    return y, _rows_to_wkv(s_fin)


def _ssd_body(z_ref, xbc_ref, dt_ref, conv0_ref, h0_ref, cw_ref, cb_ref, dtb_ref, alog_ref, dsk_ref, nw_ref,
              e_ref, y_ref, hfin_ref, h_sc, ext_sc, *, t_valid):
    c = pl.program_id(1)
    ln = SSD_CHUNK

    @pl.when(c == 0)
    def _():
        h_sc[...] = h0_ref[0]
        ext_sc[0:8, :] = conv0_ref[0]

    xbc = xbc_ref[0]
    ext_sc[8:8 + ln, :] = xbc
    cw = cw_ref[...]
    conv = cb_ref[...] + ext_sc[5:5 + ln, :] * cw[0:1]
    conv = conv + ext_sc[6:6 + ln, :] * cw[1:2]
    conv = conv + ext_sc[7:7 + ln, :] * cw[2:3]
    conv = conv + xbc * cw[3:4]
    ext_sc[0:8, :] = xbc[ln - 8:ln, :]
    act = _silu(conv)
    xs = act[:, :SSM_INNER]
    bm = act[:, SSM_INNER:SSM_INNER + SSM_GROUPS * SSM_STATE].astype(BF16)
    cm = act[:, SSM_INNER + SSM_GROUPS * SSM_STATE:].astype(BF16)

    lane = lax.broadcasted_iota(jnp.int32, (ln, 128), 1)
    rowi = lax.broadcasted_iota(jnp.int32, (ln, 128), 0)
    dt = _softplus(dt_ref[0] + dtb_ref[...])
    if t_valid < ln:
        dt = jnp.where(rowi < t_valid, dt, 0.0)
    a = dt * jnp.where(lane[0:1] < SSM_HEADS, -jnp.exp(alog_ref[...]), 0.0)
    causal = lane <= rowi
    tri = jnp.where(causal, 1.0, 0.0).astype(BF16)
    tri3 = jnp.concatenate([tri, tri, tri], axis=1)
    acum = _dot(tri3, jnp.concatenate(_split3(a), axis=0))
    acum_t = _nt(jnp.concatenate(_split3(a.T), axis=1), tri3)
    e3 = e_ref[...]
    dt_full = _dot(jnp.concatenate(_split3(dt), axis=1), e3)
    ac_full = _dot(jnp.concatenate(_split3(acum), axis=1), e3)
    xdt = xs * dt_full
    xw = xdt * jnp.exp(ac_full[ln - 1:ln, :] - ac_full)
    eac = jnp.exp(ac_full)
    cd = jnp.broadcast_to(jnp.exp(acum_t[:, ln - 1:ln]), (128, 128))
    cd_rows = jnp.concatenate([jnp.broadcast_to(cd[h:h + 1, :], (SSM_HEAD, SSM_STATE)) for h in range(SSM_HEADS)],
                              axis=0)

    ys = []
    for g in range(SSM_GROUPS):
        bm_g = bm[:, g * SSM_STATE:(g + 1) * SSM_STATE]
        cm_g = cm[:, g * SSM_STATE:(g + 1) * SSM_STATE]
        cbm = _nt(cm_g, bm_g)
        pairs = SSM_HEADS // SSM_GROUPS // 2
        for q in range(g * pairs, (g + 1) * pairs):
            sl = slice(q * 128, (q + 1) * 128)
            xdt_p = xdt[:, sl].astype(BF16)
            yd = []
            for h in (2 * q, 2 * q + 1):
                seg = jnp.where(causal, acum[:, h:h + 1] - acum_t[h:h + 1, :], NEG_INF)
                yd.append(_dot((cbm * jnp.exp(seg)).astype(BF16), xdt_p))
            hp = h_sc[sl, :]
            y_off = _nt(cm_g, hp.astype(BF16)) * eac[:, sl]
            st = _dot(xw[:, sl].T.astype(BF16), bm_g)
            h_sc[sl, :] = cd_rows[sl, :] * hp + st
            ys.append(jnp.where(lane < SSM_HEAD, yd[0], yd[1]) + y_off)
    y = jnp.concatenate(ys, axis=1) + dsk_ref[...] * xs
    y = y * _silu(z_ref[0])
    gw = SSM_INNER // SSM_GROUPS
    outs = []
    for g in range(SSM_GROUPS):
        yg = y[:, g * gw:(g + 1) * gw]
        outs.append(yg * lax.rsqrt(jnp.mean(yg * yg, axis=1, keepdims=True) + NORM_EPS))
    y_ref[0] = (jnp.concatenate(outs, axis=1) * nw_ref[...]).astype(y_ref.dtype)

    @pl.when(c == pl.num_programs(1) - 1)
    def _():
        hfin_ref[0] = h_sc[...]


def mamba2(proj, conv0, ssm0, prm, t_valid):
    b, t, _ = proj.shape
    ln = SSD_CHUNK
    vec = lambda n: pl.BlockSpec((1, n), lambda i, c: (0, 0))
    full = lambda a: pl.BlockSpec(a.shape, lambda i, c: (0,) * a.ndim)
    return pl.pallas_call(
        functools.partial(_ssd_body, t_valid=t_valid),
        grid=(b, t // ln),
        in_specs=[pl.BlockSpec((1, ln, SSM_INNER), lambda i, c: (i, c, PROJ_Z // SSM_INNER)),
                  pl.BlockSpec((1, ln, SSM_CONV_DIM), lambda i, c: (i, c, PROJ_XBC // SSM_CONV_DIM)),
                  pl.BlockSpec((1, ln, 128), lambda i, c: (i, c, PROJ_DT // 128)),
                  pl.BlockSpec((1, 8, SSM_CONV_DIM), lambda i, c: (i, 0, 0)),
                  pl.BlockSpec((1, SSM_INNER, SSM_STATE), lambda i, c: (i, 0, 0)),
                  full(prm["conv_w"]), vec(SSM_CONV_DIM), vec(128), vec(128), vec(SSM_INNER), vec(SSM_INNER),
                  full(prm["expand"])],
        out_specs=[pl.BlockSpec((1, ln, SSM_INNER), lambda i, c: (i, c, 0)),
                   pl.BlockSpec((1, SSM_INNER, SSM_STATE), lambda i, c: (i, 0, 0))],
        out_shape=[jax.ShapeDtypeStruct((b, t, SSM_INNER), BF16),
                   jax.ShapeDtypeStruct((b, SSM_INNER, SSM_STATE), F32)],
        scratch_shapes=[pltpu.VMEM((SSM_INNER, SSM_STATE), F32), pltpu.VMEM((8 + ln, SSM_CONV_DIM), F32)],
        compiler_params=_cparams(("parallel", "arbitrary")),
    )(proj, proj, proj, conv0, ssm0, prm["conv_w"], prm["conv_b"], prm["dt_bias"], prm["a_log"],
      prm["d_full"], prm["norm_w"], prm["expand"])


def _alibi_slopes():
    return (2.0 ** (-8.0 * np.arange(1, ATT_HEADS + 1) / ATT_HEADS)).astype(np.float32)


def _slope_features():
    s = _alibi_slopes()
    bf = lambda a: a.astype(jnp.bfloat16).astype(np.float32)
    hi = bf(s)
    mid = bf(s - hi)
    lo = bf(s - hi - mid)
    out = np.zeros((ATT_HEADS, FEAT_BLOCKS), np.float32)
    for i, piece in enumerate((hi, mid, lo)):
        out[:, i] = piece
        out[:, SLOPE_PARTS + i] = piece
    return out


def _top3_threshold(gate):
    v = gate
    for _ in range(MOBA_TOPK - 1):
        v = jnp.where(v == jnp.max(v, axis=1, keepdims=True), NEG_INF, v)
    return jnp.max(v, axis=1, keepdims=True)


def _qkv_body(x_ref, g_ref, w_ref, qkv_ref, kf_ref, vf_ref, mean_ref, *, tiles_per_seq):
    tm = x_ref.shape[0]
    qkv = _dot(_rms(x_ref[...], g_ref[...]).astype(BF16), w_ref[...])
    qkv_ref[...] = qkv
    for r in range(tm // MOBA_BLOCK):
        mean_ref[r] = jnp.mean(qkv[r * MOBA_BLOCK:(r + 1) * MOBA_BLOCK, ATT_Q_COLS:ATT_Q_COLS + ATT_KV_COLS],
                               axis=0, keepdims=True)
    pos = (pl.program_id(0) % tiles_per_seq) * tm + lax.broadcasted_iota(jnp.int32, (tm, ATT_HEAD), 0)
    lane = lax.broadcasted_iota(jnp.int32, (tm, ATT_HEAD), 1)
    blk_id, off = pos // MOBA_BLOCK, pos % MOBA_BLOCK
    kfeat = jnp.where(lane == blk_id, 1, 0)
    kfeat = jnp.where((lane >= FEAT_START) & (lane < FEAT_START + SLOPE_PARTS), blk_id * MOBA_BLOCK, kfeat)
    kfeat = jnp.where((lane >= FEAT_OFFSET) & (lane < FEAT_OFFSET + SLOPE_PARTS), off, kfeat)
    kfeat = kfeat.astype(F32).astype(BF16)
    vfeat = jnp.where(lane == 0, 1.0, 0.0).astype(BF16)
    for g in range(ATT_KV_HEADS):
        k0 = ATT_Q_COLS + g * ATT_HEAD
        v0 = ATT_Q_COLS + ATT_KV_COLS + g * ATT_HEAD
        kf_ref[0, g] = jnp.concatenate([qkv[:, k0:k0 + ATT_HEAD].astype(BF16), kfeat], axis=1)
        vf_ref[0, g] = jnp.concatenate([qkv[:, v0:v0 + ATT_HEAD].astype(BF16), vfeat], axis=1)


def qkv_proj(x, g, w, b, t, tm):
    m, d = x.shape
    n = w.shape[1]
    tps = t // tm
    ext = jax.ShapeDtypeStruct((b, ATT_KV_HEADS, t, 128), BF16)
    ext_spec = pl.BlockSpec((1, ATT_KV_HEADS, tm, 128), lambda i: (i // tps, 0, i % tps, 0))
    return pl.pallas_call(
        functools.partial(_qkv_body, tiles_per_seq=tps),
        grid=(m // tm,),
        in_specs=[pl.BlockSpec((tm, d), lambda i: (i, 0)), pl.BlockSpec((1, d), lambda i: (0, 0)),
                  pl.BlockSpec((d, n), lambda i: (0, 0))],
        out_specs=[pl.BlockSpec((tm, n), lambda i: (i, 0)), ext_spec, ext_spec,
                   pl.BlockSpec((tm // MOBA_BLOCK, 1, ATT_KV_COLS), lambda i: (i, 0, 0))],
        out_shape=[jax.ShapeDtypeStruct((m, n), F32), ext, ext,
                   jax.ShapeDtypeStruct((m // MOBA_BLOCK, 1, ATT_KV_COLS), F32)],
        compiler_params=_cparams(("parallel",)),
    )(x, g, w)


def _moba_prompt_body(q_ref, kf_ref, vf_ref, mean_ref, slf_ref, o_ref, qf_sc, m_sc, acc_sc, sa0_sc, sa1_sc, sb0_sc, sb1_sc):
    qb = pl.program_id(2)
    blk = MOBA_BLOCK
    rows = ATT_GROUP * blk
    scale = ATT_HEAD ** -0.5

    q4 = q_ref[0]
    qg = jnp.concatenate([q4[:, hh * ATT_HEAD:(hh + 1) * ATT_HEAD] for hh in range(ATT_GROUP)], axis=0)
    past_blk = lax.broadcasted_iota(jnp.int32, (128, rows), 0) < qb
    gate = jnp.where(past_blk, _nt(_cat3w(mean_ref[0, 0], 1), _cat3(qg, 1)), NEG_INF)
    v = gate
    for _ in range(MOBA_TOPK - 1):
        v = jnp.where(v == jnp.max(v, axis=0, keepdims=True), NEG_INF, v)
    thr = jnp.max(v, axis=0, keepdims=True)
    selneg = jnp.where(past_blk & (gate < thr), MASK_NEG, 0.0).T[:, :FEAT_BLOCKS]
    slf = jnp.concatenate([jnp.broadcast_to(slf_ref[0, hh:hh + 1, :], (blk, FEAT_BLOCKS))
                           for hh in range(ATT_GROUP)], axis=0)
    qf_sc[...] = jnp.concatenate([(qg * scale).astype(BF16), selneg.astype(BF16), slf.astype(BF16)], axis=1)

    def kv_tiles(j):
        start = pl.multiple_of(j * blk, blk)
        return kf_ref[0, 0, pl.ds(start, blk), :], vf_ref[0, 0, pl.ds(start, blk), :]

    qi = lax.broadcasted_iota(jnp.int32, (rows, blk), 0) % blk
    ki = lax.broadcasted_iota(jnp.int32, (rows, blk), 1)
    kf, vf = kv_tiles(qb)
    s = jnp.where(ki <= qi, _nt(qf_sc[...], kf), NEG_INF)
    m = jnp.max(s, axis=1, keepdims=True)
    m_sc[...] = jnp.broadcast_to(m, (rows, 128))
    acc_sc[...] = _dot(jnp.exp(s - m).astype(BF16), vf)

    def scores(j):
        return _nt(qf_sc[...], kv_tiles(jnp.minimum(j, qb))[0])

    def fold(blocks, s_cur, s_next=()):
        for ref, j in zip(s_next, (blocks[-1] + 1, blocks[-1] + 2)):
            ref[...] = scores(j)
        ss = [ref[...] for ref in s_cur]
        m_old = m_sc[...]
        m_new = m_old
        for s in ss:
            m_new = jnp.maximum(m_new, jnp.max(s, axis=1, keepdims=True))
        m2 = jnp.concatenate([m_new, m_new], axis=1)
        acc = acc_sc[...] * jnp.exp(m_old - m_new)
        for s, j in zip(ss, blocks):
            acc = acc + _dot(jnp.exp(s - m2).astype(BF16), kv_tiles(j)[1])
        acc_sc[...] = acc
        m_sc[...] = m_new

    set_a, set_b = (sa0_sc, sa1_sc), (sb0_sc, sb1_sc)
    sa0_sc[...] = scores(0)
    sa1_sc[...] = scores(1)

    def quad(i, carry):
        fold((4 * i, 4 * i + 1), set_a, set_b)
        fold((4 * i + 2, 4 * i + 3), set_b, set_a)
        return carry

    lax.fori_loop(0, qb // 4, quad, 0)
    done = qb // 4 * 4

    @pl.when(qb % 4 >= 2)
    def _():
        fold((done, done + 1), set_a)

    @pl.when(qb % 4 == 1)
    def _():
        fold((done,), set_a[:1])

    @pl.when(qb % 4 == 3)
    def _():
        sb0_sc[...] = scores(done + 2)
        fold((done + 2,), set_b[:1])

    acc = acc_sc[...]
    out = acc[:, :ATT_HEAD] / acc[:, ATT_HEAD:ATT_HEAD + 1]
    o_ref[0] = jnp.concatenate([out[hh * blk:(hh + 1) * blk] for hh in range(ATT_GROUP)], axis=1).astype(o_ref.dtype)


def moba_prompt(qkv, kf, vf, means, slf):
    b, t, _ = qkv.shape
    nb = t // MOBA_BLOCK
    assert nb <= FEAT_BLOCKS
    nbl = 128
    means = jnp.pad(means, ((0, 0), (0, 0), (0, nbl - nb), (0, 0)))
    rows = ATT_GROUP * MOBA_BLOCK
    gw = ATT_GROUP * ATT_HEAD
    return pl.pallas_call(
        _moba_prompt_body,
        grid=(b, ATT_KV_HEADS, nb),
        in_specs=[pl.BlockSpec((1, MOBA_BLOCK, gw), lambda i, g, j: (i, j, g)),
                  pl.BlockSpec((1, 1, t, 128), lambda i, g, j: (i, g, 0, 0)),
                  pl.BlockSpec((1, 1, t, 128), lambda i, g, j: (i, g, 0, 0)),
                  pl.BlockSpec((1, 1, nbl, ATT_HEAD), lambda i, g, j: (i, g, 0, 0)),
                  pl.BlockSpec((1, ATT_GROUP, FEAT_BLOCKS), lambda i, g, j: (g, 0, 0))],
        out_specs=pl.BlockSpec((1, MOBA_BLOCK, gw), lambda i, g, j: (i, j, g)),
        out_shape=jax.ShapeDtypeStruct((b, t, ATT_Q_COLS), BF16),
        scratch_shapes=[pltpu.VMEM((rows, 128), BF16), pltpu.VMEM((rows, 128), F32), pltpu.VMEM((rows, 128), F32),
                        *([pltpu.VMEM((rows, MOBA_BLOCK), F32)] * 4)],
        compiler_params=_cparams(("parallel", "parallel", "arbitrary")),
    )(qkv, kf, vf, means, slf)


def _moba_sample_body(pt_ref, *refs, nbp, bps, nq):
    npg = 2 * bps
    k_refs, v_refs = refs[:npg], refs[npg:2 * npg]
    q_ref, kn_ref, vn_ref, slope_ref, o_ref, mean_sc, m_sc, l_sc, acc_sc = refs[2 * npg:]
    j = pl.program_id(1)
    blk = MOBA_BLOCK
    rows = ATT_HEADS * nq
    grows = ATT_GROUP * nq
    past = nbp * blk
    scale = ATT_HEAD ** -0.5
    lane = lax.broadcasted_iota(jnp.int32, (rows, 128), 1)
    qi = (lax.broadcasted_iota(jnp.int32, (rows, 1), 0) % nq).astype(F32)
    qf = q_ref[0]
    qs = (qf * scale).astype(BF16)
    slope = slope_ref[...]

    @pl.when(j == 0)
    def _():
        mean_sc[...] = jnp.zeros_like(mean_sc)
        m_sc[...] = jnp.full(m_sc.shape, NEG_INF, F32)
        l_sc[...] = jnp.zeros_like(l_sc)

    kcat = jnp.concatenate([ref[0] for ref in k_refs], axis=0)
    off = lax.broadcasted_iota(jnp.int32, (1, bps * blk), 1).astype(F32)
    sc = _nt(qs, kcat.astype(BF16)) - slope * ((past - j * bps * blk).astype(F32) + qi - off)
    m_new, l_new = m_sc[...], l_sc[...]
    for s in range(bps):
        bi = j * bps + s
        ksum = jnp.sum(kcat[s * blk:(s + 1) * blk], axis=0, keepdims=True)
        mean_sc[pl.ds(bi, 1), :] = ksum * (1.0 / blk)
        sc_b = sc[:, s * blk:(s + 1) * blk]
        m = jnp.max(sc_b, axis=1, keepdims=True)
        p = jnp.exp(sc_b - m)
        m_new = jnp.where(lane == bi, m, m_new)
        l_new = jnp.where(lane == bi, jnp.sum(p, axis=1, keepdims=True), l_new)
        vblk = jnp.concatenate([v_refs[2 * s][0], v_refs[2 * s + 1][0]], axis=0)
        acc_sc[bi] = _dot(p.astype(BF16), vblk.astype(BF16))
    m_sc[...] = m_new
    l_sc[...] = l_new

    @pl.when(j == pl.num_programs(1) - 1)
    def _():
        gate = jnp.where(lane < nbp, _nt(_cat3(qf, 1), _cat3w(mean_sc[...], 1)), NEG_INF)
        sel = gate >= _top3_threshold(gate)
        ki = lax.broadcasted_iota(jnp.int32, (rows, 8), 1).astype(F32)
        s_own = jnp.where(ki <= qi, _nt(qs, kn_ref[0].astype(BF16)) - slope * (qi - ki), NEG_INF)
        m_all = jnp.where(sel, m_sc[...], NEG_INF)
        mx = jnp.maximum(jnp.max(m_all, axis=1, keepdims=True), jnp.max(s_own, axis=1, keepdims=True))
        w = jnp.where(sel, jnp.exp(m_all - mx), 0.0)
        p_own = jnp.exp(s_own - mx)
        den = jnp.sum(w * l_sc[...], axis=1, keepdims=True) + jnp.sum(p_own, axis=1, keepdims=True)
        acc = _dot(p_own.astype(BF16), vn_ref[0].astype(BF16))
        for n in range(nbp):
            acc = acc + w[:, n:n + 1] * acc_sc[n]
        out = acc / den
        for g in range(ATT_KV_HEADS):
            o_ref[0, g] = out[g * grows:(g + 1) * grows, g * ATT_HEAD:(g + 1) * ATT_HEAD]


SAMPLE_BLOCKS_PER_STEP = 8


def _pages2d(cache):
    return cache.reshape(cache.shape[0], PAGE_SIZE, ATT_KV_COLS)


def _block_diag_queries(q):
    db, nq, _ = q.shape
    qh = q.reshape(db, nq, ATT_KV_HEADS, ATT_GROUP, ATT_HEAD).transpose(0, 2, 3, 1, 4)
    eye = jnp.eye(ATT_KV_HEADS, dtype=q.dtype)
    return (qh[:, :, :, :, None, :] * eye[None, :, None, None, :, None]).reshape(db, ATT_HEADS * nq, ATT_KV_COLS)


def _new_rows(a):
    db, nq = a.shape[:2]
    return jnp.pad(a.reshape(db, nq, ATT_KV_COLS), ((0, 0), (0, 8 - nq), (0, 0)))


def moba_sample(page_table, cache_k, cache_v, q_bd, k_new, v_new, slope_rows, nq):
    db, n_pages = page_table.shape
    nbp = n_pages * PAGE_SIZE // MOBA_BLOCK
    bps = min(SAMPLE_BLOCKS_PER_STEP, nbp)
    rows = ATT_HEADS * nq
    grows = ATT_GROUP * nq

    def page_spec(slot):
        return pl.BlockSpec((1, PAGE_SIZE, ATT_KV_COLS),
                            lambda b, j, pt: (pt[b * n_pages + j * 2 * bps + slot], 0, 0))

    small = lambda shp: pl.BlockSpec((1,) + shp, lambda b, j, pt: (b,) + (0,) * len(shp))
    grid_spec = pltpu.PrefetchScalarGridSpec(
        num_scalar_prefetch=1,
        grid=(db, nbp // bps),
        in_specs=[page_spec(s) for s in range(2 * bps)] * 2
        + [small((rows, ATT_KV_COLS)), small((8, ATT_KV_COLS)), small((8, ATT_KV_COLS)),
           pl.BlockSpec((rows, 1), lambda b, j, pt: (0, 0))],
        out_specs=small((ATT_KV_HEADS, grows, ATT_HEAD)),
        scratch_shapes=[pltpu.VMEM((128, ATT_KV_COLS), F32), pltpu.VMEM((rows, 128), F32),
                        pltpu.VMEM((rows, 128), F32), pltpu.VMEM((nbp, rows, ATT_KV_COLS), F32)],
    )
    return pl.pallas_call(
        functools.partial(_moba_sample_body, nbp=nbp, bps=bps, nq=nq),
        grid_spec=grid_spec,
        out_shape=jax.ShapeDtypeStruct((db, ATT_KV_HEADS, grows, ATT_HEAD), F32),
        compiler_params=_cparams(("parallel", "arbitrary")),
    )(page_table.reshape(-1), *([cache_k] * (2 * bps)), *([cache_v] * (2 * bps)), q_bd, k_new, v_new, slope_rows)


def _row(a):
    return a.reshape(1, -1).astype(F32)


def _even_params(i, w_in_ab, rwkv_mu, rwkv_w0, rwkv_w2, rwkv_a0, rwkv_a2, rwkv_g2, rwkv_k_k, rwkv_k_a, rwkv_r_k,
                 rwkv_ln_w, rwkv_ln_b, ssm_conv_w, ssm_conv_b, ssm_dt_bias, ssm_a_log, ssm_d, ssm_norm_w):
    w = w_in_ab[i]
    ssm0 = RW_COLS
    zeros = lambda n: jnp.zeros((D_MODEL, n), w.dtype)
    w_pack = jnp.concatenate([
        w[:, :RW_COLS],
        w[:, ssm0 + SSM_INNER + SSM_CONV_DIM:], zeros(PROJ_Z - PROJ_DT - SSM_HEADS),
        w[:, ssm0:ssm0 + SSM_INNER],
        w[:, ssm0 + SSM_INNER:ssm0 + SSM_INNER + SSM_CONV_DIM]], axis=1).astype(BF16)
    half = RW_LORA // 2
    zl = jnp.zeros((half, RW_DIM), F32)
    wl = jnp.concatenate([jnp.concatenate([rwkv_w2[i], zl], axis=1),
                          jnp.concatenate([zl, rwkv_a2[i]], axis=1)], axis=0)
    half_of = np.arange(128) // RW_HEAD
    bones2 = np.tile((half_of[:, None] == half_of[None, :]).astype(np.float32), (2, 1))
    rw = dict(mu=_row(rwkv_mu[i]), w0=_row(rwkv_w0[i]), a0=_row(rwkv_a0[i]), k_k=_row(rwkv_k_k[i]),
              k_a=_row(rwkv_k_a[i]), r_k=_row(rwkv_r_k[i]), ln_w=_row(rwkv_ln_w[i]), ln_b=_row(rwkv_ln_b[i]),
              wl=_cat3w(wl, 0), g2=_cat3w(rwkv_g2[i], 0), bones2=jnp.asarray(bones2, BF16),
              pair_w=jnp.asarray(np.kron(np.eye(4, dtype=np.float32), np.ones((RW_HEAD, RW_HEAD), np.float32)), BF16))
    pad128 = lambda a: jnp.pad(_row(a), ((0, 0), (0, 128 - a.shape[-1])))
    expand = (np.arange(128)[:, None] == (np.arange(SSM_INNER) // SSM_HEAD)[None, :]).astype(np.float32)
    ssm = dict(conv_w=ssm_conv_w[i], conv_b=_row(ssm_conv_b[i]), dt_bias=pad128(ssm_dt_bias[i]),
               a_log=pad128(ssm_a_log[i]), d_full=_row(jnp.repeat(ssm_d[i], SSM_HEAD)), norm_w=_row(ssm_norm_w[i]),
               expand=jnp.asarray(np.tile(expand, (3, 1)), BF16))
    return w_pack, rw, ssm


def _mixer_ab(x, b, t, norm_g, w_pack, rw, ssm, w_out, shift0, wkv0, conv0, ssm0, tm):
    tm_in = 2 * tm if (b * t) % (2 * tm) == 0 else tm
    proj = norm_matmul(x, norm_g, w_pack, tm_in, SSM_CONV_DIM).reshape(b, t, PROJ_COLS)
    shift_new = proj[:, t - 1, :RW_COLS]
    conv_new = proj[:, t - (SSM_CONV - 1):, PROJ_XBC:]
    if t % SSD_CHUNK == 0:
        proj_rw, proj_ssm, tc = proj, proj, SSD_CHUNK
    else:
        proj_rw = jnp.pad(proj, ((0, 0), (0, 8 - t), (0, 0)))
        proj_ssm = jnp.pad(proj, ((0, 0), (0, SSD_CHUNK - t), (0, 0)))
        tc = 8
    nbb = next(n for n in (4, 2, 1) if b % n == 0)
    y_rw, wkv_new = rwkv7(proj_rw, shift0.reshape(b, 1, RW_COLS), wkv0, rw, nbb, tc, min(t, tc))
    conv0p = jnp.pad(conv0, ((0, 0), (8 - (SSM_CONV - 1), 0), (0, 0)))
    y_ssm, ssm_new = mamba2(proj_ssm, conv0p, ssm0.reshape(b, SSM_INNER, SSM_STATE), ssm, min(t, SSD_CHUNK))
    mixes = [y_rw[:, :t].reshape(b * t, RW_DIM), y_ssm[:, :t].reshape(b * t, SSM_INNER)]
    return (mixes, [w_out[:RW_DIM], w_out[RW_DIM:]],
            (shift_new, wkv_new, conv_new, ssm_new.reshape(b, SSM_HEADS, SSM_HEAD, SSM_STATE)))


def kernel(x_prompt, x_sample, state_rwkv_shift, state_rwkv_wkv, state_ssm_conv, state_ssm, cache_k, cache_v, page_table, norm_mix, norm_ffn, norm_final, w_in_ab, rwkv_mu, rwkv_w0, rwkv_w2, rwkv_a0, rwkv_a2, rwkv_g2, rwkv_k_k, rwkv_k_a, rwkv_r_k, rwkv_ln_w, rwkv_ln_b, ssm_conv_w, ssm_conv_b, ssm_dt_bias, ssm_a_log, ssm_d, ssm_norm_w, w_out_ab, ffn_w_gate, ffn_w_up, ffn_w_down, attn_w_qkv, attn_w_o, moe_router, moe_w_gate, moe_w_up, moe_w_down):
    bp, tp, _ = x_prompt.shape
    db, ts, _ = x_sample.shape
    depth = norm_mix.shape[0]
    tm_p, tm_s = 512, db * ts
    xp = x_prompt.reshape(bp * tp, D_MODEL)
    xs = x_sample.reshape(db * ts, D_MODEL)
    slopes = _alibi_slopes()
    slf = jnp.asarray(_slope_features().reshape(ATT_KV_HEADS, ATT_GROUP, FEAT_BLOCKS))
    slope_rows = jnp.asarray(np.repeat(slopes, ts).reshape(ATT_HEADS * ts, 1))
    st = {n: [] for n in ("p_shift", "p_wkv", "p_conv", "p_ssm", "p_k", "p_v",
                          "s_shift", "s_wkv", "s_conv", "s_ssm", "s_k", "s_v")}
    for l in range(depth):
        i = l // 2
        g_mix, g_ffn = _row(norm_mix[l]), _row(norm_ffn[l])
        if l % 2 == 0:
            w_pack, rw, ssm = _even_params(i, w_in_ab, rwkv_mu, rwkv_w0, rwkv_w2, rwkv_a0, rwkv_a2, rwkv_g2,
                                           rwkv_k_k, rwkv_k_a, rwkv_r_k, rwkv_ln_w, rwkv_ln_b, ssm_conv_w,
                                           ssm_conv_b, ssm_dt_bias, ssm_a_log, ssm_d, ssm_norm_w)
            w_out = w_out_ab[i].astype(BF16)
            mix_p, wmix, sp = _mixer_ab(xp, bp, tp, g_mix, w_pack, rw, ssm, w_out,
                                        jnp.zeros((bp, RW_COLS), F32),
                                        jnp.zeros((bp, RW_HEADS, RW_HEAD, RW_HEAD), F32),
                                        jnp.zeros((bp, SSM_CONV - 1, SSM_CONV_DIM), F32),
                                        jnp.zeros((bp, SSM_HEADS, SSM_HEAD, SSM_STATE), F32), tm_p)
            mix_s, _, ss = _mixer_ab(xs, db, ts, g_mix, w_pack, rw, ssm, w_out, state_rwkv_shift[i],
                                     state_rwkv_wkv[i], state_ssm_conv[i], state_ssm[i], tm_s)
            for pre, new in (("p", sp), ("s", ss)):
                for name, val in zip(("shift", "wkv", "conv", "ssm"), new):
                    st[f"{pre}_{name}"].append(val)
            wg, wu, wd = ffn_w_gate[i].astype(BF16), ffn_w_up[i].astype(BF16), ffn_w_down[i].astype(BF16)
            xp = ffn_swiglu(xp, mix_p, wmix, g_ffn, wg, wu, wd, 2 * tm_p, 256)
            xs = ffn_swiglu(xs, mix_s, wmix, g_ffn, wg, wu, wd, tm_s, 256)
        else:
            w_qkv, w_o = attn_w_qkv[i].astype(BF16), attn_w_o[i].astype(BF16)
            qkv_p, kf, vf, means = qkv_proj(xp, g_mix, w_qkv, bp, tp, 2 * tm_p)
            nb = tp // MOBA_BLOCK
            means = means.reshape(bp, nb, ATT_KV_HEADS, ATT_HEAD).transpose(0, 2, 1, 3)
            k_p = qkv_p[:, ATT_Q_COLS:ATT_Q_COLS + ATT_KV_COLS].reshape(bp, tp, ATT_KV_HEADS, ATT_HEAD)
            v_p = qkv_p[:, ATT_Q_COLS + ATT_KV_COLS:].reshape(bp, tp, ATT_KV_HEADS, ATT_HEAD)
            o_p = moba_prompt(qkv_p.reshape(bp, tp, -1), kf, vf, means, slf)
            xp = matmul_residual(xp, [o_p.reshape(bp * tp, ATT_Q_COLS)], [w_o], 2 * tm_p)
            st["p_k"].append(k_p)
            st["p_v"].append(v_p)

            qkv_s = norm_matmul(xs, g_mix, w_qkv, tm_s, 512).reshape(db, ts, -1)
            kn = qkv_s[..., ATT_Q_COLS:ATT_Q_COLS + ATT_KV_COLS].reshape(db, ts, ATT_KV_HEADS, ATT_HEAD)
            vn = qkv_s[..., ATT_Q_COLS + ATT_KV_COLS:].reshape(db, ts, ATT_KV_HEADS, ATT_HEAD)
            o_s = moba_sample(page_table, _pages2d(cache_k[i]), _pages2d(cache_v[i]),
                              _block_diag_queries(qkv_s[..., :ATT_Q_COLS]), _new_rows(kn), _new_rows(vn),
                              slope_rows, ts)
            o_s = o_s.reshape(db, ATT_HEADS, ts, ATT_HEAD).transpose(0, 2, 1, 3).reshape(db * ts, ATT_Q_COLS)
            xs = matmul_residual(xs, [o_s], [w_o], tm_s)
            st["s_k"].append(kn)
            st["s_v"].append(vn)

            router = _cat3w(jnp.pad(moe_router[i], ((0, 0), (0, 128 - N_EXPERTS))), 0)
            wg, wu, wd = moe_w_gate[i].astype(BF16), moe_w_up[i].astype(BF16), moe_w_down[i].astype(BF16)
            final = l == depth - 1
            xp = moe_swiglu(xp, g_ffn, router, wg, wu, wd, _row(norm_final), 2 * tm_p, final)
            xs = moe_swiglu(xs, g_ffn, router, wg, wu, wd, _row(norm_final), tm_s, final)
    y_prompt = xp.reshape(bp, tp, D_MODEL)
    y_sample = xs.reshape(db, ts, D_MODEL)
    stack = lambda n: jnp.stack(st[n])
    return (y_prompt, y_sample,
            stack("p_shift"), stack("p_wkv"), stack("p_conv"), stack("p_ssm"), stack("p_k"), stack("p_v"),
            stack("s_shift"), stack("s_wkv"), stack("s_conv"), stack("s_ssm"), stack("s_k"), stack("s_v"))
```
